```python
import jax, jax.numpy as jnp
from jax import lax
import numpy as np

D_MODEL = 2048
BATCH = 4
SEQ = 2048
DEPTH = 2
DEC_BATCH = 128
DEC_SEQ = 4
PAST_LEN = 16384
PAGE_SIZE = 128

N_MIXERS = 4
GROUP_WIDTH = D_MODEL // N_MIXERS
POOL_WINDOWS = (2, 4, 8, 16)
POOL_MAX = 16
POOL_CH = GROUP_WIDTH // len(POOL_WINDOWS)
SGU_HEADS = 4
SGU_HEAD_DIM = GROUP_WIDTH // SGU_HEADS
SGU_CHUNK = 128
CONF_KERNEL = 31
SC_KERNEL = 3
IN_COLS = GROUP_WIDTH * 8
N_EXPERTS = 32
TOP_K = 4
D_EXPERT = D_MODEL
SWIGLU_ALPHA = 1.702
SWIGLU_LIMIT = 7.0
MOE_BLOCK = 64
EPS = 1e-5

kernel_name = 'hybrid_pool_sgu_conv_moe_step'


def _rmsnorm(x, g):
    xf = x.astype(jnp.float32)
    y = xf * lax.rsqrt(jnp.mean(xf * xf, axis=-1, keepdims=True) + EPS)
    return y.astype(x.dtype) * g


def _layernorm(x, g, b=None):
    xf = x.astype(jnp.float32)
    xc = xf - jnp.mean(xf, axis=-1, keepdims=True)
    y = (xc * lax.rsqrt(jnp.mean(xc * xc, axis=-1, keepdims=True) + EPS)).astype(x.dtype) * g
    return y if b is None else y + b


def _causal_dwconv(ext, w):
    C = ext.shape[-1]
    return lax.conv_general_dilated(ext, w[:, None, :].astype(ext.dtype), window_strides=(1,), padding='VALID',
                                    dimension_numbers=('NWC', 'WIO', 'NWC'), feature_group_count=C)


def _pool_mixer(a, buf, start_pos, w_lin, scale):
    B, S, C = a.shape
    P = POOL_MAX - 1
    ext = jnp.concatenate([buf.astype(a.dtype), a], axis=1)
    cs = jnp.cumsum(ext.astype(jnp.float32), axis=1)
    cs0 = jnp.concatenate([jnp.zeros((B, 1, C), jnp.float32), cs], axis=1)
    pos = (start_pos + jnp.arange(S)).astype(jnp.float32)
    means = []
    for gi, w in enumerate(POOL_WINDOWS):
        sl = slice(gi * POOL_CH, (gi + 1) * POOL_CH)
        win_sum = cs0[:, P + 1:P + 1 + S, sl] - cs0[:, P + 1 - w:P + 1 - w + S, sl]
        count = jnp.minimum(float(w), pos + 1.0)
        means.append(win_sum / count[None, :, None])
    pooled = (jnp.concatenate(means, axis=-1).astype(a.dtype) - a).reshape(B, S, len(POOL_WINDOWS), POOL_CH)
    y = jnp.einsum('bsgc,gcd->bsgd', pooled, w_lin).reshape(B, S, C) * scale
    return y, ext[:, -P:]


def _sgu_mixer(uv, norm_g, w_s, b_s):
    B, S, _ = uv.shape
    uv = jax.nn.gelu(uv, approximate=False)
    u, v = uv[..., :GROUP_WIDTH], uv[..., GROUP_WIDTH:]
    v = _layernorm(v, norm_g)
    n_chunks = -(-S // SGU_CHUNK)
    vp = jnp.pad(v, ((0, 0), (0, n_chunks * SGU_CHUNK - S), (0, 0)))
    vp = vp.reshape(B, n_chunks, SGU_CHUNK, SGU_HEADS, SGU_HEAD_DIM)
    mask = jnp.tril(jnp.ones((SGU_CHUNK, SGU_CHUNK), w_s.dtype))
    mixed = jnp.einsum('gts,bnsgc->bntgc', w_s * mask, vp) + b_s.T[None, None, :, :, None]
    mixed = mixed.reshape(B, n_chunks * SGU_CHUNK, GROUP_WIDTH)[:, :S]
    return u * mixed, v


def _conformer_conv(ab, buf, w, b, ln_g, ln_b):
    g = ab[..., :GROUP_WIDTH] * jax.nn.sigmoid(ab[..., GROUP_WIDTH:])
    ext = jnp.concatenate([buf.astype(g.dtype), g], axis=1)
    y = _causal_dwconv(ext, w) + b
    y = jax.nn.silu(_layernorm(y, ln_g, ln_b))
    return y, ext[:, -(CONF_KERNEL - 1):]


def _short_conv(bch, buf, w):
    G = GROUP_WIDTH
    gate_b, gate_c, h = bch[..., :G], bch[..., G:2 * G], bch[..., 2 * G:]
    z = gate_c * h
    ext = jnp.concatenate([buf.astype(z.dtype), z], axis=1)
    y = gate_b * _causal_dwconv(ext, w)
    return y, ext[:, -(SC_KERNEL - 1):]


def _moe(h, router_w, router_b, w_up, b_up, w_down, b_down):
    Bt, S, D = h.shape
    T = Bt * S
    hf = h.reshape(T, D)
    logits = (hf @ router_w).astype(jnp.float32) + router_b.astype(jnp.float32)
    top_v, top_i = lax.top_k(logits, TOP_K)
    gates = jax.nn.softmax(top_v, axis=-1)
    n_assign = T * TOP_K
    flat_e = top_i.reshape(-1).astype(jnp.int32)
    flat_t = jnp.repeat(jnp.arange(T, dtype=jnp.int32), TOP_K)
    flat_g = gates.reshape(-1)
    order = jnp.argsort(flat_e)
    se = flat_e[order]
    counts = jnp.bincount(flat_e, length=N_EXPERTS)
    padded = (counts + MOE_BLOCK - 1) // MOE_BLOCK * MOE_BLOCK
    pad_end = jnp.cumsum(padded)
    pad_start = pad_end - padded
    start = jnp.cumsum(counts) - counts
    dest = pad_start[se] + jnp.arange(n_assign) - start[se]
    n_blocks = -(-(n_assign + N_EXPERTS * (MOE_BLOCK - 1)) // MOE_BLOCK)
    n_rows = n_blocks * MOE_BLOCK
    row_tok = jnp.full((n_rows,), T, jnp.int32).at[dest].set(flat_t[order])
    row_gate = jnp.zeros((n_rows,), jnp.float32).at[dest].set(flat_g[order])
    block_e = jnp.minimum(jnp.searchsorted(pad_end, jnp.arange(n_blocks) * MOE_BLOCK, side='right'), N_EXPERTS - 1)
    h_pad = jnp.concatenate([hf, jnp.zeros((1, D), hf.dtype)], axis=0)
    xb = h_pad[row_tok].reshape(n_blocks, MOE_BLOCK, D)

    def expert_block(args):
        xe, e = args
        up = xe @ w_up[e] + b_up[e]
        glu = jnp.minimum(up[..., :D_EXPERT], SWIGLU_LIMIT)
        lin = jnp.clip(up[..., D_EXPERT:], -SWIGLU_LIMIT, SWIGLU_LIMIT)
        act = glu * jax.nn.sigmoid(SWIGLU_ALPHA * glu) * (lin + 1.0)
        return act @ w_down[e] + b_down[e]

    yb = lax.map(expert_block, (xb, block_e)).reshape(n_rows, D)
    out = jnp.zeros((T + 1, D), jnp.float32).at[row_tok].add(yb.astype(jnp.float32) * row_gate[:, None])[:T]
    return out.astype(h.dtype).reshape(Bt, S, D)


def _trunk(x, c, pool_bufs, conf_bufs, sc_bufs, start_pos, params):
    (w_ada, b_ada, norm_mix, norm_ffn, norm_final, w_in, pool_w, pool_scale, sgu_norm, sgu_w, sgu_b,
     conf_w, conf_b, conf_ln_g, conf_ln_b, sc_w, out_norm, w_out, router_w, router_b,
     w_up, b_up, w_down, b_down) = params
    G = GROUP_WIDTH
    B, S, _ = x.shape
    new_pool, new_v, new_conf, new_sc = [], [], [], []
    for l in range(DEPTH):
        mod = jax.nn.silu(c) @ w_ada[l] + b_ada[l]
        sh1, sc1, g1, sh2, sc2, g2 = [m[:, None, :] for m in jnp.split(mod, 6, axis=-1)]
        h = _rmsnorm(x, norm_mix[l]) * (1.0 + sc1) + sh1
        p = h @ w_in[l]
        ya, pb = _pool_mixer(p[..., :G], pool_bufs[l], start_pos, pool_w[l], pool_scale[l])
        yb, v = _sgu_mixer(p[..., G:3 * G], sgu_norm[l], sgu_w[l], sgu_b[l])
        yc, cb = _conformer_conv(p[..., 3 * G:5 * G], conf_bufs[l], conf_w[l], conf_b[l], conf_ln_g[l], conf_ln_b[l])
        yd, sb = _short_conv(p[..., 5 * G:], sc_bufs[l], sc_w[l])
        ycat = jnp.stack([ya, yb, yc, yd], axis=2)
        ycat = _rmsnorm(ycat, out_norm[l].reshape(N_MIXERS, G)).reshape(B, S, N_MIXERS * G)
        x = x + g1 * (ycat @ w_out[l])
        h2 = _rmsnorm(x, norm_ffn[l]) * (1.0 + sc2) + sh2
        x = x + g2 * _moe(h2, router_w[l], router_b[l], w_up[l], b_up[l], w_down[l], b_down[l])
        new_pool.append(pb)
        new_v.append(v)
        new_conf.append(cb)
        new_sc.append(sb)
    y = _rmsnorm(x, norm_final)
    return y, jnp.stack(new_pool), jnp.stack(new_v), jnp.stack(new_conf), jnp.stack(new_sc)


def setup_inputs(seed: int = 0) -> dict:
    key = jax.random.key(seed)
    ks = jax.random.split(key, 32)

    def nrm(i, shape, s):
        return jax.random.normal(ks[i], shape, jnp.float32) * s

    G = GROUP_WIDTH
    Dm = D_MODEL
    return {
        'x_prompt': nrm(0, (BATCH, SEQ, Dm), 1.0),
        'x_sample': nrm(1, (DEC_BATCH, DEC_SEQ, Dm), 1.0),
        'c_prompt': nrm(2, (BATCH, Dm), 1.0),
        'c_sample': nrm(3, (DEC_BATCH, Dm), 1.0),
        'state_pool': nrm(4, (DEPTH, DEC_BATCH, POOL_MAX - 1, G), 1.0),
        'state_conformer': nrm(5, (DEPTH, DEC_BATCH, CONF_KERNEL - 1, G), 1.0),
        'state_shortconv': nrm(6, (DEPTH, DEC_BATCH, SC_KERNEL - 1, G), 1.0),
        'w_ada': nrm(7, (DEPTH, Dm, 6 * Dm), 0.5 * Dm ** -0.5),
        'b_ada': nrm(8, (DEPTH, 6 * Dm), 0.02),
        'norm_mix': 1.0 + nrm(9, (DEPTH, Dm), 0.02),
        'norm_ffn': 1.0 + nrm(10, (DEPTH, Dm), 0.02),
        'norm_final': 1.0 + nrm(11, (Dm,), 0.02),
        'w_in': nrm(12, (DEPTH, Dm, IN_COLS), Dm ** -0.5),
        'pool_w': nrm(13, (DEPTH, len(POOL_WINDOWS), POOL_CH, POOL_CH), POOL_CH ** -0.5),
        'pool_scale': 1.0 + nrm(14, (DEPTH, G), 0.1),
        'sgu_norm': 1.0 + nrm(15, (DEPTH, G), 0.02),
        'sgu_w': nrm(16, (DEPTH, SGU_HEADS, SGU_CHUNK, SGU_CHUNK), SGU_CHUNK ** -0.5),
        'sgu_b': 1.0 + nrm(17, (DEPTH, SGU_HEADS, SGU_CHUNK), 0.1),
        'conf_w': nrm(18, (DEPTH, CONF_KERNEL, G), CONF_KERNEL ** -0.5),
        'conf_b': nrm(19, (DEPTH, G), 0.02),
        'conf_ln_g': 1.0 + nrm(20, (DEPTH, G), 0.02),
        'conf_ln_b': nrm(21, (DEPTH, G), 0.02),
        'sc_w': nrm(22, (DEPTH, SC_KERNEL, G), SC_KERNEL ** -0.5),
        'out_norm': 1.0 + nrm(23, (DEPTH, N_MIXERS * G), 0.02),
        'w_out': nrm(24, (DEPTH, N_MIXERS * G, Dm), (N_MIXERS * G) ** -0.5),
        'router_w': nrm(25, (DEPTH, Dm, N_EXPERTS), Dm ** -0.5),
        'router_b': nrm(26, (DEPTH, N_EXPERTS), 0.01),
        'w_up': nrm(27, (DEPTH, N_EXPERTS, Dm, 2 * D_EXPERT), Dm ** -0.5),
        'b_up': nrm(28, (DEPTH, N_EXPERTS, 2 * D_EXPERT), 0.01),
        'w_down': nrm(29, (DEPTH, N_EXPERTS, D_EXPERT, Dm), D_EXPERT ** -0.5),
        'b_down': nrm(30, (DEPTH, N_EXPERTS, Dm), 0.01),
    }


def reference(x_prompt, x_sample, c_prompt, c_sample, state_pool, state_conformer, state_shortconv,
              w_ada, b_ada, norm_mix, norm_ffn, norm_final, w_in, pool_w, pool_scale, sgu_norm, sgu_w, sgu_b,
              conf_w, conf_b, conf_ln_g, conf_ln_b, sc_w, out_norm, w_out, router_w, router_b,
              w_up, b_up, w_down, b_down):
    params = (w_ada, b_ada, norm_mix, norm_ffn, norm_final, w_in, pool_w, pool_scale, sgu_norm, sgu_w, sgu_b,
              conf_w, conf_b, conf_ln_g, conf_ln_b, sc_w, out_norm, w_out, router_w, router_b,
              w_up, b_up, w_down, b_down)
    zero_pool = jnp.zeros((DEPTH, BATCH, POOL_MAX - 1, GROUP_WIDTH), x_prompt.dtype)
    zero_conf = jnp.zeros((DEPTH, BATCH, CONF_KERNEL - 1, GROUP_WIDTH), x_prompt.dtype)
    zero_sc = jnp.zeros((DEPTH, BATCH, SC_KERNEL - 1, GROUP_WIDTH), x_prompt.dtype)
    y_prompt, pool_p, _v_prompt, conf_p, sc_p = _trunk(x_prompt, c_prompt, zero_pool, zero_conf, zero_sc, 0, params)
    y_sample, pool_s, v_s, conf_s, sc_s = _trunk(x_sample, c_sample, state_pool, state_conformer, state_shortconv,
                                                 PAST_LEN, params)
    return (y_prompt, y_sample, pool_p, pool_s, conf_p, conf_s, sc_p, sc_s, v_s)
```

```python
import functools

import jax
import jax.numpy as jnp
from jax import lax
from jax.experimental import pallas as pl
from jax.experimental.pallas import tpu as pltpu

D_MODEL = 2048
BATCH = 4
SEQ = 2048
DEPTH = 2
DEC_BATCH = 128
DEC_SEQ = 4
PAST_LEN = 16384
GROUP_WIDTH = 512
POOL_WINDOWS = (2, 4, 8, 16)
POOL_MAX = 16
POOL_CH = 128
SGU_HEADS = 4
SGU_CHUNK = 128
CONF_KERNEL = 31
SC_KERNEL = 3
IN_COLS = 8 * GROUP_WIDTH
N_EXPERTS = 32
TOP_K = 4
D_EXPERT = 2048
SWIGLU_ALPHA = 1.702
SWIGLU_LIMIT = 7.0
EPS = 1e-5

T_PROMPT = BATCH * SEQ
T_SAMPLE = DEC_BATCH * DEC_SEQ
T_ALL = T_PROMPT + T_SAMPLE

LANES = 128
VMEM_LIMIT = 56 * 1024 * 1024

A_TM = 1024
A_TN = 512
A_STEPS = IN_COLS // A_TN
POOL_HALO = 16
CONF_HALO = 32
SC_HALO = 8

B_TM = 512
C_TM = 256

MOE_TM = 256
MOE_MT = 8
MOE_TN = 512
MOE_UP_STEPS = D_EXPERT // MOE_TN
MOE_DOWN_STEPS = D_MODEL // MOE_TN
MOE_STEPS = MOE_UP_STEPS + MOE_DOWN_STEPS
MOE_YSLOTS = 4
MOE_TILES = (T_ALL * TOP_K + N_EXPERTS * (MOE_TM - 1)) // MOE_TM
MOE_ROWS = MOE_TILES * MOE_TM
MOE_ITEMS = (MOE_TILES + N_EXPERTS * (MOE_MT - 1)) // MOE_MT

_f32 = jnp.float32
_bf16 = jnp.bfloat16


def _rms(x):
    return x * lax.rsqrt(jnp.mean(x * x, axis=-1, keepdims=True) + EPS)


def _ln(x):
    xc = x - jnp.mean(x, axis=-1, keepdims=True)
    return xc * lax.rsqrt(jnp.mean(xc * xc, axis=-1, keepdims=True) + EPS)


def _gelu(x):
    return 0.5 * x * (1.0 + lax.erf(x * (0.5 ** 0.5)))


def _slab_mul(x, m):
    r, c = x.shape
    return (x.reshape(r // 128, 128, c) * m[None]).reshape(r, c)


def _slab_add(x, m):
    r, c = x.shape
    return (x.reshape(r // 128, 128, c) + m[None]).reshape(r, c)


ADA_TN = 1024
ADA_ROWS = DEC_BATCH + 8


def _ada_body(c_ref, w_ref, b_ref, os_ref, op_ref):
    c = c_ref[...]
    a = (c * jax.nn.sigmoid(c)).astype(_bf16)
    r = jnp.dot(a, w_ref[...].astype(_bf16), preferred_element_type=_f32) + b_ref[...]
    os_ref[...] = r[:DEC_BATCH]
    op_ref[...] = r[DEC_BATCH:]


def _ada(c_all, w_ada, b_ada):
    nj = 6 * D_MODEL // ADA_TN
    return pl.pallas_call(
        _ada_body,
        grid=(DEPTH, nj),
        in_specs=[
            pl.BlockSpec((ADA_ROWS, D_MODEL), lambda l, j: (0, 0)),
            pl.BlockSpec((None, D_MODEL, ADA_TN), lambda l, j: (l, 0, j)),
            pl.BlockSpec((None, 1, ADA_TN), lambda l, j: (l, 0, j)),
        ],
        out_specs=[
            pl.BlockSpec((None, DEC_BATCH, ADA_TN), lambda l, j: (l, 0, j)),
            pl.BlockSpec((None, 8, ADA_TN), lambda l, j: (l, 0, j)),
        ],
        out_shape=[
            jax.ShapeDtypeStruct((DEPTH, DEC_BATCH, 6 * D_MODEL), _f32),
            jax.ShapeDtypeStruct((DEPTH, 8, 6 * D_MODEL), _f32),
        ],
        compiler_params=pltpu.CompilerParams(
            dimension_semantics=("arbitrary", "arbitrary"), vmem_limit_bytes=VMEM_LIMIT),
        name="ada_mod",
    )(c_all, w_ada, b_ada.reshape(DEPTH, 1, 6 * D_MODEL))


def _a_prompt_body(x_ref, sh1_ref, sc1_ref, nmix_ref, w_ref, poolw_ref, pscale_ref, sgun_ref, sguw_ref,
                   sgub_ref, cw_ref, cb_ref, clg_ref, clb_ref, scw_ref, onorm_ref,
                   y_ref, pool_o, conf_o, sc_o,
                   h_ref, bufa, bufb, exta, extc, extz):
    i = pl.program_id(1)
    n = pl.program_id(2)
    tm = A_TM
    gw = GROUP_WIDTH

    @pl.when(n == 0)
    def _():
        hn = _rms(x_ref[...]) * nmix_ref[...]
        h_ref[...] = (hn * (1.0 + sc1_ref[...]) + sh1_ref[...]).astype(_bf16)

    def proj():
        return jnp.dot(h_ref[...], w_ref[...], preferred_element_type=_f32)

    def emit(g, y):
        yn = _rms(y) * onorm_ref[:, g * gw:(g + 1) * gw]
        y_ref[:, g * gw:(g + 1) * gw] = yn.astype(_bf16)

    def carry_halo(ext, halo):
        @pl.when(i == 0)
        def _():
            ext[0:halo, :] = jnp.zeros((halo, gw), _f32)

        @pl.when(i > 0)
        def _():
            ext[0:halo, :] = ext[tm:tm + halo, :]

    @pl.when(n == 0)
    def _():
        a = proj()
        carry_halo(exta, POOL_HALO)
        exta[POOL_HALO:, :] = a
        pool_o[...] = a[tm - POOL_HALO:, :]
        pos = (i * tm + lax.broadcasted_iota(jnp.int32, (tm, 1), 0)).astype(_f32)
        for gi, w in enumerate(POOL_WINDOWS):
            c0 = gi * POOL_CH
            s = exta[POOL_HALO:POOL_HALO + tm, c0:c0 + POOL_CH]
            for j in range(1, w):
                s = s + exta[POOL_HALO - j:POOL_HALO - j + tm, c0:c0 + POOL_CH]
            cnt = jnp.minimum(float(w), pos + 1.0)
            pooled = s / cnt - a[:, c0:c0 + POOL_CH]
            yg = jnp.dot(pooled.astype(_bf16), poolw_ref[gi].astype(_bf16), preferred_element_type=_f32)
            bufb[:, c0:c0 + POOL_CH] = yg
        emit(0, bufb[...] * pscale_ref[...])

    @pl.when(n == 1)
    def _():
        bufa[...] = _gelu(proj())

    @pl.when(n == 2)
    def _():
        v = _ln(_gelu(proj())) * sgun_ref[...]
        vb = v.astype(_bf16)
        row = lax.broadcasted_iota(jnp.int32, (SGU_CHUNK, SGU_CHUNK), 0)
        col = lax.broadcasted_iota(jnp.int32, (SGU_CHUNK, SGU_CHUNK), 1)
        for g in range(SGU_HEADS):
            c0 = g * SGU_CHUNK
            wg = jnp.where(row >= col, sguw_ref[g], 0.0).astype(_bf16)
            for c in range(tm // SGU_CHUNK):
                r0 = c * SGU_CHUNK
                m = jnp.dot(wg, vb[r0:r0 + SGU_CHUNK, c0:c0 + SGU_CHUNK], preferred_element_type=_f32)
                bufb[r0:r0 + SGU_CHUNK, c0:c0 + SGU_CHUNK] = m + sgub_ref[:, c0:c0 + SGU_CHUNK]
        emit(1, bufa[...] * bufb[...])

    @pl.when(n == 3)
    def _():
        bufa[...] = proj()

    @pl.when(n == 4)
    def _():
        g = bufa[...] * jax.nn.sigmoid(proj())
        carry_halo(extc, CONF_HALO)
        extc[CONF_HALO:, :] = g
        conf_o[...] = g[tm - CONF_HALO:, :]
        off = CONF_HALO - (CONF_KERNEL - 1)
        acc = jnp.broadcast_to(cb_ref[...], (tm, gw))
        for k in range(CONF_KERNEL):
            acc = acc + cw_ref[k:k + 1, :] * extc[off + k:off + k + tm, :]
        yn = _ln(acc) * clg_ref[...] + clb_ref[...]
        emit(2, yn * jax.nn.sigmoid(yn))

    @pl.when(n == 5)
    def _():
        bufa[...] = proj()

    @pl.when(n == 6)
    def _():
        bufb[...] = proj()

    @pl.when(n == 7)
    def _():
        z = bufb[...] * proj()
        carry_halo(extz, SC_HALO)
        extz[SC_HALO:, :] = z
        sc_o[...] = z[tm - SC_HALO:, :]
        off = SC_HALO - (SC_KERNEL - 1)
        acc = scw_ref[0:1, :] * extz[off:off + tm, :]
        for k in range(1, SC_KERNEL):
            acc = acc + scw_ref[k:k + 1, :] * extz[off + k:off + k + tm, :]
        emit(3, bufa[...] * acc)


def _layer_spec(shape, l, nd):
    zeros = (0,) * len(shape)
    if nd == 1:
        return pl.BlockSpec((None,) + shape, lambda a: (l,) + zeros)
    if nd == 2:
        return pl.BlockSpec((None,) + shape, lambda a, b: (l,) + zeros)
    return pl.BlockSpec((None,) + shape, lambda a, b, c: (l,) + zeros)


def _mixer_params(l, p, nd):
    gw = GROUP_WIDTH
    arrs = [p["cw"], p["cb"], p["clg"], p["clb"], p["scw"], p["onorm"]]
    shapes = [(CONF_KERNEL, gw), (1, gw), (1, gw), (1, gw), (SC_KERNEL, gw), (1, D_MODEL)]
    return arrs, [_layer_spec(s, l, nd) for s in shapes]


def _a_prompt(l, x_all, mod_p4, w_in_b, p):
    gw = GROUP_WIDTH
    ni = SEQ // A_TM
    marrs, mspecs = _mixer_params(l, p, 3)
    in_specs = [
        pl.BlockSpec((A_TM, D_MODEL), lambda b, i, n: (b * ni + i, 0)),
        pl.BlockSpec((None, None, 1, D_MODEL), lambda b, i, n: (l, b, 0, 0)),
        pl.BlockSpec((None, None, 1, D_MODEL), lambda b, i, n: (l, b, 0, 1)),
        _layer_spec((1, D_MODEL), l, 3),
        pl.BlockSpec((None, D_MODEL, A_TN), lambda b, i, n: (l, 0, n)),
        _layer_spec((len(POOL_WINDOWS), POOL_CH, POOL_CH), l, 3),
        _layer_spec((1, gw), l, 3),
        _layer_spec((1, gw), l, 3),
        _layer_spec((SGU_HEADS, SGU_CHUNK, SGU_CHUNK), l, 3),
        _layer_spec((SGU_CHUNK, gw), l, 3),
    ] + mspecs
    out_specs = [
        pl.BlockSpec((A_TM, D_MODEL), lambda b, i, n: (b * ni + i, 0)),
        pl.BlockSpec((None, POOL_HALO, gw), lambda b, i, n: (b, 0, 0)),
        pl.BlockSpec((None, CONF_HALO, gw), lambda b, i, n: (b, 0, 0)),
        pl.BlockSpec((None, SC_HALO, gw), lambda b, i, n: (b, 0, 0)),
    ]
    out_shape = [
        jax.ShapeDtypeStruct((T_PROMPT, D_MODEL), _bf16),
        jax.ShapeDtypeStruct((BATCH, POOL_HALO, gw), _f32),
        jax.ShapeDtypeStruct((BATCH, CONF_HALO, gw), _f32),
        jax.ShapeDtypeStruct((BATCH, SC_HALO, gw), _f32),
    ]
    scratch = [
        pltpu.VMEM((A_TM, D_MODEL), _bf16),
        pltpu.VMEM((A_TM, gw), _f32),
        pltpu.VMEM((A_TM, gw), _f32),
        pltpu.VMEM((POOL_HALO + A_TM, gw), _f32),
        pltpu.VMEM((CONF_HALO + A_TM, gw), _f32),
        pltpu.VMEM((SC_HALO + A_TM, gw), _f32),
    ]
    return pl.pallas_call(
        _a_prompt_body,
        grid=(BATCH, ni, A_STEPS),
        in_specs=in_specs,
        out_specs=out_specs,
        out_shape=out_shape,
        scratch_shapes=scratch,
        compiler_params=pltpu.CompilerParams(
            dimension_semantics=("arbitrary", "arbitrary", "arbitrary"), vmem_limit_bytes=VMEM_LIMIT),
        name="mixers_prompt",
    )(x_all, mod_p4, mod_p4, p["nmix"], w_in_b, p["poolw"], p["pscale"], p["sgun"], p["sguw"], p["sgub"], *marrs)


def _a_sample_body(x_ref, sh1_ref, sc1_ref, nmix_ref, w_ref, poolw_ref, pscale_ref, sgun_ref, sgwc_ref,
                   sgbc_ref, cw_ref, cb_ref, clg_ref, clb_ref, scw_ref, onorm_ref, pst_ref, cst_ref, sst_ref,
                   y_ref, a_o, g_o, z_o, v_o,
                   h_ref, bufa, bufb):
    n = pl.program_id(0)
    gw = GROUP_WIDTH
    nb = DEC_BATCH
    ns = DEC_SEQ

    @pl.when(n == 0)
    def _():
        hn = _rms(x_ref[...]) * nmix_ref[...]
        h_ref[...] = _slab_add(_slab_mul(hn, 1.0 + sc1_ref[...]), sh1_ref[...]).astype(_bf16)

    def proj():
        return jnp.dot(h_ref[...], w_ref[...], preferred_element_type=_f32)

    def emit(g, y):
        yn = _rms(y) * onorm_ref[:, g * gw:(g + 1) * gw]
        y_ref[:, g * gw:(g + 1) * gw] = yn.astype(_bf16)

    def slab(x, s):
        return x[s * nb:(s + 1) * nb, :]

    @pl.when(n == 0)
    def _():
        a = proj()
        a_o[...] = a.reshape(ns, nb, gw)
        npast = POOL_MAX - 1
        for s in range(ns):
            for gi, w in enumerate(POOL_WINDOWS):
                c0 = gi * POOL_CH
                acc = None
                for j in range(w):
                    e = npast + s - j
                    term = pst_ref[e, :, c0:c0 + POOL_CH] if e < npast else slab(a, e - npast)[:, c0:c0 + POOL_CH]
                    acc = term if acc is None else acc + term
                cnt = min(float(w), PAST_LEN + s + 1.0)
                pooled = acc / cnt - slab(a, s)[:, c0:c0 + POOL_CH]
                bufb[s * nb:(s + 1) * nb, c0:c0 + POOL_CH] = pooled
        pooled = bufb[...].astype(_bf16)
        for gi in range(len(POOL_WINDOWS)):
            c0 = gi * POOL_CH
            bufa[:, c0:c0 + POOL_CH] = jnp.dot(pooled[:, c0:c0 + POOL_CH], poolw_ref[gi].astype(_bf16),
                                               preferred_element_type=_f32)
        emit(0, bufa[...] * pscale_ref[...])

    @pl.when(n == 1)
    def _():
        bufa[...] = _gelu(proj())

    @pl.when(n == 2)
    def _():
        v = _ln(_gelu(proj())) * sgun_ref[...]
        v_o[...] = v.reshape(ns, nb, gw)
        for t in range(ns):
            m = jnp.broadcast_to(sgbc_ref[t:t + 1, :], (nb, gw))
            for s in range(t + 1):
                m = m + sgwc_ref[t * ns + s:t * ns + s + 1, :] * slab(v, s)
            bufb[t * nb:(t + 1) * nb, :] = m
        emit(1, bufa[...] * bufb[...])

    @pl.when(n == 3)
    def _():
        bufa[...] = proj()

    @pl.when(n == 4)
    def _():
        g = bufa[...] * jax.nn.sigmoid(proj())
        g_o[...] = g.reshape(ns, nb, gw)
        npast = CONF_KERNEL - 1
        for s in range(ns):
            acc = jnp.broadcast_to(cb_ref[...], (nb, gw))
            for k in range(CONF_KERNEL):
                e = s + k
                term = cst_ref[e] if e < npast else slab(g, e - npast)
                acc = acc + cw_ref[k:k + 1, :] * term
            bufb[s * nb:(s + 1) * nb, :] = acc
        yn = _ln(bufb[...]) * clg_ref[...] + clb_ref[...]
        emit(2, yn * jax.nn.sigmoid(yn))

    @pl.when(n == 5)
    def _():
        bufa[...] = proj()

    @pl.when(n == 6)
    def _():
        bufb[...] = proj()

    @pl.when(n == 7)
    def _():
        z = bufb[...] * proj()
        z_o[...] = z.reshape(ns, nb, gw)
        npast = SC_KERNEL - 1
        for s in range(ns):
            acc = None
            for k in range(SC_KERNEL):
                e = s + k
                term = scw_ref[k:k + 1, :] * (sst_ref[e] if e < npast else slab(z, e - npast))
                acc = term if acc is None else acc + term
            bufb[s * nb:(s + 1) * nb, :] = acc
        emit(3, bufa[...] * bufb[...])


def _a_sample(l, x_all, mod_s, w_in_b, p, pool_t, conf_t, sc_t):
    gw = GROUP_WIDTH
    ns, nb = DEC_SEQ, DEC_BATCH
    marrs, mspecs = _mixer_params(l, p, 1)
    in_specs = [
        pl.BlockSpec((T_SAMPLE, D_MODEL), lambda n: (T_PROMPT // T_SAMPLE, 0)),
        pl.BlockSpec((None, nb, D_MODEL), lambda n: (l, 0, 0)),
        pl.BlockSpec((None, nb, D_MODEL), lambda n: (l, 0, 1)),
        _layer_spec((1, D_MODEL), l, 1),
        pl.BlockSpec((None, D_MODEL, A_TN), lambda n: (l, 0, n)),
        _layer_spec((len(POOL_WINDOWS), POOL_CH, POOL_CH), l, 1),
        _layer_spec((1, gw), l, 1),
        _layer_spec((1, gw), l, 1),
        _layer_spec((ns * ns, gw), l, 1),
        _layer_spec((ns, gw), l, 1),
    ] + mspecs + [
        _layer_spec((POOL_MAX - 1, nb, gw), l, 1),
        _layer_spec((CONF_KERNEL - 1, nb, gw), l, 1),
        _layer_spec((SC_KERNEL - 1, nb, gw), l, 1),
    ]
    out_specs = [pl.BlockSpec((T_SAMPLE, D_MODEL), lambda n: (0, 0))] + [
        pl.BlockSpec((ns, nb, gw), lambda n: (0, 0, 0)) for _ in range(4)]
    out_shape = [jax.ShapeDtypeStruct((T_SAMPLE, D_MODEL), _bf16)] + [
        jax.ShapeDtypeStruct((ns, nb, gw), _f32) for _ in range(4)]
    scratch = [
        pltpu.VMEM((T_SAMPLE, D_MODEL), _bf16),
        pltpu.VMEM((T_SAMPLE, gw), _f32),
        pltpu.VMEM((T_SAMPLE, gw), _f32),
    ]
    return pl.pallas_call(
        _a_sample_body,
        grid=(A_STEPS,),
        in_specs=in_specs,
        out_specs=out_specs,
        out_shape=out_shape,
        scratch_shapes=scratch,
        compiler_params=pltpu.CompilerParams(
            dimension_semantics=("arbitrary",), vmem_limit_bytes=VMEM_LIMIT),
        name="mixers_sample",
    )(x_all, mod_s, mod_s, p["nmix"], w_in_b, p["poolw"], p["pscale"], p["sgun"], p["sgwc"], p["sgbc"],
      *marrs, pool_t, conf_t, sc_t)


B_PROMPT_TILES = T_PROMPT // B_TM


def _pick_mod(is_prompt, mp_ref, ms_ref):
    return jnp.where(is_prompt, jnp.broadcast_to(mp_ref[...], (DEC_BATCH, D_MODEL)), ms_ref[...])


def _b_body(x_ref, yp_ref, ys_ref, w_ref, mpg1, mpsh2, mpsc2, msg1, mssh2, mssc2, nffn_ref, rw_ref, rb_ref,
            x1_ref, h2_ref, gate_ref, idx_ref):
    i = pl.program_id(0)
    isp = i < B_PROMPT_TILES
    ycat = jnp.where(isp, yp_ref[...], ys_ref[...])
    o = jnp.dot(ycat, w_ref[...], preferred_element_type=_f32)
    x1 = x_ref[...] + _slab_mul(o, _pick_mod(isp, mpg1, msg1))
    x1_ref[...] = x1
    hn = _rms(x1) * nffn_ref[...]
    h2 = _slab_add(_slab_mul(hn, 1.0 + _pick_mod(isp, mpsc2, mssc2)), _pick_mod(isp, mpsh2, mssh2))
    hi = h2.astype(_bf16)
    h2_ref[...] = hi
    lo = (h2 - hi.astype(_f32)).astype(_bf16)
    l1 = jnp.dot(hi, rw_ref[...], preferred_element_type=_f32)
    l2 = jnp.dot(lo, rw_ref[:, 0:LANES], preferred_element_type=_f32)
    logits = l1[:, 0:LANES] + l1[:, LANES:] + l2 + rb_ref[...]
    lane = lax.broadcasted_iota(jnp.int32, (B_TM, LANES), 1).astype(_f32)
    neg = jnp.float32(-jnp.inf)
    cur = jnp.where(lane < N_EXPERTS, logits, neg)
    vals, ids = [], []
    for _ in range(TOP_K):
        m = jnp.max(cur, axis=-1, keepdims=True)
        ix = jnp.min(jnp.where(cur == m, lane, float(LANES)), axis=-1, keepdims=True)
        vals.append(m)
        ids.append(ix)
        cur = jnp.where(lane == ix, neg, cur)
    es = [jnp.exp(v - vals[0]) for v in vals]
    tot = es[0] + es[1] + es[2] + es[3]
    gate = jnp.zeros((B_TM, LANES), _f32)
    idx = jnp.zeros((B_TM, LANES), _f32)
    for k in range(TOP_K):
        gate = jnp.where(lane == float(k), es[k] / tot, gate)
        idx = jnp.where(lane == float(k), ids[k], idx)
    gate_ref[...] = gate
    idx_ref[...] = idx.astype(jnp.int32)


def _stage_b(l, x_all, ycat_p, ycat_s, w_out_b, mod_p4, mod_s, p):
    def pb(i):
        return jnp.minimum(i // (SEQ // B_TM), BATCH - 1)

    def mp_spec(j):
        return pl.BlockSpec((None, None, 1, D_MODEL), lambda i: (l, pb(i), 0, j))

    def ms_spec(j):
        return pl.BlockSpec((None, DEC_BATCH, D_MODEL), lambda i: (l, 0, j))

    row_spec = pl.BlockSpec((B_TM, D_MODEL), lambda i: (i, 0))
    in_specs = [
        row_spec,
        pl.BlockSpec((B_TM, D_MODEL), lambda i: (jnp.minimum(i, B_PROMPT_TILES - 1), 0)),
        pl.BlockSpec((T_SAMPLE, D_MODEL), lambda i: (0, 0), pipeline_mode=pl.Buffered(1)),
        pl.BlockSpec((None, D_MODEL, D_MODEL), lambda i: (l, 0, 0), pipeline_mode=pl.Buffered(1)),
        mp_spec(2), mp_spec(3), mp_spec(4), ms_spec(2), ms_spec(3), ms_spec(4),
        _layer_spec((1, D_MODEL), l, 1),
        _layer_spec((D_MODEL, 2 * LANES), l, 1),
        _layer_spec((1, LANES), l, 1),
    ]
    lane_spec = pl.BlockSpec((B_TM, LANES), lambda i: (i, 0))
    return pl.pallas_call(
        _b_body,
        grid=(T_ALL // B_TM,),
        in_specs=in_specs,
        out_specs=[row_spec, row_spec, lane_spec, lane_spec],
        out_shape=[
            jax.ShapeDtypeStruct((T_ALL, D_MODEL), _f32),
            jax.ShapeDtypeStruct((T_ALL, D_MODEL), _bf16),
            jax.ShapeDtypeStruct((T_ALL, LANES), _f32),
            jax.ShapeDtypeStruct((T_ALL, LANES), jnp.int32),
        ],
        compiler_params=pltpu.CompilerParams(
            dimension_semantics=("arbitrary",), vmem_limit_bytes=VMEM_LIMIT),
        name="outproj_router",
    )(x_all, ycat_p, ycat_s, w_out_b, mod_p4, mod_p4, mod_p4, mod_s, mod_s, mod_s, p["nffn"], p["rw"], p["rb"])


def _moe_body(ie_ref, ist_ref, int_ref, tot_ref, wg_ref, wl_ref, wd_ref, bg_ref, bl_ref, bd_ref, x_hbm, y_hbm,
              xbuf, hbuf, ybuf, wc, flag, sem_x, sem_y):
    del ie_ref
    i = pl.program_id(0)
    n = pl.program_id(1)
    nt = int_ref[i]
    st = ist_ref[i]
    tm, tn = MOE_TM, MOE_TN

    def x_copy(t):
        return pltpu.make_async_copy(x_hbm.at[pl.ds((st + t) * tm, tm)], xbuf.at[pl.ds(t * tm, tm)], sem_x.at[t])

    def y_copy(slot, t, c):
        return pltpu.make_async_copy(ybuf.at[slot], y_hbm.at[pl.ds((st + t) * tm, tm), pl.ds(c * tn, tn)],
                                     sem_y.at[slot])

    @pl.when((i == 0) & (n == 0))
    def _():
        for s in range(MOE_YSLOTS):
            flag[s] = 0

    for c in range(MOE_UP_STEPS):
        @pl.when((n == c) & (nt > 0))
        def _(c=c):
            if c == 0:
                def issue(t, carry):
                    x_copy(t).start()
                    return carry
                lax.fori_loop(0, nt, issue, 0)
            wc[:, 0:tn] = wg_ref[...].astype(_bf16)
            wc[:, tn:] = wl_ref[...].astype(_bf16)

            def body(t, carry):
                if c == 0:
                    x_copy(t).wait()
                r0 = pl.multiple_of(t * tm, tm)
                up = jnp.dot(xbuf[pl.ds(r0, tm), :], wc[...], preferred_element_type=_f32)
                glu = jnp.minimum(up[:, 0:tn] + bg_ref[...], SWIGLU_LIMIT)
                lin = jnp.clip(up[:, tn:] + bl_ref[...], -SWIGLU_LIMIT, SWIGLU_LIMIT)
                act = glu * jax.nn.sigmoid(SWIGLU_ALPHA * glu) * (lin + 1.0)
                hbuf[pl.ds(r0, tm), c * tn:(c + 1) * tn] = act.astype(_bf16)
                return carry
            lax.fori_loop(0, nt, body, 0)

    for c in range(MOE_DOWN_STEPS):
        @pl.when((n == MOE_UP_STEPS + c) & (nt > 0))
        def _(c=c):
            wc[:, 0:tn] = wd_ref[...].astype(_bf16)

            def body(t, carry):
                slot = t % MOE_YSLOTS

                @pl.when(flag[slot] == 1)
                def _():
                    y_copy(slot, 0, 0).wait()
                r0 = pl.multiple_of(t * tm, tm)
                y = jnp.dot(hbuf[pl.ds(r0, tm), :], wc[:, 0:tn], preferred_element_type=_f32) + bd_ref[...]
                ybuf[slot] = y
                y_copy(slot, t, c).start()
                flag[slot] = 1
                return carry
            lax.fori_loop(0, nt, body, 0)

    @pl.when((i == MOE_ITEMS - 1) & (n == MOE_STEPS - 1))
    def _():
        for s in range(MOE_YSLOTS):
            @pl.when(flag[s] == 1)
            def _(s=s):
                y_copy(s, 0, 0).wait()
                flag[s] = 0

        ybuf[0] = jnp.zeros((tm, tn), _f32)

        def zero_copy(t, c):
            return pltpu.make_async_copy(ybuf.at[0], y_hbm.at[pl.ds(t * tm, tm), pl.ds(c * tn, tn)], sem_y.at[0])

        def zstart(t, carry):
            for c in range(MOE_DOWN_STEPS):
                zero_copy(t, c).start()
            return carry

        def zwait(t, carry):
            for c in range(MOE_DOWN_STEPS):
                zero_copy(t, c).wait()
            return carry
        lax.fori_loop(tot_ref[0], MOE_TILES, zstart, 0)
        lax.fori_loop(tot_ref[0], MOE_TILES, zwait, 0)


def _moe(l, item_e, item_st, item_nt, n_tiles, w_up, b_up4, w_down, b_down4, x_sorted):
    def col_up(n, valid):
        return jnp.where(valid, jnp.minimum(n, MOE_UP_STEPS - 1), MOE_UP_STEPS - 1)

    def col_down(n, valid):
        return jnp.where(valid, jnp.maximum(n - MOE_UP_STEPS, 0), MOE_DOWN_STEPS - 1)

    def glu_map(i, n, ie, ist, int_, tot):
        return (l, ie[i], 0, col_up(n, int_[i] > 0))

    def lin_map(i, n, ie, ist, int_, tot):
        return (l, ie[i], 0, MOE_UP_STEPS + col_up(n, int_[i] > 0))

    def down_map(i, n, ie, ist, int_, tot):
        return (l, ie[i], 0, col_down(n, int_[i] > 0))

    grid_spec = pltpu.PrefetchScalarGridSpec(
        num_scalar_prefetch=4,
        grid=(MOE_ITEMS, MOE_STEPS),
        in_specs=[
            pl.BlockSpec((None, None, D_MODEL, MOE_TN), glu_map),
            pl.BlockSpec((None, None, D_MODEL, MOE_TN), lin_map),
            pl.BlockSpec((None, None, D_EXPERT, MOE_TN), down_map),
            pl.BlockSpec((None, None, 1, MOE_TN), glu_map),
            pl.BlockSpec((None, None, 1, MOE_TN), lin_map),
            pl.BlockSpec((None, None, 1, MOE_TN), down_map),
            pl.BlockSpec(memory_space=pl.ANY),
        ],
        out_specs=pl.BlockSpec(memory_space=pl.ANY),
        scratch_shapes=[
            pltpu.VMEM((MOE_MT * MOE_TM, D_MODEL), _bf16),
            pltpu.VMEM((MOE_MT * MOE_TM, D_EXPERT), _bf16),
            pltpu.VMEM((MOE_YSLOTS, MOE_TM, MOE_TN), _f32),
            pltpu.VMEM((D_MODEL, 2 * MOE_TN), _bf16),
            pltpu.SMEM((MOE_YSLOTS,), jnp.int32),
            pltpu.SemaphoreType.DMA((MOE_MT,)),
            pltpu.SemaphoreType.DMA((MOE_YSLOTS,)),
        ],
    )
    return pl.pallas_call(
        _moe_body,
        grid_spec=grid_spec,
        out_shape=jax.ShapeDtypeStruct((MOE_ROWS, D_MODEL), _f32),
        compiler_params=pltpu.CompilerParams(
            dimension_semantics=("arbitrary", "arbitrary"), vmem_limit_bytes=VMEM_LIMIT),
        name="moe_experts",
    )(item_e, item_st, item_nt, n_tiles, w_up, w_up, w_down, b_up4, b_up4, b_down4, x_sorted)


def _route(idx):
    n = T_ALL * TOP_K
    flat_e = idx.reshape(-1)
    counts = jnp.zeros((N_EXPERTS,), jnp.int32).at[flat_e].add(1)
    nt = (counts + MOE_TM - 1) // MOE_TM
    tile_start = jnp.cumsum(nt) - nt
    order = jnp.argsort(flat_e, stable=True).astype(jnp.int32)
    se = flat_e[order]
    start = jnp.cumsum(counts) - counts
    dest_sorted = tile_start[se] * MOE_TM + jnp.arange(n, dtype=jnp.int32) - start[se]
    dest = jnp.zeros((n,), jnp.int32).at[order].set(dest_sorted)
    row_tok = jnp.zeros((MOE_ROWS,), jnp.int32).at[dest_sorted].set(order // TOP_K)
    ipe = (nt + MOE_MT - 1) // MOE_MT
    cum = jnp.cumsum(ipe)
    first = cum - ipe
    total = cum[-1]
    ii = jnp.arange(MOE_ITEMS, dtype=jnp.int32)
    e_of = jnp.minimum(jnp.searchsorted(cum, ii, side="right"), N_EXPERTS - 1).astype(jnp.int32)
    valid = ii < total
    e_last = jnp.minimum(jnp.searchsorted(cum, total - 1, side="right"), N_EXPERTS - 1).astype(jnp.int32)
    e_i = jnp.where(valid, e_of, e_last)
    j = ii - first[e_i]
    st_i = jnp.where(valid, tile_start[e_i] + j * MOE_MT, 0)
    nt_i = jnp.where(valid, jnp.minimum(MOE_MT, nt[e_i] - j * MOE_MT), 0)
    n_tiles = jnp.sum(nt).astype(jnp.int32).reshape(1)
    return dest, row_tok, e_i.astype(jnp.int32), st_i.astype(jnp.int32), nt_i.astype(jnp.int32), n_tiles


C_PROMPT_TILES = T_PROMPT // C_TM


def _c_body(x1_ref, yg_ref, gate_ref, mpg2, msg2, nfin_ref, out_ref, *, final):
    i = pl.program_id(0)
    isp = i < C_PROMPT_TILES
    g = gate_ref[...]
    acc = g[:, 0:1] * yg_ref[0]
    for k in range(1, TOP_K):
        acc = acc + g[:, k:k + 1] * yg_ref[k]
    x2 = x1_ref[...] + _slab_mul(acc, _pick_mod(isp, mpg2, msg2))
    if final:
        x2 = _rms(x2) * nfin_ref[...]
    out_ref[...] = x2


def _combine(l, x1, yg, gates, mod_p4, mod_s, nfin, final):
    def pb(i):
        return jnp.minimum(i // (SEQ // C_TM), BATCH - 1)

    row_spec = pl.BlockSpec((C_TM, D_MODEL), lambda i: (i, 0))
    return pl.pallas_call(
        functools.partial(_c_body, final=final),
        grid=(T_ALL // C_TM,),
        in_specs=[
            row_spec,
            pl.BlockSpec((TOP_K, C_TM, D_MODEL), lambda i: (0, i, 0)),
            pl.BlockSpec((C_TM, LANES), lambda i: (i, 0)),
            pl.BlockSpec((None, None, 1, D_MODEL), lambda i: (l, pb(i), 0, 5)),
            pl.BlockSpec((None, DEC_BATCH, D_MODEL), lambda i: (l, 0, 5)),
            pl.BlockSpec((1, D_MODEL), lambda i: (0, 0)),
        ],
        out_specs=row_spec,
        out_shape=jax.ShapeDtypeStruct((T_ALL, D_MODEL), _f32),
        compiler_params=pltpu.CompilerParams(
            dimension_semantics=("arbitrary",), vmem_limit_bytes=VMEM_LIMIT),
        name="moe_combine",
    )(x1, yg, gates, mod_p4, mod_s, nfin)


def kernel(x_prompt, x_sample, c_prompt, c_sample, state_pool, state_conformer, state_shortconv, w_ada, b_ada, norm_mix, norm_ffn, norm_final, w_in, pool_w, pool_scale, sgu_norm, sgu_w, sgu_b, conf_w, conf_b, conf_ln_g, conf_ln_b, sc_w, out_norm, w_out, router_w, router_b, w_up, b_up, w_down, b_down):
    gw = GROUP_WIDTH
    ns, nb = DEC_SEQ, DEC_BATCH

    c_all = jnp.concatenate([c_sample, c_prompt, jnp.zeros((ADA_ROWS - nb - BATCH, D_MODEL), _f32)], axis=0)
    mod_s, mod_p8 = _ada(c_all, w_ada, b_ada)
    mod_p4 = mod_p8[:, :BATCH].reshape(DEPTH, BATCH, 1, 6 * D_MODEL)

    rw_hi = router_w.astype(_bf16)
    rw_lo = (router_w - rw_hi.astype(_f32)).astype(_bf16)
    lane_pad = ((0, 0), (0, 0), (0, LANES - N_EXPERTS))
    params = {
        "nmix": norm_mix.reshape(DEPTH, 1, D_MODEL),
        "nffn": norm_ffn.reshape(DEPTH, 1, D_MODEL),
        "poolw": pool_w,
        "pscale": pool_scale.reshape(DEPTH, 1, gw),
        "sgun": sgu_norm.reshape(DEPTH, 1, gw),
        "sguw": sgu_w,
        "sgub": jnp.repeat(jnp.swapaxes(sgu_b, 1, 2), SGU_CHUNK, axis=2),
        "sgwc": jnp.repeat(jnp.transpose(sgu_w[:, :, :ns, :ns], (0, 2, 3, 1)).reshape(DEPTH, ns * ns, SGU_HEADS),
                           SGU_CHUNK, axis=2),
        "sgbc": jnp.repeat(jnp.swapaxes(sgu_b[:, :, :ns], 1, 2), SGU_CHUNK, axis=2),
        "cw": conf_w,
        "cb": conf_b.reshape(DEPTH, 1, gw),
        "clg": conf_ln_g.reshape(DEPTH, 1, gw),
        "clb": conf_ln_b.reshape(DEPTH, 1, gw),
        "scw": sc_w,
        "onorm": out_norm.reshape(DEPTH, 1, D_MODEL),
        "rw": jnp.concatenate([jnp.pad(rw_hi, lane_pad), jnp.pad(rw_lo, lane_pad)], axis=2),
        "rb": jnp.pad(router_b, ((0, 0), (0, LANES - N_EXPERTS))).reshape(DEPTH, 1, LANES),
    }
    w_in_b = w_in.astype(_bf16)
    w_out_b = w_out.astype(_bf16)
    b_up4 = b_up.reshape(DEPTH, N_EXPERTS, 1, 2 * D_EXPERT)
    b_down4 = b_down.reshape(DEPTH, N_EXPERTS, 1, D_MODEL)
    nfin = norm_final.reshape(1, D_MODEL)

    pool_t = jnp.transpose(state_pool, (0, 2, 1, 3))
    conf_t = jnp.transpose(state_conformer, (0, 2, 1, 3))
    sc_t = jnp.transpose(state_shortconv, (0, 2, 1, 3))

    x_all = jnp.concatenate([x_prompt.reshape(T_PROMPT, D_MODEL),
                             jnp.transpose(x_sample, (1, 0, 2)).reshape(T_SAMPLE, D_MODEL)], axis=0)

    pool_p, conf_p, sc_p, pool_s, conf_s, sc_s, v_s = [], [], [], [], [], [], []
    for l in range(DEPTH):
        ycat_p, po, co, so = _a_prompt(l, x_all, mod_p4, w_in_b, params)
        ycat_s, a_new, g_new, z_new, v_new = _a_sample(l, x_all, mod_s, w_in_b, params, pool_t, conf_t, sc_t)
        pool_p.append(po[:, POOL_HALO - (POOL_MAX - 1):])
        conf_p.append(co[:, CONF_HALO - (CONF_KERNEL - 1):])
        sc_p.append(so[:, SC_HALO - (SC_KERNEL - 1):])
        pool_s.append(jnp.concatenate([state_pool[l][:, ns:], jnp.transpose(a_new, (1, 0, 2))], axis=1))
        conf_s.append(jnp.concatenate([state_conformer[l][:, ns:], jnp.transpose(g_new, (1, 0, 2))], axis=1))
        sc_s.append(jnp.transpose(z_new, (1, 0, 2))[:, ns - (SC_KERNEL - 1):])
        v_s.append(jnp.transpose(v_new, (1, 0, 2)))

        x1, h2, gates, idx = _stage_b(l, x_all, ycat_p, ycat_s, w_out_b, mod_p4, mod_s, params)
        dest, row_tok, item_e, item_st, item_nt, n_tiles = _route(idx[:, :TOP_K])
        x_sorted = jnp.take(h2, row_tok, axis=0)
        y_rows = _moe(l, item_e, item_st, item_nt, n_tiles, w_up, b_up4, w_down, b_down4, x_sorted)
        dest_km = jnp.transpose(dest.reshape(T_ALL, TOP_K)).reshape(-1)
        yg = jnp.take(y_rows, dest_km, axis=0).reshape(TOP_K, T_ALL, D_MODEL)
        x_all = _combine(l, x1, yg, gates, mod_p4, mod_s, nfin, final=(l == DEPTH - 1))

    y_prompt = x_all[:T_PROMPT].reshape(BATCH, SEQ, D_MODEL)
    y_sample = jnp.transpose(x_all[T_PROMPT:].reshape(ns, nb, D_MODEL), (1, 0, 2))
    return (y_prompt, y_sample, jnp.stack(pool_p), jnp.stack(pool_s), jnp.stack(conf_p), jnp.stack(conf_s),
            jnp.stack(sc_p), jnp.stack(sc_s), jnp.stack(v_s))
```

```python
import functools

import jax
import jax.numpy as jnp
from jax import lax
from jax.experimental import pallas as pl
from jax.experimental.pallas import tpu as pltpu

D_MODEL = 2048
BATCH = 4
SEQ = 2048
DEPTH = 2
DEC_BATCH = 128
DEC_SEQ = 4
PAST_LEN = 16384
GROUP_WIDTH = 512
POOL_WINDOWS = (2, 4, 8, 16)
POOL_MAX = 16
POOL_CH = 128
SGU_HEADS = 4
SGU_CHUNK = 128
CONF_KERNEL = 31
SC_KERNEL = 3
IN_COLS = 8 * GROUP_WIDTH
N_EXPERTS = 32
TOP_K = 4
D_EXPERT = 2048
SWIGLU_ALPHA = 1.702
SWIGLU_LIMIT = 7.0
EPS = 1e-5

T_PROMPT = BATCH * SEQ
T_SAMPLE = DEC_BATCH * DEC_SEQ
T_ALL = T_PROMPT + T_SAMPLE

LANES = 128
BF16_ROWS = 16
VMEM_LIMIT = 56 * 1024 * 1024

A_TM = 1024
A_TN = 512
A_STEPS = IN_COLS // A_TN
POOL_HALO = 16
CONF_HALO = 32
SC_HALO = 8

B_TM = 512
N_BLOCKS = T_ALL // B_TM
B_PROMPT_BLOCKS = T_PROMPT // B_TM
SEG = BF16_ROWS
SORT_TM = 512
BLOCK_ROWS = -(-(B_TM * TOP_K + N_EXPERTS * (SEG - 1)) // SORT_TM) * SORT_TM
SORT_CHUNKS = BLOCK_ROWS // SORT_TM

MOE_TM = 256
MOE_MT = 8
MOE_TN = 512
MOE_UP_STEPS = D_EXPERT // MOE_TN
MOE_DOWN_STEPS = D_MODEL // MOE_TN
MOE_STEPS = MOE_UP_STEPS + MOE_DOWN_STEPS
MOE_YSLOTS = 4
TILE_SEGS = MOE_TM // SEG
MOE_TILES = (T_ALL * TOP_K + N_BLOCKS * N_EXPERTS * (SEG - 1) + N_EXPERTS * (MOE_TM - 1)) // MOE_TM
MOE_ITEMS = (MOE_TILES + N_EXPERTS * (MOE_MT - 1)) // MOE_MT

_f32 = jnp.float32
_bf16 = jnp.bfloat16


def _rms(x):
    return x * lax.rsqrt(jnp.mean(x * x, axis=-1, keepdims=True) + EPS)


def _ln(x):
    xc = x - jnp.mean(x, axis=-1, keepdims=True)
    return xc * lax.rsqrt(jnp.mean(xc * xc, axis=-1, keepdims=True) + EPS)


def _gelu(x):
    return 0.5 * x * (1.0 + lax.erf(x * (0.5 ** 0.5)))


def _slab_mul(x, m):
    r, c = x.shape
    return (x.reshape(r // 128, 128, c) * m[None]).reshape(r, c)


def _slab_add(x, m):
    r, c = x.shape
    return (x.reshape(r // 128, 128, c) + m[None]).reshape(r, c)


ADA_TN = 1024
ADA_ROWS = DEC_BATCH + 8


def _ada_body(c_ref, w_ref, b_ref, os_ref, op_ref):
    c = c_ref[...]
    a = (c * jax.nn.sigmoid(c)).astype(_bf16)
    r = jnp.dot(a, w_ref[...].astype(_bf16), preferred_element_type=_f32) + b_ref[...]
    os_ref[...] = r[:DEC_BATCH]
    op_ref[...] = r[DEC_BATCH:]


def _ada(c_all, w_ada, b_ada):
    nj = 6 * D_MODEL // ADA_TN
    return pl.pallas_call(
        _ada_body,
        grid=(DEPTH, nj),
        in_specs=[
            pl.BlockSpec((ADA_ROWS, D_MODEL), lambda l, j: (0, 0)),
            pl.BlockSpec((None, D_MODEL, ADA_TN), lambda l, j: (l, 0, j)),
            pl.BlockSpec((None, 1, ADA_TN), lambda l, j: (l, 0, j)),
        ],
        out_specs=[
            pl.BlockSpec((None, DEC_BATCH, ADA_TN), lambda l, j: (l, 0, j)),
            pl.BlockSpec((None, 8, ADA_TN), lambda l, j: (l, 0, j)),
        ],
        out_shape=[
            jax.ShapeDtypeStruct((DEPTH, DEC_BATCH, 6 * D_MODEL), _f32),
            jax.ShapeDtypeStruct((DEPTH, 8, 6 * D_MODEL), _f32),
        ],
        compiler_params=pltpu.CompilerParams(
            dimension_semantics=("arbitrary", "arbitrary"), vmem_limit_bytes=VMEM_LIMIT),
        name="ada_mod",
    )(c_all, w_ada, b_ada.reshape(DEPTH, 1, 6 * D_MODEL))


def _a_prompt_body(x_ref, sh1_ref, sc1_ref, nmix_ref, w_ref, poolw_ref, pscale_ref, sgun_ref, sguw_ref,
                   sgub_ref, cw_ref, cb_ref, clg_ref, clb_ref, scw_ref, onorm_ref,
                   y_ref, pool_o, conf_o, sc_o,
                   h_ref, bufa, bufb, exta, extc, extz):
    i = pl.program_id(1)
    n = pl.program_id(2)
    tm = A_TM
    gw = GROUP_WIDTH

    @pl.when(n == 0)
    def _():
        hn = _rms(x_ref[...]) * nmix_ref[...]
        h_ref[...] = (hn * (1.0 + sc1_ref[...]) + sh1_ref[...]).astype(_bf16)

    def proj():
        return jnp.dot(h_ref[...], w_ref[...], preferred_element_type=_f32)

    def emit(g, y):
        yn = _rms(y) * onorm_ref[:, g * gw:(g + 1) * gw]
        y_ref[:, g * gw:(g + 1) * gw] = yn.astype(_bf16)

    def carry_halo(ext, halo):
        @pl.when(i == 0)
        def _():
            ext[0:halo, :] = jnp.zeros((halo, gw), _f32)

        @pl.when(i > 0)
        def _():
            ext[0:halo, :] = ext[tm:tm + halo, :]

    @pl.when(n == 0)
    def _():
        a = proj()
        carry_halo(exta, POOL_HALO)
        exta[POOL_HALO:, :] = a
        pool_o[...] = a[tm - POOL_HALO:, :]
        pos = (i * tm + lax.broadcasted_iota(jnp.int32, (tm, 1), 0)).astype(_f32)
        for gi, w in enumerate(POOL_WINDOWS):
            c0 = gi * POOL_CH
            s = exta[POOL_HALO:POOL_HALO + tm, c0:c0 + POOL_CH]
            for j in range(1, w):
                s = s + exta[POOL_HALO - j:POOL_HALO - j + tm, c0:c0 + POOL_CH]
            cnt = jnp.minimum(float(w), pos + 1.0)
            pooled = s / cnt - a[:, c0:c0 + POOL_CH]
            yg = jnp.dot(pooled.astype(_bf16), poolw_ref[gi].astype(_bf16), preferred_element_type=_f32)
            bufb[:, c0:c0 + POOL_CH] = yg
        emit(0, bufb[...] * pscale_ref[...])

    @pl.when(n == 1)
    def _():
        bufa[...] = _gelu(proj())

    @pl.when(n == 2)
    def _():
        v = _ln(_gelu(proj())) * sgun_ref[...]
        vb = v.astype(_bf16)
        row = lax.broadcasted_iota(jnp.int32, (SGU_CHUNK, SGU_CHUNK), 0)
        col = lax.broadcasted_iota(jnp.int32, (SGU_CHUNK, SGU_CHUNK), 1)
        for g in range(SGU_HEADS):
            c0 = g * SGU_CHUNK
            wg = jnp.where(row >= col, sguw_ref[g], 0.0).astype(_bf16)
            for c in range(tm // SGU_CHUNK):
                r0 = c * SGU_CHUNK
                m = jnp.dot(wg, vb[r0:r0 + SGU_CHUNK, c0:c0 + SGU_CHUNK], preferred_element_type=_f32)
                bufb[r0:r0 + SGU_CHUNK, c0:c0 + SGU_CHUNK] = m + sgub_ref[:, c0:c0 + SGU_CHUNK]
        emit(1, bufa[...] * bufb[...])

    @pl.when(n == 3)
    def _():
        bufa[...] = proj()

    @pl.when(n == 4)
    def _():
        g = bufa[...] * jax.nn.sigmoid(proj())
        carry_halo(extc, CONF_HALO)
        extc[CONF_HALO:, :] = g
        conf_o[...] = g[tm - CONF_HALO:, :]
        off = CONF_HALO - (CONF_KERNEL - 1)
        acc = jnp.broadcast_to(cb_ref[...], (tm, gw))
        for k in range(CONF_KERNEL):
            acc = acc + cw_ref[k:k + 1, :] * extc[off + k:off + k + tm, :]
        yn = _ln(acc) * clg_ref[...] + clb_ref[...]
        emit(2, yn * jax.nn.sigmoid(yn))

    @pl.when(n == 5)
    def _():
        bufa[...] = proj()

    @pl.when(n == 6)
    def _():
        bufb[...] = proj()

    @pl.when(n == 7)
    def _():
        z = bufb[...] * proj()
        carry_halo(extz, SC_HALO)
        extz[SC_HALO:, :] = z
        sc_o[...] = z[tm - SC_HALO:, :]
        off = SC_HALO - (SC_KERNEL - 1)
        acc = scw_ref[0:1, :] * extz[off:off + tm, :]
        for k in range(1, SC_KERNEL):
            acc = acc + scw_ref[k:k + 1, :] * extz[off + k:off + k + tm, :]
        emit(3, bufa[...] * acc)


def _layer_spec(shape, l, nd):
    zeros = (0,) * len(shape)
    if nd == 1:
        return pl.BlockSpec((None,) + shape, lambda a: (l,) + zeros)
    if nd == 2:
        return pl.BlockSpec((None,) + shape, lambda a, b: (l,) + zeros)
    return pl.BlockSpec((None,) + shape, lambda a, b, c: (l,) + zeros)


def _mixer_params(l, p, nd):
    gw = GROUP_WIDTH
    arrs = [p["cw"], p["cb"], p["clg"], p["clb"], p["scw"], p["onorm"]]
    shapes = [(CONF_KERNEL, gw), (1, gw), (1, gw), (1, gw), (SC_KERNEL, gw), (1, D_MODEL)]
    return arrs, [_layer_spec(s, l, nd) for s in shapes]


def _a_prompt(l, x_all, mod_p4, w_in_b, p):
    gw = GROUP_WIDTH
    ni = SEQ // A_TM
    marrs, mspecs = _mixer_params(l, p, 3)
    in_specs = [
        pl.BlockSpec((A_TM, D_MODEL), lambda b, i, n: (b * ni + i, 0)),
        pl.BlockSpec((None, None, 1, D_MODEL), lambda b, i, n: (l, b, 0, 0)),
        pl.BlockSpec((None, None, 1, D_MODEL), lambda b, i, n: (l, b, 0, 1)),
        _layer_spec((1, D_MODEL), l, 3),
        pl.BlockSpec((None, D_MODEL, A_TN), lambda b, i, n: (l, 0, n)),
        _layer_spec((len(POOL_WINDOWS), POOL_CH, POOL_CH), l, 3),
        _layer_spec((1, gw), l, 3),
        _layer_spec((1, gw), l, 3),
        _layer_spec((SGU_HEADS, SGU_CHUNK, SGU_CHUNK), l, 3),
        _layer_spec((SGU_CHUNK, gw), l, 3),
    ] + mspecs
    out_specs = [
        pl.BlockSpec((A_TM, D_MODEL), lambda b, i, n: (b * ni + i, 0)),
        pl.BlockSpec((None, POOL_HALO, gw), lambda b, i, n: (b, 0, 0)),
        pl.BlockSpec((None, CONF_HALO, gw), lambda b, i, n: (b, 0, 0)),
        pl.BlockSpec((None, SC_HALO, gw), lambda b, i, n: (b, 0, 0)),
    ]
    out_shape = [
        jax.ShapeDtypeStruct((T_PROMPT, D_MODEL), _bf16),
        jax.ShapeDtypeStruct((BATCH, POOL_HALO, gw), _f32),
        jax.ShapeDtypeStruct((BATCH, CONF_HALO, gw), _f32),
        jax.ShapeDtypeStruct((BATCH, SC_HALO, gw), _f32),
    ]
    scratch = [
        pltpu.VMEM((A_TM, D_MODEL), _bf16),
        pltpu.VMEM((A_TM, gw), _f32),
        pltpu.VMEM((A_TM, gw), _f32),
        pltpu.VMEM((POOL_HALO + A_TM, gw), _f32),
        pltpu.VMEM((CONF_HALO + A_TM, gw), _f32),
        pltpu.VMEM((SC_HALO + A_TM, gw), _f32),
    ]
    return pl.pallas_call(
        _a_prompt_body,
        grid=(BATCH, ni, A_STEPS),
        in_specs=in_specs,
        out_specs=out_specs,
        out_shape=out_shape,
        scratch_shapes=scratch,
        compiler_params=pltpu.CompilerParams(
            dimension_semantics=("arbitrary", "arbitrary", "arbitrary"), vmem_limit_bytes=VMEM_LIMIT),
        name="mixers_prompt",
    )(x_all, mod_p4, mod_p4, p["nmix"], w_in_b, p["poolw"], p["pscale"], p["sgun"], p["sguw"], p["sgub"], *marrs)


def _a_sample_body(x_ref, sh1_ref, sc1_ref, nmix_ref, w_ref, poolw_ref, pscale_ref, sgun_ref, sgwc_ref,
                   sgbc_ref, cw_ref, cb_ref, clg_ref, clb_ref, scw_ref, onorm_ref, pst_ref, cst_ref, sst_ref,
                   y_ref, a_o, g_o, z_o, v_o,
                   h_ref, bufa, bufb):
    n = pl.program_id(0)
    gw = GROUP_WIDTH
    nb = DEC_BATCH
    ns = DEC_SEQ

    @pl.when(n == 0)
    def _():
        hn = _rms(x_ref[...]) * nmix_ref[...]
        h_ref[...] = _slab_add(_slab_mul(hn, 1.0 + sc1_ref[...]), sh1_ref[...]).astype(_bf16)

    def proj():
        return jnp.dot(h_ref[...], w_ref[...], preferred_element_type=_f32)

    def emit(g, y):
        yn = _rms(y) * onorm_ref[:, g * gw:(g + 1) * gw]
        y_ref[:, g * gw:(g + 1) * gw] = yn.astype(_bf16)

    def slab(x, s):
        return x[s * nb:(s + 1) * nb, :]

    @pl.when(n == 0)
    def _():
        a = proj()
        a_o[...] = a.reshape(ns, nb, gw)
        npast = POOL_MAX - 1
        for s in range(ns):
            for gi, w in enumerate(POOL_WINDOWS):
                c0 = gi * POOL_CH
                acc = None
                for j in range(w):
                    e = npast + s - j
                    term = pst_ref[e, :, c0:c0 + POOL_CH] if e < npast else slab(a, e - npast)[:, c0:c0 + POOL_CH]
                    acc = term if acc is None else acc + term
                cnt = min(float(w), PAST_LEN + s + 1.0)
                pooled = acc / cnt - slab(a, s)[:, c0:c0 + POOL_CH]
                bufb[s * nb:(s + 1) * nb, c0:c0 + POOL_CH] = pooled
        pooled = bufb[...].astype(_bf16)
        for gi in range(len(POOL_WINDOWS)):
            c0 = gi * POOL_CH
            bufa[:, c0:c0 + POOL_CH] = jnp.dot(pooled[:, c0:c0 + POOL_CH], poolw_ref[gi].astype(_bf16),
                                               preferred_element_type=_f32)
        emit(0, bufa[...] * pscale_ref[...])

    @pl.when(n == 1)
    def _():
        bufa[...] = _gelu(proj())

    @pl.when(n == 2)
    def _():
        v = _ln(_gelu(proj())) * sgun_ref[...]
        v_o[...] = v.reshape(ns, nb, gw)
        for t in range(ns):
            m = jnp.broadcast_to(sgbc_ref[t:t + 1, :], (nb, gw))
            for s in range(t + 1):
                m = m + sgwc_ref[t * ns + s:t * ns + s + 1, :] * slab(v, s)
            bufb[t * nb:(t + 1) * nb, :] = m
        emit(1, bufa[...] * bufb[...])

    @pl.when(n == 3)
    def _():
        bufa[...] = proj()

    @pl.when(n == 4)
    def _():
        g = bufa[...] * jax.nn.sigmoid(proj())
        g_o[...] = g.reshape(ns, nb, gw)
        npast = CONF_KERNEL - 1
        for s in range(ns):
            acc = jnp.broadcast_to(cb_ref[...], (nb, gw))
            for k in range(CONF_KERNEL):
                e = s + k
                term = cst_ref[e] if e < npast else slab(g, e - npast)
                acc = acc + cw_ref[k:k + 1, :] * term
            bufb[s * nb:(s + 1) * nb, :] = acc
        yn = _ln(bufb[...]) * clg_ref[...] + clb_ref[...]
        emit(2, yn * jax.nn.sigmoid(yn))

    @pl.when(n == 5)
    def _():
        bufa[...] = proj()

    @pl.when(n == 6)
    def _():
        bufb[...] = proj()

    @pl.when(n == 7)
    def _():
        z = bufb[...] * proj()
        z_o[...] = z.reshape(ns, nb, gw)
        npast = SC_KERNEL - 1
        for s in range(ns):
            acc = None
            for k in range(SC_KERNEL):
                e = s + k
                term = scw_ref[k:k + 1, :] * (sst_ref[e] if e < npast else slab(z, e - npast))
                acc = term if acc is None else acc + term
            bufb[s * nb:(s + 1) * nb, :] = acc
        emit(3, bufa[...] * bufb[...])


def _a_sample(l, x_all, mod_s, w_in_b, p, pool_t, conf_t, sc_t):
    gw = GROUP_WIDTH
    ns, nb = DEC_SEQ, DEC_BATCH
    marrs, mspecs = _mixer_params(l, p, 1)
    in_specs = [
        pl.BlockSpec((T_SAMPLE, D_MODEL), lambda n: (T_PROMPT // T_SAMPLE, 0)),
        pl.BlockSpec((None, nb, D_MODEL), lambda n: (l, 0, 0)),
        pl.BlockSpec((None, nb, D_MODEL), lambda n: (l, 0, 1)),
        _layer_spec((1, D_MODEL), l, 1),
        pl.BlockSpec((None, D_MODEL, A_TN), lambda n: (l, 0, n)),
        _layer_spec((len(POOL_WINDOWS), POOL_CH, POOL_CH), l, 1),
        _layer_spec((1, gw), l, 1),
        _layer_spec((1, gw), l, 1),
        _layer_spec((ns * ns, gw), l, 1),
        _layer_spec((ns, gw), l, 1),
    ] + mspecs + [
        _layer_spec((POOL_MAX - 1, nb, gw), l, 1),
        _layer_spec((CONF_KERNEL - 1, nb, gw), l, 1),
        _layer_spec((SC_KERNEL - 1, nb, gw), l, 1),
    ]
    out_specs = [pl.BlockSpec((T_SAMPLE, D_MODEL), lambda n: (0, 0))] + [
        pl.BlockSpec((ns, nb, gw), lambda n: (0, 0, 0)) for _ in range(4)]
    out_shape = [jax.ShapeDtypeStruct((T_SAMPLE, D_MODEL), _bf16)] + [
        jax.ShapeDtypeStruct((ns, nb, gw), _f32) for _ in range(4)]
    scratch = [
        pltpu.VMEM((T_SAMPLE, D_MODEL), _bf16),
        pltpu.VMEM((T_SAMPLE, gw), _f32),
        pltpu.VMEM((T_SAMPLE, gw), _f32),
    ]
    return pl.pallas_call(
        _a_sample_body,
        grid=(A_STEPS,),
        in_specs=in_specs,
        out_specs=out_specs,
        out_shape=out_shape,
        scratch_shapes=scratch,
        compiler_params=pltpu.CompilerParams(
            dimension_semantics=("arbitrary",), vmem_limit_bytes=VMEM_LIMIT),
        name="mixers_sample",
    )(x_all, mod_s, mod_s, p["nmix"], w_in_b, p["poolw"], p["pscale"], p["sgun"], p["sgwc"], p["sgbc"],
      *marrs, pool_t, conf_t, sc_t)


def _pick_mod(is_prompt, mp_ref, ms_ref):
    return jnp.where(is_prompt, jnp.broadcast_to(mp_ref[...], (DEC_BATCH, D_MODEL)), ms_ref[...])


def _b_body(x_ref, yp_ref, ys_ref, w_ref, mpg1, mpsh2, mpsc2, msg1, mssh2, mssc2, nffn_ref, rw_ref, rb_ref,
            x1_ref, h2_ref, gate_ref, pos_ref, post_ref, meta_ref):
    i = pl.program_id(0)
    isp = i < B_PROMPT_BLOCKS
    ycat = jnp.where(isp, yp_ref[...], ys_ref[...])
    o = jnp.dot(ycat, w_ref[...], preferred_element_type=_f32)
    x1 = x_ref[...] + _slab_mul(o, _pick_mod(isp, mpg1, msg1))
    x1_ref[...] = x1
    hn = _rms(x1) * nffn_ref[...]
    h2 = _slab_add(_slab_mul(hn, 1.0 + _pick_mod(isp, mpsc2, mssc2)), _pick_mod(isp, mpsh2, mssh2))
    hi = h2.astype(_bf16)
    h2_ref[...] = hi
    lo = (h2 - hi.astype(_f32)).astype(_bf16)
    l1 = jnp.dot(hi, rw_ref[...], preferred_element_type=_f32)
    l2 = jnp.dot(lo, rw_ref[:, 0:LANES], preferred_element_type=_f32)
    logits = l1[:, 0:LANES] + l1[:, LANES:] + l2 + rb_ref[...]
    lane = lax.broadcasted_iota(jnp.int32, (B_TM, LANES), 1).astype(_f32)
    neg = jnp.float32(-jnp.inf)
    cur = jnp.where(lane < N_EXPERTS, logits, neg)
    vals, ids = [], []
    for _ in range(TOP_K):
        m = jnp.max(cur, axis=-1, keepdims=True)
        ix = jnp.min(jnp.where(cur == m, lane, float(LANES)), axis=-1, keepdims=True)
        vals.append(m)
        ids.append(ix)
        cur = jnp.where(lane == ix, neg, cur)
    es = [jnp.exp(v - vals[0]) for v in vals]
    tot = es[0] + es[1] + es[2] + es[3]
    gate = jnp.zeros((B_TM, LANES), _f32)
    for k in range(TOP_K):
        gate = jnp.where(lane == float(k), es[k] / tot, gate)
    gate_ref[...] = gate

    hit = [lane == ix for ix in ids]
    chosen = jnp.zeros((B_TM, LANES), _f32)
    for h in hit:
        chosen = jnp.where(h, 1.0, chosen)
    cnt = jnp.sum(chosen, axis=0, keepdims=True)
    seg_rows = jnp.floor((cnt + (SEG - 1.0)) * (1.0 / SEG)) * SEG
    er = lax.broadcasted_iota(jnp.int32, (LANES, LANES), 0)
    ec = lax.broadcasted_iota(jnp.int32, (LANES, LANES), 1)
    before = jnp.where(er < ec, 1.0, 0.0).astype(_bf16)
    seg_start = jnp.dot(jnp.broadcast_to(seg_rows, (8, LANES)).astype(_bf16), before,
                        preferred_element_type=_f32)[0:1, :]
    tr = lax.broadcasted_iota(jnp.int32, (B_TM, B_TM), 0)
    tc = lax.broadcasted_iota(jnp.int32, (B_TM, B_TM), 1)
    earlier = jnp.where(tr > tc, 1.0, 0.0).astype(_bf16)
    rank = jnp.dot(earlier, chosen.astype(_bf16), preferred_element_type=_f32)
    base = seg_start + rank
    pos = jnp.zeros((B_TM, LANES), _f32)
    for k in range(TOP_K):
        pk = jnp.sum(jnp.where(hit[k], base, 0.0), axis=-1, keepdims=True)
        pos = jnp.where(lane == float(k), pk, pos)
    pos_ref[...] = pos
    post_ref[...] = jnp.transpose(pos)[0:8, :]
    sub = lax.broadcasted_iota(jnp.int32, (8, LANES), 0)
    meta = jnp.where(sub == 0, jnp.broadcast_to(seg_start, (8, LANES)),
                     jnp.where(sub == 1, jnp.broadcast_to(seg_rows, (8, LANES)), 0.0))
    meta_ref[...] = meta.astype(jnp.int32)


def _stage_b(l, x_all, ycat_p, ycat_s, w_out_b, mod_p4, mod_s, p):
    def pb(i):
        return jnp.minimum(i // (SEQ // B_TM), BATCH - 1)

    def mp_spec(j):
        return pl.BlockSpec((None, None, 1, D_MODEL), lambda i: (l, pb(i), 0, j))

    def ms_spec(j):
        return pl.BlockSpec((None, DEC_BATCH, D_MODEL), lambda i: (l, 0, j))

    row_spec = pl.BlockSpec((B_TM, D_MODEL), lambda i: (i, 0))
    in_specs = [
        row_spec,
        pl.BlockSpec((B_TM, D_MODEL), lambda i: (jnp.minimum(i, B_PROMPT_BLOCKS - 1), 0)),
        pl.BlockSpec((T_SAMPLE, D_MODEL), lambda i: (0, 0), pipeline_mode=pl.Buffered(1)),
        pl.BlockSpec((None, D_MODEL, D_MODEL), lambda i: (l, 0, 0), pipeline_mode=pl.Buffered(1)),
        mp_spec(2), mp_spec(3), mp_spec(4), ms_spec(2), ms_spec(3), ms_spec(4),
        _layer_spec((1, D_MODEL), l, 1),
        _layer_spec((D_MODEL, 2 * LANES), l, 1),
        _layer_spec((1, LANES), l, 1),
    ]
    lane_spec = pl.BlockSpec((B_TM, LANES), lambda i: (i, 0))
    return pl.pallas_call(
        _b_body,
        grid=(N_BLOCKS,),
        in_specs=in_specs,
        out_specs=[row_spec, row_spec, lane_spec, lane_spec,
                   pl.BlockSpec((None, 8, B_TM), lambda i: (i, 0, 0)),
                   pl.BlockSpec((None, 8, LANES), lambda i: (i, 0, 0))],
        out_shape=[
            jax.ShapeDtypeStruct((T_ALL, D_MODEL), _f32),
            jax.ShapeDtypeStruct((T_ALL, D_MODEL), _bf16),
            jax.ShapeDtypeStruct((T_ALL, LANES), _f32),
            jax.ShapeDtypeStruct((T_ALL, LANES), _f32),
            jax.ShapeDtypeStruct((N_BLOCKS, 8, B_TM), _f32),
            jax.ShapeDtypeStruct((N_BLOCKS, 8, LANES), jnp.int32),
        ],
        compiler_params=pltpu.CompilerParams(
            dimension_semantics=("arbitrary",), vmem_limit_bytes=VMEM_LIMIT),
        name="outproj_router",
    )(x_all, ycat_p, ycat_s, w_out_b, mod_p4, mod_p4, mod_p4, mod_s, mod_s, mod_s, p["nffn"], p["rw"], p["rb"])


def _sort_body(post_ref, h_ref, o_ref):
    c = pl.program_id(1)
    j = (c * SORT_TM + lax.broadcasted_iota(jnp.int32, (SORT_TM, B_TM), 0)).astype(_f32)
    sel = jnp.zeros((SORT_TM, B_TM), _f32)
    for k in range(TOP_K):
        sel = jnp.where(j == post_ref[k:k + 1, :], 1.0, sel)
    o_ref[...] = jnp.dot(sel.astype(_bf16), h_ref[...], preferred_element_type=_f32).astype(_bf16)


def _local_sort(post, h2):
    return pl.pallas_call(
        _sort_body,
        grid=(N_BLOCKS, SORT_CHUNKS),
        in_specs=[
            pl.BlockSpec((None, 8, B_TM), lambda b, c: (b, 0, 0)),
            pl.BlockSpec((B_TM, D_MODEL), lambda b, c: (b, 0)),
        ],
        out_specs=pl.BlockSpec((SORT_TM, D_MODEL), lambda b, c: (b * SORT_CHUNKS + c, 0)),
        out_shape=jax.ShapeDtypeStruct((N_BLOCKS * BLOCK_ROWS, D_MODEL), _bf16),
        compiler_params=pltpu.CompilerParams(
            dimension_semantics=("arbitrary", "arbitrary"), vmem_limit_bytes=VMEM_LIMIT),
        name="local_sort",
    )(post, h2)


def _moe_body(ie_ref, ilo_ref, int_ref, start_ref, rows_ref, used_ref,
              wg_ref, wl_ref, wd_ref, bg_ref, bl_ref, bd_ref, x_hbm, y_hbm,
              xbuf, hbuf, ybuf, wc, seg_src, nseg, pending, sem_x, sem_y):
    i = pl.program_id(0)
    n = pl.program_id(1)
    e = ie_ref[i]
    lo = ilo_ref[i]
    nt = int_ref[i]
    hi = lo + nt * MOE_TM
    tm, tn = MOE_TM, MOE_TN

    def x_copy(src, dst):
        return pltpu.make_async_copy(x_hbm.at[pl.ds(src, SEG)], xbuf.at[pl.ds(dst, SEG)], sem_x.at[dst // tm])

    def y_copy(slot, q, dst, c):
        return pltpu.make_async_copy(ybuf.at[slot, pl.ds(q * SEG, SEG)],
                                     y_hbm.at[pl.ds(dst, SEG), pl.ds(c * tn, tn)], sem_y.at[slot])

    def drain(slot):
        def w(_, carry):
            y_copy(slot, 0, 0, 0).wait()
            return carry
        lax.fori_loop(0, pending[slot], w, 0)
        pending[slot] = 0

    @pl.when((i == 0) & (n == 0))
    def _():
        for s in range(MOE_YSLOTS):
            pending[s] = 0

    @pl.when((n == 0) & (nt > 0))
    def _():
        def per_block(b, p):
            rows = rows_ref[b * N_EXPERTS + e]
            start = start_ref[b * N_EXPERTS + e]

            def per_piece(g, carry):
                q = p + g * SEG

                @pl.when((q >= lo) & (q < hi))
                def _():
                    src = pl.multiple_of(b * BLOCK_ROWS + start + g * SEG, SEG)
                    dst = pl.multiple_of(q - lo, SEG)
                    seg_src[dst // SEG] = src
                    x_copy(src, dst).start()
                return carry
            lax.fori_loop(0, rows // SEG, per_piece, 0)
            return p + rows
        end = lax.fori_loop(0, N_BLOCKS, per_block, 0)
        ns = (jnp.minimum(end, hi) - lo) // SEG
        nseg[0] = ns

        def zero_piece(g, carry):
            xbuf[pl.ds(pl.multiple_of(g * SEG, SEG), SEG), :] = jnp.zeros((SEG, D_MODEL), _bf16)
            return carry
        lax.fori_loop(ns, nt * TILE_SEGS, zero_piece, 0)

    def tile_pieces(t):
        return jnp.clip(nseg[0] - t * TILE_SEGS, 0, TILE_SEGS)

    for c in range(MOE_UP_STEPS):
        @pl.when((n == c) & (nt > 0))
        def _(c=c):
            wc[:, 0:tn] = wg_ref[...].astype(_bf16)
            wc[:, tn:] = wl_ref[...].astype(_bf16)

            def body(t, carry):
                if c == 0:
                    def w(_, cc):
                        x_copy(0, pl.multiple_of(t * tm, tm)).wait()
                        return cc
                    lax.fori_loop(0, tile_pieces(t), w, 0)
                r0 = pl.multiple_of(t * tm, tm)
                up = jnp.dot(xbuf[pl.ds(r0, tm), :], wc[...], preferred_element_type=_f32)
                glu = jnp.minimum(up[:, 0:tn] + bg_ref[...], SWIGLU_LIMIT)
                lin = jnp.clip(up[:, tn:] + bl_ref[...], -SWIGLU_LIMIT, SWIGLU_LIMIT)
                act = glu * jax.nn.sigmoid(SWIGLU_ALPHA * glu) * (lin + 1.0)
                hbuf[pl.ds(r0, tm), c * tn:(c + 1) * tn] = act.astype(_bf16)
                return carry
            lax.fori_loop(0, nt, body, 0)

    for c in range(MOE_DOWN_STEPS):
        @pl.when((n == MOE_UP_STEPS + c) & (nt > 0))
        def _(c=c):
            wc[:, 0:tn] = wd_ref[...].astype(_bf16)

            def body(t, carry):
                slot = t % MOE_YSLOTS
                drain(slot)
                r0 = pl.multiple_of(t * tm, tm)
                y = jnp.dot(hbuf[pl.ds(r0, tm), :], wc[:, 0:tn], preferred_element_type=_f32) + bd_ref[...]
                ybuf[slot] = y.astype(_bf16)
                np_ = tile_pieces(t)
                for q in range(TILE_SEGS):
                    @pl.when(q < np_)
                    def _(q=q):
                        y_copy(slot, q, pl.multiple_of(seg_src[t * TILE_SEGS + q], SEG), c).start()
                pending[slot] = np_
                return carry
            lax.fori_loop(0, nt, body, 0)

    @pl.when((i == MOE_ITEMS - 1) & (n == MOE_STEPS - 1))
    def _():
        for s in range(MOE_YSLOTS):
            drain(s)
        hbuf[0:SEG, :] = jnp.zeros((SEG, D_EXPERT), _bf16)

        def zero_copy(row):
            return pltpu.make_async_copy(hbuf.at[pl.ds(0, SEG)], y_hbm.at[pl.ds(row, SEG)], sem_y.at[0])

        def per_block(b, carry):
            def start(g, cc):
                zero_copy(pl.multiple_of(b * BLOCK_ROWS + g * SEG, SEG)).start()
                return cc

            def wait(g, cc):
                zero_copy(0).wait()
                return cc
            first = used_ref[b] // SEG
            lax.fori_loop(first, BLOCK_ROWS // SEG, start, 0)
            lax.fori_loop(first, BLOCK_ROWS // SEG, wait, 0)
            return carry
        lax.fori_loop(0, N_BLOCKS, per_block, 0)


def _moe(l, item_e, item_lo, item_nt, seg_start, seg_rows, used, w_up, b_up4, w_down, b_down4, x_sorted):
    def col_up(n, valid):
        return jnp.where(valid, jnp.minimum(n, MOE_UP_STEPS - 1), MOE_UP_STEPS - 1)

    def col_down(n, valid):
        return jnp.where(valid, jnp.maximum(n - MOE_UP_STEPS, 0), MOE_DOWN_STEPS - 1)

    def glu_map(i, n, ie, ilo, int_, *_):
        return (l, ie[i], 0, col_up(n, int_[i] > 0))

    def lin_map(i, n, ie, ilo, int_, *_):
        return (l, ie[i], 0, MOE_UP_STEPS + col_up(n, int_[i] > 0))

    def down_map(i, n, ie, ilo, int_, *_):
        return (l, ie[i], 0, col_down(n, int_[i] > 0))

    grid_spec = pltpu.PrefetchScalarGridSpec(
        num_scalar_prefetch=6,
        grid=(MOE_ITEMS, MOE_STEPS),
        in_specs=[
            pl.BlockSpec((None, None, D_MODEL, MOE_TN), glu_map),
            pl.BlockSpec((None, None, D_MODEL, MOE_TN), lin_map),
            pl.BlockSpec((None, None, D_EXPERT, MOE_TN), down_map),
            pl.BlockSpec((None, None, 1, MOE_TN), glu_map),
            pl.BlockSpec((None, None, 1, MOE_TN), lin_map),
            pl.BlockSpec((None, None, 1, MOE_TN), down_map),
            pl.BlockSpec(memory_space=pl.ANY),
        ],
        out_specs=pl.BlockSpec(memory_space=pl.ANY),
        scratch_shapes=[
            pltpu.VMEM((MOE_MT * MOE_TM, D_MODEL), _bf16),
            pltpu.VMEM((MOE_MT * MOE_TM, D_EXPERT), _bf16),
            pltpu.VMEM((MOE_YSLOTS, MOE_TM, MOE_TN), _bf16),
            pltpu.VMEM((D_MODEL, 2 * MOE_TN), _bf16),
            pltpu.SMEM((MOE_MT * TILE_SEGS,), jnp.int32),
            pltpu.SMEM((1,), jnp.int32),
            pltpu.SMEM((MOE_YSLOTS,), jnp.int32),
            pltpu.SemaphoreType.DMA((MOE_MT,)),
            pltpu.SemaphoreType.DMA((MOE_YSLOTS,)),
        ],
    )
    return pl.pallas_call(
        _moe_body,
        grid_spec=grid_spec,
        out_shape=jax.ShapeDtypeStruct((N_BLOCKS * BLOCK_ROWS, D_MODEL), _bf16),
        compiler_params=pltpu.CompilerParams(
            dimension_semantics=("arbitrary", "arbitrary"), vmem_limit_bytes=VMEM_LIMIT),
        name="moe_experts",
    )(item_e, item_lo, item_nt, seg_start, seg_rows, used, w_up, w_up, w_down, b_up4, b_up4, b_down4, x_sorted)


def _work_items(meta):
    seg_start = meta[:, 0, :N_EXPERTS]
    seg_rows = meta[:, 1, :N_EXPERTS]
    rows_e = jnp.sum(seg_rows, axis=0)
    nt = (rows_e + MOE_TM - 1) // MOE_TM
    ipe = (nt + MOE_MT - 1) // MOE_MT
    cum = jnp.cumsum(ipe)
    first = cum - ipe
    total = cum[-1]
    ii = jnp.arange(MOE_ITEMS, dtype=jnp.int32)
    e_of = jnp.minimum(jnp.searchsorted(cum, ii, side="right"), N_EXPERTS - 1).astype(jnp.int32)
    valid = ii < total
    e_last = jnp.minimum(jnp.searchsorted(cum, total - 1, side="right"), N_EXPERTS - 1).astype(jnp.int32)
    e_i = jnp.where(valid, e_of, e_last)
    j = ii - first[e_i]
    lo_i = jnp.where(valid, j * (MOE_MT * MOE_TM), 0)
    nt_i = jnp.where(valid, jnp.minimum(MOE_MT, nt[e_i] - j * MOE_MT), 0)
    used = jnp.sum(seg_rows, axis=1)
    i32 = jnp.int32
    return (e_i.astype(i32), lo_i.astype(i32), nt_i.astype(i32), seg_start.reshape(-1).astype(i32),
            seg_rows.reshape(-1).astype(i32), used.astype(i32))


def _c_body(x1_ref, y_ref, gate_ref, pos_ref, mpg2, msg2, nfin_ref, out_ref, acc_ref, *, final):
    i = pl.program_id(0)
    c = pl.program_id(1)
    isp = i < B_PROMPT_BLOCKS
    j = (c * SORT_TM + lax.broadcasted_iota(jnp.int32, (B_TM, SORT_TM), 1)).astype(_f32)
    g_hi = jnp.zeros((B_TM, SORT_TM), _f32)
    g_lo = jnp.zeros((B_TM, SORT_TM), _f32)
    for k in range(TOP_K):
        sel = j == pos_ref[:, k:k + 1]
        gk = gate_ref[:, k:k + 1]
        gk_hi = gk.astype(_bf16).astype(_f32)
        g_hi = jnp.where(sel, gk_hi, g_hi)
        g_lo = jnp.where(sel, gk - gk_hi, g_lo)
    y = y_ref[...]
    part = (jnp.dot(g_hi.astype(_bf16), y, preferred_element_type=_f32)
            + jnp.dot(g_lo.astype(_bf16), y, preferred_element_type=_f32))

    @pl.when(c == 0)
    def _():
        acc_ref[...] = part

    @pl.when(c > 0)
    def _():
        acc_ref[...] += part

    @pl.when(c == SORT_CHUNKS - 1)
    def _():
        x2 = x1_ref[...] + _slab_mul(acc_ref[...], _pick_mod(isp, mpg2, msg2))
        if final:
            x2 = _rms(x2) * nfin_ref[...]
        out_ref[...] = x2


def _combine(l, x1, y_sorted, gates, pos, mod_p4, mod_s, nfin, final):
    def pb(i):
        return jnp.minimum(i // (SEQ // B_TM), BATCH - 1)

    row_spec = pl.BlockSpec((B_TM, D_MODEL), lambda i, c: (i, 0))
    lane_spec = pl.BlockSpec((B_TM, LANES), lambda i, c: (i, 0))
    return pl.pallas_call(
        functools.partial(_c_body, final=final),
        grid=(N_BLOCKS, SORT_CHUNKS),
        in_specs=[
            row_spec,
            pl.BlockSpec((SORT_TM, D_MODEL), lambda i, c: (i * SORT_CHUNKS + c, 0)),
            lane_spec,
            lane_spec,
            pl.BlockSpec((None, None, 1, D_MODEL), lambda i, c: (l, pb(i), 0, 5)),
            pl.BlockSpec((None, DEC_BATCH, D_MODEL), lambda i, c: (l, 0, 5)),
            pl.BlockSpec((1, D_MODEL), lambda i, c: (0, 0)),
        ],
        out_specs=row_spec,
        out_shape=jax.ShapeDtypeStruct((T_ALL, D_MODEL), _f32),
        scratch_shapes=[pltpu.VMEM((B_TM, D_MODEL), _f32)],
        compiler_params=pltpu.CompilerParams(
            dimension_semantics=("arbitrary", "arbitrary"), vmem_limit_bytes=VMEM_LIMIT),
        name="moe_combine",
    )(x1, y_sorted, gates, pos, mod_p4, mod_s, nfin)


def kernel(x_prompt, x_sample, c_prompt, c_sample, state_pool, state_conformer, state_shortconv, w_ada, b_ada, norm_mix, norm_ffn, norm_final, w_in, pool_w, pool_scale, sgu_norm, sgu_w, sgu_b, conf_w, conf_b, conf_ln_g, conf_ln_b, sc_w, out_norm, w_out, router_w, router_b, w_up, b_up, w_down, b_down):
    gw = GROUP_WIDTH
    ns, nb = DEC_SEQ, DEC_BATCH

    c_all = jnp.concatenate([c_sample, c_prompt, jnp.zeros((ADA_ROWS - nb - BATCH, D_MODEL), _f32)], axis=0)
    mod_s, mod_p8 = _ada(c_all, w_ada, b_ada)
    mod_p4 = mod_p8[:, :BATCH].reshape(DEPTH, BATCH, 1, 6 * D_MODEL)

    rw_hi = router_w.astype(_bf16)
    rw_lo = (router_w - rw_hi.astype(_f32)).astype(_bf16)
    lane_pad = ((0, 0), (0, 0), (0, LANES - N_EXPERTS))
    params = {
        "nmix": norm_mix.reshape(DEPTH, 1, D_MODEL),
        "nffn": norm_ffn.reshape(DEPTH, 1, D_MODEL),
        "poolw": pool_w,
        "pscale": pool_scale.reshape(DEPTH, 1, gw),
        "sgun": sgu_norm.reshape(DEPTH, 1, gw),
        "sguw": sgu_w,
        "sgub": jnp.repeat(jnp.swapaxes(sgu_b, 1, 2), SGU_CHUNK, axis=2),
        "sgwc": jnp.repeat(jnp.transpose(sgu_w[:, :, :ns, :ns], (0, 2, 3, 1)).reshape(DEPTH, ns * ns, SGU_HEADS),
                           SGU_CHUNK, axis=2),
        "sgbc": jnp.repeat(jnp.swapaxes(sgu_b[:, :, :ns], 1, 2), SGU_CHUNK, axis=2),
        "cw": conf_w,
        "cb": conf_b.reshape(DEPTH, 1, gw),
        "clg": conf_ln_g.reshape(DEPTH, 1, gw),
        "clb": conf_ln_b.reshape(DEPTH, 1, gw),
        "scw": sc_w,
        "onorm": out_norm.reshape(DEPTH, 1, D_MODEL),
        "rw": jnp.concatenate([jnp.pad(rw_hi, lane_pad), jnp.pad(rw_lo, lane_pad)], axis=2),
        "rb": jnp.pad(router_b, ((0, 0), (0, LANES - N_EXPERTS))).reshape(DEPTH, 1, LANES),
    }
    w_in_b = w_in.astype(_bf16)
    w_out_b = w_out.astype(_bf16)
    b_up4 = b_up.reshape(DEPTH, N_EXPERTS, 1, 2 * D_EXPERT)
    b_down4 = b_down.reshape(DEPTH, N_EXPERTS, 1, D_MODEL)
    nfin = norm_final.reshape(1, D_MODEL)

    pool_t = jnp.transpose(state_pool, (0, 2, 1, 3))
    conf_t = jnp.transpose(state_conformer, (0, 2, 1, 3))
    sc_t = jnp.transpose(state_shortconv, (0, 2, 1, 3))

    x_all = jnp.concatenate([x_prompt.reshape(T_PROMPT, D_MODEL),
                             jnp.transpose(x_sample, (1, 0, 2)).reshape(T_SAMPLE, D_MODEL)], axis=0)

    pool_p, conf_p, sc_p, pool_s, conf_s, sc_s, v_s = [], [], [], [], [], [], []
    for l in range(DEPTH):
        ycat_p, po, co, so = _a_prompt(l, x_all, mod_p4, w_in_b, params)
        ycat_s, a_new, g_new, z_new, v_new = _a_sample(l, x_all, mod_s, w_in_b, params, pool_t, conf_t, sc_t)
        pool_p.append(po[:, POOL_HALO - (POOL_MAX - 1):])
        conf_p.append(co[:, CONF_HALO - (CONF_KERNEL - 1):])
        sc_p.append(so[:, SC_HALO - (SC_KERNEL - 1):])
        pool_s.append(jnp.concatenate([state_pool[l][:, ns:], jnp.transpose(a_new, (1, 0, 2))], axis=1))
        conf_s.append(jnp.concatenate([state_conformer[l][:, ns:], jnp.transpose(g_new, (1, 0, 2))], axis=1))
        sc_s.append(jnp.transpose(z_new, (1, 0, 2))[:, ns - (SC_KERNEL - 1):])
        v_s.append(jnp.transpose(v_new, (1, 0, 2)))

        x1, h2, gates, pos, post, meta = _stage_b(l, x_all, ycat_p, ycat_s, w_out_b, mod_p4, mod_s, params)
        x_sorted = _local_sort(post, h2)
        item_e, item_lo, item_nt, seg_start, seg_rows, used = _work_items(meta)
        y_sorted = _moe(l, item_e, item_lo, item_nt, seg_start, seg_rows, used, w_up, b_up4, w_down, b_down4,
                        x_sorted)
        x_all = _combine(l, x1, y_sorted, gates, pos, mod_p4, mod_s, nfin, final=(l == DEPTH - 1))

    y_prompt = x_all[:T_PROMPT].reshape(BATCH, SEQ, D_MODEL)
    y_sample = jnp.transpose(x_all[T_PROMPT:].reshape(ns, nb, D_MODEL), (1, 0, 2))
    return (y_prompt, y_sample, jnp.stack(pool_p), jnp.stack(pool_s), jnp.stack(conf_p), jnp.stack(conf_s),
            jnp.stack(sc_p), jnp.stack(sc_s), jnp.stack(v_s))
```

```python
import functools

import jax
import jax.numpy as jnp
from jax import lax
from jax.experimental import pallas as pl
from jax.experimental.pallas import tpu as pltpu

D_MODEL = 2048
BATCH = 4
SEQ = 2048
DEPTH = 2
DEC_BATCH = 128
DEC_SEQ = 4
PAST_LEN = 16384
GROUP_WIDTH = 512
POOL_WINDOWS = (2, 4, 8, 16)
POOL_MAX = 16
POOL_CH = 128
SGU_HEADS = 4
SGU_CHUNK = 128
CONF_KERNEL = 31
SC_KERNEL = 3
IN_COLS = 8 * GROUP_WIDTH
N_EXPERTS = 32
TOP_K = 4
D_EXPERT = 2048
SWIGLU_ALPHA = 1.702
SWIGLU_LIMIT = 7.0
EPS = 1e-5

T_PROMPT = BATCH * SEQ
T_SAMPLE = DEC_BATCH * DEC_SEQ
T_ALL = T_PROMPT + T_SAMPLE

LANES = 128
BF16_ROWS = 16
VMEM_LIMIT = 56 * 1024 * 1024

A_TM = 1024
A_TN = 512
A_STEPS = IN_COLS // A_TN
POOL_HALO = 16
CONF_HALO = 32
SC_HALO = 8

B_TM = 512
N_BLOCKS = T_ALL // B_TM
B_PROMPT_BLOCKS = T_PROMPT // B_TM
SEG = BF16_ROWS
SORT_TM = 512
BLOCK_ROWS = -(-(B_TM * TOP_K + N_EXPERTS * (SEG - 1)) // SORT_TM) * SORT_TM
SORT_CHUNKS = BLOCK_ROWS // SORT_TM

MOE_TM = 256
MOE_MT = 8
MOE_TN = 512
MOE_UP_STEPS = D_EXPERT // MOE_TN
MOE_DOWN_STEPS = D_MODEL // MOE_TN
MOE_STEPS = MOE_UP_STEPS + MOE_DOWN_STEPS
MOE_YSLOTS = 4
TILE_SEGS = MOE_TM // SEG
MOE_TILES = (T_ALL * TOP_K + N_BLOCKS * N_EXPERTS * (SEG - 1) + N_EXPERTS * (MOE_TM - 1)) // MOE_TM
MOE_ITEMS = (MOE_TILES + N_EXPERTS * (MOE_MT - 1)) // MOE_MT

_f32 = jnp.float32
_bf16 = jnp.bfloat16


def _rms(x):
    return x * lax.rsqrt(jnp.mean(x * x, axis=-1, keepdims=True) + EPS)


def _ln(x):
    xc = x - jnp.mean(x, axis=-1, keepdims=True)
    return xc * lax.rsqrt(jnp.mean(xc * xc, axis=-1, keepdims=True) + EPS)


def _gelu(x):
    return 0.5 * x * (1.0 + lax.erf(x * (0.5 ** 0.5)))


def _slab_mul(x, m):
    r, c = x.shape
    return (x.reshape(r // 128, 128, c) * m[None]).reshape(r, c)


def _slab_add(x, m):
    r, c = x.shape
    return (x.reshape(r // 128, 128, c) + m[None]).reshape(r, c)


ADA_TN = 1024
ADA_ROWS = DEC_BATCH + 8


def _ada_body(c_ref, w_ref, b_ref, os_ref, op_ref):
    c = c_ref[...]
    a = (c * jax.nn.sigmoid(c)).astype(_bf16)
    r = jnp.dot(a, w_ref[...].astype(_bf16), preferred_element_type=_f32) + b_ref[...]
    os_ref[...] = r[:DEC_BATCH]
    op_ref[...] = r[DEC_BATCH:]


def _ada(c_all, w_ada, b_ada):
    nj = 6 * D_MODEL // ADA_TN
    return pl.pallas_call(
        _ada_body,
        grid=(DEPTH, nj),
        in_specs=[
            pl.BlockSpec((ADA_ROWS, D_MODEL), lambda l, j: (0, 0)),
            pl.BlockSpec((None, D_MODEL, ADA_TN), lambda l, j: (l, 0, j)),
            pl.BlockSpec((None, 1, ADA_TN), lambda l, j: (l, 0, j)),
        ],
        out_specs=[
            pl.BlockSpec((None, DEC_BATCH, ADA_TN), lambda l, j: (l, 0, j)),
            pl.BlockSpec((None, 8, ADA_TN), lambda l, j: (l, 0, j)),
        ],
        out_shape=[
            jax.ShapeDtypeStruct((DEPTH, DEC_BATCH, 6 * D_MODEL), _f32),
            jax.ShapeDtypeStruct((DEPTH, 8, 6 * D_MODEL), _f32),
        ],
        compiler_params=pltpu.CompilerParams(
            dimension_semantics=("arbitrary", "arbitrary"), vmem_limit_bytes=VMEM_LIMIT),
        name="ada_mod",
    )(c_all, w_ada, b_ada.reshape(DEPTH, 1, 6 * D_MODEL))


def _a_prompt_body(x_ref, sh1_ref, sc1_ref, nmix_ref, w_ref, poolw_ref, pscale_ref, sgun_ref, sguw_ref,
                   sgub_ref, cw_ref, cb_ref, clg_ref, clb_ref, scw_ref, onorm_ref,
                   y_ref, pool_o, conf_o, sc_o,
                   h_ref, bufa, bufb, exta, extc, extz):
    i = pl.program_id(1)
    n = pl.program_id(2)
    tm = A_TM
    gw = GROUP_WIDTH

    @pl.when(n == 0)
    def _():
        hn = _rms(x_ref[...]) * nmix_ref[...]
        h_ref[...] = (hn * (1.0 + sc1_ref[...]) + sh1_ref[...]).astype(_bf16)

    def proj():
        return jnp.dot(h_ref[...], w_ref[...], preferred_element_type=_f32)

    def emit(g, y):
        yn = _rms(y) * onorm_ref[:, g * gw:(g + 1) * gw]
        y_ref[:, g * gw:(g + 1) * gw] = yn.astype(_bf16)

    def carry_halo(ext, halo):
        @pl.when(i == 0)
        def _():
            ext[0:halo, :] = jnp.zeros((halo, gw), _f32)

        @pl.when(i > 0)
        def _():
            ext[0:halo, :] = ext[tm:tm + halo, :]

    @pl.when(n == 0)
    def _():
        a = proj()
        carry_halo(exta, POOL_HALO)
        exta[POOL_HALO:, :] = a
        pool_o[...] = a[tm - POOL_HALO:, :]
        pos = (i * tm + lax.broadcasted_iota(jnp.int32, (tm, 1), 0)).astype(_f32)
        for gi, w in enumerate(POOL_WINDOWS):
            c0 = gi * POOL_CH
            s = exta[POOL_HALO:POOL_HALO + tm, c0:c0 + POOL_CH]
            for j in range(1, w):
                s = s + exta[POOL_HALO - j:POOL_HALO - j + tm, c0:c0 + POOL_CH]
            cnt = jnp.minimum(float(w), pos + 1.0)
            pooled = s / cnt - a[:, c0:c0 + POOL_CH]
            yg = jnp.dot(pooled.astype(_bf16), poolw_ref[gi].astype(_bf16), preferred_element_type=_f32)
            bufb[:, c0:c0 + POOL_CH] = yg
        emit(0, bufb[...] * pscale_ref[...])

    @pl.when(n == 1)
    def _():
        bufa[...] = _gelu(proj())

    @pl.when(n == 2)
    def _():
        v = _ln(_gelu(proj())) * sgun_ref[...]
        vb = v.astype(_bf16)
        row = lax.broadcasted_iota(jnp.int32, (SGU_CHUNK, SGU_CHUNK), 0)
        col = lax.broadcasted_iota(jnp.int32, (SGU_CHUNK, SGU_CHUNK), 1)
        for g in range(SGU_HEADS):
            c0 = g * SGU_CHUNK
            wg = jnp.where(row >= col, sguw_ref[g], 0.0).astype(_bf16)
            for c in range(tm // SGU_CHUNK):
                r0 = c * SGU_CHUNK
                m = jnp.dot(wg, vb[r0:r0 + SGU_CHUNK, c0:c0 + SGU_CHUNK], preferred_element_type=_f32)
                bufb[r0:r0 + SGU_CHUNK, c0:c0 + SGU_CHUNK] = m + sgub_ref[:, c0:c0 + SGU_CHUNK]
        emit(1, bufa[...] * bufb[...])

    @pl.when(n == 3)
    def _():
        bufa[...] = proj()

    @pl.when(n == 4)
    def _():
        g = bufa[...] * jax.nn.sigmoid(proj())
        carry_halo(extc, CONF_HALO)
        extc[CONF_HALO:, :] = g
        conf_o[...] = g[tm - CONF_HALO:, :]
        off = CONF_HALO - (CONF_KERNEL - 1)
        acc = jnp.broadcast_to(cb_ref[...], (tm, gw))
        for k in range(CONF_KERNEL):
            acc = acc + cw_ref[k:k + 1, :] * extc[off + k:off + k + tm, :]
        yn = _ln(acc) * clg_ref[...] + clb_ref[...]
        emit(2, yn * jax.nn.sigmoid(yn))

    @pl.when(n == 5)
    def _():
        bufa[...] = proj()

    @pl.when(n == 6)
    def _():
        bufb[...] = proj()

    @pl.when(n == 7)
    def _():
        z = bufb[...] * proj()
        carry_halo(extz, SC_HALO)
        extz[SC_HALO:, :] = z
        sc_o[...] = z[tm - SC_HALO:, :]
        off = SC_HALO - (SC_KERNEL - 1)
        acc = scw_ref[0:1, :] * extz[off:off + tm, :]
        for k in range(1, SC_KERNEL):
            acc = acc + scw_ref[k:k + 1, :] * extz[off + k:off + k + tm, :]
        emit(3, bufa[...] * acc)


def _layer_spec(shape, l, nd):
    zeros = (0,) * len(shape)
    if nd == 1:
        return pl.BlockSpec((None,) + shape, lambda a: (l,) + zeros)
    if nd == 2:
        return pl.BlockSpec((None,) + shape, lambda a, b: (l,) + zeros)
    return pl.BlockSpec((None,) + shape, lambda a, b, c: (l,) + zeros)


def _mixer_params(l, p, nd):
    gw = GROUP_WIDTH
    arrs = [p["cw"], p["cb"], p["clg"], p["clb"], p["scw"], p["onorm"]]
    shapes = [(CONF_KERNEL, gw), (1, gw), (1, gw), (1, gw), (SC_KERNEL, gw), (1, D_MODEL)]
    return arrs, [_layer_spec(s, l, nd) for s in shapes]


def _a_prompt(l, x_all, mod_p4, w_in_b, p):
    gw = GROUP_WIDTH
    ni = SEQ // A_TM
    marrs, mspecs = _mixer_params(l, p, 3)
    in_specs = [
        pl.BlockSpec((A_TM, D_MODEL), lambda b, i, n: (b * ni + i, 0)),
        pl.BlockSpec((None, None, 1, D_MODEL), lambda b, i, n: (l, b, 0, 0)),
        pl.BlockSpec((None, None, 1, D_MODEL), lambda b, i, n: (l, b, 0, 1)),
        _layer_spec((1, D_MODEL), l, 3),
        pl.BlockSpec((None, D_MODEL, A_TN), lambda b, i, n: (l, 0, n)),
        _layer_spec((len(POOL_WINDOWS), POOL_CH, POOL_CH), l, 3),
        _layer_spec((1, gw), l, 3),
        _layer_spec((1, gw), l, 3),
        _layer_spec((SGU_HEADS, SGU_CHUNK, SGU_CHUNK), l, 3),
        _layer_spec((SGU_CHUNK, gw), l, 3),
    ] + mspecs
    out_specs = [
        pl.BlockSpec((A_TM, D_MODEL), lambda b, i, n: (b * ni + i, 0)),
        pl.BlockSpec((None, POOL_HALO, gw), lambda b, i, n: (b, 0, 0)),
        pl.BlockSpec((None, CONF_HALO, gw), lambda b, i, n: (b, 0, 0)),
        pl.BlockSpec((None, SC_HALO, gw), lambda b, i, n: (b, 0, 0)),
    ]
    out_shape = [
        jax.ShapeDtypeStruct((T_PROMPT, D_MODEL), _bf16),
        jax.ShapeDtypeStruct((BATCH, POOL_HALO, gw), _f32),
        jax.ShapeDtypeStruct((BATCH, CONF_HALO, gw), _f32),
        jax.ShapeDtypeStruct((BATCH, SC_HALO, gw), _f32),
    ]
    scratch = [
        pltpu.VMEM((A_TM, D_MODEL), _bf16),
        pltpu.VMEM((A_TM, gw), _f32),
        pltpu.VMEM((A_TM, gw), _f32),
        pltpu.VMEM((POOL_HALO + A_TM, gw), _f32),
        pltpu.VMEM((CONF_HALO + A_TM, gw), _f32),
        pltpu.VMEM((SC_HALO + A_TM, gw), _f32),
    ]
    return pl.pallas_call(
        _a_prompt_body,
        grid=(BATCH, ni, A_STEPS),
        in_specs=in_specs,
        out_specs=out_specs,
        out_shape=out_shape,
        scratch_shapes=scratch,
        compiler_params=pltpu.CompilerParams(
            dimension_semantics=("arbitrary", "arbitrary", "arbitrary"), vmem_limit_bytes=VMEM_LIMIT),
        name="mixers_prompt",
    )(x_all, mod_p4, mod_p4, p["nmix"], w_in_b, p["poolw"], p["pscale"], p["sgun"], p["sguw"], p["sgub"], *marrs)


def _a_sample_body(x_ref, sh1_ref, sc1_ref, nmix_ref, w_ref, poolw_ref, pscale_ref, sgun_ref, sgwc_ref,
                   sgbc_ref, cw_ref, cb_ref, clg_ref, clb_ref, scw_ref, onorm_ref, pst_ref, cst_ref, sst_ref,
                   y_ref, a_o, g_o, z_o, v_o,
                   h_ref, bufa, bufb):
    n = pl.program_id(0)
    gw = GROUP_WIDTH
    nb = DEC_BATCH
    ns = DEC_SEQ

    @pl.when(n == 0)
    def _():
        hn = _rms(x_ref[...]) * nmix_ref[...]
        h_ref[...] = _slab_add(_slab_mul(hn, 1.0 + sc1_ref[...]), sh1_ref[...]).astype(_bf16)

    def proj():
        return jnp.dot(h_ref[...], w_ref[...], preferred_element_type=_f32)

    def emit(g, y):
        yn = _rms(y) * onorm_ref[:, g * gw:(g + 1) * gw]
        y_ref[:, g * gw:(g + 1) * gw] = yn.astype(_bf16)

    def slab(x, s):
        return x[s * nb:(s + 1) * nb, :]

    @pl.when(n == 0)
    def _():
        a = proj()
        a_o[...] = a.reshape(ns, nb, gw)
        npast = POOL_MAX - 1
        for s in range(ns):
            for gi, w in enumerate(POOL_WINDOWS):
                c0 = gi * POOL_CH
                acc = None
                for j in range(w):
                    e = npast + s - j
                    term = pst_ref[e, :, c0:c0 + POOL_CH] if e < npast else slab(a, e - npast)[:, c0:c0 + POOL_CH]
                    acc = term if acc is None else acc + term
                cnt = min(float(w), PAST_LEN + s + 1.0)
                pooled = acc / cnt - slab(a, s)[:, c0:c0 + POOL_CH]
                bufb[s * nb:(s + 1) * nb, c0:c0 + POOL_CH] = pooled
        pooled = bufb[...].astype(_bf16)
        for gi in range(len(POOL_WINDOWS)):
            c0 = gi * POOL_CH
            bufa[:, c0:c0 + POOL_CH] = jnp.dot(pooled[:, c0:c0 + POOL_CH], poolw_ref[gi].astype(_bf16),
                                               preferred_element_type=_f32)
        emit(0, bufa[...] * pscale_ref[...])

    @pl.when(n == 1)
    def _():
        bufa[...] = _gelu(proj())

    @pl.when(n == 2)
    def _():
        v = _ln(_gelu(proj())) * sgun_ref[...]
        v_o[...] = v.reshape(ns, nb, gw)
        for t in range(ns):
            m = jnp.broadcast_to(sgbc_ref[t:t + 1, :], (nb, gw))
            for s in range(t + 1):
                m = m + sgwc_ref[t * ns + s:t * ns + s + 1, :] * slab(v, s)
            bufb[t * nb:(t + 1) * nb, :] = m
        emit(1, bufa[...] * bufb[...])

    @pl.when(n == 3)
    def _():
        bufa[...] = proj()

    @pl.when(n == 4)
    def _():
        g = bufa[...] * jax.nn.sigmoid(proj())
        g_o[...] = g.reshape(ns, nb, gw)
        npast = CONF_KERNEL - 1
        for s in range(ns):
            acc = jnp.broadcast_to(cb_ref[...], (nb, gw))
            for k in range(CONF_KERNEL):
                e = s + k
                term = cst_ref[e] if e < npast else slab(g, e - npast)
                acc = acc + cw_ref[k:k + 1, :] * term
            bufb[s * nb:(s + 1) * nb, :] = acc
        yn = _ln(bufb[...]) * clg_ref[...] + clb_ref[...]
        emit(2, yn * jax.nn.sigmoid(yn))

    @pl.when(n == 5)
    def _():
        bufa[...] = proj()

    @pl.when(n == 6)
    def _():
        bufb[...] = proj()

    @pl.when(n == 7)
    def _():
        z = bufb[...] * proj()
        z_o[...] = z.reshape(ns, nb, gw)
        npast = SC_KERNEL - 1
        for s in range(ns):
            acc = None
            for k in range(SC_KERNEL):
                e = s + k
                term = scw_ref[k:k + 1, :] * (sst_ref[e] if e < npast else slab(z, e - npast))
                acc = term if acc is None else acc + term
            bufb[s * nb:(s + 1) * nb, :] = acc
        emit(3, bufa[...] * bufb[...])


def _a_sample(l, x_all, x_blk, mod_s, w_in_b, p, pool_t, conf_t, sc_t):
    gw = GROUP_WIDTH
    ns, nb = DEC_SEQ, DEC_BATCH
    marrs, mspecs = _mixer_params(l, p, 1)
    in_specs = [
        pl.BlockSpec((T_SAMPLE, D_MODEL), lambda n: (x_blk, 0)),
        pl.BlockSpec((None, nb, D_MODEL), lambda n: (l, 0, 0)),
        pl.BlockSpec((None, nb, D_MODEL), lambda n: (l, 0, 1)),
        _layer_spec((1, D_MODEL), l, 1),
        pl.BlockSpec((None, D_MODEL, A_TN), lambda n: (l, 0, n)),
        _layer_spec((len(POOL_WINDOWS), POOL_CH, POOL_CH), l, 1),
        _layer_spec((1, gw), l, 1),
        _layer_spec((1, gw), l, 1),
        _layer_spec((ns * ns, gw), l, 1),
        _layer_spec((ns, gw), l, 1),
    ] + mspecs + [
        _layer_spec((POOL_MAX - 1, nb, gw), l, 1),
        _layer_spec((CONF_KERNEL - 1, nb, gw), l, 1),
        _layer_spec((SC_KERNEL - 1, nb, gw), l, 1),
    ]
    out_specs = [pl.BlockSpec((T_SAMPLE, D_MODEL), lambda n: (0, 0))] + [
        pl.BlockSpec((ns, nb, gw), lambda n: (0, 0, 0)) for _ in range(4)]
    out_shape = [jax.ShapeDtypeStruct((T_SAMPLE, D_MODEL), _bf16)] + [
        jax.ShapeDtypeStruct((ns, nb, gw), _f32) for _ in range(4)]
    scratch = [
        pltpu.VMEM((T_SAMPLE, D_MODEL), _bf16),
        pltpu.VMEM((T_SAMPLE, gw), _f32),
        pltpu.VMEM((T_SAMPLE, gw), _f32),
    ]
    return pl.pallas_call(
        _a_sample_body,
        grid=(A_STEPS,),
        in_specs=in_specs,
        out_specs=out_specs,
        out_shape=out_shape,
        scratch_shapes=scratch,
        compiler_params=pltpu.CompilerParams(
            dimension_semantics=("arbitrary",), vmem_limit_bytes=VMEM_LIMIT),
        name="mixers_sample",
    )(x_all, mod_s, mod_s, p["nmix"], w_in_b, p["poolw"], p["pscale"], p["sgun"], p["sgwc"], p["sgbc"],
      *marrs, pool_t, conf_t, sc_t)


def _pick_mod(is_prompt, mp_ref, ms_ref):
    return jnp.where(is_prompt, jnp.broadcast_to(mp_ref[...], (DEC_BATCH, D_MODEL)), ms_ref[...])


def _b_body(xp_ref, xs_ref, yp_ref, ys_ref, w_ref, mpg1, mpsh2, mpsc2, msg1, mssh2, mssc2, nffn_ref, rw_ref, rb_ref,
            x1_ref, h2_ref, gate_ref, pos_ref, post_ref, meta_ref):
    i = pl.program_id(0)
    isp = i < B_PROMPT_BLOCKS
    ycat = jnp.where(isp, yp_ref[...], ys_ref[...])
    o = jnp.dot(ycat, w_ref[...], preferred_element_type=_f32)
    x1 = jnp.where(isp, xp_ref[...], xs_ref[...]) + _slab_mul(o, _pick_mod(isp, mpg1, msg1))
    x1_ref[...] = x1
    hn = _rms(x1) * nffn_ref[...]
    h2 = _slab_add(_slab_mul(hn, 1.0 + _pick_mod(isp, mpsc2, mssc2)), _pick_mod(isp, mpsh2, mssh2))
    hi = h2.astype(_bf16)
    h2_ref[...] = hi
    lo = (h2 - hi.astype(_f32)).astype(_bf16)
    l1 = jnp.dot(hi, rw_ref[...], preferred_element_type=_f32)
    l2 = jnp.dot(lo, rw_ref[:, 0:LANES], preferred_element_type=_f32)
    logits = l1[:, 0:LANES] + l1[:, LANES:] + l2 + rb_ref[...]
    lane = lax.broadcasted_iota(jnp.int32, (B_TM, LANES), 1).astype(_f32)
    neg = jnp.float32(-jnp.inf)
    cur = jnp.where(lane < N_EXPERTS, logits, neg)
    vals, ids = [], []
    for _ in range(TOP_K):
        m = jnp.max(cur, axis=-1, keepdims=True)
        ix = jnp.min(jnp.where(cur == m, lane, float(LANES)), axis=-1, keepdims=True)
        vals.append(m)
        ids.append(ix)
        cur = jnp.where(lane == ix, neg, cur)
    es = [jnp.exp(v - vals[0]) for v in vals]
    tot = es[0] + es[1] + es[2] + es[3]
    gate = jnp.zeros((B_TM, LANES), _f32)
    for k in range(TOP_K):
        gate = jnp.where(lane == float(k), es[k] / tot, gate)
    gate_ref[...] = gate

    hit = [lane == ix for ix in ids]
    chosen = jnp.zeros((B_TM, LANES), _f32)
    for h in hit:
        chosen = jnp.where(h, 1.0, chosen)
    cnt = jnp.sum(chosen, axis=0, keepdims=True)
    seg_rows = jnp.floor((cnt + (SEG - 1.0)) * (1.0 / SEG)) * SEG
    er = lax.broadcasted_iota(jnp.int32, (LANES, LANES), 0)
    ec = lax.broadcasted_iota(jnp.int32, (LANES, LANES), 1)
    before = jnp.where(er < ec, 1.0, 0.0).astype(_bf16)
    seg_start = jnp.dot(jnp.broadcast_to(seg_rows, (8, LANES)).astype(_bf16), before,
                        preferred_element_type=_f32)[0:1, :]
    tr = lax.broadcasted_iota(jnp.int32, (B_TM, B_TM), 0)
    tc = lax.broadcasted_iota(jnp.int32, (B_TM, B_TM), 1)
    earlier = jnp.where(tr > tc, 1.0, 0.0).astype(_bf16)
    rank = jnp.dot(earlier, chosen.astype(_bf16), preferred_element_type=_f32)
    base = seg_start + rank
    pos = jnp.zeros((B_TM, LANES), _f32)
    for k in range(TOP_K):
        pk = jnp.sum(jnp.where(hit[k], base, 0.0), axis=-1, keepdims=True)
        pos = jnp.where(lane == float(k), pk, pos)
    pos_ref[...] = pos
    post_ref[...] = jnp.transpose(pos)[0:8, :]
    sub = lax.broadcasted_iota(jnp.int32, (8, LANES), 0)
    meta = jnp.where(sub == 0, jnp.broadcast_to(seg_start, (8, LANES)),
                     jnp.where(sub == 1, jnp.broadcast_to(seg_rows, (8, LANES)), 0.0))
    meta_ref[...] = meta.astype(jnp.int32)


def _stage_b(l, x_p, x_s, xs_blk, ycat_p, ycat_s, w_out_b, mod_p4, mod_s, p):
    def pb(i):
        return jnp.minimum(i // (SEQ // B_TM), BATCH - 1)

    def mp_spec(j):
        return pl.BlockSpec((None, None, 1, D_MODEL), lambda i: (l, pb(i), 0, j))

    def ms_spec(j):
        return pl.BlockSpec((None, DEC_BATCH, D_MODEL), lambda i: (l, 0, j))

    row_spec = pl.BlockSpec((B_TM, D_MODEL), lambda i: (i, 0))
    in_specs = [
        pl.BlockSpec((B_TM, D_MODEL), lambda i: (jnp.minimum(i, B_PROMPT_BLOCKS - 1), 0)),
        pl.BlockSpec((T_SAMPLE, D_MODEL), lambda i: (xs_blk, 0), pipeline_mode=pl.Buffered(1)),
        pl.BlockSpec((B_TM, D_MODEL), lambda i: (jnp.minimum(i, B_PROMPT_BLOCKS - 1), 0)),
        pl.BlockSpec((T_SAMPLE, D_MODEL), lambda i: (0, 0), pipeline_mode=pl.Buffered(1)),
        pl.BlockSpec((None, D_MODEL, D_MODEL), lambda i: (l, 0, 0), pipeline_mode=pl.Buffered(1)),
        mp_spec(2), mp_spec(3), mp_spec(4), ms_spec(2), ms_spec(3), ms_spec(4),
        _layer_spec((1, D_MODEL), l, 1),
        _layer_spec((D_MODEL, 2 * LANES), l, 1),
        _layer_spec((1, LANES), l, 1),
    ]
    lane_spec = pl.BlockSpec((B_TM, LANES), lambda i: (i, 0))
    return pl.pallas_call(
        _b_body,
        grid=(N_BLOCKS,),
        in_specs=in_specs,
        out_specs=[row_spec, row_spec, lane_spec, lane_spec,
                   pl.BlockSpec((None, 8, B_TM), lambda i: (i, 0, 0)),
                   pl.BlockSpec((None, 8, LANES), lambda i: (i, 0, 0))],
        out_shape=[
            jax.ShapeDtypeStruct((T_ALL, D_MODEL), _f32),
            jax.ShapeDtypeStruct((T_ALL, D_MODEL), _bf16),
            jax.ShapeDtypeStruct((T_ALL, LANES), _f32),
            jax.ShapeDtypeStruct((T_ALL, LANES), _f32),
            jax.ShapeDtypeStruct((N_BLOCKS, 8, B_TM), _f32),
            jax.ShapeDtypeStruct((N_BLOCKS, 8, LANES), jnp.int32),
        ],
        compiler_params=pltpu.CompilerParams(
            dimension_semantics=("arbitrary",), vmem_limit_bytes=VMEM_LIMIT),
        name="outproj_router",
    )(x_p, x_s, ycat_p, ycat_s, w_out_b, mod_p4, mod_p4, mod_p4, mod_s, mod_s, mod_s, p["nffn"], p["rw"], p["rb"])


def _sort_body(post_ref, h_ref, gate_ref, o_ref, go_ref):
    c = pl.program_id(1)
    j = (c * SORT_TM + lax.broadcasted_iota(jnp.int32, (SORT_TM, B_TM), 0)).astype(_f32)
    lane = lax.broadcasted_iota(jnp.int32, (B_TM, LANES), 1)
    sel = jnp.zeros((SORT_TM, B_TM), _f32)
    gs = jnp.zeros((SORT_TM, LANES), _f32)
    for k in range(TOP_K):
        hit = j == post_ref[k:k + 1, :]
        sel = jnp.where(hit, 1.0, sel)
        g = gate_ref[:, k:k + 1]
        g0 = g.astype(_bf16).astype(_f32)
        g1 = (g - g0).astype(_bf16).astype(_f32)
        g2 = g - g0 - g1
        terms = jnp.where(lane == 0, g0, jnp.where(lane == 1, g1, jnp.where(lane == 2, g2, 0.0)))
        gs = gs + jnp.dot(jnp.where(hit, 1.0, 0.0).astype(_bf16), terms.astype(_bf16),
                          preferred_element_type=_f32)
    o_ref[...] = jnp.dot(sel.astype(_bf16), h_ref[...], preferred_element_type=_f32).astype(_bf16)
    go_ref[...] = gs


def _local_sort(post, h2, gates):
    return pl.pallas_call(
        _sort_body,
        grid=(N_BLOCKS, SORT_CHUNKS),
        in_specs=[
            pl.BlockSpec((None, 8, B_TM), lambda b, c: (b, 0, 0)),
            pl.BlockSpec((B_TM, D_MODEL), lambda b, c: (b, 0)),
            pl.BlockSpec((B_TM, LANES), lambda b, c: (b, 0)),
        ],
        out_specs=[pl.BlockSpec((SORT_TM, D_MODEL), lambda b, c: (b * SORT_CHUNKS + c, 0)),
                   pl.BlockSpec((SORT_TM, LANES), lambda b, c: (b * SORT_CHUNKS + c, 0))],
        out_shape=[jax.ShapeDtypeStruct((N_BLOCKS * BLOCK_ROWS, D_MODEL), _bf16),
                   jax.ShapeDtypeStruct((N_BLOCKS * BLOCK_ROWS, LANES), _f32)],
        compiler_params=pltpu.CompilerParams(
            dimension_semantics=("arbitrary", "arbitrary"), vmem_limit_bytes=VMEM_LIMIT),
        name="local_sort",
    )(post, h2, gates)


def _moe_body(ie_ref, ilo_ref, int_ref, start_ref, rows_ref, used_ref,
              wg_ref, wl_ref, wd_ref, bg_ref, bl_ref, bd_ref, x_hbm, g_hbm, y_hbm,
              xbuf, hbuf, gbuf, wc, seg_src, nseg, pending, sem_x, sem_y):
    i = pl.program_id(0)
    n = pl.program_id(1)
    e = ie_ref[i]
    lo = ilo_ref[i]
    nt = int_ref[i]
    hi = lo + nt * MOE_TM
    tm, tn = MOE_TM, MOE_TN

    def x_copy(src, dst):
        return pltpu.make_async_copy(x_hbm.at[pl.ds(src, SEG)], xbuf.at[pl.ds(dst, SEG)], sem_x.at[dst // tm])

    def g_copy(src, dst):
        return pltpu.make_async_copy(g_hbm.at[pl.ds(src, SEG)], gbuf.at[pl.ds(dst, SEG)], sem_x.at[dst // tm])

    def y_copy(src, dst):
        return pltpu.make_async_copy(xbuf.at[pl.ds(src, SEG)], y_hbm.at[pl.ds(dst, SEG)], sem_y.at[0])

    def drain():
        def w(_, carry):
            y_copy(0, 0).wait()
            return carry
        lax.fori_loop(0, pending[0], w, 0)
        pending[0] = 0

    @pl.when((i == 0) & (n == 0))
    def _():
        pending[0] = 0

    @pl.when((n == 0) & (nt > 0))
    def _():
        drain()

        def per_block(b, p):
            rows = rows_ref[b * N_EXPERTS + e]
            start = start_ref[b * N_EXPERTS + e]

            def per_piece(g, carry):
                q = p + g * SEG

                @pl.when((q >= lo) & (q < hi))
                def _():
                    src = pl.multiple_of(b * BLOCK_ROWS + start + g * SEG, SEG)
                    dst = pl.multiple_of(q - lo, SEG)
                    seg_src[dst // SEG] = src
                    x_copy(src, dst).start()
                    g_copy(src, dst).start()
                return carry
            lax.fori_loop(0, rows // SEG, per_piece, 0)
            return p + rows
        end = lax.fori_loop(0, N_BLOCKS, per_block, 0)
        ns = (jnp.minimum(end, hi) - lo) // SEG
        nseg[0] = ns

        def zero_piece(g, carry):
            r = pl.multiple_of(g * SEG, SEG)
            xbuf[pl.ds(r, SEG), :] = jnp.zeros((SEG, D_MODEL), _bf16)
            gbuf[pl.ds(r, SEG), :] = jnp.zeros((SEG, LANES), _f32)
            return carry
        lax.fori_loop(ns, nt * TILE_SEGS, zero_piece, 0)

    def tile_pieces(t):
        return jnp.clip(nseg[0] - t * TILE_SEGS, 0, TILE_SEGS)

    for c in range(MOE_UP_STEPS):
        @pl.when((n == c) & (nt > 0))
        def _(c=c):
            wc[:, 0:tn] = wg_ref[...].astype(_bf16)
            wc[:, tn:] = wl_ref[...].astype(_bf16)

            def body(t, carry):
                if c == 0:
                    def w(_, cc):
                        x_copy(0, pl.multiple_of(t * tm, tm)).wait()
                        g_copy(0, pl.multiple_of(t * tm, tm)).wait()
                        return cc
                    lax.fori_loop(0, tile_pieces(t), w, 0)
                r0 = pl.multiple_of(t * tm, tm)
                up = jnp.dot(xbuf[pl.ds(r0, tm), :], wc[...], preferred_element_type=_f32)
                glu = jnp.minimum(up[:, 0:tn] + bg_ref[...], SWIGLU_LIMIT)
                lin = jnp.clip(up[:, tn:] + bl_ref[...], -SWIGLU_LIMIT, SWIGLU_LIMIT)
                act = glu * jax.nn.sigmoid(SWIGLU_ALPHA * glu) * (lin + 1.0)
                hbuf[pl.ds(r0, tm), c * tn:(c + 1) * tn] = act.astype(_bf16)
                return carry
            lax.fori_loop(0, nt, body, 0)

    for c in range(MOE_DOWN_STEPS):
        @pl.when((n == MOE_UP_STEPS + c) & (nt > 0))
        def _(c=c):
            wc[:, 0:tn] = wd_ref[...].astype(_bf16)

            def body(t, carry):
                r0 = pl.multiple_of(t * tm, tm)
                y = jnp.dot(hbuf[pl.ds(r0, tm), :], wc[:, 0:tn], preferred_element_type=_f32) + bd_ref[...]
                g = gbuf[pl.ds(r0, tm), :]
                gate = g[:, 0:1] + g[:, 1:2] + g[:, 2:3]
                xbuf[pl.ds(r0, tm), c * tn:(c + 1) * tn] = (y * gate).astype(_bf16)
                if c == MOE_DOWN_STEPS - 1:
                    np_ = tile_pieces(t)
                    for q in range(TILE_SEGS):
                        @pl.when(q < np_)
                        def _(q=q):
                            y_copy(pl.multiple_of(r0 + q * SEG, SEG),
                                   pl.multiple_of(seg_src[t * TILE_SEGS + q], SEG)).start()
                return carry
            lax.fori_loop(0, nt, body, 0)
            if c == MOE_DOWN_STEPS - 1:
                pending[0] = nseg[0]

    @pl.when((i == MOE_ITEMS - 1) & (n == MOE_STEPS - 1))
    def _():
        drain()
        hbuf[0:SEG, :] = jnp.zeros((SEG, D_EXPERT), _bf16)

        def zero_copy(row):
            return pltpu.make_async_copy(hbuf.at[pl.ds(0, SEG)], y_hbm.at[pl.ds(row, SEG)], sem_y.at[0])

        def per_block(b, carry):
            def start(g, cc):
                zero_copy(pl.multiple_of(b * BLOCK_ROWS + g * SEG, SEG)).start()
                return cc

            def wait(g, cc):
                zero_copy(0).wait()
                return cc
            first = used_ref[b] // SEG
            lax.fori_loop(first, BLOCK_ROWS // SEG, start, 0)
            lax.fori_loop(first, BLOCK_ROWS // SEG, wait, 0)
            return carry
        lax.fori_loop(0, N_BLOCKS, per_block, 0)


def _moe(l, item_e, item_lo, item_nt, seg_start, seg_rows, used, w_up, b_up4, w_down, b_down4, x_sorted,
         g_sorted):
    def col_up(n, valid):
        return jnp.where(valid, jnp.minimum(n, MOE_UP_STEPS - 1), MOE_UP_STEPS - 1)

    def col_down(n, valid):
        return jnp.where(valid, jnp.maximum(n - MOE_UP_STEPS, 0), MOE_DOWN_STEPS - 1)

    def glu_map(i, n, ie, ilo, int_, *_):
        return (l, ie[i], 0, col_up(n, int_[i] > 0))

    def lin_map(i, n, ie, ilo, int_, *_):
        return (l, ie[i], 0, MOE_UP_STEPS + col_up(n, int_[i] > 0))

    def down_map(i, n, ie, ilo, int_, *_):
        return (l, ie[i], 0, col_down(n, int_[i] > 0))

    grid_spec = pltpu.PrefetchScalarGridSpec(
        num_scalar_prefetch=6,
        grid=(MOE_ITEMS, MOE_STEPS),
        in_specs=[
            pl.BlockSpec((None, None, D_MODEL, MOE_TN), glu_map),
            pl.BlockSpec((None, None, D_MODEL, MOE_TN), lin_map),
            pl.BlockSpec((None, None, D_EXPERT, MOE_TN), down_map),
            pl.BlockSpec((None, None, 1, MOE_TN), glu_map),
            pl.BlockSpec((None, None, 1, MOE_TN), lin_map),
            pl.BlockSpec((None, None, 1, MOE_TN), down_map),
            pl.BlockSpec(memory_space=pl.ANY),
            pl.BlockSpec(memory_space=pl.ANY),
        ],
        out_specs=pl.BlockSpec(memory_space=pl.ANY),
        scratch_shapes=[
            pltpu.VMEM((MOE_MT * MOE_TM, D_MODEL), _bf16),
            pltpu.VMEM((MOE_MT * MOE_TM, D_EXPERT), _bf16),
            pltpu.VMEM((MOE_MT * MOE_TM, LANES), _f32),
            pltpu.VMEM((D_MODEL, 2 * MOE_TN), _bf16),
            pltpu.SMEM((MOE_MT * TILE_SEGS,), jnp.int32),
            pltpu.SMEM((1,), jnp.int32),
            pltpu.SMEM((1,), jnp.int32),
            pltpu.SemaphoreType.DMA((MOE_MT,)),
            pltpu.SemaphoreType.DMA((1,)),
        ],
    )
    return pl.pallas_call(
        _moe_body,
        grid_spec=grid_spec,
        out_shape=jax.ShapeDtypeStruct((N_BLOCKS * BLOCK_ROWS, D_MODEL), _bf16),
        compiler_params=pltpu.CompilerParams(
            dimension_semantics=("arbitrary", "arbitrary"), vmem_limit_bytes=VMEM_LIMIT),
        name="moe_experts",
    )(item_e, item_lo, item_nt, seg_start, seg_rows, used, w_up, w_up, w_down, b_up4, b_up4, b_down4, x_sorted,
      g_sorted)


def _work_items(meta):
    seg_start = meta[:, 0, :N_EXPERTS]
    seg_rows = meta[:, 1, :N_EXPERTS]
    rows_e = jnp.sum(seg_rows, axis=0)
    nt = (rows_e + MOE_TM - 1) // MOE_TM
    ipe = (nt + MOE_MT - 1) // MOE_MT
    cum = jnp.cumsum(ipe)
    first = cum - ipe
    total = cum[-1]
    ii = jnp.arange(MOE_ITEMS, dtype=jnp.int32)
    ii_c = jnp.minimum(ii, total - 1)
    e_i = jnp.minimum(jnp.sum((cum[None, :] <= ii_c[:, None]).astype(jnp.int32), axis=1), N_EXPERTS - 1)
    valid = ii < total
    j = ii - first[e_i]
    lo_i = jnp.where(valid, j * (MOE_MT * MOE_TM), 0)
    nt_i = jnp.where(valid, jnp.minimum(MOE_MT, nt[e_i] - j * MOE_MT), 0)
    used = jnp.sum(seg_rows, axis=1)
    i32 = jnp.int32
    return (e_i.astype(i32), lo_i.astype(i32), nt_i.astype(i32), seg_start.reshape(-1).astype(i32),
            seg_rows.reshape(-1).astype(i32), used.astype(i32))


def _c_body(x1_ref, y_ref, pos_ref, mpg2, msg2, nfin_ref, *rest, final):
    if final:
        outp_ref, outs_ref, acc_ref = rest
    else:
        out_ref, acc_ref = rest
    i = pl.program_id(0)
    c = pl.program_id(1)
    isp = i < B_PROMPT_BLOCKS
    j = (c * SORT_TM + lax.broadcasted_iota(jnp.int32, (B_TM, SORT_TM), 1)).astype(_f32)
    sel = jnp.zeros((B_TM, SORT_TM), _f32)
    for k in range(TOP_K):
        sel = jnp.where(j == pos_ref[:, k:k + 1], 1.0, sel)
    part = jnp.dot(sel.astype(_bf16), y_ref[...], preferred_element_type=_f32)

    @pl.when(c == 0)
    def _():
        acc_ref[...] = part

    @pl.when(c > 0)
    def _():
        acc_ref[...] += part

    @pl.when(c == SORT_CHUNKS - 1)
    def _():
        x2 = x1_ref[...] + _slab_mul(acc_ref[...], _pick_mod(isp, mpg2, msg2))
        if final:
            y = _rms(x2) * nfin_ref[...]

            @pl.when(isp)
            def _():
                outp_ref[...] = y

            @pl.when(jnp.logical_not(isp))
            def _():
                outs_ref[...] = y
        else:
            out_ref[...] = x2


def _combine(l, x1, y_sorted, pos, mod_p4, mod_s, nfin, final):
    def pb(i):
        return jnp.minimum(i // (SEQ // B_TM), BATCH - 1)

    row_spec = pl.BlockSpec((B_TM, D_MODEL), lambda i, c: (i, 0))
    lane_spec = pl.BlockSpec((B_TM, LANES), lambda i, c: (i, 0))
    if final:
        out_specs = [pl.BlockSpec((B_TM, D_MODEL), lambda i, c: (jnp.minimum(i, B_PROMPT_BLOCKS - 1), 0)),
                     pl.BlockSpec((T_SAMPLE, D_MODEL), lambda i, c: (0, 0))]
        out_shape = [jax.ShapeDtypeStruct((T_PROMPT, D_MODEL), _f32),
                     jax.ShapeDtypeStruct((T_SAMPLE, D_MODEL), _f32)]
    else:
        out_specs = row_spec
        out_shape = jax.ShapeDtypeStruct((T_ALL, D_MODEL), _f32)
    return pl.pallas_call(
        functools.partial(_c_body, final=final),
        grid=(N_BLOCKS, SORT_CHUNKS),
        in_specs=[
            row_spec,
            pl.BlockSpec((SORT_TM, D_MODEL), lambda i, c: (i * SORT_CHUNKS + c, 0)),
            lane_spec,
            pl.BlockSpec((None, None, 1, D_MODEL), lambda i, c: (l, pb(i), 0, 5)),
            pl.BlockSpec((None, DEC_BATCH, D_MODEL), lambda i, c: (l, 0, 5)),
            pl.BlockSpec((1, D_MODEL), lambda i, c: (0, 0)),
        ],
        out_specs=out_specs,
        out_shape=out_shape,
        scratch_shapes=[pltpu.VMEM((B_TM, D_MODEL), _f32)],
        compiler_params=pltpu.CompilerParams(
            dimension_semantics=("arbitrary", "arbitrary"), vmem_limit_bytes=VMEM_LIMIT),
        name="moe_combine",
    )(x1, y_sorted, pos, mod_p4, mod_s, nfin)


def kernel(x_prompt, x_sample, c_prompt, c_sample, state_pool, state_conformer, state_shortconv, w_ada, b_ada, norm_mix, norm_ffn, norm_final, w_in, pool_w, pool_scale, sgu_norm, sgu_w, sgu_b, conf_w, conf_b, conf_ln_g, conf_ln_b, sc_w, out_norm, w_out, router_w, router_b, w_up, b_up, w_down, b_down):
    gw = GROUP_WIDTH
    ns, nb = DEC_SEQ, DEC_BATCH

    c_all = jnp.concatenate([c_sample, c_prompt, jnp.zeros((ADA_ROWS - nb - BATCH, D_MODEL), _f32)], axis=0)
    mod_s, mod_p8 = _ada(c_all, w_ada, b_ada)
    mod_p4 = mod_p8[:, :BATCH].reshape(DEPTH, BATCH, 1, 6 * D_MODEL)

    rw_hi = router_w.astype(_bf16)
    rw_lo = (router_w - rw_hi.astype(_f32)).astype(_bf16)
    lane_pad = ((0, 0), (0, 0), (0, LANES - N_EXPERTS))
    params = {
        "nmix": norm_mix.reshape(DEPTH, 1, D_MODEL),
        "nffn": norm_ffn.reshape(DEPTH, 1, D_MODEL),
        "poolw": pool_w,
        "pscale": pool_scale.reshape(DEPTH, 1, gw),
        "sgun": sgu_norm.reshape(DEPTH, 1, gw),
        "sguw": sgu_w,
        "sgub": jnp.repeat(jnp.swapaxes(sgu_b, 1, 2), SGU_CHUNK, axis=2),
        "sgwc": jnp.repeat(jnp.transpose(sgu_w[:, :, :ns, :ns], (0, 2, 3, 1)).reshape(DEPTH, ns * ns, SGU_HEADS),
                           SGU_CHUNK, axis=2),
        "sgbc": jnp.repeat(jnp.swapaxes(sgu_b[:, :, :ns], 1, 2), SGU_CHUNK, axis=2),
        "cw": conf_w,
        "cb": conf_b.reshape(DEPTH, 1, gw),
        "clg": conf_ln_g.reshape(DEPTH, 1, gw),
        "clb": conf_ln_b.reshape(DEPTH, 1, gw),
        "scw": sc_w,
        "onorm": out_norm.reshape(DEPTH, 1, D_MODEL),
        "rw": jnp.concatenate([jnp.pad(rw_hi, lane_pad), jnp.pad(rw_lo, lane_pad)], axis=2),
        "rb": jnp.pad(router_b, ((0, 0), (0, LANES - N_EXPERTS))).reshape(DEPTH, 1, LANES),
    }
    w_in_b = w_in.astype(_bf16)
    w_out_b = w_out.astype(_bf16)
    b_up4 = b_up.reshape(DEPTH, N_EXPERTS, 1, 2 * D_EXPERT)
    b_down4 = b_down.reshape(DEPTH, N_EXPERTS, 1, D_MODEL)
    nfin = norm_final.reshape(1, D_MODEL)

    pool_t = jnp.transpose(state_pool, (0, 2, 1, 3))
    conf_t = jnp.transpose(state_conformer, (0, 2, 1, 3))
    sc_t = jnp.transpose(state_shortconv, (0, 2, 1, 3))

    x_p = x_prompt.reshape(T_PROMPT, D_MODEL)
    x_s = jnp.transpose(x_sample, (1, 0, 2)).reshape(T_SAMPLE, D_MODEL)
    xs_blk = 0

    pool_p, conf_p, sc_p, pool_s, conf_s, sc_s, v_s = [], [], [], [], [], [], []
    for l in range(DEPTH):
        ycat_p, po, co, so = _a_prompt(l, x_p, mod_p4, w_in_b, params)
        ycat_s, a_new, g_new, z_new, v_new = _a_sample(l, x_s, xs_blk, mod_s, w_in_b, params, pool_t, conf_t, sc_t)
        pool_p.append(po[:, POOL_HALO - (POOL_MAX - 1):])
        conf_p.append(co[:, CONF_HALO - (CONF_KERNEL - 1):])
        sc_p.append(so[:, SC_HALO - (SC_KERNEL - 1):])
        pool_s.append(jnp.concatenate([state_pool[l][:, ns:], jnp.transpose(a_new, (1, 0, 2))], axis=1))
        conf_s.append(jnp.concatenate([state_conformer[l][:, ns:], jnp.transpose(g_new, (1, 0, 2))], axis=1))
        sc_s.append(jnp.transpose(z_new, (1, 0, 2))[:, ns - (SC_KERNEL - 1):])
        v_s.append(jnp.transpose(v_new, (1, 0, 2)))

        x1, h2, gates, pos, post, meta = _stage_b(l, x_p, x_s, xs_blk, ycat_p, ycat_s, w_out_b, mod_p4, mod_s,
                                                  params)
        x_sorted, g_sorted = _local_sort(post, h2, gates)
        item_e, item_lo, item_nt, seg_start, seg_rows, used = _work_items(meta)
        y_sorted = _moe(l, item_e, item_lo, item_nt, seg_start, seg_rows, used, w_up, b_up4, w_down, b_down4,
                        x_sorted, g_sorted)
        if l < DEPTH - 1:
            x_p = x_s = _combine(l, x1, y_sorted, pos, mod_p4, mod_s, nfin, final=False)
            xs_blk = T_PROMPT // T_SAMPLE
        else:
            y_p, y_s = _combine(l, x1, y_sorted, pos, mod_p4, mod_s, nfin, final=True)

    y_prompt = y_p.reshape(BATCH, SEQ, D_MODEL)
    y_sample = jnp.transpose(y_s.reshape(ns, nb, D_MODEL), (1, 0, 2))
    return (y_prompt, y_sample, jnp.stack(pool_p), jnp.stack(pool_s), jnp.stack(conf_p), jnp.stack(conf_s),
            jnp.stack(sc_p), jnp.stack(sc_s), jnp.stack(v_s))
```

```python
import functools

import jax
import jax.numpy as jnp
from jax import lax
from jax.experimental import pallas as pl
from jax.experimental.pallas import tpu as pltpu

D_MODEL = 2048
BATCH = 4
SEQ = 2048
DEPTH = 2
DEC_BATCH = 128
DEC_SEQ = 4
PAST_LEN = 16384
GROUP_WIDTH = 512
POOL_WINDOWS = (2, 4, 8, 16)
POOL_MAX = 16
POOL_CH = 128
SGU_HEADS = 4
SGU_CHUNK = 128
CONF_KERNEL = 31
SC_KERNEL = 3
IN_COLS = 8 * GROUP_WIDTH
N_EXPERTS = 32
TOP_K = 4
D_EXPERT = 2048
SWIGLU_ALPHA = 1.702
SWIGLU_LIMIT = 7.0
EPS = 1e-5

T_PROMPT = BATCH * SEQ
T_SAMPLE = DEC_BATCH * DEC_SEQ
T_ALL = T_PROMPT + T_SAMPLE

LANES = 128
BF16_ROWS = 16
VMEM_LIMIT = 56 * 1024 * 1024

A_TM = 1024
A_TN = 512
A_STEPS = IN_COLS // A_TN
POOL_HALO = 16
CONF_HALO = 32
SC_HALO = 8

B_TM = 512
N_BLOCKS = T_ALL // B_TM
B_PROMPT_BLOCKS = T_PROMPT // B_TM
SEG = BF16_ROWS
SORT_TM = 512
BLOCK_ROWS = -(-(B_TM * TOP_K + N_EXPERTS * (SEG - 1)) // SORT_TM) * SORT_TM
SORT_CHUNKS = BLOCK_ROWS // SORT_TM

MOE_TM = 256
MOE_MT = 8
MOE_TN = 512
MOE_UP_STEPS = D_EXPERT // MOE_TN
MOE_DOWN_STEPS = D_MODEL // MOE_TN
MOE_STEPS = MOE_UP_STEPS + MOE_DOWN_STEPS
MOE_YSLOTS = 4
TILE_SEGS = MOE_TM // SEG
MOE_TILES = (T_ALL * TOP_K + N_BLOCKS * N_EXPERTS * (SEG - 1) + N_EXPERTS * (MOE_TM - 1)) // MOE_TM
MOE_ITEMS = (MOE_TILES + N_EXPERTS * (MOE_MT - 1)) // MOE_MT

_f32 = jnp.float32
_bf16 = jnp.bfloat16


def _rms(x):
    return x * lax.rsqrt(jnp.mean(x * x, axis=-1, keepdims=True) + EPS)


def _ln(x):
    xc = x - jnp.mean(x, axis=-1, keepdims=True)
    return xc * lax.rsqrt(jnp.mean(xc * xc, axis=-1, keepdims=True) + EPS)


def _gelu(x):
    return 0.5 * x * (1.0 + lax.erf(x * (0.5 ** 0.5)))


def _slab_mul(x, m):
    r, c = x.shape
    return (x.reshape(r // 128, 128, c) * m[None]).reshape(r, c)


def _slab_add(x, m):
    r, c = x.shape
    return (x.reshape(r // 128, 128, c) + m[None]).reshape(r, c)


ADA_TN = 1024
ADA_ROWS = DEC_BATCH + 8


def _ada_body(c_ref, w_ref, b_ref, os_ref, op_ref):
    c = c_ref[...]
    a = (c * jax.nn.sigmoid(c)).astype(_bf16)
    r = jnp.dot(a, w_ref[...].astype(_bf16), preferred_element_type=_f32) + b_ref[...]
    os_ref[...] = r[:DEC_BATCH]
    op_ref[...] = r[DEC_BATCH:]


def _ada(c_all, w_ada, b_ada):
    nj = 6 * D_MODEL // ADA_TN
    return pl.pallas_call(
        _ada_body,
        grid=(DEPTH, nj),
        in_specs=[
            pl.BlockSpec((ADA_ROWS, D_MODEL), lambda l, j: (0, 0)),
            pl.BlockSpec((None, D_MODEL, ADA_TN), lambda l, j: (l, 0, j)),
            pl.BlockSpec((None, 1, ADA_TN), lambda l, j: (l, 0, j)),
        ],
        out_specs=[
            pl.BlockSpec((None, DEC_BATCH, ADA_TN), lambda l, j: (l, 0, j)),
            pl.BlockSpec((None, 8, ADA_TN), lambda l, j: (l, 0, j)),
        ],
        out_shape=[
            jax.ShapeDtypeStruct((DEPTH, DEC_BATCH, 6 * D_MODEL), _f32),
            jax.ShapeDtypeStruct((DEPTH, 8, 6 * D_MODEL), _f32),
        ],
        compiler_params=pltpu.CompilerParams(
            dimension_semantics=("arbitrary", "arbitrary"), vmem_limit_bytes=VMEM_LIMIT),
        name="ada_mod",
    )(c_all, w_ada, b_ada.reshape(DEPTH, 1, 6 * D_MODEL))


def _a_prompt_body(x_ref, sh1_ref, sc1_ref, nmix_ref, w_ref, poolw_ref, pscale_ref, sgun_ref, sguw_ref,
                   sgub_ref, cw_ref, cb_ref, clg_ref, clb_ref, scw_ref, onorm_ref,
                   y_ref, pool_o, conf_o, sc_o,
                   h_ref, bufa, bufb, exta, extc, extz):
    i = pl.program_id(1)
    n = pl.program_id(2)
    tm = A_TM
    gw = GROUP_WIDTH

    @pl.when(n == 0)
    def _():
        hn = _rms(x_ref[...]) * nmix_ref[...]
        h_ref[...] = (hn * (1.0 + sc1_ref[...]) + sh1_ref[...]).astype(_bf16)

    def proj():
        return jnp.dot(h_ref[...], w_ref[...], preferred_element_type=_f32)

    def emit(g, y):
        yn = _rms(y) * onorm_ref[:, g * gw:(g + 1) * gw]
        y_ref[:, g * gw:(g + 1) * gw] = yn.astype(_bf16)

    def carry_halo(ext, halo):
        @pl.when(i == 0)
        def _():
            ext[0:halo, :] = jnp.zeros((halo, gw), _f32)

        @pl.when(i > 0)
        def _():
            ext[0:halo, :] = ext[tm:tm + halo, :]

    @pl.when(n == 0)
    def _():
        a = proj()
        carry_halo(exta, POOL_HALO)
        exta[POOL_HALO:, :] = a
        pool_o[...] = a[tm - POOL_HALO:, :]
        pos = (i * tm + lax.broadcasted_iota(jnp.int32, (tm, 1), 0)).astype(_f32)
        for gi, w in enumerate(POOL_WINDOWS):
            c0 = gi * POOL_CH
            s = exta[POOL_HALO:POOL_HALO + tm, c0:c0 + POOL_CH]
            for j in range(1, w):
                s = s + exta[POOL_HALO - j:POOL_HALO - j + tm, c0:c0 + POOL_CH]
            cnt = jnp.minimum(float(w), pos + 1.0)
            pooled = s / cnt - a[:, c0:c0 + POOL_CH]
            yg = jnp.dot(pooled.astype(_bf16), poolw_ref[gi].astype(_bf16), preferred_element_type=_f32)
            bufb[:, c0:c0 + POOL_CH] = yg
        emit(0, bufb[...] * pscale_ref[...])

    @pl.when(n == 1)
    def _():
        bufa[...] = _gelu(proj())

    @pl.when(n == 2)
    def _():
        v = _ln(_gelu(proj())) * sgun_ref[...]
        vb = v.astype(_bf16)
        row = lax.broadcasted_iota(jnp.int32, (SGU_CHUNK, SGU_CHUNK), 0)
        col = lax.broadcasted_iota(jnp.int32, (SGU_CHUNK, SGU_CHUNK), 1)
        for g in range(SGU_HEADS):
            c0 = g * SGU_CHUNK
            wg = jnp.where(row >= col, sguw_ref[g], 0.0).astype(_bf16)
            for c in range(tm // SGU_CHUNK):
                r0 = c * SGU_CHUNK
                m = jnp.dot(wg, vb[r0:r0 + SGU_CHUNK, c0:c0 + SGU_CHUNK], preferred_element_type=_f32)
                bufb[r0:r0 + SGU_CHUNK, c0:c0 + SGU_CHUNK] = m + sgub_ref[:, c0:c0 + SGU_CHUNK]
        emit(1, bufa[...] * bufb[...])

    @pl.when(n == 3)
    def _():
        bufa[...] = proj()

    @pl.when(n == 4)
    def _():
        g = bufa[...] * jax.nn.sigmoid(proj())
        carry_halo(extc, CONF_HALO)
        extc[CONF_HALO:, :] = g
        conf_o[...] = g[tm - CONF_HALO:, :]
        off = CONF_HALO - (CONF_KERNEL - 1)
        acc = jnp.broadcast_to(cb_ref[...], (tm, gw))
        for k in range(CONF_KERNEL):
            acc = acc + cw_ref[k:k + 1, :] * extc[off + k:off + k + tm, :]
        yn = _ln(acc) * clg_ref[...] + clb_ref[...]
        emit(2, yn * jax.nn.sigmoid(yn))

    @pl.when(n == 5)
    def _():
        bufa[...] = proj()

    @pl.when(n == 6)
    def _():
        bufb[...] = proj()

    @pl.when(n == 7)
    def _():
        z = bufb[...] * proj()
        carry_halo(extz, SC_HALO)
        extz[SC_HALO:, :] = z
        sc_o[...] = z[tm - SC_HALO:, :]
        off = SC_HALO - (SC_KERNEL - 1)
        acc = scw_ref[0:1, :] * extz[off:off + tm, :]
        for k in range(1, SC_KERNEL):
            acc = acc + scw_ref[k:k + 1, :] * extz[off + k:off + k + tm, :]
        emit(3, bufa[...] * acc)


def _layer_spec(shape, l, nd):
    zeros = (0,) * len(shape)
    if nd == 1:
        return pl.BlockSpec((None,) + shape, lambda a: (l,) + zeros)
    if nd == 2:
        return pl.BlockSpec((None,) + shape, lambda a, b: (l,) + zeros)
    return pl.BlockSpec((None,) + shape, lambda a, b, c: (l,) + zeros)


def _mixer_params(l, p, nd):
    gw = GROUP_WIDTH
    arrs = [p["cw"], p["cb"], p["clg"], p["clb"], p["scw"], p["onorm"]]
    shapes = [(CONF_KERNEL, gw), (1, gw), (1, gw), (1, gw), (SC_KERNEL, gw), (1, D_MODEL)]
    return arrs, [_layer_spec(s, l, nd) for s in shapes]


def _a_prompt(l, x_all, mod_p4, w_in_b, p):
    gw = GROUP_WIDTH
    ni = SEQ // A_TM
    marrs, mspecs = _mixer_params(l, p, 3)
    in_specs = [
        pl.BlockSpec((A_TM, D_MODEL), lambda b, i, n: (b * ni + i, 0)),
        pl.BlockSpec((None, None, 1, D_MODEL), lambda b, i, n: (l, b, 0, 0)),
        pl.BlockSpec((None, None, 1, D_MODEL), lambda b, i, n: (l, b, 0, 1)),
        _layer_spec((1, D_MODEL), l, 3),
        pl.BlockSpec((None, D_MODEL, A_TN), lambda b, i, n: (l, 0, n)),
        _layer_spec((len(POOL_WINDOWS), POOL_CH, POOL_CH), l, 3),
        _layer_spec((1, gw), l, 3),
        _layer_spec((1, gw), l, 3),
        _layer_spec((SGU_HEADS, SGU_CHUNK, SGU_CHUNK), l, 3),
        _layer_spec((SGU_CHUNK, gw), l, 3),
    ] + mspecs
    out_specs = [
        pl.BlockSpec((A_TM, D_MODEL), lambda b, i, n: (b * ni + i, 0)),
        pl.BlockSpec((None, POOL_HALO, gw), lambda b, i, n: (b, 0, 0)),
        pl.BlockSpec((None, CONF_HALO, gw), lambda b, i, n: (b, 0, 0)),
        pl.BlockSpec((None, SC_HALO, gw), lambda b, i, n: (b, 0, 0)),
    ]
    out_shape = [
        jax.ShapeDtypeStruct((T_PROMPT, D_MODEL), _bf16),
        jax.ShapeDtypeStruct((BATCH, POOL_HALO, gw), _f32),
        jax.ShapeDtypeStruct((BATCH, CONF_HALO, gw), _f32),
        jax.ShapeDtypeStruct((BATCH, SC_HALO, gw), _f32),
    ]
    scratch = [
        pltpu.VMEM((A_TM, D_MODEL), _bf16),
        pltpu.VMEM((A_TM, gw), _f32),
        pltpu.VMEM((A_TM, gw), _f32),
        pltpu.VMEM((POOL_HALO + A_TM, gw), _f32),
        pltpu.VMEM((CONF_HALO + A_TM, gw), _f32),
        pltpu.VMEM((SC_HALO + A_TM, gw), _f32),
    ]
    return pl.pallas_call(
        _a_prompt_body,
        grid=(BATCH, ni, A_STEPS),
        in_specs=in_specs,
        out_specs=out_specs,
        out_shape=out_shape,
        scratch_shapes=scratch,
        compiler_params=pltpu.CompilerParams(
            dimension_semantics=("arbitrary", "arbitrary", "arbitrary"), vmem_limit_bytes=VMEM_LIMIT),
        name="mixers_prompt",
    )(x_all, mod_p4, mod_p4, p["nmix"], w_in_b, p["poolw"], p["pscale"], p["sgun"], p["sguw"], p["sgub"], *marrs)


def _a_sample_body(x_ref, sh1_ref, sc1_ref, nmix_ref, w_ref, poolw_ref, pscale_ref, sgun_ref, sgwc_ref,
                   sgbc_ref, cw_ref, cb_ref, clg_ref, clb_ref, scw_ref, onorm_ref, pst_ref, cst_ref, sst_ref,
                   y_ref, a_o, g_o, z_o, v_o,
                   h_ref, bufa, bufb):
    n = pl.program_id(0)
    gw = GROUP_WIDTH
    nb = DEC_BATCH
    ns = DEC_SEQ

    @pl.when(n == 0)
    def _():
        hn = _rms(x_ref[...]) * nmix_ref[...]
        h_ref[...] = _slab_add(_slab_mul(hn, 1.0 + sc1_ref[...]), sh1_ref[...]).astype(_bf16)

    def proj():
        return jnp.dot(h_ref[...], w_ref[...], preferred_element_type=_f32)

    def emit(g, y):
        yn = _rms(y) * onorm_ref[:, g * gw:(g + 1) * gw]
        y_ref[:, g * gw:(g + 1) * gw] = yn.astype(_bf16)

    def slab(x, s):
        return x[s * nb:(s + 1) * nb, :]

    @pl.when(n == 0)
    def _():
        a = proj()
        a_o[...] = a.reshape(ns, nb, gw)
        npast = POOL_MAX - 1
        for s in range(ns):
            for gi, w in enumerate(POOL_WINDOWS):
                c0 = gi * POOL_CH
                acc = None
                for j in range(w):
                    e = npast + s - j
                    term = pst_ref[e, :, c0:c0 + POOL_CH] if e < npast else slab(a, e - npast)[:, c0:c0 + POOL_CH]
                    acc = term if acc is None else acc + term
                cnt = min(float(w), PAST_LEN + s + 1.0)
                pooled = acc / cnt - slab(a, s)[:, c0:c0 + POOL_CH]
                bufb[s * nb:(s + 1) * nb, c0:c0 + POOL_CH] = pooled
        pooled = bufb[...].astype(_bf16)
        for gi in range(len(POOL_WINDOWS)):
            c0 = gi * POOL_CH
            bufa[:, c0:c0 + POOL_CH] = jnp.dot(pooled[:, c0:c0 + POOL_CH], poolw_ref[gi].astype(_bf16),
                                               preferred_element_type=_f32)
        emit(0, bufa[...] * pscale_ref[...])

    @pl.when(n == 1)
    def _():
        bufa[...] = _gelu(proj())

    @pl.when(n == 2)
    def _():
        v = _ln(_gelu(proj())) * sgun_ref[...]
        v_o[...] = v.reshape(ns, nb, gw)
        for t in range(ns):
            m = jnp.broadcast_to(sgbc_ref[t:t + 1, :], (nb, gw))
            for s in range(t + 1):
                m = m + sgwc_ref[t * ns + s:t * ns + s + 1, :] * slab(v, s)
            bufb[t * nb:(t + 1) * nb, :] = m
        emit(1, bufa[...] * bufb[...])

    @pl.when(n == 3)
    def _():
        bufa[...] = proj()

    @pl.when(n == 4)
    def _():
        g = bufa[...] * jax.nn.sigmoid(proj())
        g_o[...] = g.reshape(ns, nb, gw)
        npast = CONF_KERNEL - 1
        for s in range(ns):
            acc = jnp.broadcast_to(cb_ref[...], (nb, gw))
            for k in range(CONF_KERNEL):
                e = s + k
                term = cst_ref[e] if e < npast else slab(g, e - npast)
                acc = acc + cw_ref[k:k + 1, :] * term
            bufb[s * nb:(s + 1) * nb, :] = acc
        yn = _ln(bufb[...]) * clg_ref[...] + clb_ref[...]
        emit(2, yn * jax.nn.sigmoid(yn))

    @pl.when(n == 5)
    def _():
        bufa[...] = proj()

    @pl.when(n == 6)
    def _():
        bufb[...] = proj()

    @pl.when(n == 7)
    def _():
        z = bufb[...] * proj()
        z_o[...] = z.reshape(ns, nb, gw)
        npast = SC_KERNEL - 1
        for s in range(ns):
            acc = None
            for k in range(SC_KERNEL):
                e = s + k
                term = scw_ref[k:k + 1, :] * (sst_ref[e] if e < npast else slab(z, e - npast))
                acc = term if acc is None else acc + term
            bufb[s * nb:(s + 1) * nb, :] = acc
        emit(3, bufa[...] * bufb[...])


def _a_sample(l, x_all, x_blk, mod_s, w_in_b, p, pool_t, conf_t, sc_t):
    gw = GROUP_WIDTH
    ns, nb = DEC_SEQ, DEC_BATCH
    marrs, mspecs = _mixer_params(l, p, 1)
    in_specs = [
        pl.BlockSpec((T_SAMPLE, D_MODEL), lambda n: (x_blk, 0)),
        pl.BlockSpec((None, nb, D_MODEL), lambda n: (l, 0, 0)),
        pl.BlockSpec((None, nb, D_MODEL), lambda n: (l, 0, 1)),
        _layer_spec((1, D_MODEL), l, 1),
        pl.BlockSpec((None, D_MODEL, A_TN), lambda n: (l, 0, n)),
        _layer_spec((len(POOL_WINDOWS), POOL_CH, POOL_CH), l, 1),
        _layer_spec((1, gw), l, 1),
        _layer_spec((1, gw), l, 1),
        _layer_spec((ns * ns, gw), l, 1),
        _layer_spec((ns, gw), l, 1),
    ] + mspecs + [
        _layer_spec((POOL_MAX - 1, nb, gw), l, 1),
        _layer_spec((CONF_KERNEL - 1, nb, gw), l, 1),
        _layer_spec((SC_KERNEL - 1, nb, gw), l, 1),
    ]
    out_specs = [pl.BlockSpec((T_SAMPLE, D_MODEL), lambda n: (0, 0))] + [
        pl.BlockSpec((ns, nb, gw), lambda n: (0, 0, 0)) for _ in range(4)]
    out_shape = [jax.ShapeDtypeStruct((T_SAMPLE, D_MODEL), _bf16)] + [
        jax.ShapeDtypeStruct((ns, nb, gw), _f32) for _ in range(4)]
    scratch = [
        pltpu.VMEM((T_SAMPLE, D_MODEL), _bf16),
        pltpu.VMEM((T_SAMPLE, gw), _f32),
        pltpu.VMEM((T_SAMPLE, gw), _f32),
    ]
    return pl.pallas_call(
        _a_sample_body,
        grid=(A_STEPS,),
        in_specs=in_specs,
        out_specs=out_specs,
        out_shape=out_shape,
        scratch_shapes=scratch,
        compiler_params=pltpu.CompilerParams(
            dimension_semantics=("arbitrary",), vmem_limit_bytes=VMEM_LIMIT),
        name="mixers_sample",
    )(x_all, mod_s, mod_s, p["nmix"], w_in_b, p["poolw"], p["pscale"], p["sgun"], p["sgwc"], p["sgbc"],
      *marrs, pool_t, conf_t, sc_t)


def _pick_mod(is_prompt, mp_ref, ms_ref):
    return jnp.where(is_prompt, jnp.broadcast_to(mp_ref[...], (DEC_BATCH, D_MODEL)), ms_ref[...])


def _b_body(xp_ref, xs_ref, yp_ref, ys_ref, w_ref, mpg1, mpsh2, mpsc2, msg1, mssh2, mssc2, nffn_ref, rw_ref, rb_ref,
            x1_ref, h2_ref, gate_ref, pos_ref, post_ref, meta_ref):
    i = pl.program_id(0)
    isp = i < B_PROMPT_BLOCKS
    ycat = jnp.where(isp, yp_ref[...], ys_ref[...])
    o = jnp.dot(ycat, w_ref[...], preferred_element_type=_f32)
    x1 = jnp.where(isp, xp_ref[...], xs_ref[...]) + _slab_mul(o, _pick_mod(isp, mpg1, msg1))
    x1_ref[...] = x1
    hn = _rms(x1) * nffn_ref[...]
    h2 = _slab_add(_slab_mul(hn, 1.0 + _pick_mod(isp, mpsc2, mssc2)), _pick_mod(isp, mpsh2, mssh2))
    hi = h2.astype(_bf16)
    h2_ref[...] = hi
    lo = (h2 - hi.astype(_f32)).astype(_bf16)
    l1 = jnp.dot(hi, rw_ref[...], preferred_element_type=_f32)
    l2 = jnp.dot(lo, rw_ref[:, 0:LANES], preferred_element_type=_f32)
    logits = l1[:, 0:LANES] + l1[:, LANES:] + l2 + rb_ref[...]
    lane = lax.broadcasted_iota(jnp.int32, (B_TM, LANES), 1).astype(_f32)
    neg = jnp.float32(-jnp.inf)
    cur = jnp.where(lane < N_EXPERTS, logits, neg)
    vals, ids = [], []
    for _ in range(TOP_K):
        m = jnp.max(cur, axis=-1, keepdims=True)
        ix = jnp.min(jnp.where(cur == m, lane, float(LANES)), axis=-1, keepdims=True)
        vals.append(m)
        ids.append(ix)
        cur = jnp.where(lane == ix, neg, cur)
    es = [jnp.exp(v - vals[0]) for v in vals]
    tot = es[0] + es[1] + es[2] + es[3]
    gate = jnp.zeros((B_TM, LANES), _f32)
    for k in range(TOP_K):
        gate = jnp.where(lane == float(k), es[k] / tot, gate)
    gate_ref[...] = gate

    hit = [lane == ix for ix in ids]
    chosen = jnp.zeros((B_TM, LANES), _f32)
    for h in hit:
        chosen = jnp.where(h, 1.0, chosen)
    cnt = jnp.sum(chosen, axis=0, keepdims=True)
    seg_rows = jnp.floor((cnt + (SEG - 1.0)) * (1.0 / SEG)) * SEG
    er = lax.broadcasted_iota(jnp.int32, (LANES, LANES), 0)
    ec = lax.broadcasted_iota(jnp.int32, (LANES, LANES), 1)
    before = jnp.where(er < ec, 1.0, 0.0).astype(_bf16)
    seg_start = jnp.dot(jnp.broadcast_to(seg_rows, (8, LANES)).astype(_bf16), before,
                        preferred_element_type=_f32)[0:1, :]
    tr = lax.broadcasted_iota(jnp.int32, (B_TM, B_TM), 0)
    tc = lax.broadcasted_iota(jnp.int32, (B_TM, B_TM), 1)
    earlier = jnp.where(tr > tc, 1.0, 0.0).astype(_bf16)
    rank = jnp.dot(earlier, chosen.astype(_bf16), preferred_element_type=_f32)
    base = seg_start + rank
    pos = jnp.zeros((B_TM, LANES), _f32)
    for k in range(TOP_K):
        pk = jnp.sum(jnp.where(hit[k], base, 0.0), axis=-1, keepdims=True)
        pos = jnp.where(lane == float(k), pk, pos)
    pos_ref[...] = pos
    post_ref[...] = jnp.transpose(pos)[0:8, :]
    sub = lax.broadcasted_iota(jnp.int32, (8, LANES), 0)
    meta = jnp.where(sub == 0, jnp.broadcast_to(seg_start, (8, LANES)),
                     jnp.where(sub == 1, jnp.broadcast_to(seg_rows, (8, LANES)), 0.0))
    meta_ref[...] = meta.astype(jnp.int32)


def _stage_b(l, x_p, x_s, xs_blk, ycat_p, ycat_s, w_out_b, mod_p4, mod_s, p):
    def pb(i):
        return jnp.minimum(i // (SEQ // B_TM), BATCH - 1)

    def mp_spec(j):
        return pl.BlockSpec((None, None, 1, D_MODEL), lambda i: (l, pb(i), 0, j))

    def ms_spec(j):
        return pl.BlockSpec((None, DEC_BATCH, D_MODEL), lambda i: (l, 0, j))

    row_spec = pl.BlockSpec((B_TM, D_MODEL), lambda i: (i, 0))
    in_specs = [
        pl.BlockSpec((B_TM, D_MODEL), lambda i: (jnp.minimum(i, B_PROMPT_BLOCKS - 1), 0)),
        pl.BlockSpec((T_SAMPLE, D_MODEL), lambda i: (xs_blk, 0), pipeline_mode=pl.Buffered(1)),
        pl.BlockSpec((B_TM, D_MODEL), lambda i: (jnp.minimum(i, B_PROMPT_BLOCKS - 1), 0)),
        pl.BlockSpec((T_SAMPLE, D_MODEL), lambda i: (0, 0), pipeline_mode=pl.Buffered(1)),
        pl.BlockSpec((None, D_MODEL, D_MODEL), lambda i: (l, 0, 0), pipeline_mode=pl.Buffered(1)),
        mp_spec(2), mp_spec(3), mp_spec(4), ms_spec(2), ms_spec(3), ms_spec(4),
        _layer_spec((1, D_MODEL), l, 1),
        _layer_spec((D_MODEL, 2 * LANES), l, 1),
        _layer_spec((1, LANES), l, 1),
    ]
    lane_spec = pl.BlockSpec((B_TM, LANES), lambda i: (i, 0))
    return pl.pallas_call(
        _b_body,
        grid=(N_BLOCKS,),
        in_specs=in_specs,
        out_specs=[row_spec, row_spec, lane_spec, lane_spec,
                   pl.BlockSpec((None, 8, B_TM), lambda i: (i, 0, 0)),
                   pl.BlockSpec((None, 8, LANES), lambda i: (i, 0, 0))],
        out_shape=[
            jax.ShapeDtypeStruct((T_ALL, D_MODEL), _f32),
            jax.ShapeDtypeStruct((T_ALL, D_MODEL), _bf16),
            jax.ShapeDtypeStruct((T_ALL, LANES), _f32),
            jax.ShapeDtypeStruct((T_ALL, LANES), _f32),
            jax.ShapeDtypeStruct((N_BLOCKS, 8, B_TM), _f32),
            jax.ShapeDtypeStruct((N_BLOCKS, 8, LANES), jnp.int32),
        ],
        compiler_params=pltpu.CompilerParams(
            dimension_semantics=("arbitrary",), vmem_limit_bytes=VMEM_LIMIT),
        name="outproj_router",
    )(x_p, x_s, ycat_p, ycat_s, w_out_b, mod_p4, mod_p4, mod_p4, mod_s, mod_s, mod_s, p["nffn"], p["rw"], p["rb"])


def _sort_body(post_ref, h_ref, gate_ref, o_ref, go_ref):
    c = pl.program_id(1)
    j = (c * SORT_TM + lax.broadcasted_iota(jnp.int32, (SORT_TM, B_TM), 0)).astype(_f32)
    lane = lax.broadcasted_iota(jnp.int32, (B_TM, LANES), 1)
    sel = jnp.zeros((SORT_TM, B_TM), _f32)
    gs = jnp.zeros((SORT_TM, LANES), _f32)
    for k in range(TOP_K):
        hit = j == post_ref[k:k + 1, :]
        sel = jnp.where(hit, 1.0, sel)
        g = gate_ref[:, k:k + 1]
        g0 = g.astype(_bf16).astype(_f32)
        g1 = (g - g0).astype(_bf16).astype(_f32)
        g2 = g - g0 - g1
        terms = jnp.where(lane == 0, g0, jnp.where(lane == 1, g1, jnp.where(lane == 2, g2, 0.0)))
        gs = gs + jnp.dot(jnp.where(hit, 1.0, 0.0).astype(_bf16), terms.astype(_bf16),
                          preferred_element_type=_f32)
    o_ref[...] = jnp.dot(sel.astype(_bf16), h_ref[...], preferred_element_type=_f32).astype(_bf16)
    go_ref[...] = gs


def _local_sort(post, h2, gates):
    return pl.pallas_call(
        _sort_body,
        grid=(N_BLOCKS, SORT_CHUNKS),
        in_specs=[
            pl.BlockSpec((None, 8, B_TM), lambda b, c: (b, 0, 0)),
            pl.BlockSpec((B_TM, D_MODEL), lambda b, c: (b, 0)),
            pl.BlockSpec((B_TM, LANES), lambda b, c: (b, 0)),
        ],
        out_specs=[pl.BlockSpec((SORT_TM, D_MODEL), lambda b, c: (b * SORT_CHUNKS + c, 0)),
                   pl.BlockSpec((SORT_TM, LANES), lambda b, c: (b * SORT_CHUNKS + c, 0))],
        out_shape=[jax.ShapeDtypeStruct((N_BLOCKS * BLOCK_ROWS, D_MODEL), _bf16),
                   jax.ShapeDtypeStruct((N_BLOCKS * BLOCK_ROWS, LANES), _f32)],
        compiler_params=pltpu.CompilerParams(
            dimension_semantics=("arbitrary", "arbitrary"), vmem_limit_bytes=VMEM_LIMIT),
        name="local_sort",
    )(post, h2, gates)


def _moe_body(ie_ref, ilo_ref, int_ref, start_ref, rows_ref, used_ref,
              wg_ref, wl_ref, wd_ref, bg_ref, bl_ref, bd_ref, x_hbm, g_hbm, y_hbm,
              xbuf, hbuf, gbuf, wc, seg_src, nseg, pending, sem_x, sem_y):
    i = pl.program_id(0)
    n = pl.program_id(1)
    e = ie_ref[i]
    lo = ilo_ref[i]
    nt = int_ref[i]
    hi = lo + nt * MOE_TM
    tm, tn = MOE_TM, MOE_TN

    def x_copy(src, dst):
        return pltpu.make_async_copy(x_hbm.at[pl.ds(src, SEG)], xbuf.at[pl.ds(dst, SEG)], sem_x.at[dst // tm])

    def g_copy(src, dst):
        return pltpu.make_async_copy(g_hbm.at[pl.ds(src, SEG)], gbuf.at[pl.ds(dst, SEG)], sem_x.at[dst // tm])

    def y_copy(src, dst):
        return pltpu.make_async_copy(xbuf.at[pl.ds(src, SEG)], y_hbm.at[pl.ds(dst, SEG)], sem_y.at[0])

    def drain():
        def w(_, carry):
            y_copy(0, 0).wait()
            return carry
        lax.fori_loop(0, pending[0], w, 0)
        pending[0] = 0

    @pl.when((i == 0) & (n == 0))
    def _():
        pending[0] = 0

    @pl.when((n == 0) & (nt > 0))
    def _():
        drain()

        def per_block(b, p):
            rows = rows_ref[b * N_EXPERTS + e]
            start = start_ref[b * N_EXPERTS + e]

            def per_piece(g, carry):
                q = p + g * SEG

                @pl.when((q >= lo) & (q < hi))
                def _():
                    src = pl.multiple_of(b * BLOCK_ROWS + start + g * SEG, SEG)
                    dst = pl.multiple_of(q - lo, SEG)
                    seg_src[dst // SEG] = src
                    x_copy(src, dst).start()
                    g_copy(src, dst).start()
                return carry
            lax.fori_loop(0, rows // SEG, per_piece, 0)
            return p + rows
        end = lax.fori_loop(0, N_BLOCKS, per_block, 0)
        ns = (jnp.minimum(end, hi) - lo) // SEG
        nseg[0] = ns

        def zero_piece(g, carry):
            r = pl.multiple_of(g * SEG, SEG)
            xbuf[pl.ds(r, SEG), :] = jnp.zeros((SEG, D_MODEL), _bf16)
            gbuf[pl.ds(r, SEG), :] = jnp.zeros((SEG, LANES), _f32)
            return carry
        lax.fori_loop(ns, nt * TILE_SEGS, zero_piece, 0)

    def tile_pieces(t):
        return jnp.clip(nseg[0] - t * TILE_SEGS, 0, TILE_SEGS)

    for c in range(MOE_UP_STEPS):
        @pl.when((n == c) & (nt > 0))
        def _(c=c):
            wc[:, 0:tn] = wg_ref[...].astype(_bf16)
            wc[:, tn:] = wl_ref[...].astype(_bf16)

            def body(t, carry):
                if c == 0:
                    def w(_, cc):
                        x_copy(0, pl.multiple_of(t * tm, tm)).wait()
                        g_copy(0, pl.multiple_of(t * tm, tm)).wait()
                        return cc
                    lax.fori_loop(0, tile_pieces(t), w, 0)
                r0 = pl.multiple_of(t * tm, tm)
                up = jnp.dot(xbuf[pl.ds(r0, tm), :], wc[...], preferred_element_type=_f32)
                glu = jnp.minimum(up[:, 0:tn] + bg_ref[...], SWIGLU_LIMIT)
                lin = jnp.clip(up[:, tn:] + bl_ref[...], -SWIGLU_LIMIT, SWIGLU_LIMIT)
                act = glu * jax.nn.sigmoid(SWIGLU_ALPHA * glu) * (lin + 1.0)
                hbuf[pl.ds(r0, tm), c * tn:(c + 1) * tn] = act.astype(_bf16)
                return carry
            lax.fori_loop(0, nt, body, 0)

    for c in range(MOE_DOWN_STEPS):
        @pl.when((n == MOE_UP_STEPS + c) & (nt > 0))
        def _(c=c):
            wc[:, 0:tn] = wd_ref[...].astype(_bf16)

            def body(t, carry):
                r0 = pl.multiple_of(t * tm, tm)
                y = jnp.dot(hbuf[pl.ds(r0, tm), :], wc[:, 0:tn], preferred_element_type=_f32) + bd_ref[...]
                g = gbuf[pl.ds(r0, tm), :]
                gate = g[:, 0:1] + g[:, 1:2] + g[:, 2:3]
                xbuf[pl.ds(r0, tm), c * tn:(c + 1) * tn] = (y * gate).astype(_bf16)
                if c == MOE_DOWN_STEPS - 1:
                    np_ = tile_pieces(t)
                    for q in range(TILE_SEGS):
                        @pl.when(q < np_)
                        def _(q=q):
                            y_copy(pl.multiple_of(r0 + q * SEG, SEG),
                                   pl.multiple_of(seg_src[t * TILE_SEGS + q], SEG)).start()
                return carry
            lax.fori_loop(0, nt, body, 0)
            if c == MOE_DOWN_STEPS - 1:
                pending[0] = nseg[0]

    @pl.when((i == MOE_ITEMS - 1) & (n == MOE_STEPS - 1))
    def _():
        drain()
        hbuf[0:SEG, :] = jnp.zeros((SEG, D_EXPERT), _bf16)

        def zero_copy(row):
            return pltpu.make_async_copy(hbuf.at[pl.ds(0, SEG)], y_hbm.at[pl.ds(row, SEG)], sem_y.at[0])

        def per_block(b, carry):
            def start(g, cc):
                zero_copy(pl.multiple_of(b * BLOCK_ROWS + g * SEG, SEG)).start()
                return cc

            def wait(g, cc):
                zero_copy(0).wait()
                return cc
            first = used_ref[b] // SEG
            lax.fori_loop(first, BLOCK_ROWS // SEG, start, 0)
            lax.fori_loop(first, BLOCK_ROWS // SEG, wait, 0)
            return carry
        lax.fori_loop(0, N_BLOCKS, per_block, 0)


def _moe(l, item_e, item_lo, item_nt, seg_start, seg_rows, used, w_up, b_up4, w_down, b_down4, x_sorted,
         g_sorted):
    def col_up(n, valid):
        return jnp.where(valid, jnp.minimum(n, MOE_UP_STEPS - 1), MOE_UP_STEPS - 1)

    def col_down(n, valid):
        return jnp.where(valid, jnp.maximum(n - MOE_UP_STEPS, 0), MOE_DOWN_STEPS - 1)

    def glu_map(i, n, ie, ilo, int_, *_):
        return (l, ie[i], 0, col_up(n, int_[i] > 0))

    def lin_map(i, n, ie, ilo, int_, *_):
        return (l, ie[i], 0, MOE_UP_STEPS + col_up(n, int_[i] > 0))

    def down_map(i, n, ie, ilo, int_, *_):
        return (l, ie[i], 0, col_down(n, int_[i] > 0))

    grid_spec = pltpu.PrefetchScalarGridSpec(
        num_scalar_prefetch=6,
        grid=(MOE_ITEMS, MOE_STEPS),
        in_specs=[
            pl.BlockSpec((None, None, D_MODEL, MOE_TN), glu_map),
            pl.BlockSpec((None, None, D_MODEL, MOE_TN), lin_map),
            pl.BlockSpec((None, None, D_EXPERT, MOE_TN), down_map),
            pl.BlockSpec((None, None, 1, MOE_TN), glu_map),
            pl.BlockSpec((None, None, 1, MOE_TN), lin_map),
            pl.BlockSpec((None, None, 1, MOE_TN), down_map),
            pl.BlockSpec(memory_space=pl.ANY),
            pl.BlockSpec(memory_space=pl.ANY),
        ],
        out_specs=pl.BlockSpec(memory_space=pl.ANY),
        scratch_shapes=[
            pltpu.VMEM((MOE_MT * MOE_TM, D_MODEL), _bf16),
            pltpu.VMEM((MOE_MT * MOE_TM, D_EXPERT), _bf16),
            pltpu.VMEM((MOE_MT * MOE_TM, LANES), _f32),
            pltpu.VMEM((D_MODEL, 2 * MOE_TN), _bf16),
            pltpu.SMEM((MOE_MT * TILE_SEGS,), jnp.int32),
            pltpu.SMEM((1,), jnp.int32),
            pltpu.SMEM((1,), jnp.int32),
            pltpu.SemaphoreType.DMA((MOE_MT,)),
            pltpu.SemaphoreType.DMA((1,)),
        ],
    )
    return pl.pallas_call(
        _moe_body,
        grid_spec=grid_spec,
        out_shape=jax.ShapeDtypeStruct((N_BLOCKS * BLOCK_ROWS, D_MODEL), _bf16),
        compiler_params=pltpu.CompilerParams(
            dimension_semantics=("arbitrary", "arbitrary"), vmem_limit_bytes=VMEM_LIMIT),
        name="moe_experts",
    )(item_e, item_lo, item_nt, seg_start, seg_rows, used, w_up, w_up, w_down, b_up4, b_up4, b_down4, x_sorted,
      g_sorted)


MOE_DN = 1024
STREAM_UP_STEPS = D_EXPERT // MOE_TN
STREAM_DOWN_STEPS = D_MODEL // MOE_DN
STREAM_STEPS = STREAM_UP_STEPS + STREAM_DOWN_STEPS
MOE_WSLOTS = 3
CAST_ROWS = D_MODEL // MOE_MT


def _moe_stream_body(ie_ref, ilo_ref, int_ref, start_ref, rows_ref, used_ref, nitems_ref,
                     wup_hbm, wdn_hbm, bup_hbm, bdn_hbm, x_hbm, g_hbm, y_hbm,
                     wbuf, wc, xbuf, hbuf, gbuf, bup, bdn, seg_src, nseg, pending,
                     sem_w, sem_b, sem_x, sem_y, *, l):
    tm, tn = MOE_TM, MOE_TN
    n_items = nitems_ref[0]
    n_steps = n_items * STREAM_STEPS

    def w_half(src_ref, e, col0, slot, half):
        return pltpu.make_async_copy(src_ref.at[l, e, :, pl.ds(col0, tn)],
                                     wbuf.at[slot, :, pl.ds(half * tn, tn)], sem_w.at[slot])

    def issue_w(s):
        e = ie_ref[s // STREAM_STEPS]
        n = s % STREAM_STEPS
        slot = s % MOE_WSLOTS
        for c in range(STREAM_UP_STEPS):
            @pl.when(n == c)
            def _(c=c):
                w_half(wup_hbm, e, c * tn, slot, 0).start()
                w_half(wup_hbm, e, D_EXPERT + c * tn, slot, 1).start()
        for c in range(STREAM_DOWN_STEPS):
            @pl.when(n == STREAM_UP_STEPS + c)
            def _(c=c):
                w_half(wdn_hbm, e, c * MOE_DN, slot, 0).start()
                w_half(wdn_hbm, e, c * MOE_DN + tn, slot, 1).start()

    def wait_w(s):
        slot = s % MOE_WSLOTS
        for half in range(2):
            w_half(wdn_hbm, 0, 0, slot, half).wait()

    def b_copies(item):
        e = ie_ref[item]
        par = item % 2
        return (pltpu.make_async_copy(bup_hbm.at[l, e], bup.at[par], sem_b.at[par]),
                pltpu.make_async_copy(bdn_hbm.at[l, e], bdn.at[par], sem_b.at[par]))

    def x_copy(src, dst):
        return pltpu.make_async_copy(x_hbm.at[pl.ds(src, SEG)], xbuf.at[pl.ds(dst, SEG)], sem_x.at[dst // tm])

    def g_copy(src, dst):
        return pltpu.make_async_copy(g_hbm.at[pl.ds(src, SEG)], gbuf.at[pl.ds(dst, SEG)], sem_x.at[dst // tm])

    def y_copy(src, dst):
        return pltpu.make_async_copy(xbuf.at[pl.ds(src, SEG)], y_hbm.at[pl.ds(dst, SEG)], sem_y.at[0])

    def drain():
        def w(_, carry):
            y_copy(0, 0).wait()
            return carry
        lax.fori_loop(0, pending[0], w, 0)
        pending[0] = 0

    def tile_pieces(t):
        return jnp.clip(nseg[0] - t * TILE_SEGS, 0, TILE_SEGS)

    pending[0] = 0
    issue_w(0)
    issue_w(1)
    for cp in b_copies(0):
        cp.start()
    wait_w(0)
    wc[0] = wbuf[0].astype(_bf16)

    def step(s, carry):
        item = s // STREAM_STEPS
        n = s % STREAM_STEPS
        e = ie_ref[item]
        lo = ilo_ref[item]
        nt = int_ref[item]
        hi = lo + nt * tm
        par = item % 2
        cur = s % 2
        nxt = (s + 1) % 2
        nslot = (s + 1) % MOE_WSLOTS

        @pl.when(s + 2 < n_steps)
        def _():
            issue_w(s + 2)

        @pl.when(s + 1 < n_steps)
        def _():
            wait_w(s + 1)

        def cast_slice(t):
            r = pl.multiple_of(t * CAST_ROWS, CAST_ROWS)
            wc[nxt, pl.ds(r, CAST_ROWS), :] = wbuf[nslot, pl.ds(r, CAST_ROWS), :].astype(_bf16)

        @pl.when(n == 0)
        def _():
            drain()

            def per_block(b, p):
                rows = rows_ref[b * N_EXPERTS + e]
                start = start_ref[b * N_EXPERTS + e]

                def per_piece(g, cc):
                    q = p + g * SEG

                    @pl.when((q >= lo) & (q < hi))
                    def _():
                        src = pl.multiple_of(b * BLOCK_ROWS + start + g * SEG, SEG)
                        dst = pl.multiple_of(q - lo, SEG)
                        seg_src[dst // SEG] = src
                        x_copy(src, dst).start()
                        g_copy(src, dst).start()
                    return cc
                lax.fori_loop(0, rows // SEG, per_piece, 0)
                return p + rows
            end = lax.fori_loop(0, N_BLOCKS, per_block, 0)
            ns = (jnp.minimum(end, hi) - lo) // SEG
            nseg[0] = ns

            def zero_piece(g, cc):
                r = pl.multiple_of(g * SEG, SEG)
                xbuf[pl.ds(r, SEG), :] = jnp.zeros((SEG, D_MODEL), _bf16)
                gbuf[pl.ds(r, SEG), :] = jnp.zeros((SEG, LANES), _f32)
                return cc
            lax.fori_loop(ns, nt * TILE_SEGS, zero_piece, 0)

            for cp in b_copies(item):
                cp.wait()

            @pl.when(item + 1 < n_items)
            def _():
                for cp in b_copies(item + 1):
                    cp.start()

        for c in range(STREAM_UP_STEPS):
            @pl.when(n == c)
            def _(c=c):
                def body(t, cc):
                    if c == 0:
                        def w(_, c3):
                            x_copy(0, pl.multiple_of(t * tm, tm)).wait()
                            g_copy(0, pl.multiple_of(t * tm, tm)).wait()
                            return c3
                        lax.fori_loop(0, tile_pieces(t), w, 0)
                    cast_slice(t)
                    r0 = pl.multiple_of(t * tm, tm)
                    up = jnp.dot(xbuf[pl.ds(r0, tm), :], wc[cur], preferred_element_type=_f32)
                    bg = bup[par, :, c * tn:(c + 1) * tn]
                    bl = bup[par, :, D_EXPERT + c * tn:D_EXPERT + (c + 1) * tn]
                    glu = jnp.minimum(up[:, 0:tn] + bg, SWIGLU_LIMIT)
                    lin = jnp.clip(up[:, tn:] + bl, -SWIGLU_LIMIT, SWIGLU_LIMIT)
                    act = glu * jax.nn.sigmoid(SWIGLU_ALPHA * glu) * (lin + 1.0)
                    hbuf[pl.ds(r0, tm), c * tn:(c + 1) * tn] = act.astype(_bf16)
                    return cc
                lax.fori_loop(0, nt, body, 0)

        for c in range(STREAM_DOWN_STEPS):
            @pl.when(n == STREAM_UP_STEPS + c)
            def _(c=c):
                def body(t, cc):
                    cast_slice(t)
                    r0 = pl.multiple_of(t * tm, tm)
                    y = (jnp.dot(hbuf[pl.ds(r0, tm), :], wc[cur], preferred_element_type=_f32)
                         + bdn[par, :, c * MOE_DN:(c + 1) * MOE_DN])
                    g = gbuf[pl.ds(r0, tm), :]
                    gate = g[:, 0:1] + g[:, 1:2] + g[:, 2:3]
                    xbuf[pl.ds(r0, tm), c * MOE_DN:(c + 1) * MOE_DN] = (y * gate).astype(_bf16)
                    if c == STREAM_DOWN_STEPS - 1:
                        np_ = tile_pieces(t)
                        for q in range(TILE_SEGS):
                            @pl.when(q < np_)
                            def _(q=q):
                                y_copy(pl.multiple_of(r0 + q * SEG, SEG),
                                       pl.multiple_of(seg_src[t * TILE_SEGS + q], SEG)).start()
                    return cc
                lax.fori_loop(0, nt, body, 0)
                if c == STREAM_DOWN_STEPS - 1:
                    pending[0] = nseg[0]

        def rest(t, cc):
            cast_slice(t)
            return cc
        lax.fori_loop(nt, MOE_MT, rest, 0)
        return carry
    lax.fori_loop(0, n_steps, step, 0)

    drain()
    hbuf[0:SEG, :] = jnp.zeros((SEG, D_EXPERT), _bf16)

    def zero_copy(row):
        return pltpu.make_async_copy(hbuf.at[pl.ds(0, SEG)], y_hbm.at[pl.ds(row, SEG)], sem_y.at[0])

    def zero_block(b, carry):
        def zstart(g, cc):
            zero_copy(pl.multiple_of(b * BLOCK_ROWS + g * SEG, SEG)).start()
            return cc

        def zwait(g, cc):
            zero_copy(0).wait()
            return cc
        first = used_ref[b] // SEG
        lax.fori_loop(first, BLOCK_ROWS // SEG, zstart, 0)
        lax.fori_loop(first, BLOCK_ROWS // SEG, zwait, 0)
        return carry
    lax.fori_loop(0, N_BLOCKS, zero_block, 0)


def _moe_stream(l, item_e, item_lo, item_nt, seg_start, seg_rows, used, n_items, w_up, b_up4, w_down, b_down4,
                x_sorted, g_sorted):
    any_spec = pl.BlockSpec(memory_space=pl.ANY)
    grid_spec = pltpu.PrefetchScalarGridSpec(
        num_scalar_prefetch=7,
        grid=(1,),
        in_specs=[any_spec] * 6,
        out_specs=any_spec,
        scratch_shapes=[
            pltpu.VMEM((MOE_WSLOTS, D_MODEL, 2 * MOE_TN), _f32),
            pltpu.VMEM((2, D_MODEL, 2 * MOE_TN), _bf16),
            pltpu.VMEM((MOE_MT * MOE_TM, D_MODEL), _bf16),
            pltpu.VMEM((MOE_MT * MOE_TM, D_EXPERT), _bf16),
            pltpu.VMEM((MOE_MT * MOE_TM, LANES), _f32),
            pltpu.VMEM((2, 1, 2 * D_EXPERT), _f32),
            pltpu.VMEM((2, 1, D_MODEL), _f32),
            pltpu.SMEM((MOE_MT * TILE_SEGS,), jnp.int32),
            pltpu.SMEM((1,), jnp.int32),
            pltpu.SMEM((1,), jnp.int32),
            pltpu.SemaphoreType.DMA((MOE_WSLOTS,)),
            pltpu.SemaphoreType.DMA((2,)),
            pltpu.SemaphoreType.DMA((MOE_MT,)),
            pltpu.SemaphoreType.DMA((1,)),
        ],
    )
    return pl.pallas_call(
        functools.partial(_moe_stream_body, l=l),
        grid_spec=grid_spec,
        out_shape=jax.ShapeDtypeStruct((N_BLOCKS * BLOCK_ROWS, D_MODEL), _bf16),
        compiler_params=pltpu.CompilerParams(
            dimension_semantics=("arbitrary",), vmem_limit_bytes=60 * 1024 * 1024),
        name="moe_experts",
    )(item_e, item_lo, item_nt, seg_start, seg_rows, used, n_items, w_up, w_down, b_up4, b_down4, x_sorted,
      g_sorted)


def _work_items(meta):
    seg_start = meta[:, 0, :N_EXPERTS]
    seg_rows = meta[:, 1, :N_EXPERTS]
    rows_e = jnp.sum(seg_rows, axis=0)
    nt = (rows_e + MOE_TM - 1) // MOE_TM
    ipe = (nt + MOE_MT - 1) // MOE_MT
    cum = jnp.cumsum(ipe)
    first = cum - ipe
    total = cum[-1]
    ii = jnp.arange(MOE_ITEMS, dtype=jnp.int32)
    ii_c = jnp.minimum(ii, total - 1)
    e_i = jnp.minimum(jnp.sum((cum[None, :] <= ii_c[:, None]).astype(jnp.int32), axis=1), N_EXPERTS - 1)
    valid = ii < total
    j = ii - first[e_i]
    lo_i = jnp.where(valid, j * (MOE_MT * MOE_TM), 0)
    nt_i = jnp.where(valid, jnp.minimum(MOE_MT, nt[e_i] - j * MOE_MT), 0)
    used = jnp.sum(seg_rows, axis=1)
    i32 = jnp.int32
    return (e_i.astype(i32), lo_i.astype(i32), nt_i.astype(i32), seg_start.reshape(-1).astype(i32),
            seg_rows.reshape(-1).astype(i32), used.astype(i32), total.astype(i32).reshape(1))


def _c_body(x1_ref, y_ref, pos_ref, mpg2, msg2, nfin_ref, *rest, final):
    if final:
        outp_ref, outs_ref, acc_ref = rest
    else:
        out_ref, acc_ref = rest
    i = pl.program_id(0)
    c = pl.program_id(1)
    isp = i < B_PROMPT_BLOCKS
    j = (c * SORT_TM + lax.broadcasted_iota(jnp.int32, (B_TM, SORT_TM), 1)).astype(_f32)
    sel = jnp.zeros((B_TM, SORT_TM), _f32)
    for k in range(TOP_K):
        sel = jnp.where(j == pos_ref[:, k:k + 1], 1.0, sel)
    part = jnp.dot(sel.astype(_bf16), y_ref[...], preferred_element_type=_f32)

    @pl.when(c == 0)
    def _():
        acc_ref[...] = part

    @pl.when(c > 0)
    def _():
        acc_ref[...] += part

    @pl.when(c == SORT_CHUNKS - 1)
    def _():
        x2 = x1_ref[...] + _slab_mul(acc_ref[...], _pick_mod(isp, mpg2, msg2))
        if final:
            y = _rms(x2) * nfin_ref[...]

            @pl.when(isp)
            def _():
                outp_ref[...] = y

            @pl.when(jnp.logical_not(isp))
            def _():
                outs_ref[...] = y
        else:
            out_ref[...] = x2


def _combine(l, x1, y_sorted, pos, mod_p4, mod_s, nfin, final):
    def pb(i):
        return jnp.minimum(i // (SEQ // B_TM), BATCH - 1)

    row_spec = pl.BlockSpec((B_TM, D_MODEL), lambda i, c: (i, 0))
    lane_spec = pl.BlockSpec((B_TM, LANES), lambda i, c: (i, 0))
    if final:
        out_specs = [pl.BlockSpec((B_TM, D_MODEL), lambda i, c: (jnp.minimum(i, B_PROMPT_BLOCKS - 1), 0)),
                     pl.BlockSpec((T_SAMPLE, D_MODEL), lambda i, c: (0, 0))]
        out_shape = [jax.ShapeDtypeStruct((T_PROMPT, D_MODEL), _f32),
                     jax.ShapeDtypeStruct((T_SAMPLE, D_MODEL), _f32)]
    else:
        out_specs = row_spec
        out_shape = jax.ShapeDtypeStruct((T_ALL, D_MODEL), _f32)
    return pl.pallas_call(
        functools.partial(_c_body, final=final),
        grid=(N_BLOCKS, SORT_CHUNKS),
        in_specs=[
            row_spec,
            pl.BlockSpec((SORT_TM, D_MODEL), lambda i, c: (i * SORT_CHUNKS + c, 0)),
            lane_spec,
            pl.BlockSpec((None, None, 1, D_MODEL), lambda i, c: (l, pb(i), 0, 5)),
            pl.BlockSpec((None, DEC_BATCH, D_MODEL), lambda i, c: (l, 0, 5)),
            pl.BlockSpec((1, D_MODEL), lambda i, c: (0, 0)),
        ],
        out_specs=out_specs,
        out_shape=out_shape,
        scratch_shapes=[pltpu.VMEM((B_TM, D_MODEL), _f32)],
        compiler_params=pltpu.CompilerParams(
            dimension_semantics=("arbitrary", "arbitrary"), vmem_limit_bytes=VMEM_LIMIT),
        name="moe_combine",
    )(x1, y_sorted, pos, mod_p4, mod_s, nfin)


def kernel(x_prompt, x_sample, c_prompt, c_sample, state_pool, state_conformer, state_shortconv, w_ada, b_ada, norm_mix, norm_ffn, norm_final, w_in, pool_w, pool_scale, sgu_norm, sgu_w, sgu_b, conf_w, conf_b, conf_ln_g, conf_ln_b, sc_w, out_norm, w_out, router_w, router_b, w_up, b_up, w_down, b_down):
    gw = GROUP_WIDTH
    ns, nb = DEC_SEQ, DEC_BATCH

    c_all = jnp.concatenate([c_sample, c_prompt, jnp.zeros((ADA_ROWS - nb - BATCH, D_MODEL), _f32)], axis=0)
    mod_s, mod_p8 = _ada(c_all, w_ada, b_ada)
    mod_p4 = mod_p8[:, :BATCH].reshape(DEPTH, BATCH, 1, 6 * D_MODEL)

    rw_hi = router_w.astype(_bf16)
    rw_lo = (router_w - rw_hi.astype(_f32)).astype(_bf16)
    lane_pad = ((0, 0), (0, 0), (0, LANES - N_EXPERTS))
    params = {
        "nmix": norm_mix.reshape(DEPTH, 1, D_MODEL),
        "nffn": norm_ffn.reshape(DEPTH, 1, D_MODEL),
        "poolw": pool_w,
        "pscale": pool_scale.reshape(DEPTH, 1, gw),
        "sgun": sgu_norm.reshape(DEPTH, 1, gw),
        "sguw": sgu_w,
        "sgub": jnp.repeat(jnp.swapaxes(sgu_b, 1, 2), SGU_CHUNK, axis=2),
        "sgwc": jnp.repeat(jnp.transpose(sgu_w[:, :, :ns, :ns], (0, 2, 3, 1)).reshape(DEPTH, ns * ns, SGU_HEADS),
                           SGU_CHUNK, axis=2),
        "sgbc": jnp.repeat(jnp.swapaxes(sgu_b[:, :, :ns], 1, 2), SGU_CHUNK, axis=2),
        "cw": conf_w,
        "cb": conf_b.reshape(DEPTH, 1, gw),
        "clg": conf_ln_g.reshape(DEPTH, 1, gw),
        "clb": conf_ln_b.reshape(DEPTH, 1, gw),
        "scw": sc_w,
        "onorm": out_norm.reshape(DEPTH, 1, D_MODEL),
        "rw": jnp.concatenate([jnp.pad(rw_hi, lane_pad), jnp.pad(rw_lo, lane_pad)], axis=2),
        "rb": jnp.pad(router_b, ((0, 0), (0, LANES - N_EXPERTS))).reshape(DEPTH, 1, LANES),
    }
    w_in_b = w_in.astype(_bf16)
    w_out_b = w_out.astype(_bf16)
    b_up4 = b_up.reshape(DEPTH, N_EXPERTS, 1, 2 * D_EXPERT)
    b_down4 = b_down.reshape(DEPTH, N_EXPERTS, 1, D_MODEL)
    nfin = norm_final.reshape(1, D_MODEL)

    pool_t = jnp.transpose(state_pool, (0, 2, 1, 3))
    conf_t = jnp.transpose(state_conformer, (0, 2, 1, 3))
    sc_t = jnp.transpose(state_shortconv, (0, 2, 1, 3))

    x_p = x_prompt.reshape(T_PROMPT, D_MODEL)
    x_s = jnp.transpose(x_sample, (1, 0, 2)).reshape(T_SAMPLE, D_MODEL)
    xs_blk = 0

    pool_p, conf_p, sc_p, pool_s, conf_s, sc_s, v_s = [], [], [], [], [], [], []
    for l in range(DEPTH):
        ycat_p, po, co, so = _a_prompt(l, x_p, mod_p4, w_in_b, params)
        ycat_s, a_new, g_new, z_new, v_new = _a_sample(l, x_s, xs_blk, mod_s, w_in_b, params, pool_t, conf_t, sc_t)
        pool_p.append(po[:, POOL_HALO - (POOL_MAX - 1):])
        conf_p.append(co[:, CONF_HALO - (CONF_KERNEL - 1):])
        sc_p.append(so[:, SC_HALO - (SC_KERNEL - 1):])
        pool_s.append(jnp.concatenate([state_pool[l][:, ns:], jnp.transpose(a_new, (1, 0, 2))], axis=1))
        conf_s.append(jnp.concatenate([state_conformer[l][:, ns:], jnp.transpose(g_new, (1, 0, 2))], axis=1))
        sc_s.append(jnp.transpose(z_new, (1, 0, 2))[:, ns - (SC_KERNEL - 1):])
        v_s.append(jnp.transpose(v_new, (1, 0, 2)))

        x1, h2, gates, pos, post, meta = _stage_b(l, x_p, x_s, xs_blk, ycat_p, ycat_s, w_out_b, mod_p4, mod_s,
                                                  params)
        x_sorted, g_sorted = _local_sort(post, h2, gates)
        item_e, item_lo, item_nt, seg_start, seg_rows, used, n_items = _work_items(meta)
        y_sorted = _moe_stream(l, item_e, item_lo, item_nt, seg_start, seg_rows, used, n_items, w_up, b_up4,
                               w_down, b_down4, x_sorted, g_sorted)
        if l < DEPTH - 1:
            x_p = x_s = _combine(l, x1, y_sorted, pos, mod_p4, mod_s, nfin, final=False)
            xs_blk = T_PROMPT // T_SAMPLE
        else:
            y_p, y_s = _combine(l, x1, y_sorted, pos, mod_p4, mod_s, nfin, final=True)

    y_prompt = y_p.reshape(BATCH, SEQ, D_MODEL)
    y_sample = jnp.transpose(y_s.reshape(ns, nb, D_MODEL), (1, 0, 2))
    return (y_prompt, y_sample, jnp.stack(pool_p), jnp.stack(pool_s), jnp.stack(conf_p), jnp.stack(conf_s),
            jnp.stack(sc_p), jnp.stack(sc_s), jnp.stack(v_s))
```

```python
import functools

import jax
import jax.numpy as jnp
from jax import lax
from jax.experimental import pallas as pl
from jax.experimental.pallas import tpu as pltpu

D_MODEL = 2048
BATCH = 4
SEQ = 2048
DEPTH = 2
DEC_BATCH = 128
DEC_SEQ = 4
PAST_LEN = 16384
GROUP_WIDTH = 512
POOL_WINDOWS = (2, 4, 8, 16)
POOL_MAX = 16
POOL_CH = 128
SGU_HEADS = 4
SGU_CHUNK = 128
CONF_KERNEL = 31
SC_KERNEL = 3
IN_COLS = 8 * GROUP_WIDTH
N_EXPERTS = 32
TOP_K = 4
D_EXPERT = 2048
SWIGLU_ALPHA = 1.702
SWIGLU_LIMIT = 7.0
EPS = 1e-5

T_PROMPT = BATCH * SEQ
T_SAMPLE = DEC_BATCH * DEC_SEQ
T_ALL = T_PROMPT + T_SAMPLE

LANES = 128
BF16_ROWS = 16
VMEM_LIMIT = 56 * 1024 * 1024

A_TM = 1024
A_TN = 512
A_STEPS = IN_COLS // A_TN
POOL_HALO = 16
CONF_HALO = 32
SC_HALO = 8

B_TM = 512
N_BLOCKS = T_ALL // B_TM
B_PROMPT_BLOCKS = T_PROMPT // B_TM
SEG = BF16_ROWS
SORT_TM = 512
BLOCK_ROWS = -(-(B_TM * TOP_K + N_EXPERTS * (SEG - 1)) // SORT_TM) * SORT_TM
SORT_CHUNKS = BLOCK_ROWS // SORT_TM
GATE_TERMS = 3
SIDE_ID0 = GATE_TERMS * TOP_K
H_EXT = D_MODEL + LANES

MOE_TM = 256
MOE_MT = 8
MOE_TN = 512
MOE_UP_STEPS = D_EXPERT // MOE_TN
MOE_DOWN_STEPS = D_MODEL // MOE_TN
MOE_STEPS = MOE_UP_STEPS + MOE_DOWN_STEPS
MOE_YSLOTS = 4
TILE_SEGS = MOE_TM // SEG
MOE_TILES = (T_ALL * TOP_K + N_BLOCKS * N_EXPERTS * (SEG - 1) + N_EXPERTS * (MOE_TM - 1)) // MOE_TM
MOE_ITEMS = (MOE_TILES + N_EXPERTS * (MOE_MT - 1)) // MOE_MT

_f32 = jnp.float32
_bf16 = jnp.bfloat16


def _rms(x):
    return x * lax.rsqrt(jnp.mean(x * x, axis=-1, keepdims=True) + EPS)


def _ln(x):
    xc = x - jnp.mean(x, axis=-1, keepdims=True)
    return xc * lax.rsqrt(jnp.mean(xc * xc, axis=-1, keepdims=True) + EPS)


def _gelu(x):
    return 0.5 * x * (1.0 + lax.erf(x * (0.5 ** 0.5)))


def _slab_mul(x, m):
    r, c = x.shape
    return (x.reshape(r // 128, 128, c) * m[None]).reshape(r, c)


def _slab_add(x, m):
    r, c = x.shape
    return (x.reshape(r // 128, 128, c) + m[None]).reshape(r, c)


ADA_TN = 1024
ADA_ROWS = DEC_BATCH + 8


def _ada_body(c_ref, w_ref, b_ref, os_ref, op_ref):
    c = c_ref[...]
    a = (c * jax.nn.sigmoid(c)).astype(_bf16)
    r = jnp.dot(a, w_ref[...].astype(_bf16), preferred_element_type=_f32) + b_ref[...]
    os_ref[...] = r[:DEC_BATCH]
    op_ref[...] = r[DEC_BATCH:]


def _ada(c_all, w_ada, b_ada):
    nj = 6 * D_MODEL // ADA_TN
    return pl.pallas_call(
        _ada_body,
        grid=(DEPTH, nj),
        in_specs=[
            pl.BlockSpec((ADA_ROWS, D_MODEL), lambda l, j: (0, 0)),
            pl.BlockSpec((None, D_MODEL, ADA_TN), lambda l, j: (l, 0, j)),
            pl.BlockSpec((None, 1, ADA_TN), lambda l, j: (l, 0, j)),
        ],
        out_specs=[
            pl.BlockSpec((None, DEC_BATCH, ADA_TN), lambda l, j: (l, 0, j)),
            pl.BlockSpec((None, 8, ADA_TN), lambda l, j: (l, 0, j)),
        ],
        out_shape=[
            jax.ShapeDtypeStruct((DEPTH, DEC_BATCH, 6 * D_MODEL), _f32),
            jax.ShapeDtypeStruct((DEPTH, 8, 6 * D_MODEL), _f32),
        ],
        compiler_params=pltpu.CompilerParams(
            dimension_semantics=("arbitrary", "arbitrary"), vmem_limit_bytes=VMEM_LIMIT),
        name="ada_mod",
    )(c_all, w_ada, b_ada.reshape(DEPTH, 1, 6 * D_MODEL))


def _a_prompt_body(x_ref, sh1_ref, sc1_ref, nmix_ref, w_ref, poolw_ref, pscale_ref, sgun_ref, sguw_ref,
                   sgub_ref, cw_ref, cb_ref, clg_ref, clb_ref, scw_ref, onorm_ref,
                   y_ref, pool_o, conf_o, sc_o,
                   h_ref, bufa, bufb, exta, extc, extz):
    i = pl.program_id(1)
    n = pl.program_id(2)
    tm = A_TM
    gw = GROUP_WIDTH

    @pl.when(n == 0)
    def _():
        hn = _rms(x_ref[...]) * nmix_ref[...]
        h_ref[...] = (hn * (1.0 + sc1_ref[...]) + sh1_ref[...]).astype(_bf16)

    def proj():
        return jnp.dot(h_ref[...], w_ref[...], preferred_element_type=_f32)

    def emit(g, y):
        yn = _rms(y) * onorm_ref[:, g * gw:(g + 1) * gw]
        y_ref[:, g * gw:(g + 1) * gw] = yn.astype(_bf16)

    def carry_halo(ext, halo):
        @pl.when(i == 0)
        def _():
            ext[0:halo, :] = jnp.zeros((halo, gw), _f32)

        @pl.when(i > 0)
        def _():
            ext[0:halo, :] = ext[tm:tm + halo, :]

    @pl.when(n == 0)
    def _():
        a = proj()
        carry_halo(exta, POOL_HALO)
        exta[POOL_HALO:, :] = a
        pool_o[...] = a[tm - POOL_HALO:, :]
        pos = (i * tm + lax.broadcasted_iota(jnp.int32, (tm, 1), 0)).astype(_f32)
        for gi, w in enumerate(POOL_WINDOWS):
            c0 = gi * POOL_CH
            s = exta[POOL_HALO:POOL_HALO + tm, c0:c0 + POOL_CH]
            for j in range(1, w):
                s = s + exta[POOL_HALO - j:POOL_HALO - j + tm, c0:c0 + POOL_CH]
            cnt = jnp.minimum(float(w), pos + 1.0)
            pooled = s / cnt - a[:, c0:c0 + POOL_CH]
            yg = jnp.dot(pooled.astype(_bf16), poolw_ref[gi].astype(_bf16), preferred_element_type=_f32)
            bufb[:, c0:c0 + POOL_CH] = yg
        emit(0, bufb[...] * pscale_ref[...])

    @pl.when(n == 1)
    def _():
        bufa[...] = _gelu(proj())

    @pl.when(n == 2)
    def _():
        v = _ln(_gelu(proj())) * sgun_ref[...]
        vb = v.astype(_bf16)
        row = lax.broadcasted_iota(jnp.int32, (SGU_CHUNK, SGU_CHUNK), 0)
        col = lax.broadcasted_iota(jnp.int32, (SGU_CHUNK, SGU_CHUNK), 1)
        for g in range(SGU_HEADS):
            c0 = g * SGU_CHUNK
            wg = jnp.where(row >= col, sguw_ref[g], 0.0).astype(_bf16)
            for c in range(tm // SGU_CHUNK):
                r0 = c * SGU_CHUNK
                m = jnp.dot(wg, vb[r0:r0 + SGU_CHUNK, c0:c0 + SGU_CHUNK], preferred_element_type=_f32)
                bufb[r0:r0 + SGU_CHUNK, c0:c0 + SGU_CHUNK] = m + sgub_ref[:, c0:c0 + SGU_CHUNK]
        emit(1, bufa[...] * bufb[...])

    @pl.when(n == 3)
    def _():
        bufa[...] = proj()

    @pl.when(n == 4)
    def _():
        g = bufa[...] * jax.nn.sigmoid(proj())
        carry_halo(extc, CONF_HALO)
        extc[CONF_HALO:, :] = g
        conf_o[...] = g[tm - CONF_HALO:, :]
        off = CONF_HALO - (CONF_KERNEL - 1)
        acc = jnp.broadcast_to(cb_ref[...], (tm, gw))
        for k in range(CONF_KERNEL):
            acc = acc + cw_ref[k:k + 1, :] * extc[off + k:off + k + tm, :]
        yn = _ln(acc) * clg_ref[...] + clb_ref[...]
        emit(2, yn * jax.nn.sigmoid(yn))

    @pl.when(n == 5)
    def _():
        bufa[...] = proj()

    @pl.when(n == 6)
    def _():
        bufb[...] = proj()

    @pl.when(n == 7)
    def _():
        z = bufb[...] * proj()
        carry_halo(extz, SC_HALO)
        extz[SC_HALO:, :] = z
        sc_o[...] = z[tm - SC_HALO:, :]
        off = SC_HALO - (SC_KERNEL - 1)
        acc = scw_ref[0:1, :] * extz[off:off + tm, :]
        for k in range(1, SC_KERNEL):
            acc = acc + scw_ref[k:k + 1, :] * extz[off + k:off + k + tm, :]
        emit(3, bufa[...] * acc)


def _layer_spec(shape, l, nd):
    zeros = (0,) * len(shape)
    if nd == 1:
        return pl.BlockSpec((None,) + shape, lambda a: (l,) + zeros)
    if nd == 2:
        return pl.BlockSpec((None,) + shape, lambda a, b: (l,) + zeros)
    return pl.BlockSpec((None,) + shape, lambda a, b, c: (l,) + zeros)


def _mixer_params(l, p, nd):
    gw = GROUP_WIDTH
    arrs = [p["cw"], p["cb"], p["clg"], p["clb"], p["scw"], p["onorm"]]
    shapes = [(CONF_KERNEL, gw), (1, gw), (1, gw), (1, gw), (SC_KERNEL, gw), (1, D_MODEL)]
    return arrs, [_layer_spec(s, l, nd) for s in shapes]


def _a_prompt(l, x_all, mod_p4, w_in_b, p):
    gw = GROUP_WIDTH
    ni = SEQ // A_TM
    marrs, mspecs = _mixer_params(l, p, 3)
    in_specs = [
        pl.BlockSpec((A_TM, D_MODEL), lambda b, i, n: (b * ni + i, 0)),
        pl.BlockSpec((None, None, 1, D_MODEL), lambda b, i, n: (l, b, 0, 0)),
        pl.BlockSpec((None, None, 1, D_MODEL), lambda b, i, n: (l, b, 0, 1)),
        _layer_spec((1, D_MODEL), l, 3),
        pl.BlockSpec((None, D_MODEL, A_TN), lambda b, i, n: (l, 0, n)),
        _layer_spec((len(POOL_WINDOWS), POOL_CH, POOL_CH), l, 3),
        _layer_spec((1, gw), l, 3),
        _layer_spec((1, gw), l, 3),
        _layer_spec((SGU_HEADS, SGU_CHUNK, SGU_CHUNK), l, 3),
        _layer_spec((SGU_CHUNK, gw), l, 3),
    ] + mspecs
    out_specs = [
        pl.BlockSpec((A_TM, D_MODEL), lambda b, i, n: (b * ni + i, 0)),
        pl.BlockSpec((None, POOL_HALO, gw), lambda b, i, n: (b, 0, 0)),
        pl.BlockSpec((None, CONF_HALO, gw), lambda b, i, n: (b, 0, 0)),
        pl.BlockSpec((None, SC_HALO, gw), lambda b, i, n: (b, 0, 0)),
    ]
    out_shape = [
        jax.ShapeDtypeStruct((T_PROMPT, D_MODEL), _bf16),
        jax.ShapeDtypeStruct((BATCH, POOL_HALO, gw), _f32),
        jax.ShapeDtypeStruct((BATCH, CONF_HALO, gw), _f32),
        jax.ShapeDtypeStruct((BATCH, SC_HALO, gw), _f32),
    ]
    scratch = [
        pltpu.VMEM((A_TM, D_MODEL), _bf16),
        pltpu.VMEM((A_TM, gw), _f32),
        pltpu.VMEM((A_TM, gw), _f32),
        pltpu.VMEM((POOL_HALO + A_TM, gw), _f32),
        pltpu.VMEM((CONF_HALO + A_TM, gw), _f32),
        pltpu.VMEM((SC_HALO + A_TM, gw), _f32),
    ]
    return pl.pallas_call(
        _a_prompt_body,
        grid=(BATCH, ni, A_STEPS),
        in_specs=in_specs,
        out_specs=out_specs,
        out_shape=out_shape,
        scratch_shapes=scratch,
        compiler_params=pltpu.CompilerParams(
            dimension_semantics=("arbitrary", "arbitrary", "arbitrary"), vmem_limit_bytes=VMEM_LIMIT),
        name="mixers_prompt",
    )(x_all, mod_p4, mod_p4, p["nmix"], w_in_b, p["poolw"], p["pscale"], p["sgun"], p["sguw"], p["sgub"], *marrs)


def _a_sample_body(x_ref, sh1_ref, sc1_ref, nmix_ref, w_ref, poolw_ref, pscale_ref, sgun_ref, sgwc_ref,
                   sgbc_ref, cw_ref, cb_ref, clg_ref, clb_ref, scw_ref, onorm_ref, pst_ref, cst_ref, sst_ref,
                   y_ref, a_o, g_o, z_o, v_o,
                   h_ref, bufa, bufb):
    n = pl.program_id(0)
    gw = GROUP_WIDTH
    nb = DEC_BATCH
    ns = DEC_SEQ

    @pl.when(n == 0)
    def _():
        hn = _rms(x_ref[...]) * nmix_ref[...]
        h_ref[...] = _slab_add(_slab_mul(hn, 1.0 + sc1_ref[...]), sh1_ref[...]).astype(_bf16)

    def proj():
        return jnp.dot(h_ref[...], w_ref[...], preferred_element_type=_f32)

    def emit(g, y):
        yn = _rms(y) * onorm_ref[:, g * gw:(g + 1) * gw]
        y_ref[:, g * gw:(g + 1) * gw] = yn.astype(_bf16)

    def slab(x, s):
        return x[s * nb:(s + 1) * nb, :]

    @pl.when(n == 0)
    def _():
        a = proj()
        a_o[...] = a.reshape(ns, nb, gw)
        npast = POOL_MAX - 1
        for s in range(ns):
            for gi, w in enumerate(POOL_WINDOWS):
                c0 = gi * POOL_CH
                acc = None
                for j in range(w):
                    e = npast + s - j
                    term = pst_ref[e, :, c0:c0 + POOL_CH] if e < npast else slab(a, e - npast)[:, c0:c0 + POOL_CH]
                    acc = term if acc is None else acc + term
                cnt = min(float(w), PAST_LEN + s + 1.0)
                pooled = acc / cnt - slab(a, s)[:, c0:c0 + POOL_CH]
                bufb[s * nb:(s + 1) * nb, c0:c0 + POOL_CH] = pooled
        pooled = bufb[...].astype(_bf16)
        for gi in range(len(POOL_WINDOWS)):
            c0 = gi * POOL_CH
            bufa[:, c0:c0 + POOL_CH] = jnp.dot(pooled[:, c0:c0 + POOL_CH], poolw_ref[gi].astype(_bf16),
                                               preferred_element_type=_f32)
        emit(0, bufa[...] * pscale_ref[...])

    @pl.when(n == 1)
    def _():
        bufa[...] = _gelu(proj())

    @pl.when(n == 2)
    def _():
        v = _ln(_gelu(proj())) * sgun_ref[...]
        v_o[...] = v.reshape(ns, nb, gw)
        for t in range(ns):
            m = jnp.broadcast_to(sgbc_ref[t:t + 1, :], (nb, gw))
            for s in range(t + 1):
                m = m + sgwc_ref[t * ns + s:t * ns + s + 1, :] * slab(v, s)
            bufb[t * nb:(t + 1) * nb, :] = m
        emit(1, bufa[...] * bufb[...])

    @pl.when(n == 3)
    def _():
        bufa[...] = proj()

    @pl.when(n == 4)
    def _():
        g = bufa[...] * jax.nn.sigmoid(proj())
        g_o[...] = g.reshape(ns, nb, gw)
        npast = CONF_KERNEL - 1
        for s in range(ns):
            acc = jnp.broadcast_to(cb_ref[...], (nb, gw))
            for k in range(CONF_KERNEL):
                e = s + k
                term = cst_ref[e] if e < npast else slab(g, e - npast)
                acc = acc + cw_ref[k:k + 1, :] * term
            bufb[s * nb:(s + 1) * nb, :] = acc
        yn = _ln(bufb[...]) * clg_ref[...] + clb_ref[...]
        emit(2, yn * jax.nn.sigmoid(yn))

    @pl.when(n == 5)
    def _():
        bufa[...] = proj()

    @pl.when(n == 6)
    def _():
        bufb[...] = proj()

    @pl.when(n == 7)
    def _():
        z = bufb[...] * proj()
        z_o[...] = z.reshape(ns, nb, gw)
        npast = SC_KERNEL - 1
        for s in range(ns):
            acc = None
            for k in range(SC_KERNEL):
                e = s + k
                term = scw_ref[k:k + 1, :] * (sst_ref[e] if e < npast else slab(z, e - npast))
                acc = term if acc is None else acc + term
            bufb[s * nb:(s + 1) * nb, :] = acc
        emit(3, bufa[...] * bufb[...])


def _a_sample(l, x_all, x_blk, mod_s, w_in_b, p, pool_t, conf_t, sc_t):
    gw = GROUP_WIDTH
    ns, nb = DEC_SEQ, DEC_BATCH
    marrs, mspecs = _mixer_params(l, p, 1)
    in_specs = [
        pl.BlockSpec((T_SAMPLE, D_MODEL), lambda n: (x_blk, 0)),
        pl.BlockSpec((None, nb, D_MODEL), lambda n: (l, 0, 0)),
        pl.BlockSpec((None, nb, D_MODEL), lambda n: (l, 0, 1)),
        _layer_spec((1, D_MODEL), l, 1),
        pl.BlockSpec((None, D_MODEL, A_TN), lambda n: (l, 0, n)),
        _layer_spec((len(POOL_WINDOWS), POOL_CH, POOL_CH), l, 1),
        _layer_spec((1, gw), l, 1),
        _layer_spec((1, gw), l, 1),
        _layer_spec((ns * ns, gw), l, 1),
        _layer_spec((ns, gw), l, 1),
    ] + mspecs + [
        _layer_spec((POOL_MAX - 1, nb, gw), l, 1),
        _layer_spec((CONF_KERNEL - 1, nb, gw), l, 1),
        _layer_spec((SC_KERNEL - 1, nb, gw), l, 1),
    ]
    out_specs = [pl.BlockSpec((T_SAMPLE, D_MODEL), lambda n: (0, 0))] + [
        pl.BlockSpec((ns, nb, gw), lambda n: (0, 0, 0)) for _ in range(4)]
    out_shape = [jax.ShapeDtypeStruct((T_SAMPLE, D_MODEL), _bf16)] + [
        jax.ShapeDtypeStruct((ns, nb, gw), _f32) for _ in range(4)]
    scratch = [
        pltpu.VMEM((T_SAMPLE, D_MODEL), _bf16),
        pltpu.VMEM((T_SAMPLE, gw), _f32),
        pltpu.VMEM((T_SAMPLE, gw), _f32),
    ]
    return pl.pallas_call(
        _a_sample_body,
        grid=(A_STEPS,),
        in_specs=in_specs,
        out_specs=out_specs,
        out_shape=out_shape,
        scratch_shapes=scratch,
        compiler_params=pltpu.CompilerParams(
            dimension_semantics=("arbitrary",), vmem_limit_bytes=VMEM_LIMIT),
        name="mixers_sample",
    )(x_all, mod_s, mod_s, p["nmix"], w_in_b, p["poolw"], p["pscale"], p["sgun"], p["sgwc"], p["sgbc"],
      *marrs, pool_t, conf_t, sc_t)


def _pick_mod(is_prompt, mp_ref, ms_ref):
    return jnp.where(is_prompt, jnp.broadcast_to(mp_ref[...], (DEC_BATCH, D_MODEL)), ms_ref[...])


def _b_body(xp_ref, xs_ref, yp_ref, ys_ref, w_ref, mpg1, mpsh2, mpsc2, msg1, mssh2, mssc2, nffn_ref, rw_ref, rb_ref,
            x1_ref, h2_ref, pos_ref, post_ref, meta_ref):
    i = pl.program_id(0)
    isp = i < B_PROMPT_BLOCKS
    ycat = jnp.where(isp, yp_ref[...], ys_ref[...])
    o = jnp.dot(ycat, w_ref[...], preferred_element_type=_f32)
    x1 = jnp.where(isp, xp_ref[...], xs_ref[...]) + _slab_mul(o, _pick_mod(isp, mpg1, msg1))
    x1_ref[...] = x1
    hn = _rms(x1) * nffn_ref[...]
    h2 = _slab_add(_slab_mul(hn, 1.0 + _pick_mod(isp, mpsc2, mssc2)), _pick_mod(isp, mpsh2, mssh2))
    hi = h2.astype(_bf16)
    h2_ref[:, 0:D_MODEL] = hi
    lo = (h2 - hi.astype(_f32)).astype(_bf16)
    l1 = jnp.dot(hi, rw_ref[...], preferred_element_type=_f32)
    l2 = jnp.dot(lo, rw_ref[:, 0:LANES], preferred_element_type=_f32)
    logits = l1[:, 0:LANES] + l1[:, LANES:] + l2 + rb_ref[...]
    lane = lax.broadcasted_iota(jnp.int32, (B_TM, LANES), 1).astype(_f32)
    neg = jnp.float32(-jnp.inf)
    cur = jnp.where(lane < N_EXPERTS, logits, neg)
    vals, ids = [], []
    for _ in range(TOP_K):
        m = jnp.max(cur, axis=-1, keepdims=True)
        ix = jnp.min(jnp.where(cur == m, lane, float(LANES)), axis=-1, keepdims=True)
        vals.append(m)
        ids.append(ix)
        cur = jnp.where(lane == ix, neg, cur)
    es = [jnp.exp(v - vals[0]) for v in vals]
    tot = es[0] + es[1] + es[2] + es[3]
    side = jnp.zeros((B_TM, LANES), _f32)
    for k in range(TOP_K):
        g = es[k] / tot
        g0 = g.astype(_bf16).astype(_f32)
        g1 = (g - g0).astype(_bf16).astype(_f32)
        for j, term in enumerate((g0, g1, g - g0 - g1)):
            side = jnp.where(lane == float(GATE_TERMS * k + j), term, side)
        side = jnp.where(lane == float(SIDE_ID0 + k), ids[k], side)
    h2_ref[:, D_MODEL:] = side.astype(_bf16)

    hit = [lane == ix for ix in ids]
    chosen = jnp.zeros((B_TM, LANES), _f32)
    for h in hit:
        chosen = jnp.where(h, 1.0, chosen)
    cnt = jnp.sum(chosen, axis=0, keepdims=True)
    seg_rows = jnp.floor((cnt + (SEG - 1.0)) * (1.0 / SEG)) * SEG
    er = lax.broadcasted_iota(jnp.int32, (LANES, LANES), 0)
    ec = lax.broadcasted_iota(jnp.int32, (LANES, LANES), 1)
    before = jnp.where(er < ec, 1.0, 0.0).astype(_bf16)
    seg_start = jnp.dot(jnp.broadcast_to(seg_rows, (8, LANES)).astype(_bf16), before,
                        preferred_element_type=_f32)[0:1, :]
    tr = lax.broadcasted_iota(jnp.int32, (B_TM, B_TM), 0)
    tc = lax.broadcasted_iota(jnp.int32, (B_TM, B_TM), 1)
    earlier = jnp.where(tr > tc, 1.0, 0.0).astype(_bf16)
    rank = jnp.dot(earlier, chosen.astype(_bf16), preferred_element_type=_f32)
    base = seg_start + rank
    pos = jnp.zeros((B_TM, LANES), _f32)
    for k in range(TOP_K):
        pk = jnp.sum(jnp.where(hit[k], base, 0.0), axis=-1, keepdims=True)
        pos = jnp.where(lane == float(k), pk, pos)
    pos_ref[...] = pos
    post_ref[...] = jnp.transpose(pos)[0:8, :]
    sub = lax.broadcasted_iota(jnp.int32, (8, LANES), 0)
    meta = jnp.where(sub == 0, jnp.broadcast_to(seg_start, (8, LANES)),
                     jnp.where(sub == 1, jnp.broadcast_to(seg_rows, (8, LANES)), 0.0))
    meta_ref[...] = meta.astype(jnp.int32)


def _stage_b(l, x_p, x_s, xs_blk, ycat_p, ycat_s, w_out_b, mod_p4, mod_s, p):
    def pb(i):
        return jnp.minimum(i // (SEQ // B_TM), BATCH - 1)

    def mp_spec(j):
        return pl.BlockSpec((None, None, 1, D_MODEL), lambda i: (l, pb(i), 0, j))

    def ms_spec(j):
        return pl.BlockSpec((None, DEC_BATCH, D_MODEL), lambda i: (l, 0, j))

    row_spec = pl.BlockSpec((B_TM, D_MODEL), lambda i: (i, 0))
    in_specs = [
        pl.BlockSpec((B_TM, D_MODEL), lambda i: (jnp.minimum(i, B_PROMPT_BLOCKS - 1), 0)),
        pl.BlockSpec((T_SAMPLE, D_MODEL), lambda i: (xs_blk, 0), pipeline_mode=pl.Buffered(1)),
        pl.BlockSpec((B_TM, D_MODEL), lambda i: (jnp.minimum(i, B_PROMPT_BLOCKS - 1), 0)),
        pl.BlockSpec((T_SAMPLE, D_MODEL), lambda i: (0, 0), pipeline_mode=pl.Buffered(1)),
        pl.BlockSpec((None, D_MODEL, D_MODEL), lambda i: (l, 0, 0), pipeline_mode=pl.Buffered(1)),
        mp_spec(2), mp_spec(3), mp_spec(4), ms_spec(2), ms_spec(3), ms_spec(4),
        _layer_spec((1, D_MODEL), l, 1),
        _layer_spec((D_MODEL, 2 * LANES), l, 1),
        _layer_spec((1, LANES), l, 1),
    ]
    lane_spec = pl.BlockSpec((B_TM, LANES), lambda i: (i, 0))
    return pl.pallas_call(
        _b_body,
        grid=(N_BLOCKS,),
        in_specs=in_specs,
        out_specs=[row_spec, pl.BlockSpec((B_TM, H_EXT), lambda i: (i, 0)), lane_spec,
                   pl.BlockSpec((None, 8, B_TM), lambda i: (i, 0, 0)),
                   pl.BlockSpec((None, 8, LANES), lambda i: (i, 0, 0))],
        out_shape=[
            jax.ShapeDtypeStruct((T_ALL, D_MODEL), _f32),
            jax.ShapeDtypeStruct((T_ALL, H_EXT), _bf16),
            jax.ShapeDtypeStruct((T_ALL, LANES), _f32),
            jax.ShapeDtypeStruct((N_BLOCKS, 8, B_TM), _f32),
            jax.ShapeDtypeStruct((N_BLOCKS, 8, LANES), jnp.int32),
        ],
        compiler_params=pltpu.CompilerParams(
            dimension_semantics=("arbitrary",), vmem_limit_bytes=VMEM_LIMIT),
        name="outproj_router",
    )(x_p, x_s, ycat_p, ycat_s, w_out_b, mod_p4, mod_p4, mod_p4, mod_s, mod_s, mod_s, p["nffn"], p["rw"], p["rb"])


def _sort_body(post_ref, h_ref, o_ref, so_ref):
    for c in range(SORT_CHUNKS):
        j = (c * SORT_TM + lax.broadcasted_iota(jnp.int32, (SORT_TM, B_TM), 0)).astype(_f32)
        sel = jnp.zeros((SORT_TM, B_TM), _f32)
        for k in range(TOP_K):
            sel = jnp.where(j == post_ref[k:k + 1, :], 1.0, sel)
        res = jnp.dot(sel.astype(_bf16), h_ref[...], preferred_element_type=_f32)
        o_ref[c * SORT_TM:(c + 1) * SORT_TM, :] = res[:, 0:D_MODEL].astype(_bf16)
        so_ref[c * SORT_TM:(c + 1) * SORT_TM, :] = res[:, D_MODEL:]


def _local_sort(post, h_ext):
    return pl.pallas_call(
        _sort_body,
        grid=(N_BLOCKS,),
        in_specs=[
            pl.BlockSpec((None, 8, B_TM), lambda b: (b, 0, 0)),
            pl.BlockSpec((B_TM, H_EXT), lambda b: (b, 0)),
        ],
        out_specs=[pl.BlockSpec((BLOCK_ROWS, D_MODEL), lambda b: (b, 0)),
                   pl.BlockSpec((BLOCK_ROWS, LANES), lambda b: (b, 0))],
        out_shape=[jax.ShapeDtypeStruct((N_BLOCKS * BLOCK_ROWS, D_MODEL), _bf16),
                   jax.ShapeDtypeStruct((N_BLOCKS * BLOCK_ROWS, LANES), _f32)],
        compiler_params=pltpu.CompilerParams(
            dimension_semantics=("arbitrary",), vmem_limit_bytes=VMEM_LIMIT),
        name="local_sort",
    )(post, h_ext)


def _moe_body(ie_ref, ilo_ref, int_ref, start_ref, rows_ref, used_ref,
              wg_ref, wl_ref, wd_ref, bg_ref, bl_ref, bd_ref, x_hbm, g_hbm, y_hbm,
              xbuf, hbuf, gbuf, wc, seg_src, nseg, pending, sem_x, sem_y):
    i = pl.program_id(0)
    n = pl.program_id(1)
    e = ie_ref[i]
    lo = ilo_ref[i]
    nt = int_ref[i]
    hi = lo + nt * MOE_TM
    tm, tn = MOE_TM, MOE_TN

    def x_copy(src, dst):
        return pltpu.make_async_copy(x_hbm.at[pl.ds(src, SEG)], xbuf.at[pl.ds(dst, SEG)], sem_x.at[dst // tm])

    def g_copy(src, dst):
        return pltpu.make_async_copy(g_hbm.at[pl.ds(src, SEG)], gbuf.at[pl.ds(dst, SEG)], sem_x.at[dst // tm])

    def y_copy(src, dst):
        return pltpu.make_async_copy(xbuf.at[pl.ds(src, SEG)], y_hbm.at[pl.ds(dst, SEG)], sem_y.at[0])

    def drain():
        def w(_, carry):
            y_copy(0, 0).wait()
            return carry
        lax.fori_loop(0, pending[0], w, 0)
        pending[0] = 0

    @pl.when((i == 0) & (n == 0))
    def _():
        pending[0] = 0

    @pl.when((n == 0) & (nt > 0))
    def _():
        drain()

        def per_block(b, p):
            rows = rows_ref[b * N_EXPERTS + e]
            start = start_ref[b * N_EXPERTS + e]

            def per_piece(g, carry):
                q = p + g * SEG

                @pl.when((q >= lo) & (q < hi))
                def _():
                    src = pl.multiple_of(b * BLOCK_ROWS + start + g * SEG, SEG)
                    dst = pl.multiple_of(q - lo, SEG)
                    seg_src[dst // SEG] = src
                    x_copy(src, dst).start()
                    g_copy(src, dst).start()
                return carry
            lax.fori_loop(0, rows // SEG, per_piece, 0)
            return p + rows
        end = lax.fori_loop(0, N_BLOCKS, per_block, 0)
        ns = (jnp.minimum(end, hi) - lo) // SEG
        nseg[0] = ns

        def zero_piece(g, carry):
            r = pl.multiple_of(g * SEG, SEG)
            xbuf[pl.ds(r, SEG), :] = jnp.zeros((SEG, D_MODEL), _bf16)
            gbuf[pl.ds(r, SEG), :] = jnp.zeros((SEG, LANES), _f32)
            return carry
        lax.fori_loop(ns, nt * TILE_SEGS, zero_piece, 0)

    def tile_pieces(t):
        return jnp.clip(nseg[0] - t * TILE_SEGS, 0, TILE_SEGS)

    for c in range(MOE_UP_STEPS):
        @pl.when((n == c) & (nt > 0))
        def _(c=c):
            wc[:, 0:tn] = wg_ref[...].astype(_bf16)
            wc[:, tn:] = wl_ref[...].astype(_bf16)

            def body(t, carry):
                if c == 0:
                    def w(_, cc):
                        x_copy(0, pl.multiple_of(t * tm, tm)).wait()
                        g_copy(0, pl.multiple_of(t * tm, tm)).wait()
                        return cc
                    lax.fori_loop(0, tile_pieces(t), w, 0)
                r0 = pl.multiple_of(t * tm, tm)
                up = jnp.dot(xbuf[pl.ds(r0, tm), :], wc[...], preferred_element_type=_f32)
                glu = jnp.minimum(up[:, 0:tn] + bg_ref[...], SWIGLU_LIMIT)
                lin = jnp.clip(up[:, tn:] + bl_ref[...], -SWIGLU_LIMIT, SWIGLU_LIMIT)
                act = glu * jax.nn.sigmoid(SWIGLU_ALPHA * glu) * (lin + 1.0)
                hbuf[pl.ds(r0, tm), c * tn:(c + 1) * tn] = act.astype(_bf16)
                return carry
            lax.fori_loop(0, nt, body, 0)

    for c in range(MOE_DOWN_STEPS):
        @pl.when((n == MOE_UP_STEPS + c) & (nt > 0))
        def _(c=c):
            wc[:, 0:tn] = wd_ref[...].astype(_bf16)

            def body(t, carry):
                r0 = pl.multiple_of(t * tm, tm)
                y = jnp.dot(hbuf[pl.ds(r0, tm), :], wc[:, 0:tn], preferred_element_type=_f32) + bd_ref[...]
                g = gbuf[pl.ds(r0, tm), :]
                gate = g[:, 0:1] + g[:, 1:2] + g[:, 2:3]
                xbuf[pl.ds(r0, tm), c * tn:(c + 1) * tn] = (y * gate).astype(_bf16)
                if c == MOE_DOWN_STEPS - 1:
                    np_ = tile_pieces(t)
                    for q in range(TILE_SEGS):
                        @pl.when(q < np_)
                        def _(q=q):
                            y_copy(pl.multiple_of(r0 + q * SEG, SEG),
                                   pl.multiple_of(seg_src[t * TILE_SEGS + q], SEG)).start()
                return carry
            lax.fori_loop(0, nt, body, 0)
            if c == MOE_DOWN_STEPS - 1:
                pending[0] = nseg[0]

    @pl.when((i == MOE_ITEMS - 1) & (n == MOE_STEPS - 1))
    def _():
        drain()
        hbuf[0:SEG, :] = jnp.zeros((SEG, D_EXPERT), _bf16)

        def zero_copy(row):
            return pltpu.make_async_copy(hbuf.at[pl.ds(0, SEG)], y_hbm.at[pl.ds(row, SEG)], sem_y.at[0])

        def per_block(b, carry):
            def start(g, cc):
                zero_copy(pl.multiple_of(b * BLOCK_ROWS + g * SEG, SEG)).start()
                return cc

            def wait(g, cc):
                zero_copy(0).wait()
                return cc
            first = used_ref[b] // SEG
            lax.fori_loop(first, BLOCK_ROWS // SEG, start, 0)
            lax.fori_loop(first, BLOCK_ROWS // SEG, wait, 0)
            return carry
        lax.fori_loop(0, N_BLOCKS, per_block, 0)


def _moe(l, item_e, item_lo, item_nt, seg_start, seg_rows, used, w_up, b_up4, w_down, b_down4, x_sorted,
         g_sorted):
    def col_up(n, valid):
        return jnp.where(valid, jnp.minimum(n, MOE_UP_STEPS - 1), MOE_UP_STEPS - 1)

    def col_down(n, valid):
        return jnp.where(valid, jnp.maximum(n - MOE_UP_STEPS, 0), MOE_DOWN_STEPS - 1)

    def glu_map(i, n, ie, ilo, int_, *_):
        return (l, ie[i], 0, col_up(n, int_[i] > 0))

    def lin_map(i, n, ie, ilo, int_, *_):
        return (l, ie[i], 0, MOE_UP_STEPS + col_up(n, int_[i] > 0))

    def down_map(i, n, ie, ilo, int_, *_):
        return (l, ie[i], 0, col_down(n, int_[i] > 0))

    grid_spec = pltpu.PrefetchScalarGridSpec(
        num_scalar_prefetch=6,
        grid=(MOE_ITEMS, MOE_STEPS),
        in_specs=[
            pl.BlockSpec((None, None, D_MODEL, MOE_TN), glu_map),
            pl.BlockSpec((None, None, D_MODEL, MOE_TN), lin_map),
            pl.BlockSpec((None, None, D_EXPERT, MOE_TN), down_map),
            pl.BlockSpec((None, None, 1, MOE_TN), glu_map),
            pl.BlockSpec((None, None, 1, MOE_TN), lin_map),
            pl.BlockSpec((None, None, 1, MOE_TN), down_map),
            pl.BlockSpec(memory_space=pl.ANY),
            pl.BlockSpec(memory_space=pl.ANY),
        ],
        out_specs=pl.BlockSpec(memory_space=pl.ANY),
        scratch_shapes=[
            pltpu.VMEM((MOE_MT * MOE_TM, D_MODEL), _bf16),
            pltpu.VMEM((MOE_MT * MOE_TM, D_EXPERT), _bf16),
            pltpu.VMEM((MOE_MT * MOE_TM, LANES), _f32),
            pltpu.VMEM((D_MODEL, 2 * MOE_TN), _bf16),
            pltpu.SMEM((MOE_MT * TILE_SEGS,), jnp.int32),
            pltpu.SMEM((1,), jnp.int32),
            pltpu.SMEM((1,), jnp.int32),
            pltpu.SemaphoreType.DMA((MOE_MT,)),
            pltpu.SemaphoreType.DMA((1,)),
        ],
    )
    return pl.pallas_call(
        _moe_body,
        grid_spec=grid_spec,
        out_shape=jax.ShapeDtypeStruct((N_BLOCKS * BLOCK_ROWS, D_MODEL), _bf16),
        compiler_params=pltpu.CompilerParams(
            dimension_semantics=("arbitrary", "arbitrary"), vmem_limit_bytes=VMEM_LIMIT),
        name="moe_experts",
    )(item_e, item_lo, item_nt, seg_start, seg_rows, used, w_up, w_up, w_down, b_up4, b_up4, b_down4, x_sorted,
      g_sorted)


MOE_DN = 1024
STREAM_UP_STEPS = D_EXPERT // MOE_TN
STREAM_DOWN_STEPS = D_MODEL // MOE_DN
STREAM_STEPS = STREAM_UP_STEPS + STREAM_DOWN_STEPS
MOE_WSLOTS = 3
CAST_ROWS = D_MODEL // MOE_MT


def _moe_stream_body(ie_ref, ilo_ref, int_ref, start_ref, rows_ref, used_ref, nitems_ref,
                     wup_hbm, wdn_hbm, bup_hbm, bdn_hbm, x_hbm, g_hbm, y_hbm,
                     wbuf, wc, xbuf, hbuf, gbuf, bup, bdn, seg_src, nseg, pending,
                     sem_w, sem_b, sem_x, sem_y, *, l):
    tm, tn = MOE_TM, MOE_TN
    n_items = nitems_ref[0]
    n_steps = n_items * STREAM_STEPS

    def w_half(src_ref, e, col0, slot, half):
        return pltpu.make_async_copy(src_ref.at[l, e, :, pl.ds(col0, tn)],
                                     wbuf.at[slot, :, pl.ds(half * tn, tn)], sem_w.at[slot])

    def issue_w(s):
        e = ie_ref[s // STREAM_STEPS]
        n = s % STREAM_STEPS
        slot = s % MOE_WSLOTS
        for c in range(STREAM_UP_STEPS):
            @pl.when(n == c)
            def _(c=c):
                w_half(wup_hbm, e, c * tn, slot, 0).start()
                w_half(wup_hbm, e, D_EXPERT + c * tn, slot, 1).start()
        for c in range(STREAM_DOWN_STEPS):
            @pl.when(n == STREAM_UP_STEPS + c)
            def _(c=c):
                w_half(wdn_hbm, e, c * MOE_DN, slot, 0).start()
                w_half(wdn_hbm, e, c * MOE_DN + tn, slot, 1).start()

    def wait_w(s):
        slot = s % MOE_WSLOTS
        for half in range(2):
            w_half(wdn_hbm, 0, 0, slot, half).wait()

    def b_copies(item):
        e = ie_ref[item]
        par = item % 2
        return (pltpu.make_async_copy(bup_hbm.at[l, e], bup.at[par], sem_b.at[par]),
                pltpu.make_async_copy(bdn_hbm.at[l, e], bdn.at[par], sem_b.at[par]))

    def x_copy(src, dst):
        return pltpu.make_async_copy(x_hbm.at[pl.ds(src, SEG)], xbuf.at[pl.ds(dst, SEG)], sem_x.at[dst // tm])

    def g_copy(src, dst):
        return pltpu.make_async_copy(g_hbm.at[pl.ds(src, SEG)], gbuf.at[pl.ds(dst, SEG)], sem_x.at[dst // tm])

    def y_copy(src, dst):
        return pltpu.make_async_copy(xbuf.at[pl.ds(src, SEG)], y_hbm.at[pl.ds(dst, SEG)], sem_y.at[0])

    def drain():
        def w(_, carry):
            y_copy(0, 0).wait()
            return carry
        lax.fori_loop(0, pending[0], w, 0)
        pending[0] = 0

    def tile_pieces(t):
        return jnp.clip(nseg[0] - t * TILE_SEGS, 0, TILE_SEGS)

    pending[0] = 0
    issue_w(0)
    issue_w(1)
    for cp in b_copies(0):
        cp.start()
    wait_w(0)
    wc[0] = wbuf[0].astype(_bf16)

    def step(s, carry):
        item = s // STREAM_STEPS
        n = s % STREAM_STEPS
        e = ie_ref[item]
        lo = ilo_ref[item]
        nt = int_ref[item]
        hi = lo + nt * tm
        par = item % 2
        cur = s % 2
        nxt = (s + 1) % 2
        nslot = (s + 1) % MOE_WSLOTS

        @pl.when(s + 2 < n_steps)
        def _():
            issue_w(s + 2)

        @pl.when(s + 1 < n_steps)
        def _():
            wait_w(s + 1)

        def cast_slice(t):
            r = pl.multiple_of(t * CAST_ROWS, CAST_ROWS)
            wc[nxt, pl.ds(r, CAST_ROWS), :] = wbuf[nslot, pl.ds(r, CAST_ROWS), :].astype(_bf16)

        @pl.when(n == 0)
        def _():
            drain()

            def per_block(b, p):
                rows = rows_ref[b * N_EXPERTS + e]
                start = start_ref[b * N_EXPERTS + e]

                def per_piece(g, cc):
                    q = p + g * SEG

                    @pl.when((q >= lo) & (q < hi))
                    def _():
                        src = pl.multiple_of(b * BLOCK_ROWS + start + g * SEG, SEG)
                        dst = pl.multiple_of(q - lo, SEG)
                        seg_src[dst // SEG] = src
                        x_copy(src, dst).start()
                        g_copy(src, dst).start()
                    return cc
                lax.fori_loop(0, rows // SEG, per_piece, 0)
                return p + rows
            end = lax.fori_loop(0, N_BLOCKS, per_block, 0)
            ns = (jnp.minimum(end, hi) - lo) // SEG
            nseg[0] = ns

            def zero_piece(g, cc):
                r = pl.multiple_of(g * SEG, SEG)
                xbuf[pl.ds(r, SEG), :] = jnp.zeros((SEG, D_MODEL), _bf16)
                gbuf[pl.ds(r, SEG), :] = jnp.zeros((SEG, LANES), _f32)
                return cc
            lax.fori_loop(ns, nt * TILE_SEGS, zero_piece, 0)

            for cp in b_copies(item):
                cp.wait()

            @pl.when(item + 1 < n_items)
            def _():
                for cp in b_copies(item + 1):
                    cp.start()

        for c in range(STREAM_UP_STEPS):
            @pl.when(n == c)
            def _(c=c):
                def body(t, cc):
                    if c == 0:
                        def w(_, c3):
                            x_copy(0, pl.multiple_of(t * tm, tm)).wait()
                            g_copy(0, pl.multiple_of(t * tm, tm)).wait()
                            return c3
                        lax.fori_loop(0, tile_pieces(t), w, 0)
                    cast_slice(t)
                    r0 = pl.multiple_of(t * tm, tm)
                    up = jnp.dot(xbuf[pl.ds(r0, tm), :], wc[cur], preferred_element_type=_f32)
                    bg = bup[par, :, c * tn:(c + 1) * tn]
                    bl = bup[par, :, D_EXPERT + c * tn:D_EXPERT + (c + 1) * tn]
                    glu = jnp.minimum(up[:, 0:tn] + bg, SWIGLU_LIMIT)
                    lin = jnp.clip(up[:, tn:] + bl, -SWIGLU_LIMIT, SWIGLU_LIMIT)
                    act = glu * jax.nn.sigmoid(SWIGLU_ALPHA * glu) * (lin + 1.0)
                    hbuf[pl.ds(r0, tm), c * tn:(c + 1) * tn] = act.astype(_bf16)
                    return cc
                lax.fori_loop(0, nt, body, 0)

        for c in range(STREAM_DOWN_STEPS):
            @pl.when(n == STREAM_UP_STEPS + c)
            def _(c=c):
                def body(t, cc):
                    cast_slice(t)
                    r0 = pl.multiple_of(t * tm, tm)
                    y = (jnp.dot(hbuf[pl.ds(r0, tm), :], wc[cur], preferred_element_type=_f32)
                         + bdn[par, :, c * MOE_DN:(c + 1) * MOE_DN])
                    g = gbuf[pl.ds(r0, tm), :]
                    ef = e.astype(_f32)
                    gate = jnp.zeros((tm, 1), _f32)
                    for k in range(TOP_K):
                        gk = g[:, GATE_TERMS * k:GATE_TERMS * k + 1]
                        for j in range(1, GATE_TERMS):
                            gk = gk + g[:, GATE_TERMS * k + j:GATE_TERMS * k + j + 1]
                        gate = jnp.where(g[:, SIDE_ID0 + k:SIDE_ID0 + k + 1] == ef, gk, gate)
                    xbuf[pl.ds(r0, tm), c * MOE_DN:(c + 1) * MOE_DN] = (y * gate).astype(_bf16)
                    if c == STREAM_DOWN_STEPS - 1:
                        np_ = tile_pieces(t)
                        for q in range(TILE_SEGS):
                            @pl.when(q < np_)
                            def _(q=q):
                                y_copy(pl.multiple_of(r0 + q * SEG, SEG),
                                       pl.multiple_of(seg_src[t * TILE_SEGS + q], SEG)).start()
                    return cc
                lax.fori_loop(0, nt, body, 0)
                if c == STREAM_DOWN_STEPS - 1:
                    pending[0] = nseg[0]

        def rest(t, cc):
            cast_slice(t)
            return cc
        lax.fori_loop(nt, MOE_MT, rest, 0)
        return carry
    lax.fori_loop(0, n_steps, step, 0)

    drain()
    hbuf[0:SEG, :] = jnp.zeros((SEG, D_EXPERT), _bf16)

    def zero_copy(row):
        return pltpu.make_async_copy(hbuf.at[pl.ds(0, SEG)], y_hbm.at[pl.ds(row, SEG)], sem_y.at[0])

    def zero_block(b, carry):
        def zstart(g, cc):
            zero_copy(pl.multiple_of(b * BLOCK_ROWS + g * SEG, SEG)).start()
            return cc

        def zwait(g, cc):
            zero_copy(0).wait()
            return cc
        first = used_ref[b] // SEG
        lax.fori_loop(first, BLOCK_ROWS // SEG, zstart, 0)
        lax.fori_loop(first, BLOCK_ROWS // SEG, zwait, 0)
        return carry
    lax.fori_loop(0, N_BLOCKS, zero_block, 0)


def _moe_stream(l, item_e, item_lo, item_nt, seg_start, seg_rows, used, n_items, w_up, b_up4, w_down, b_down4,
                x_sorted, g_sorted):
    any_spec = pl.BlockSpec(memory_space=pl.ANY)
    grid_spec = pltpu.PrefetchScalarGridSpec(
        num_scalar_prefetch=7,
        grid=(1,),
        in_specs=[any_spec] * 6,
        out_specs=any_spec,
        scratch_shapes=[
            pltpu.VMEM((MOE_WSLOTS, D_MODEL, 2 * MOE_TN), _f32),
            pltpu.VMEM((2, D_MODEL, 2 * MOE_TN), _bf16),
            pltpu.VMEM((MOE_MT * MOE_TM, D_MODEL), _bf16),
            pltpu.VMEM((MOE_MT * MOE_TM, D_EXPERT), _bf16),
            pltpu.VMEM((MOE_MT * MOE_TM, LANES), _f32),
            pltpu.VMEM((2, 1, 2 * D_EXPERT), _f32),
            pltpu.VMEM((2, 1, D_MODEL), _f32),
            pltpu.SMEM((MOE_MT * TILE_SEGS,), jnp.int32),
            pltpu.SMEM((1,), jnp.int32),
            pltpu.SMEM((1,), jnp.int32),
            pltpu.SemaphoreType.DMA((MOE_WSLOTS,)),
            pltpu.SemaphoreType.DMA((2,)),
            pltpu.SemaphoreType.DMA((MOE_MT,)),
            pltpu.SemaphoreType.DMA((1,)),
        ],
    )
    return pl.pallas_call(
        functools.partial(_moe_stream_body, l=l),
        grid_spec=grid_spec,
        out_shape=jax.ShapeDtypeStruct((N_BLOCKS * BLOCK_ROWS, D_MODEL), _bf16),
        compiler_params=pltpu.CompilerParams(
            dimension_semantics=("arbitrary",), vmem_limit_bytes=60 * 1024 * 1024),
        name="moe_experts",
    )(item_e, item_lo, item_nt, seg_start, seg_rows, used, n_items, w_up, w_down, b_up4, b_down4, x_sorted,
      g_sorted)


def _work_items(meta):
    seg_start = meta[:, 0, :N_EXPERTS]
    seg_rows = meta[:, 1, :N_EXPERTS]
    rows_e = jnp.sum(seg_rows, axis=0)
    nt = (rows_e + MOE_TM - 1) // MOE_TM
    ipe = (nt + MOE_MT - 1) // MOE_MT
    cum = jnp.cumsum(ipe)
    first = cum - ipe
    total = cum[-1]
    ii = jnp.arange(MOE_ITEMS, dtype=jnp.int32)
    ii_c = jnp.minimum(ii, total - 1)
    e_i = jnp.minimum(jnp.sum((cum[None, :] <= ii_c[:, None]).astype(jnp.int32), axis=1), N_EXPERTS - 1)
    valid = ii < total
    j = ii - first[e_i]
    lo_i = jnp.where(valid, j * (MOE_MT * MOE_TM), 0)
    nt_i = jnp.where(valid, jnp.minimum(MOE_MT, nt[e_i] - j * MOE_MT), 0)
    used = jnp.sum(seg_rows, axis=1)
    i32 = jnp.int32
    return (e_i.astype(i32), lo_i.astype(i32), nt_i.astype(i32), seg_start.reshape(-1).astype(i32),
            seg_rows.reshape(-1).astype(i32), used.astype(i32), total.astype(i32).reshape(1))


def _c_body(x1_ref, y_ref, pos_ref, mpg2, msg2, nfin_ref, *rest, final):
    if final:
        outp_ref, outs_ref, sel_ref = rest
    else:
        out_ref, sel_ref = rest
    i = pl.program_id(0)
    isp = i < B_PROMPT_BLOCKS
    for c in range(SORT_CHUNKS):
        j = (c * SORT_TM + lax.broadcasted_iota(jnp.int32, (B_TM, SORT_TM), 1)).astype(_f32)
        sel = jnp.zeros((B_TM, SORT_TM), _f32)
        for k in range(TOP_K):
            sel = jnp.where(j == pos_ref[:, k:k + 1], 1.0, sel)
        sel_ref[:, c * SORT_TM:(c + 1) * SORT_TM] = sel.astype(_bf16)
    acc = jnp.dot(sel_ref[...], y_ref[...], preferred_element_type=_f32)
    x2 = x1_ref[...] + _slab_mul(acc, _pick_mod(isp, mpg2, msg2))
    if final:
        y = _rms(x2) * nfin_ref[...]

        @pl.when(isp)
        def _():
            outp_ref[...] = y

        @pl.when(jnp.logical_not(isp))
        def _():
            outs_ref[...] = y
    else:
        out_ref[...] = x2


def _combine(l, x1, y_sorted, pos, mod_p4, mod_s, nfin, final):
    def pb(i):
        return jnp.minimum(i // (SEQ // B_TM), BATCH - 1)

    row_spec = pl.BlockSpec((B_TM, D_MODEL), lambda i: (i, 0))
    lane_spec = pl.BlockSpec((B_TM, LANES), lambda i: (i, 0))
    if final:
        out_specs = [pl.BlockSpec((B_TM, D_MODEL), lambda i: (jnp.minimum(i, B_PROMPT_BLOCKS - 1), 0)),
                     pl.BlockSpec((T_SAMPLE, D_MODEL), lambda i: (0, 0))]
        out_shape = [jax.ShapeDtypeStruct((T_PROMPT, D_MODEL), _f32),
                     jax.ShapeDtypeStruct((T_SAMPLE, D_MODEL), _f32)]
    else:
        out_specs = row_spec
        out_shape = jax.ShapeDtypeStruct((T_ALL, D_MODEL), _f32)
    return pl.pallas_call(
        functools.partial(_c_body, final=final),
        grid=(N_BLOCKS,),
        in_specs=[
            row_spec,
            pl.BlockSpec((BLOCK_ROWS, D_MODEL), lambda i: (i, 0)),
            lane_spec,
            pl.BlockSpec((None, None, 1, D_MODEL), lambda i: (l, pb(i), 0, 5)),
            pl.BlockSpec((None, DEC_BATCH, D_MODEL), lambda i: (l, 0, 5)),
            pl.BlockSpec((1, D_MODEL), lambda i: (0, 0)),
        ],
        out_specs=out_specs,
        out_shape=out_shape,
        scratch_shapes=[pltpu.VMEM((B_TM, BLOCK_ROWS), _bf16)],
        compiler_params=pltpu.CompilerParams(
            dimension_semantics=("arbitrary",), vmem_limit_bytes=VMEM_LIMIT),
        name="moe_combine",
    )(x1, y_sorted, pos, mod_p4, mod_s, nfin)


def kernel(x_prompt, x_sample, c_prompt, c_sample, state_pool, state_conformer, state_shortconv, w_ada, b_ada, norm_mix, norm_ffn, norm_final, w_in, pool_w, pool_scale, sgu_norm, sgu_w, sgu_b, conf_w, conf_b, conf_ln_g, conf_ln_b, sc_w, out_norm, w_out, router_w, router_b, w_up, b_up, w_down, b_down):
    gw = GROUP_WIDTH
    ns, nb = DEC_SEQ, DEC_BATCH

    c_all = jnp.concatenate([c_sample, c_prompt, jnp.zeros((ADA_ROWS - nb - BATCH, D_MODEL), _f32)], axis=0)
    mod_s, mod_p8 = _ada(c_all, w_ada, b_ada)
    mod_p4 = mod_p8[:, :BATCH].reshape(DEPTH, BATCH, 1, 6 * D_MODEL)

    rw_hi = router_w.astype(_bf16)
    rw_lo = (router_w - rw_hi.astype(_f32)).astype(_bf16)
    lane_pad = ((0, 0), (0, 0), (0, LANES - N_EXPERTS))
    params = {
        "nmix": norm_mix.reshape(DEPTH, 1, D_MODEL),
        "nffn": norm_ffn.reshape(DEPTH, 1, D_MODEL),
        "poolw": pool_w,
        "pscale": pool_scale.reshape(DEPTH, 1, gw),
        "sgun": sgu_norm.reshape(DEPTH, 1, gw),
        "sguw": sgu_w,
        "sgub": jnp.repeat(jnp.swapaxes(sgu_b, 1, 2), SGU_CHUNK, axis=2),
        "sgwc": jnp.repeat(jnp.transpose(sgu_w[:, :, :ns, :ns], (0, 2, 3, 1)).reshape(DEPTH, ns * ns, SGU_HEADS),
                           SGU_CHUNK, axis=2),
        "sgbc": jnp.repeat(jnp.swapaxes(sgu_b[:, :, :ns], 1, 2), SGU_CHUNK, axis=2),
        "cw": conf_w,
        "cb": conf_b.reshape(DEPTH, 1, gw),
        "clg": conf_ln_g.reshape(DEPTH, 1, gw),
        "clb": conf_ln_b.reshape(DEPTH, 1, gw),
        "scw": sc_w,
        "onorm": out_norm.reshape(DEPTH, 1, D_MODEL),
        "rw": jnp.concatenate([jnp.pad(rw_hi, lane_pad), jnp.pad(rw_lo, lane_pad)], axis=2),
        "rb": jnp.pad(router_b, ((0, 0), (0, LANES - N_EXPERTS))).reshape(DEPTH, 1, LANES),
    }
    w_in_b = w_in.astype(_bf16)
    w_out_b = w_out.astype(_bf16)
    b_up4 = b_up.reshape(DEPTH, N_EXPERTS, 1, 2 * D_EXPERT)
    b_down4 = b_down.reshape(DEPTH, N_EXPERTS, 1, D_MODEL)
    nfin = norm_final.reshape(1, D_MODEL)

    pool_t = jnp.transpose(state_pool, (0, 2, 1, 3))
    conf_t = jnp.transpose(state_conformer, (0, 2, 1, 3))
    sc_t = jnp.transpose(state_shortconv, (0, 2, 1, 3))

    x_p = x_prompt.reshape(T_PROMPT, D_MODEL)
    x_s = jnp.transpose(x_sample, (1, 0, 2)).reshape(T_SAMPLE, D_MODEL)
    xs_blk = 0

    pool_p, conf_p, sc_p, pool_s, conf_s, sc_s, v_s = [], [], [], [], [], [], []
    for l in range(DEPTH):
        ycat_p, po, co, so = _a_prompt(l, x_p, mod_p4, w_in_b, params)
        ycat_s, a_new, g_new, z_new, v_new = _a_sample(l, x_s, xs_blk, mod_s, w_in_b, params, pool_t, conf_t, sc_t)
        pool_p.append(po[:, POOL_HALO - (POOL_MAX - 1):])
        conf_p.append(co[:, CONF_HALO - (CONF_KERNEL - 1):])
        sc_p.append(so[:, SC_HALO - (SC_KERNEL - 1):])
        pool_s.append(jnp.concatenate([state_pool[l][:, ns:], jnp.transpose(a_new, (1, 0, 2))], axis=1))
        conf_s.append(jnp.concatenate([state_conformer[l][:, ns:], jnp.transpose(g_new, (1, 0, 2))], axis=1))
        sc_s.append(jnp.transpose(z_new, (1, 0, 2))[:, ns - (SC_KERNEL - 1):])
        v_s.append(jnp.transpose(v_new, (1, 0, 2)))

        x1, h_ext, pos, post, meta = _stage_b(l, x_p, x_s, xs_blk, ycat_p, ycat_s, w_out_b, mod_p4, mod_s, params)
        x_sorted, g_sorted = _local_sort(post, h_ext)
        item_e, item_lo, item_nt, seg_start, seg_rows, used, n_items = _work_items(meta)
        y_sorted = _moe_stream(l, item_e, item_lo, item_nt, seg_start, seg_rows, used, n_items, w_up, b_up4,
                               w_down, b_down4, x_sorted, g_sorted)
        if l < DEPTH - 1:
            x_p = x_s = _combine(l, x1, y_sorted, pos, mod_p4, mod_s, nfin, final=False)
            xs_blk = T_PROMPT // T_SAMPLE
        else:
            y_p, y_s = _combine(l, x1, y_sorted, pos, mod_p4, mod_s, nfin, final=True)

    y_prompt = y_p.reshape(BATCH, SEQ, D_MODEL)
    y_sample = jnp.transpose(y_s.reshape(ns, nb, D_MODEL), (1, 0, 2))
    return (y_prompt, y_sample, jnp.stack(pool_p), jnp.stack(pool_s), jnp.stack(conf_p), jnp.stack(conf_s),
            jnp.stack(sc_p), jnp.stack(sc_s), jnp.stack(v_s))
```

```python
import functools

import jax
import jax.numpy as jnp
from jax import lax
from jax.experimental import pallas as pl
from jax.experimental.pallas import tpu as pltpu

D_MODEL = 2048
BATCH = 4
SEQ = 2048
DEPTH = 2
DEC_BATCH = 128
DEC_SEQ = 4
PAST_LEN = 16384
GROUP_WIDTH = 512
POOL_WINDOWS = (2, 4, 8, 16)
POOL_MAX = 16
POOL_CH = 128
SGU_HEADS = 4
SGU_CHUNK = 128
CONF_KERNEL = 31
SC_KERNEL = 3
IN_COLS = 8 * GROUP_WIDTH
N_EXPERTS = 32
TOP_K = 4
D_EXPERT = 2048
SWIGLU_ALPHA = 1.702
SWIGLU_LIMIT = 7.0
EPS = 1e-5

T_PROMPT = BATCH * SEQ
T_SAMPLE = DEC_BATCH * DEC_SEQ
T_ALL = T_PROMPT + T_SAMPLE

LANES = 128
BF16_ROWS = 16
VMEM_LIMIT = 56 * 1024 * 1024

A_TM = 1024
A_TN = 512
A_STEPS = IN_COLS // A_TN
POOL_HALO = 16
CONF_HALO = 32
SC_HALO = 8

B_TM = 512
N_BLOCKS = T_ALL // B_TM
B_PROMPT_BLOCKS = T_PROMPT // B_TM
SEG = BF16_ROWS
SORT_TM = 512
BLOCK_ROWS = -(-(B_TM * TOP_K + N_EXPERTS * (SEG - 1)) // SORT_TM) * SORT_TM
SORT_CHUNKS = BLOCK_ROWS // SORT_TM
GATE_TERMS = 3
SIDE_ID0 = GATE_TERMS * TOP_K
H_EXT = D_MODEL + LANES

MOE_TM = 256
MOE_MT = 8
MOE_TN = 512
MOE_UP_STEPS = D_EXPERT // MOE_TN
MOE_DOWN_STEPS = D_MODEL // MOE_TN
MOE_STEPS = MOE_UP_STEPS + MOE_DOWN_STEPS
MOE_YSLOTS = 4
TILE_SEGS = MOE_TM // SEG
MOE_TILES = (T_ALL * TOP_K + N_BLOCKS * N_EXPERTS * (SEG - 1) + N_EXPERTS * (MOE_TM - 1)) // MOE_TM
MOE_ITEMS = (MOE_TILES + N_EXPERTS * (MOE_MT - 1)) // MOE_MT

_f32 = jnp.float32
_bf16 = jnp.bfloat16


def _rms(x):
    return x * lax.rsqrt(jnp.mean(x * x, axis=-1, keepdims=True) + EPS)


def _ln(x):
    xc = x - jnp.mean(x, axis=-1, keepdims=True)
    return xc * lax.rsqrt(jnp.mean(xc * xc, axis=-1, keepdims=True) + EPS)


def _gelu(x):
    return 0.5 * x * (1.0 + lax.erf(x * (0.5 ** 0.5)))


def _slab_mul(x, m):
    r, c = x.shape
    return (x.reshape(r // 128, 128, c) * m[None]).reshape(r, c)


def _slab_add(x, m):
    r, c = x.shape
    return (x.reshape(r // 128, 128, c) + m[None]).reshape(r, c)


ADA_TN = 1024
ADA_ROWS = DEC_BATCH + 8


def _ada_body(c_ref, w_ref, b_ref, os_ref, op_ref):
    c = c_ref[...]
    a = (c * jax.nn.sigmoid(c)).astype(_bf16)
    r = jnp.dot(a, w_ref[...].astype(_bf16), preferred_element_type=_f32) + b_ref[...]
    os_ref[...] = r[:DEC_BATCH]
    op_ref[...] = r[DEC_BATCH:]


def _ada(c_all, w_ada, b_ada):
    nj = 6 * D_MODEL // ADA_TN
    return pl.pallas_call(
        _ada_body,
        grid=(DEPTH, nj),
        in_specs=[
            pl.BlockSpec((ADA_ROWS, D_MODEL), lambda l, j: (0, 0)),
            pl.BlockSpec((None, D_MODEL, ADA_TN), lambda l, j: (l, 0, j)),
            pl.BlockSpec((None, 1, ADA_TN), lambda l, j: (l, 0, j)),
        ],
        out_specs=[
            pl.BlockSpec((None, DEC_BATCH, ADA_TN), lambda l, j: (l, 0, j)),
            pl.BlockSpec((None, 8, ADA_TN), lambda l, j: (l, 0, j)),
        ],
        out_shape=[
            jax.ShapeDtypeStruct((DEPTH, DEC_BATCH, 6 * D_MODEL), _f32),
            jax.ShapeDtypeStruct((DEPTH, 8, 6 * D_MODEL), _f32),
        ],
        compiler_params=pltpu.CompilerParams(
            dimension_semantics=("arbitrary", "arbitrary"), vmem_limit_bytes=VMEM_LIMIT),
        name="ada_mod",
    )(c_all, w_ada, b_ada.reshape(DEPTH, 1, 6 * D_MODEL))


def _a_prompt_body(x_ref, sh1_ref, sc1_ref, nmix_ref, w_ref, poolw_ref, pscale_ref, sgun_ref, sguw_ref,
                   sgub_ref, cw_ref, cb_ref, clg_ref, clb_ref, scw_ref, onorm_ref,
                   y_ref, pool_o, conf_o, sc_o,
                   h_ref, bufa, bufb, exta, extc, extz):
    i = pl.program_id(1)
    n = pl.program_id(2)
    tm = A_TM
    gw = GROUP_WIDTH

    @pl.when(n == 0)
    def _():
        hn = _rms(x_ref[...]) * nmix_ref[...]
        h_ref[...] = (hn * (1.0 + sc1_ref[...]) + sh1_ref[...]).astype(_bf16)

    def proj():
        return jnp.dot(h_ref[...], w_ref[...], preferred_element_type=_f32)

    def emit(g, y):
        yn = _rms(y) * onorm_ref[:, g * gw:(g + 1) * gw]
        y_ref[:, g * gw:(g + 1) * gw] = yn.astype(_bf16)

    def carry_halo(ext, halo):
        @pl.when(i == 0)
        def _():
            ext[0:halo, :] = jnp.zeros((halo, gw), _f32)

        @pl.when(i > 0)
        def _():
            ext[0:halo, :] = ext[tm:tm + halo, :]

    @pl.when(n == 0)
    def _():
        a = proj()
        carry_halo(exta, POOL_HALO)
        exta[POOL_HALO:, :] = a
        pool_o[...] = a[tm - POOL_HALO:, :]
        pos = (i * tm + lax.broadcasted_iota(jnp.int32, (tm, 1), 0)).astype(_f32)
        for gi, w in enumerate(POOL_WINDOWS):
            c0 = gi * POOL_CH
            s = exta[POOL_HALO:POOL_HALO + tm, c0:c0 + POOL_CH]
            for j in range(1, w):
                s = s + exta[POOL_HALO - j:POOL_HALO - j + tm, c0:c0 + POOL_CH]
            cnt = jnp.minimum(float(w), pos + 1.0)
            pooled = s / cnt - a[:, c0:c0 + POOL_CH]
            yg = jnp.dot(pooled.astype(_bf16), poolw_ref[gi].astype(_bf16), preferred_element_type=_f32)
            bufb[:, c0:c0 + POOL_CH] = yg
        emit(0, bufb[...] * pscale_ref[...])

    @pl.when(n == 1)
    def _():
        bufa[...] = _gelu(proj())

    @pl.when(n == 2)
    def _():
        v = _ln(_gelu(proj())) * sgun_ref[...]
        vb = v.astype(_bf16)
        row = lax.broadcasted_iota(jnp.int32, (SGU_CHUNK, SGU_CHUNK), 0)
        col = lax.broadcasted_iota(jnp.int32, (SGU_CHUNK, SGU_CHUNK), 1)
        for g in range(SGU_HEADS):
            c0 = g * SGU_CHUNK
            wg = jnp.where(row >= col, sguw_ref[g], 0.0).astype(_bf16)
            for c in range(tm // SGU_CHUNK):
                r0 = c * SGU_CHUNK
                m = jnp.dot(wg, vb[r0:r0 + SGU_CHUNK, c0:c0 + SGU_CHUNK], preferred_element_type=_f32)
                bufb[r0:r0 + SGU_CHUNK, c0:c0 + SGU_CHUNK] = m + sgub_ref[:, c0:c0 + SGU_CHUNK]
        emit(1, bufa[...] * bufb[...])

    @pl.when(n == 3)
    def _():
        bufa[...] = proj()

    @pl.when(n == 4)
    def _():
        g = bufa[...] * jax.nn.sigmoid(proj())
        carry_halo(extc, CONF_HALO)
        extc[CONF_HALO:, :] = g
        conf_o[...] = g[tm - CONF_HALO:, :]
        off = CONF_HALO - (CONF_KERNEL - 1)
        acc = jnp.broadcast_to(cb_ref[...], (tm, gw))
        for k in range(CONF_KERNEL):
            acc = acc + cw_ref[k:k + 1, :] * extc[off + k:off + k + tm, :]
        yn = _ln(acc) * clg_ref[...] + clb_ref[...]
        emit(2, yn * jax.nn.sigmoid(yn))

    @pl.when(n == 5)
    def _():
        bufa[...] = proj()

    @pl.when(n == 6)
    def _():
        bufb[...] = proj()

    @pl.when(n == 7)
    def _():
        z = bufb[...] * proj()
        carry_halo(extz, SC_HALO)
        extz[SC_HALO:, :] = z
        sc_o[...] = z[tm - SC_HALO:, :]
        off = SC_HALO - (SC_KERNEL - 1)
        acc = scw_ref[0:1, :] * extz[off:off + tm, :]
        for k in range(1, SC_KERNEL):
            acc = acc + scw_ref[k:k + 1, :] * extz[off + k:off + k + tm, :]
        emit(3, bufa[...] * acc)


def _layer_spec(shape, l, nd):
    zeros = (0,) * len(shape)
    if nd == 1:
        return pl.BlockSpec((None,) + shape, lambda a: (l,) + zeros)
    if nd == 2:
        return pl.BlockSpec((None,) + shape, lambda a, b: (l,) + zeros)
    return pl.BlockSpec((None,) + shape, lambda a, b, c: (l,) + zeros)


def _mixer_params(l, p, nd):
    gw = GROUP_WIDTH
    arrs = [p["cw"], p["cb"], p["clg"], p["clb"], p["scw"], p["onorm"]]
    shapes = [(CONF_KERNEL, gw), (1, gw), (1, gw), (1, gw), (SC_KERNEL, gw), (1, D_MODEL)]
    return arrs, [_layer_spec(s, l, nd) for s in shapes]


def _a_prompt(l, x_all, mod_p4, w_in_b, p):
    gw = GROUP_WIDTH
    ni = SEQ // A_TM
    marrs, mspecs = _mixer_params(l, p, 3)
    in_specs = [
        pl.BlockSpec((A_TM, D_MODEL), lambda b, i, n: (b * ni + i, 0)),
        pl.BlockSpec((None, None, 1, D_MODEL), lambda b, i, n: (l, b, 0, 0)),
        pl.BlockSpec((None, None, 1, D_MODEL), lambda b, i, n: (l, b, 0, 1)),
        _layer_spec((1, D_MODEL), l, 3),
        pl.BlockSpec((None, D_MODEL, A_TN), lambda b, i, n: (l, 0, n)),
        _layer_spec((len(POOL_WINDOWS), POOL_CH, POOL_CH), l, 3),
        _layer_spec((1, gw), l, 3),
        _layer_spec((1, gw), l, 3),
        _layer_spec((SGU_HEADS, SGU_CHUNK, SGU_CHUNK), l, 3),
        _layer_spec((SGU_CHUNK, gw), l, 3),
    ] + mspecs
    out_specs = [
        pl.BlockSpec((A_TM, D_MODEL), lambda b, i, n: (b * ni + i, 0)),
        pl.BlockSpec((None, POOL_HALO, gw), lambda b, i, n: (b, 0, 0)),
        pl.BlockSpec((None, CONF_HALO, gw), lambda b, i, n: (b, 0, 0)),
        pl.BlockSpec((None, SC_HALO, gw), lambda b, i, n: (b, 0, 0)),
    ]
    out_shape = [
        jax.ShapeDtypeStruct((T_PROMPT, D_MODEL), _bf16),
        jax.ShapeDtypeStruct((BATCH, POOL_HALO, gw), _f32),
        jax.ShapeDtypeStruct((BATCH, CONF_HALO, gw), _f32),
        jax.ShapeDtypeStruct((BATCH, SC_HALO, gw), _f32),
    ]
    scratch = [
        pltpu.VMEM((A_TM, D_MODEL), _bf16),
        pltpu.VMEM((A_TM, gw), _f32),
        pltpu.VMEM((A_TM, gw), _f32),
        pltpu.VMEM((POOL_HALO + A_TM, gw), _f32),
        pltpu.VMEM((CONF_HALO + A_TM, gw), _f32),
        pltpu.VMEM((SC_HALO + A_TM, gw), _f32),
    ]
    return pl.pallas_call(
        _a_prompt_body,
        grid=(BATCH, ni, A_STEPS),
        in_specs=in_specs,
        out_specs=out_specs,
        out_shape=out_shape,
        scratch_shapes=scratch,
        compiler_params=pltpu.CompilerParams(
            dimension_semantics=("arbitrary", "arbitrary", "arbitrary"), vmem_limit_bytes=VMEM_LIMIT),
        name="mixers_prompt",
    )(x_all, mod_p4, mod_p4, p["nmix"], w_in_b, p["poolw"], p["pscale"], p["sgun"], p["sguw"], p["sgub"], *marrs)


def _a_sample_body(x_ref, sh1_ref, sc1_ref, nmix_ref, w_ref, poolw_ref, pscale_ref, sgun_ref, sgwc_ref,
                   sgbc_ref, cw_ref, cb_ref, clg_ref, clb_ref, scw_ref, onorm_ref, pst_ref, cst_ref, sst_ref,
                   y_ref, a_o, g_o, z_o, v_o,
                   h_ref, bufa, bufb):
    n = pl.program_id(0)
    gw = GROUP_WIDTH
    nb = DEC_BATCH
    ns = DEC_SEQ

    @pl.when(n == 0)
    def _():
        hn = _rms(x_ref[...]) * nmix_ref[...]
        h_ref[...] = _slab_add(_slab_mul(hn, 1.0 + sc1_ref[...]), sh1_ref[...]).astype(_bf16)

    def proj():
        return jnp.dot(h_ref[...], w_ref[...], preferred_element_type=_f32)

    def emit(g, y):
        yn = _rms(y) * onorm_ref[:, g * gw:(g + 1) * gw]
        y_ref[:, g * gw:(g + 1) * gw] = yn.astype(_bf16)

    def slab(x, s):
        return x[s * nb:(s + 1) * nb, :]

    @pl.when(n == 0)
    def _():
        a = proj()
        a_o[...] = a.reshape(ns, nb, gw)
        npast = POOL_MAX - 1
        for s in range(ns):
            for gi, w in enumerate(POOL_WINDOWS):
                c0 = gi * POOL_CH
                acc = None
                for j in range(w):
                    e = npast + s - j
                    term = pst_ref[e, :, c0:c0 + POOL_CH] if e < npast else slab(a, e - npast)[:, c0:c0 + POOL_CH]
                    acc = term if acc is None else acc + term
                cnt = min(float(w), PAST_LEN + s + 1.0)
                pooled = acc / cnt - slab(a, s)[:, c0:c0 + POOL_CH]
                bufb[s * nb:(s + 1) * nb, c0:c0 + POOL_CH] = pooled
        pooled = bufb[...].astype(_bf16)
        for gi in range(len(POOL_WINDOWS)):
            c0 = gi * POOL_CH
            bufa[:, c0:c0 + POOL_CH] = jnp.dot(pooled[:, c0:c0 + POOL_CH], poolw_ref[gi].astype(_bf16),
                                               preferred_element_type=_f32)
        emit(0, bufa[...] * pscale_ref[...])

    @pl.when(n == 1)
    def _():
        bufa[...] = _gelu(proj())

    @pl.when(n == 2)
    def _():
        v = _ln(_gelu(proj())) * sgun_ref[...]
        v_o[...] = v.reshape(ns, nb, gw)
        for t in range(ns):
            m = jnp.broadcast_to(sgbc_ref[t:t + 1, :], (nb, gw))
            for s in range(t + 1):
                m = m + sgwc_ref[t * ns + s:t * ns + s + 1, :] * slab(v, s)
            bufb[t * nb:(t + 1) * nb, :] = m
        emit(1, bufa[...] * bufb[...])

    @pl.when(n == 3)
    def _():
        bufa[...] = proj()

    @pl.when(n == 4)
    def _():
        g = bufa[...] * jax.nn.sigmoid(proj())
        g_o[...] = g.reshape(ns, nb, gw)
        npast = CONF_KERNEL - 1
        for s in range(ns):
            acc = jnp.broadcast_to(cb_ref[...], (nb, gw))
            for k in range(CONF_KERNEL):
                e = s + k
                term = cst_ref[e] if e < npast else slab(g, e - npast)
                acc = acc + cw_ref[k:k + 1, :] * term
            bufb[s * nb:(s + 1) * nb, :] = acc
        yn = _ln(bufb[...]) * clg_ref[...] + clb_ref[...]
        emit(2, yn * jax.nn.sigmoid(yn))

    @pl.when(n == 5)
    def _():
        bufa[...] = proj()

    @pl.when(n == 6)
    def _():
        bufb[...] = proj()

    @pl.when(n == 7)
    def _():
        z = bufb[...] * proj()
        z_o[...] = z.reshape(ns, nb, gw)
        npast = SC_KERNEL - 1
        for s in range(ns):
            acc = None
            for k in range(SC_KERNEL):
                e = s + k
                term = scw_ref[k:k + 1, :] * (sst_ref[e] if e < npast else slab(z, e - npast))
                acc = term if acc is None else acc + term
            bufb[s * nb:(s + 1) * nb, :] = acc
        emit(3, bufa[...] * bufb[...])


def _a_sample(l, x_all, x_blk, mod_s, w_in_b, p, pool_t, conf_t, sc_t):
    gw = GROUP_WIDTH
    ns, nb = DEC_SEQ, DEC_BATCH
    marrs, mspecs = _mixer_params(l, p, 1)
    in_specs = [
        pl.BlockSpec((T_SAMPLE, D_MODEL), lambda n: (x_blk, 0)),
        pl.BlockSpec((None, nb, D_MODEL), lambda n: (l, 0, 0)),
        pl.BlockSpec((None, nb, D_MODEL), lambda n: (l, 0, 1)),
        _layer_spec((1, D_MODEL), l, 1),
        pl.BlockSpec((None, D_MODEL, A_TN), lambda n: (l, 0, n)),
        _layer_spec((len(POOL_WINDOWS), POOL_CH, POOL_CH), l, 1),
        _layer_spec((1, gw), l, 1),
        _layer_spec((1, gw), l, 1),
        _layer_spec((ns * ns, gw), l, 1),
        _layer_spec((ns, gw), l, 1),
    ] + mspecs + [
        _layer_spec((POOL_MAX - 1, nb, gw), l, 1),
        _layer_spec((CONF_KERNEL - 1, nb, gw), l, 1),
        _layer_spec((SC_KERNEL - 1, nb, gw), l, 1),
    ]
    out_specs = [pl.BlockSpec((T_SAMPLE, D_MODEL), lambda n: (0, 0))] + [
        pl.BlockSpec((ns, nb, gw), lambda n: (0, 0, 0)) for _ in range(4)]
    out_shape = [jax.ShapeDtypeStruct((T_SAMPLE, D_MODEL), _bf16)] + [
        jax.ShapeDtypeStruct((ns, nb, gw), _f32) for _ in range(4)]
    scratch = [
        pltpu.VMEM((T_SAMPLE, D_MODEL), _bf16),
        pltpu.VMEM((T_SAMPLE, gw), _f32),
        pltpu.VMEM((T_SAMPLE, gw), _f32),
    ]
    return pl.pallas_call(
        _a_sample_body,
        grid=(A_STEPS,),
        in_specs=in_specs,
        out_specs=out_specs,
        out_shape=out_shape,
        scratch_shapes=scratch,
        compiler_params=pltpu.CompilerParams(
            dimension_semantics=("arbitrary",), vmem_limit_bytes=VMEM_LIMIT),
        name="mixers_sample",
    )(x_all, mod_s, mod_s, p["nmix"], w_in_b, p["poolw"], p["pscale"], p["sgun"], p["sgwc"], p["sgbc"],
      *marrs, pool_t, conf_t, sc_t)


def _pick_mod(is_prompt, mp_ref, ms_ref):
    return jnp.where(is_prompt, jnp.broadcast_to(mp_ref[...], (DEC_BATCH, D_MODEL)), ms_ref[...])


def _b_body(xp_ref, xs_ref, yp_ref, ys_ref, w_ref, mpg1, mpsh2, mpsc2, msg1, mssh2, mssc2, nffn_ref, rw_ref, rb_ref,
            x1_ref, h2_ref, pos_ref, post_ref, meta_ref):
    i = pl.program_id(0)
    isp = i < B_PROMPT_BLOCKS
    ycat = jnp.where(isp, yp_ref[...], ys_ref[...])
    o = jnp.dot(ycat, w_ref[...], preferred_element_type=_f32)
    x1 = jnp.where(isp, xp_ref[...], xs_ref[...]) + _slab_mul(o, _pick_mod(isp, mpg1, msg1))
    x1_ref[...] = x1
    hn = _rms(x1) * nffn_ref[...]
    h2 = _slab_add(_slab_mul(hn, 1.0 + _pick_mod(isp, mpsc2, mssc2)), _pick_mod(isp, mpsh2, mssh2))
    hi = h2.astype(_bf16)
    h2_ref[:, 0:D_MODEL] = hi
    lo = (h2 - hi.astype(_f32)).astype(_bf16)
    l1 = jnp.dot(hi, rw_ref[...], preferred_element_type=_f32)
    l2 = jnp.dot(lo, rw_ref[:, 0:LANES], preferred_element_type=_f32)
    logits = l1[:, 0:LANES] + l1[:, LANES:] + l2 + rb_ref[...]
    lane = lax.broadcasted_iota(jnp.int32, (B_TM, LANES), 1).astype(_f32)
    neg = jnp.float32(-jnp.inf)
    cur = jnp.where(lane < N_EXPERTS, logits, neg)
    vals, ids = [], []
    for _ in range(TOP_K):
        m = jnp.max(cur, axis=-1, keepdims=True)
        ix = jnp.min(jnp.where(cur == m, lane, float(LANES)), axis=-1, keepdims=True)
        vals.append(m)
        ids.append(ix)
        cur = jnp.where(lane == ix, neg, cur)
    es = [jnp.exp(v - vals[0]) for v in vals]
    tot = es[0] + es[1] + es[2] + es[3]
    side = jnp.zeros((B_TM, LANES), _f32)
    for k in range(TOP_K):
        g = es[k] / tot
        g0 = g.astype(_bf16).astype(_f32)
        g1 = (g - g0).astype(_bf16).astype(_f32)
        for j, term in enumerate((g0, g1, g - g0 - g1)):
            side = jnp.where(lane == float(GATE_TERMS * k + j), term, side)
        side = jnp.where(lane == float(SIDE_ID0 + k), ids[k], side)
    h2_ref[:, D_MODEL:] = side.astype(_bf16)

    hit = [lane == ix for ix in ids]
    chosen = jnp.zeros((B_TM, LANES), _f32)
    for h in hit:
        chosen = jnp.where(h, 1.0, chosen)
    cnt = jnp.sum(chosen, axis=0, keepdims=True)
    seg_rows = jnp.floor((cnt + (SEG - 1.0)) * (1.0 / SEG)) * SEG
    er = lax.broadcasted_iota(jnp.int32, (LANES, LANES), 0)
    ec = lax.broadcasted_iota(jnp.int32, (LANES, LANES), 1)
    before = jnp.where(er < ec, 1.0, 0.0).astype(_bf16)
    seg_start = jnp.dot(jnp.broadcast_to(seg_rows, (8, LANES)).astype(_bf16), before,
                        preferred_element_type=_f32)[0:1, :]
    tr = lax.broadcasted_iota(jnp.int32, (B_TM, B_TM), 0)
    tc = lax.broadcasted_iota(jnp.int32, (B_TM, B_TM), 1)
    earlier = jnp.where(tr > tc, 1.0, 0.0).astype(_bf16)
    rank = jnp.dot(earlier, chosen.astype(_bf16), preferred_element_type=_f32)
    base = seg_start + rank
    pos = jnp.zeros((B_TM, LANES), _f32)
    for k in range(TOP_K):
        pk = jnp.sum(jnp.where(hit[k], base, 0.0), axis=-1, keepdims=True)
        pos = jnp.where(lane == float(k), pk, pos)
    pos_ref[...] = pos
    post_ref[...] = jnp.transpose(pos)[0:8, :]
    sub = lax.broadcasted_iota(jnp.int32, (8, LANES), 0)
    meta = jnp.where(sub == 0, jnp.broadcast_to(seg_start, (8, LANES)),
                     jnp.where(sub == 1, jnp.broadcast_to(seg_rows, (8, LANES)), 0.0))
    meta_ref[...] = meta.astype(jnp.int32)


def _stage_b(l, x_p, x_s, xs_blk, ycat_p, ycat_s, w_out_b, mod_p4, mod_s, p):
    def pb(i):
        return jnp.minimum(i // (SEQ // B_TM), BATCH - 1)

    def mp_spec(j):
        return pl.BlockSpec((None, None, 1, D_MODEL), lambda i: (l, pb(i), 0, j))

    def ms_spec(j):
        return pl.BlockSpec((None, DEC_BATCH, D_MODEL), lambda i: (l, 0, j))

    row_spec = pl.BlockSpec((B_TM, D_MODEL), lambda i: (i, 0))
    in_specs = [
        pl.BlockSpec((B_TM, D_MODEL), lambda i: (jnp.minimum(i, B_PROMPT_BLOCKS - 1), 0)),
        pl.BlockSpec((T_SAMPLE, D_MODEL), lambda i: (xs_blk, 0), pipeline_mode=pl.Buffered(1)),
        pl.BlockSpec((B_TM, D_MODEL), lambda i: (jnp.minimum(i, B_PROMPT_BLOCKS - 1), 0)),
        pl.BlockSpec((T_SAMPLE, D_MODEL), lambda i: (0, 0), pipeline_mode=pl.Buffered(1)),
        pl.BlockSpec((None, D_MODEL, D_MODEL), lambda i: (l, 0, 0), pipeline_mode=pl.Buffered(1)),
        mp_spec(2), mp_spec(3), mp_spec(4), ms_spec(2), ms_spec(3), ms_spec(4),
        _layer_spec((1, D_MODEL), l, 1),
        _layer_spec((D_MODEL, 2 * LANES), l, 1),
        _layer_spec((1, LANES), l, 1),
    ]
    lane_spec = pl.BlockSpec((B_TM, LANES), lambda i: (i, 0))
    return pl.pallas_call(
        _b_body,
        grid=(N_BLOCKS,),
        in_specs=in_specs,
        out_specs=[row_spec, pl.BlockSpec((B_TM, H_EXT), lambda i: (i, 0)), lane_spec,
                   pl.BlockSpec((None, 8, B_TM), lambda i: (i, 0, 0)),
                   pl.BlockSpec((None, 8, LANES), lambda i: (i, 0, 0))],
        out_shape=[
            jax.ShapeDtypeStruct((T_ALL, D_MODEL), _f32),
            jax.ShapeDtypeStruct((T_ALL, H_EXT), _bf16),
            jax.ShapeDtypeStruct((T_ALL, LANES), _f32),
            jax.ShapeDtypeStruct((N_BLOCKS, 8, B_TM), _f32),
            jax.ShapeDtypeStruct((N_BLOCKS, 8, LANES), jnp.int32),
        ],
        compiler_params=pltpu.CompilerParams(
            dimension_semantics=("arbitrary",), vmem_limit_bytes=VMEM_LIMIT),
        name="outproj_router",
    )(x_p, x_s, ycat_p, ycat_s, w_out_b, mod_p4, mod_p4, mod_p4, mod_s, mod_s, mod_s, p["nffn"], p["rw"], p["rb"])


def _sort_body(post_ref, h_ref, o_ref, so_ref):
    for c in range(SORT_CHUNKS):
        j = (c * SORT_TM + lax.broadcasted_iota(jnp.int32, (SORT_TM, B_TM), 0)).astype(_f32)
        sel = jnp.zeros((SORT_TM, B_TM), _f32)
        for k in range(TOP_K):
            sel = jnp.where(j == post_ref[k:k + 1, :], 1.0, sel)
        res = jnp.dot(sel.astype(_bf16), h_ref[...], preferred_element_type=_f32)
        o_ref[c * SORT_TM:(c + 1) * SORT_TM, :] = res[:, 0:D_MODEL].astype(_bf16)
        so_ref[c * SORT_TM:(c + 1) * SORT_TM, :] = res[:, D_MODEL:]


def _local_sort(post, h_ext):
    return pl.pallas_call(
        _sort_body,
        grid=(N_BLOCKS,),
        in_specs=[
            pl.BlockSpec((None, 8, B_TM), lambda b: (b, 0, 0)),
            pl.BlockSpec((B_TM, H_EXT), lambda b: (b, 0)),
        ],
        out_specs=[pl.BlockSpec((BLOCK_ROWS, D_MODEL), lambda b: (b, 0)),
                   pl.BlockSpec((BLOCK_ROWS, LANES), lambda b: (b, 0))],
        out_shape=[jax.ShapeDtypeStruct((N_BLOCKS * BLOCK_ROWS, D_MODEL), _bf16),
                   jax.ShapeDtypeStruct((N_BLOCKS * BLOCK_ROWS, LANES), _f32)],
        compiler_params=pltpu.CompilerParams(
            dimension_semantics=("arbitrary",), vmem_limit_bytes=VMEM_LIMIT),
        name="local_sort",
    )(post, h_ext)


def _moe_body(ie_ref, ilo_ref, int_ref, start_ref, rows_ref, used_ref,
              wg_ref, wl_ref, wd_ref, bg_ref, bl_ref, bd_ref, x_hbm, g_hbm, y_hbm,
              xbuf, hbuf, gbuf, wc, seg_src, nseg, pending, sem_x, sem_y):
    i = pl.program_id(0)
    n = pl.program_id(1)
    e = ie_ref[i]
    lo = ilo_ref[i]
    nt = int_ref[i]
    hi = lo + nt * MOE_TM
    tm, tn = MOE_TM, MOE_TN

    def x_copy(src, dst):
        return pltpu.make_async_copy(x_hbm.at[pl.ds(src, SEG)], xbuf.at[pl.ds(dst, SEG)], sem_x.at[dst // tm])

    def g_copy(src, dst):
        return pltpu.make_async_copy(g_hbm.at[pl.ds(src, SEG)], gbuf.at[pl.ds(dst, SEG)], sem_x.at[dst // tm])

    def y_copy(src, dst):
        return pltpu.make_async_copy(xbuf.at[pl.ds(src, SEG)], y_hbm.at[pl.ds(dst, SEG)], sem_y.at[0])

    def drain():
        def w(_, carry):
            y_copy(0, 0).wait()
            return carry
        lax.fori_loop(0, pending[0], w, 0)
        pending[0] = 0

    @pl.when((i == 0) & (n == 0))
    def _():
        pending[0] = 0

    @pl.when((n == 0) & (nt > 0))
    def _():
        drain()

        def per_block(b, p):
            rows = rows_ref[b * N_EXPERTS + e]
            start = start_ref[b * N_EXPERTS + e]

            def per_piece(g, carry):
                q = p + g * SEG

                @pl.when((q >= lo) & (q < hi))
                def _():
                    src = pl.multiple_of(b * BLOCK_ROWS + start + g * SEG, SEG)
                    dst = pl.multiple_of(q - lo, SEG)
                    seg_src[dst // SEG] = src
                    x_copy(src, dst).start()
                    g_copy(src, dst).start()
                return carry
            lax.fori_loop(0, rows // SEG, per_piece, 0)
            return p + rows
        end = lax.fori_loop(0, N_BLOCKS, per_block, 0)
        ns = (jnp.minimum(end, hi) - lo) // SEG
        nseg[0] = ns

        def zero_piece(g, carry):
            r = pl.multiple_of(g * SEG, SEG)
            xbuf[pl.ds(r, SEG), :] = jnp.zeros((SEG, D_MODEL), _bf16)
            gbuf[pl.ds(r, SEG), :] = jnp.zeros((SEG, LANES), _f32)
            return carry
        lax.fori_loop(ns, nt * TILE_SEGS, zero_piece, 0)

    def tile_pieces(t):
        return jnp.clip(nseg[0] - t * TILE_SEGS, 0, TILE_SEGS)

    for c in range(MOE_UP_STEPS):
        @pl.when((n == c) & (nt > 0))
        def _(c=c):
            wc[:, 0:tn] = wg_ref[...].astype(_bf16)
            wc[:, tn:] = wl_ref[...].astype(_bf16)

            def body(t, carry):
                if c == 0:
                    def w(_, cc):
                        x_copy(0, pl.multiple_of(t * tm, tm)).wait()
                        g_copy(0, pl.multiple_of(t * tm, tm)).wait()
                        return cc
                    lax.fori_loop(0, tile_pieces(t), w, 0)
                r0 = pl.multiple_of(t * tm, tm)
                up = jnp.dot(xbuf[pl.ds(r0, tm), :], wc[...], preferred_element_type=_f32)
                glu = jnp.minimum(up[:, 0:tn] + bg_ref[...], SWIGLU_LIMIT)
                lin = jnp.clip(up[:, tn:] + bl_ref[...], -SWIGLU_LIMIT, SWIGLU_LIMIT)
                act = glu * jax.nn.sigmoid(SWIGLU_ALPHA * glu) * (lin + 1.0)
                hbuf[pl.ds(r0, tm), c * tn:(c + 1) * tn] = act.astype(_bf16)
                return carry
            lax.fori_loop(0, nt, body, 0)

    for c in range(MOE_DOWN_STEPS):
        @pl.when((n == MOE_UP_STEPS + c) & (nt > 0))
        def _(c=c):
            wc[:, 0:tn] = wd_ref[...].astype(_bf16)

            def body(t, carry):
                r0 = pl.multiple_of(t * tm, tm)
                y = jnp.dot(hbuf[pl.ds(r0, tm), :], wc[:, 0:tn], preferred_element_type=_f32) + bd_ref[...]
                g = gbuf[pl.ds(r0, tm), :]
                gate = g[:, 0:1] + g[:, 1:2] + g[:, 2:3]
                xbuf[pl.ds(r0, tm), c * tn:(c + 1) * tn] = (y * gate).astype(_bf16)
                if c == MOE_DOWN_STEPS - 1:
                    np_ = tile_pieces(t)
                    for q in range(TILE_SEGS):
                        @pl.when(q < np_)
                        def _(q=q):
                            y_copy(pl.multiple_of(r0 + q * SEG, SEG),
                                   pl.multiple_of(seg_src[t * TILE_SEGS + q], SEG)).start()
                return carry
            lax.fori_loop(0, nt, body, 0)
            if c == MOE_DOWN_STEPS - 1:
                pending[0] = nseg[0]

    @pl.when((i == MOE_ITEMS - 1) & (n == MOE_STEPS - 1))
    def _():
        drain()
        hbuf[0:SEG, :] = jnp.zeros((SEG, D_EXPERT), _bf16)

        def zero_copy(row):
            return pltpu.make_async_copy(hbuf.at[pl.ds(0, SEG)], y_hbm.at[pl.ds(row, SEG)], sem_y.at[0])

        def per_block(b, carry):
            def start(g, cc):
                zero_copy(pl.multiple_of(b * BLOCK_ROWS + g * SEG, SEG)).start()
                return cc

            def wait(g, cc):
                zero_copy(0).wait()
                return cc
            first = used_ref[b] // SEG
            lax.fori_loop(first, BLOCK_ROWS // SEG, start, 0)
            lax.fori_loop(first, BLOCK_ROWS // SEG, wait, 0)
            return carry
        lax.fori_loop(0, N_BLOCKS, per_block, 0)


def _moe(l, item_e, item_lo, item_nt, seg_start, seg_rows, used, w_up, b_up4, w_down, b_down4, x_sorted,
         g_sorted):
    def col_up(n, valid):
        return jnp.where(valid, jnp.minimum(n, MOE_UP_STEPS - 1), MOE_UP_STEPS - 1)

    def col_down(n, valid):
        return jnp.where(valid, jnp.maximum(n - MOE_UP_STEPS, 0), MOE_DOWN_STEPS - 1)

    def glu_map(i, n, ie, ilo, int_, *_):
        return (l, ie[i], 0, col_up(n, int_[i] > 0))

    def lin_map(i, n, ie, ilo, int_, *_):
        return (l, ie[i], 0, MOE_UP_STEPS + col_up(n, int_[i] > 0))

    def down_map(i, n, ie, ilo, int_, *_):
        return (l, ie[i], 0, col_down(n, int_[i] > 0))

    grid_spec = pltpu.PrefetchScalarGridSpec(
        num_scalar_prefetch=6,
        grid=(MOE_ITEMS, MOE_STEPS),
        in_specs=[
            pl.BlockSpec((None, None, D_MODEL, MOE_TN), glu_map),
            pl.BlockSpec((None, None, D_MODEL, MOE_TN), lin_map),
            pl.BlockSpec((None, None, D_EXPERT, MOE_TN), down_map),
            pl.BlockSpec((None, None, 1, MOE_TN), glu_map),
            pl.BlockSpec((None, None, 1, MOE_TN), lin_map),
            pl.BlockSpec((None, None, 1, MOE_TN), down_map),
            pl.BlockSpec(memory_space=pl.ANY),
            pl.BlockSpec(memory_space=pl.ANY),
        ],
        out_specs=pl.BlockSpec(memory_space=pl.ANY),
        scratch_shapes=[
            pltpu.VMEM((MOE_MT * MOE_TM, D_MODEL), _bf16),
            pltpu.VMEM((MOE_MT * MOE_TM, D_EXPERT), _bf16),
            pltpu.VMEM((MOE_MT * MOE_TM, LANES), _f32),
            pltpu.VMEM((D_MODEL, 2 * MOE_TN), _bf16),
            pltpu.SMEM((MOE_MT * TILE_SEGS,), jnp.int32),
            pltpu.SMEM((1,), jnp.int32),
            pltpu.SMEM((1,), jnp.int32),
            pltpu.SemaphoreType.DMA((MOE_MT,)),
            pltpu.SemaphoreType.DMA((1,)),
        ],
    )
    return pl.pallas_call(
        _moe_body,
        grid_spec=grid_spec,
        out_shape=jax.ShapeDtypeStruct((N_BLOCKS * BLOCK_ROWS, D_MODEL), _bf16),
        compiler_params=pltpu.CompilerParams(
            dimension_semantics=("arbitrary", "arbitrary"), vmem_limit_bytes=VMEM_LIMIT),
        name="moe_experts",
    )(item_e, item_lo, item_nt, seg_start, seg_rows, used, w_up, w_up, w_down, b_up4, b_up4, b_down4, x_sorted,
      g_sorted)


MOE_DN = 1024
STREAM_UP_STEPS = D_EXPERT // MOE_TN
STREAM_DOWN_STEPS = D_MODEL // MOE_DN
STREAM_STEPS = STREAM_UP_STEPS + STREAM_DOWN_STEPS
MOE_WSLOTS = 2
CAST_ROWS = D_MODEL // MOE_MT


def _moe_stream_body(ie_ref, ilo_ref, int_ref, start_ref, rows_ref, used_ref, nitems_ref,
                     wup_hbm, wdn_hbm, bup_hbm, bdn_hbm, x_hbm, g_hbm, y_hbm,
                     wbuf, wc, xbuf, hbuf, gbuf, bup, bdn, seg_src, nseg, pending,
                     sem_w, sem_b, sem_x, sem_y, *, l):
    tm, tn = MOE_TM, MOE_TN
    n_items = nitems_ref[0]
    n_steps = n_items * STREAM_STEPS

    def w_half(src_ref, e, col0, slot, half):
        return pltpu.make_async_copy(src_ref.at[l, e, :, pl.ds(col0, tn)],
                                     wbuf.at[slot, :, pl.ds(half * tn, tn)], sem_w.at[slot])

    def issue_w(s):
        e = ie_ref[s // STREAM_STEPS]
        n = s % STREAM_STEPS
        slot = s % MOE_WSLOTS
        for c in range(STREAM_UP_STEPS):
            @pl.when(n == c)
            def _(c=c):
                w_half(wup_hbm, e, c * tn, slot, 0).start()
                w_half(wup_hbm, e, D_EXPERT + c * tn, slot, 1).start()
        for c in range(STREAM_DOWN_STEPS):
            @pl.when(n == STREAM_UP_STEPS + c)
            def _(c=c):
                w_half(wdn_hbm, e, c * MOE_DN, slot, 0).start()
                w_half(wdn_hbm, e, c * MOE_DN + tn, slot, 1).start()

    def wait_w(s):
        slot = s % MOE_WSLOTS
        for half in range(2):
            w_half(wdn_hbm, 0, 0, slot, half).wait()

    def b_copies(item):
        e = ie_ref[item]
        par = item % 2
        return (pltpu.make_async_copy(bup_hbm.at[l, e], bup.at[par], sem_b.at[par]),
                pltpu.make_async_copy(bdn_hbm.at[l, e], bdn.at[par], sem_b.at[par]))

    def x_copy(p, src, dst):
        return pltpu.make_async_copy(x_hbm.at[pl.ds(src, SEG)], xbuf.at[p, pl.ds(dst, SEG)],
                                     sem_x.at[p * MOE_MT + dst // tm])

    def g_copy(p, src, dst):
        return pltpu.make_async_copy(g_hbm.at[pl.ds(src, SEG)], gbuf.at[p, pl.ds(dst, SEG)],
                                     sem_x.at[p * MOE_MT + dst // tm])

    def y_copy(p, src, dst):
        return pltpu.make_async_copy(xbuf.at[p, pl.ds(src, SEG)], y_hbm.at[pl.ds(dst, SEG)], sem_y.at[p])

    def drain(p):
        def w(_, carry):
            y_copy(p, 0, 0).wait()
            return carry
        lax.fori_loop(0, pending[p], w, 0)
        pending[p] = 0

    def tile_pieces(p, t):
        return jnp.clip(nseg[p] - t * TILE_SEGS, 0, TILE_SEGS)

    def load_item(item):
        p = item % 2
        e = ie_ref[item]
        lo = ilo_ref[item]
        nt = int_ref[item]
        hi = lo + nt * tm

        def per_block(b, q0):
            rows = rows_ref[b * N_EXPERTS + e]
            start = start_ref[b * N_EXPERTS + e]

            def per_piece(g, cc):
                q = q0 + g * SEG

                @pl.when((q >= lo) & (q < hi))
                def _():
                    src = pl.multiple_of(b * BLOCK_ROWS + start + g * SEG, SEG)
                    dst = pl.multiple_of(q - lo, SEG)
                    seg_src[p * (MOE_MT * TILE_SEGS) + dst // SEG] = src
                    x_copy(p, src, dst).start()
                    g_copy(p, src, dst).start()
                return cc
            lax.fori_loop(0, rows // SEG, per_piece, 0)
            return q0 + rows
        end = lax.fori_loop(0, N_BLOCKS, per_block, 0)
        ns = (jnp.minimum(end, hi) - lo) // SEG
        nseg[p] = ns

        def zero_piece(g, cc):
            r = pl.multiple_of(g * SEG, SEG)
            xbuf[p, pl.ds(r, SEG), :] = jnp.zeros((SEG, D_MODEL), _bf16)
            gbuf[p, pl.ds(r, SEG), :] = jnp.zeros((SEG, LANES), _f32)
            return cc
        lax.fori_loop(ns, nt * TILE_SEGS, zero_piece, 0)

    pending[0] = 0
    pending[1] = 0
    issue_w(0)
    issue_w(1)
    for cp in b_copies(0):
        cp.start()
    load_item(0)
    wait_w(0)
    wc[0] = wbuf[0].astype(_bf16)

    def step(s, carry):
        item = s // STREAM_STEPS
        n = s % STREAM_STEPS
        e = ie_ref[item]
        lo = ilo_ref[item]
        nt = int_ref[item]
        hi = lo + nt * tm
        par = item % 2
        cur = s % 2
        nxt = (s + 1) % 2
        nslot = (s + 1) % MOE_WSLOTS

        @pl.when(s + 2 < n_steps)
        def _():
            issue_w(s + 2)

        @pl.when(s + 1 < n_steps)
        def _():
            wait_w(s + 1)

        def cast_slice(t):
            r = pl.multiple_of(t * CAST_ROWS, CAST_ROWS)
            wc[nxt, pl.ds(r, CAST_ROWS), :] = wbuf[nslot, pl.ds(r, CAST_ROWS), :].astype(_bf16)

        @pl.when(n == 0)
        def _():
            for cp in b_copies(item):
                cp.wait()

        @pl.when((n == 1) & (item + 1 < n_items))
        def _():
            for cp in b_copies(item + 1):
                cp.start()
            drain(1 - par)
            load_item(item + 1)

        for c in range(STREAM_UP_STEPS):
            @pl.when(n == c)
            def _(c=c):
                def body(t, cc):
                    if c == 0:
                        def w(_, c3):
                            x_copy(par, 0, pl.multiple_of(t * tm, tm)).wait()
                            g_copy(par, 0, pl.multiple_of(t * tm, tm)).wait()
                            return c3
                        lax.fori_loop(0, tile_pieces(par, t), w, 0)
                    cast_slice(t)
                    r0 = pl.multiple_of(t * tm, tm)
                    up = jnp.dot(xbuf[par, pl.ds(r0, tm), :], wc[cur], preferred_element_type=_f32)
                    bg = bup[par, :, c * tn:(c + 1) * tn]
                    bl = bup[par, :, D_EXPERT + c * tn:D_EXPERT + (c + 1) * tn]
                    glu = jnp.minimum(up[:, 0:tn] + bg, SWIGLU_LIMIT)
                    lin = jnp.clip(up[:, tn:] + bl, -SWIGLU_LIMIT, SWIGLU_LIMIT)
                    act = glu * jax.nn.sigmoid(SWIGLU_ALPHA * glu) * (lin + 1.0)
                    hbuf[pl.ds(r0, tm), c * tn:(c + 1) * tn] = act.astype(_bf16)
                    return cc
                lax.fori_loop(0, nt, body, 0)

        for c in range(STREAM_DOWN_STEPS):
            @pl.when(n == STREAM_UP_STEPS + c)
            def _(c=c):
                def body(t, cc):
                    cast_slice(t)
                    r0 = pl.multiple_of(t * tm, tm)
                    y = (jnp.dot(hbuf[pl.ds(r0, tm), :], wc[cur], preferred_element_type=_f32)
                         + bdn[par, :, c * MOE_DN:(c + 1) * MOE_DN])
                    g = gbuf[par, pl.ds(r0, tm), :]
                    ef = e.astype(_f32)
                    gate = jnp.zeros((tm, 1), _f32)
                    for k in range(TOP_K):
                        gk = g[:, GATE_TERMS * k:GATE_TERMS * k + 1]
                        for j in range(1, GATE_TERMS):
                            gk = gk + g[:, GATE_TERMS * k + j:GATE_TERMS * k + j + 1]
                        gate = jnp.where(g[:, SIDE_ID0 + k:SIDE_ID0 + k + 1] == ef, gk, gate)
                    xbuf[par, pl.ds(r0, tm), c * MOE_DN:(c + 1) * MOE_DN] = (y * gate).astype(_bf16)
                    if c == STREAM_DOWN_STEPS - 1:
                        np_ = tile_pieces(par, t)
                        for q in range(TILE_SEGS):
                            @pl.when(q < np_)
                            def _(q=q):
                                dst = seg_src[par * (MOE_MT * TILE_SEGS) + t * TILE_SEGS + q]
                                y_copy(par, pl.multiple_of(r0 + q * SEG, SEG), pl.multiple_of(dst, SEG)).start()
                    return cc
                lax.fori_loop(0, nt, body, 0)
                if c == STREAM_DOWN_STEPS - 1:
                    pending[par] = nseg[par]

        def rest(t, cc):
            cast_slice(t)
            return cc
        lax.fori_loop(nt, MOE_MT, rest, 0)
        return carry
    lax.fori_loop(0, n_steps, step, 0)

    drain(0)
    drain(1)
    hbuf[0:SEG, :] = jnp.zeros((SEG, D_EXPERT), _bf16)

    def zero_copy(row):
        return pltpu.make_async_copy(hbuf.at[pl.ds(0, SEG)], y_hbm.at[pl.ds(row, SEG)], sem_y.at[0])

    def zero_block(b, carry):
        def zstart(g, cc):
            zero_copy(pl.multiple_of(b * BLOCK_ROWS + g * SEG, SEG)).start()
            return cc

        def zwait(g, cc):
            zero_copy(0).wait()
            return cc
        first = used_ref[b] // SEG
        lax.fori_loop(first, BLOCK_ROWS // SEG, zstart, 0)
        lax.fori_loop(first, BLOCK_ROWS // SEG, zwait, 0)
        return carry
    lax.fori_loop(0, N_BLOCKS, zero_block, 0)


def _moe_stream(l, item_e, item_lo, item_nt, seg_start, seg_rows, used, n_items, w_up, b_up4, w_down, b_down4,
                x_sorted, g_sorted):
    any_spec = pl.BlockSpec(memory_space=pl.ANY)
    grid_spec = pltpu.PrefetchScalarGridSpec(
        num_scalar_prefetch=7,
        grid=(1,),
        in_specs=[any_spec] * 6,
        out_specs=any_spec,
        scratch_shapes=[
            pltpu.VMEM((MOE_WSLOTS, D_MODEL, 2 * MOE_TN), _f32),
            pltpu.VMEM((2, D_MODEL, 2 * MOE_TN), _bf16),
            pltpu.VMEM((2, MOE_MT * MOE_TM, D_MODEL), _bf16),
            pltpu.VMEM((MOE_MT * MOE_TM, D_EXPERT), _bf16),
            pltpu.VMEM((2, MOE_MT * MOE_TM, LANES), _f32),
            pltpu.VMEM((2, 1, 2 * D_EXPERT), _f32),
            pltpu.VMEM((2, 1, D_MODEL), _f32),
            pltpu.SMEM((2 * MOE_MT * TILE_SEGS,), jnp.int32),
            pltpu.SMEM((2,), jnp.int32),
            pltpu.SMEM((2,), jnp.int32),
            pltpu.SemaphoreType.DMA((MOE_WSLOTS,)),
            pltpu.SemaphoreType.DMA((2,)),
            pltpu.SemaphoreType.DMA((2 * MOE_MT,)),
            pltpu.SemaphoreType.DMA((2,)),
        ],
    )
    return pl.pallas_call(
        functools.partial(_moe_stream_body, l=l),
        grid_spec=grid_spec,
        out_shape=jax.ShapeDtypeStruct((N_BLOCKS * BLOCK_ROWS, D_MODEL), _bf16),
        compiler_params=pltpu.CompilerParams(
            dimension_semantics=("arbitrary",), vmem_limit_bytes=60 * 1024 * 1024),
        name="moe_experts",
    )(item_e, item_lo, item_nt, seg_start, seg_rows, used, n_items, w_up, w_down, b_up4, b_down4, x_sorted,
      g_sorted)


def _work_items(meta):
    seg_start = meta[:, 0, :N_EXPERTS]
    seg_rows = meta[:, 1, :N_EXPERTS]
    rows_e = jnp.sum(seg_rows, axis=0)
    nt = (rows_e + MOE_TM - 1) // MOE_TM
    ipe = (nt + MOE_MT - 1) // MOE_MT
    cum = jnp.cumsum(ipe)
    first = cum - ipe
    total = cum[-1]
    ii = jnp.arange(MOE_ITEMS, dtype=jnp.int32)
    ii_c = jnp.minimum(ii, total - 1)
    e_i = jnp.minimum(jnp.sum((cum[None, :] <= ii_c[:, None]).astype(jnp.int32), axis=1), N_EXPERTS - 1)
    valid = ii < total
    j = ii - first[e_i]
    lo_i = jnp.where(valid, j * (MOE_MT * MOE_TM), 0)
    nt_i = jnp.where(valid, jnp.minimum(MOE_MT, nt[e_i] - j * MOE_MT), 0)
    used = jnp.sum(seg_rows, axis=1)
    i32 = jnp.int32
    return (e_i.astype(i32), lo_i.astype(i32), nt_i.astype(i32), seg_start.reshape(-1).astype(i32),
            seg_rows.reshape(-1).astype(i32), used.astype(i32), total.astype(i32).reshape(1))


def _c_body(x1_ref, y_ref, pos_ref, mpg2, msg2, nfin_ref, *rest, final):
    if final:
        outp_ref, outs_ref, sel_ref = rest
    else:
        out_ref, sel_ref = rest
    i = pl.program_id(0)
    isp = i < B_PROMPT_BLOCKS
    for c in range(SORT_CHUNKS):
        j = (c * SORT_TM + lax.broadcasted_iota(jnp.int32, (B_TM, SORT_TM), 1)).astype(_f32)
        sel = jnp.zeros((B_TM, SORT_TM), _f32)
        for k in range(TOP_K):
            sel = jnp.where(j == pos_ref[:, k:k + 1], 1.0, sel)
        sel_ref[:, c * SORT_TM:(c + 1) * SORT_TM] = sel.astype(_bf16)
    acc = jnp.dot(sel_ref[...], y_ref[...], preferred_element_type=_f32)
    x2 = x1_ref[...] + _slab_mul(acc, _pick_mod(isp, mpg2, msg2))
    if final:
        y = _rms(x2) * nfin_ref[...]

        @pl.when(isp)
        def _():
            outp_ref[...] = y

        @pl.when(jnp.logical_not(isp))
        def _():
            outs_ref[...] = y
    else:
        out_ref[...] = x2


def _combine(l, x1, y_sorted, pos, mod_p4, mod_s, nfin, final):
    def pb(i):
        return jnp.minimum(i // (SEQ // B_TM), BATCH - 1)

    row_spec = pl.BlockSpec((B_TM, D_MODEL), lambda i: (i, 0))
    lane_spec = pl.BlockSpec((B_TM, LANES), lambda i: (i, 0))
    if final:
        out_specs = [pl.BlockSpec((B_TM, D_MODEL), lambda i: (jnp.minimum(i, B_PROMPT_BLOCKS - 1), 0)),
                     pl.BlockSpec((T_SAMPLE, D_MODEL), lambda i: (0, 0))]
        out_shape = [jax.ShapeDtypeStruct((T_PROMPT, D_MODEL), _f32),
                     jax.ShapeDtypeStruct((T_SAMPLE, D_MODEL), _f32)]
    else:
        out_specs = row_spec
        out_shape = jax.ShapeDtypeStruct((T_ALL, D_MODEL), _f32)
    return pl.pallas_call(
        functools.partial(_c_body, final=final),
        grid=(N_BLOCKS,),
        in_specs=[
            row_spec,
            pl.BlockSpec((BLOCK_ROWS, D_MODEL), lambda i: (i, 0)),
            lane_spec,
            pl.BlockSpec((None, None, 1, D_MODEL), lambda i: (l, pb(i), 0, 5)),
            pl.BlockSpec((None, DEC_BATCH, D_MODEL), lambda i: (l, 0, 5)),
            pl.BlockSpec((1, D_MODEL), lambda i: (0, 0)),
        ],
        out_specs=out_specs,
        out_shape=out_shape,
        scratch_shapes=[pltpu.VMEM((B_TM, BLOCK_ROWS), _bf16)],
        compiler_params=pltpu.CompilerParams(
            dimension_semantics=("arbitrary",), vmem_limit_bytes=VMEM_LIMIT),
        name="moe_combine",
    )(x1, y_sorted, pos, mod_p4, mod_s, nfin)


def kernel(x_prompt, x_sample, c_prompt, c_sample, state_pool, state_conformer, state_shortconv, w_ada, b_ada, norm_mix, norm_ffn, norm_final, w_in, pool_w, pool_scale, sgu_norm, sgu_w, sgu_b, conf_w, conf_b, conf_ln_g, conf_ln_b, sc_w, out_norm, w_out, router_w, router_b, w_up, b_up, w_down, b_down):
    gw = GROUP_WIDTH
    ns, nb = DEC_SEQ, DEC_BATCH

    c_all = jnp.concatenate([c_sample, c_prompt, jnp.zeros((ADA_ROWS - nb - BATCH, D_MODEL), _f32)], axis=0)
    mod_s, mod_p8 = _ada(c_all, w_ada, b_ada)
    mod_p4 = mod_p8[:, :BATCH].reshape(DEPTH, BATCH, 1, 6 * D_MODEL)

    rw_hi = router_w.astype(_bf16)
    rw_lo = (router_w - rw_hi.astype(_f32)).astype(_bf16)
    lane_pad = ((0, 0), (0, 0), (0, LANES - N_EXPERTS))
    params = {
        "nmix": norm_mix.reshape(DEPTH, 1, D_MODEL),
        "nffn": norm_ffn.reshape(DEPTH, 1, D_MODEL),
        "poolw": pool_w,
        "pscale": pool_scale.reshape(DEPTH, 1, gw),
        "sgun": sgu_norm.reshape(DEPTH, 1, gw),
        "sguw": sgu_w,
        "sgub": jnp.repeat(jnp.swapaxes(sgu_b, 1, 2), SGU_CHUNK, axis=2),
        "sgwc": jnp.repeat(jnp.transpose(sgu_w[:, :, :ns, :ns], (0, 2, 3, 1)).reshape(DEPTH, ns * ns, SGU_HEADS),
                           SGU_CHUNK, axis=2),
        "sgbc": jnp.repeat(jnp.swapaxes(sgu_b[:, :, :ns], 1, 2), SGU_CHUNK, axis=2),
        "cw": conf_w,
        "cb": conf_b.reshape(DEPTH, 1, gw),
        "clg": conf_ln_g.reshape(DEPTH, 1, gw),
        "clb": conf_ln_b.reshape(DEPTH, 1, gw),
        "scw": sc_w,
        "onorm": out_norm.reshape(DEPTH, 1, D_MODEL),
        "rw": jnp.concatenate([jnp.pad(rw_hi, lane_pad), jnp.pad(rw_lo, lane_pad)], axis=2),
        "rb": jnp.pad(router_b, ((0, 0), (0, LANES - N_EXPERTS))).reshape(DEPTH, 1, LANES),
    }
    w_in_b = w_in.astype(_bf16)
    w_out_b = w_out.astype(_bf16)
    b_up4 = b_up.reshape(DEPTH, N_EXPERTS, 1, 2 * D_EXPERT)
    b_down4 = b_down.reshape(DEPTH, N_EXPERTS, 1, D_MODEL)
    nfin = norm_final.reshape(1, D_MODEL)

    pool_t = jnp.transpose(state_pool, (0, 2, 1, 3))
    conf_t = jnp.transpose(state_conformer, (0, 2, 1, 3))
    sc_t = jnp.transpose(state_shortconv, (0, 2, 1, 3))

    x_p = x_prompt.reshape(T_PROMPT, D_MODEL)
    x_s = jnp.transpose(x_sample, (1, 0, 2)).reshape(T_SAMPLE, D_MODEL)
    xs_blk = 0

    pool_p, conf_p, sc_p, pool_s, conf_s, sc_s, v_s = [], [], [], [], [], [], []
    for l in range(DEPTH):
        ycat_p, po, co, so = _a_prompt(l, x_p, mod_p4, w_in_b, params)
        ycat_s, a_new, g_new, z_new, v_new = _a_sample(l, x_s, xs_blk, mod_s, w_in_b, params, pool_t, conf_t, sc_t)
        pool_p.append(po[:, POOL_HALO - (POOL_MAX - 1):])
        conf_p.append(co[:, CONF_HALO - (CONF_KERNEL - 1):])
        sc_p.append(so[:, SC_HALO - (SC_KERNEL - 1):])
        pool_s.append(jnp.concatenate([state_pool[l][:, ns:], jnp.transpose(a_new, (1, 0, 2))], axis=1))
        conf_s.append(jnp.concatenate([state_conformer[l][:, ns:], jnp.transpose(g_new, (1, 0, 2))], axis=1))
        sc_s.append(jnp.transpose(z_new, (1, 0, 2))[:, ns - (SC_KERNEL - 1):])
        v_s.append(jnp.transpose(v_new, (1, 0, 2)))

        x1, h_ext, pos, post, meta = _stage_b(l, x_p, x_s, xs_blk, ycat_p, ycat_s, w_out_b, mod_p4, mod_s, params)
        x_sorted, g_sorted = _local_sort(post, h_ext)
        item_e, item_lo, item_nt, seg_start, seg_rows, used, n_items = _work_items(meta)
        y_sorted = _moe_stream(l, item_e, item_lo, item_nt, seg_start, seg_rows, used, n_items, w_up, b_up4,
                               w_down, b_down4, x_sorted, g_sorted)
        if l < DEPTH - 1:
            x_p = x_s = _combine(l, x1, y_sorted, pos, mod_p4, mod_s, nfin, final=False)
            xs_blk = T_PROMPT // T_SAMPLE
        else:
            y_p, y_s = _combine(l, x1, y_sorted, pos, mod_p4, mod_s, nfin, final=True)

    y_prompt = y_p.reshape(BATCH, SEQ, D_MODEL)
    y_sample = jnp.transpose(y_s.reshape(ns, nb, D_MODEL), (1, 0, 2))
    return (y_prompt, y_sample, jnp.stack(pool_p), jnp.stack(pool_s), jnp.stack(conf_p), jnp.stack(conf_s),
            jnp.stack(sc_p), jnp.stack(sc_s), jnp.stack(v_s))
```

```python
import functools

import jax
import jax.numpy as jnp
from jax import lax
from jax.experimental import pallas as pl
from jax.experimental.pallas import tpu as pltpu

D_MODEL = 2048
BATCH = 4
SEQ = 2048
DEPTH = 2
DEC_BATCH = 128
DEC_SEQ = 4
PAST_LEN = 16384
GROUP_WIDTH = 512
POOL_WINDOWS = (2, 4, 8, 16)
POOL_MAX = 16
POOL_CH = 128
SGU_HEADS = 4
SGU_CHUNK = 128
CONF_KERNEL = 31
SC_KERNEL = 3
IN_COLS = 8 * GROUP_WIDTH
N_EXPERTS = 32
TOP_K = 4
D_EXPERT = 2048
SWIGLU_ALPHA = 1.702
SWIGLU_LIMIT = 7.0
EPS = 1e-5

T_PROMPT = BATCH * SEQ
T_SAMPLE = DEC_BATCH * DEC_SEQ
T_ALL = T_PROMPT + T_SAMPLE

LANES = 128
BF16_ROWS = 16
VMEM_LIMIT = 56 * 1024 * 1024

A_TM = 1024
A_TN = 512
A_STEPS = IN_COLS // A_TN
POOL_HALO = 16
CONF_HALO = 32
SC_HALO = 8

B_TM = 512
N_BLOCKS = T_ALL // B_TM
B_PROMPT_BLOCKS = T_PROMPT // B_TM
SEG = BF16_ROWS
SORT_TM = 512
BLOCK_ROWS = -(-(B_TM * TOP_K + N_EXPERTS * (SEG - 1)) // SORT_TM) * SORT_TM
SORT_CHUNKS = BLOCK_ROWS // SORT_TM
GATE_TERMS = 3
SIDE_ID0 = GATE_TERMS * TOP_K
H_EXT = D_MODEL + LANES

MOE_TM = 256
MOE_MT = 8
MOE_TN = 512
MOE_UP_STEPS = D_EXPERT // MOE_TN
MOE_DOWN_STEPS = D_MODEL // MOE_TN
MOE_STEPS = MOE_UP_STEPS + MOE_DOWN_STEPS
MOE_YSLOTS = 4
TILE_SEGS = MOE_TM // SEG
MOE_TILES = (T_ALL * TOP_K + N_BLOCKS * N_EXPERTS * (SEG - 1) + N_EXPERTS * (MOE_TM - 1)) // MOE_TM
MOE_ITEMS = (MOE_TILES + N_EXPERTS * (MOE_MT - 1)) // MOE_MT

_f32 = jnp.float32
_bf16 = jnp.bfloat16


def _rms(x):
    return x * lax.rsqrt(jnp.mean(x * x, axis=-1, keepdims=True) + EPS)


def _ln(x):
    xc = x - jnp.mean(x, axis=-1, keepdims=True)
    return xc * lax.rsqrt(jnp.mean(xc * xc, axis=-1, keepdims=True) + EPS)


def _gelu(x):
    return 0.5 * x * (1.0 + lax.erf(x * (0.5 ** 0.5)))


def _slab_mul(x, m):
    r, c = x.shape
    return (x.reshape(r // 128, 128, c) * m[None]).reshape(r, c)


def _slab_add(x, m):
    r, c = x.shape
    return (x.reshape(r // 128, 128, c) + m[None]).reshape(r, c)


ADA_TN = 1024
ADA_ROWS = DEC_BATCH + 8


def _ada_body(c_ref, w_ref, b_ref, os_ref, op_ref):
    c = c_ref[...]
    a = (c * jax.nn.sigmoid(c)).astype(_bf16)
    r = jnp.dot(a, w_ref[...].astype(_bf16), preferred_element_type=_f32) + b_ref[...]
    os_ref[...] = r[:DEC_BATCH]
    op_ref[...] = r[DEC_BATCH:]


def _ada(c_all, w_ada, b_ada):
    nj = 6 * D_MODEL // ADA_TN
    return pl.pallas_call(
        _ada_body,
        grid=(DEPTH, nj),
        in_specs=[
            pl.BlockSpec((ADA_ROWS, D_MODEL), lambda l, j: (0, 0)),
            pl.BlockSpec((None, D_MODEL, ADA_TN), lambda l, j: (l, 0, j)),
            pl.BlockSpec((None, 1, ADA_TN), lambda l, j: (l, 0, j)),
        ],
        out_specs=[
            pl.BlockSpec((None, DEC_BATCH, ADA_TN), lambda l, j: (l, 0, j)),
            pl.BlockSpec((None, 8, ADA_TN), lambda l, j: (l, 0, j)),
        ],
        out_shape=[
            jax.ShapeDtypeStruct((DEPTH, DEC_BATCH, 6 * D_MODEL), _f32),
            jax.ShapeDtypeStruct((DEPTH, 8, 6 * D_MODEL), _f32),
        ],
        compiler_params=pltpu.CompilerParams(
            dimension_semantics=("arbitrary", "arbitrary"), vmem_limit_bytes=VMEM_LIMIT),
        name="ada_mod",
    )(c_all, w_ada, b_ada.reshape(DEPTH, 1, 6 * D_MODEL))


def _a_prompt_body(x_ref, sh1_ref, sc1_ref, nmix_ref, w_ref, poolw_ref, pscale_ref, sgun_ref, sguw_ref,
                   sgub_ref, cw_ref, cb_ref, clg_ref, clb_ref, scw_ref, onorm_ref,
                   y_ref, pool_o, conf_o, sc_o,
                   h_ref, bufa, bufb, exta, extc, extz):
    i = pl.program_id(1)
    n = pl.program_id(2)
    tm = A_TM
    gw = GROUP_WIDTH

    @pl.when(n == 0)
    def _():
        hn = _rms(x_ref[...]) * nmix_ref[...]
        h_ref[...] = (hn * (1.0 + sc1_ref[...]) + sh1_ref[...]).astype(_bf16)

    def proj():
        return jnp.dot(h_ref[...], w_ref[...], preferred_element_type=_f32)

    def emit(g, y):
        yn = _rms(y) * onorm_ref[:, g * gw:(g + 1) * gw]
        y_ref[:, g * gw:(g + 1) * gw] = yn.astype(_bf16)

    def carry_halo(ext, halo):
        @pl.when(i == 0)
        def _():
            ext[0:halo, :] = jnp.zeros((halo, gw), _f32)

        @pl.when(i > 0)
        def _():
            ext[0:halo, :] = ext[tm:tm + halo, :]

    @pl.when(n == 0)
    def _():
        a = proj()
        carry_halo(exta, POOL_HALO)
        exta[POOL_HALO:, :] = a
        pool_o[...] = a[tm - POOL_HALO:, :]
        pos = (i * tm + lax.broadcasted_iota(jnp.int32, (tm, 1), 0)).astype(_f32)
        for gi, w in enumerate(POOL_WINDOWS):
            c0 = gi * POOL_CH
            s = exta[POOL_HALO:POOL_HALO + tm, c0:c0 + POOL_CH]
            for j in range(1, w):
                s = s + exta[POOL_HALO - j:POOL_HALO - j + tm, c0:c0 + POOL_CH]
            cnt = jnp.minimum(float(w), pos + 1.0)
            pooled = s / cnt - a[:, c0:c0 + POOL_CH]
            yg = jnp.dot(pooled.astype(_bf16), poolw_ref[gi].astype(_bf16), preferred_element_type=_f32)
            bufb[:, c0:c0 + POOL_CH] = yg
        emit(0, bufb[...] * pscale_ref[...])

    @pl.when(n == 1)
    def _():
        bufa[...] = _gelu(proj())

    @pl.when(n == 2)
    def _():
        v = _ln(_gelu(proj())) * sgun_ref[...]
        vb = v.astype(_bf16)
        row = lax.broadcasted_iota(jnp.int32, (SGU_CHUNK, SGU_CHUNK), 0)
        col = lax.broadcasted_iota(jnp.int32, (SGU_CHUNK, SGU_CHUNK), 1)
        for g in range(SGU_HEADS):
            c0 = g * SGU_CHUNK
            wg = jnp.where(row >= col, sguw_ref[g], 0.0).astype(_bf16)
            for c in range(tm // SGU_CHUNK):
                r0 = c * SGU_CHUNK
                m = jnp.dot(wg, vb[r0:r0 + SGU_CHUNK, c0:c0 + SGU_CHUNK], preferred_element_type=_f32)
                bufb[r0:r0 + SGU_CHUNK, c0:c0 + SGU_CHUNK] = m + sgub_ref[:, c0:c0 + SGU_CHUNK]
        emit(1, bufa[...] * bufb[...])

    @pl.when(n == 3)
    def _():
        bufa[...] = proj()

    @pl.when(n == 4)
    def _():
        g = bufa[...] * jax.nn.sigmoid(proj())
        carry_halo(extc, CONF_HALO)
        extc[CONF_HALO:, :] = g
        conf_o[...] = g[tm - CONF_HALO:, :]
        off = CONF_HALO - (CONF_KERNEL - 1)
        acc = jnp.broadcast_to(cb_ref[...], (tm, gw))
        for k in range(CONF_KERNEL):
            acc = acc + cw_ref[k:k + 1, :] * extc[off + k:off + k + tm, :]
        yn = _ln(acc) * clg_ref[...] + clb_ref[...]
        emit(2, yn * jax.nn.sigmoid(yn))

    @pl.when(n == 5)
    def _():
        bufa[...] = proj()

    @pl.when(n == 6)
    def _():
        bufb[...] = proj()

    @pl.when(n == 7)
    def _():
        z = bufb[...] * proj()
        carry_halo(extz, SC_HALO)
        extz[SC_HALO:, :] = z
        sc_o[...] = z[tm - SC_HALO:, :]
        off = SC_HALO - (SC_KERNEL - 1)
        acc = scw_ref[0:1, :] * extz[off:off + tm, :]
        for k in range(1, SC_KERNEL):
            acc = acc + scw_ref[k:k + 1, :] * extz[off + k:off + k + tm, :]
        emit(3, bufa[...] * acc)


def _layer_spec(shape, l, nd):
    zeros = (0,) * len(shape)
    if nd == 1:
        return pl.BlockSpec((None,) + shape, lambda a: (l,) + zeros)
    if nd == 2:
        return pl.BlockSpec((None,) + shape, lambda a, b: (l,) + zeros)
    return pl.BlockSpec((None,) + shape, lambda a, b, c: (l,) + zeros)


def _mixer_params(l, p, nd):
    gw = GROUP_WIDTH
    arrs = [p["cw"], p["cb"], p["clg"], p["clb"], p["scw"], p["onorm"]]
    shapes = [(CONF_KERNEL, gw), (1, gw), (1, gw), (1, gw), (SC_KERNEL, gw), (1, D_MODEL)]
    return arrs, [_layer_spec(s, l, nd) for s in shapes]


def _a_prompt(l, x_all, mod_p4, w_in_b, p):
    gw = GROUP_WIDTH
    ni = SEQ // A_TM
    marrs, mspecs = _mixer_params(l, p, 3)
    in_specs = [
        pl.BlockSpec((A_TM, D_MODEL), lambda b, i, n: (b * ni + i, 0)),
        pl.BlockSpec((None, None, 1, D_MODEL), lambda b, i, n: (l, b, 0, 0)),
        pl.BlockSpec((None, None, 1, D_MODEL), lambda b, i, n: (l, b, 0, 1)),
        _layer_spec((1, D_MODEL), l, 3),
        pl.BlockSpec((None, D_MODEL, A_TN), lambda b, i, n: (l, 0, n)),
        _layer_spec((len(POOL_WINDOWS), POOL_CH, POOL_CH), l, 3),
        _layer_spec((1, gw), l, 3),
        _layer_spec((1, gw), l, 3),
        _layer_spec((SGU_HEADS, SGU_CHUNK, SGU_CHUNK), l, 3),
        _layer_spec((SGU_CHUNK, gw), l, 3),
    ] + mspecs
    out_specs = [
        pl.BlockSpec((A_TM, D_MODEL), lambda b, i, n: (b * ni + i, 0)),
        pl.BlockSpec((None, POOL_HALO, gw), lambda b, i, n: (b, 0, 0)),
        pl.BlockSpec((None, CONF_HALO, gw), lambda b, i, n: (b, 0, 0)),
        pl.BlockSpec((None, SC_HALO, gw), lambda b, i, n: (b, 0, 0)),
    ]
    out_shape = [
        jax.ShapeDtypeStruct((T_PROMPT, D_MODEL), _bf16),
        jax.ShapeDtypeStruct((BATCH, POOL_HALO, gw), _f32),
        jax.ShapeDtypeStruct((BATCH, CONF_HALO, gw), _f32),
        jax.ShapeDtypeStruct((BATCH, SC_HALO, gw), _f32),
    ]
    scratch = [
        pltpu.VMEM((A_TM, D_MODEL), _bf16),
        pltpu.VMEM((A_TM, gw), _f32),
        pltpu.VMEM((A_TM, gw), _f32),
        pltpu.VMEM((POOL_HALO + A_TM, gw), _f32),
        pltpu.VMEM((CONF_HALO + A_TM, gw), _f32),
        pltpu.VMEM((SC_HALO + A_TM, gw), _f32),
    ]
    return pl.pallas_call(
        _a_prompt_body,
        grid=(BATCH, ni, A_STEPS),
        in_specs=in_specs,
        out_specs=out_specs,
        out_shape=out_shape,
        scratch_shapes=scratch,
        compiler_params=pltpu.CompilerParams(
            dimension_semantics=("arbitrary", "arbitrary", "arbitrary"), vmem_limit_bytes=VMEM_LIMIT),
        name="mixers_prompt",
    )(x_all, mod_p4, mod_p4, p["nmix"], w_in_b, p["poolw"], p["pscale"], p["sgun"], p["sguw"], p["sgub"], *marrs)


def _a_sample_body(x_ref, sh1_ref, sc1_ref, nmix_ref, w_ref, poolw_ref, pscale_ref, sgun_ref, sgwc_ref,
                   sgbc_ref, cw_ref, cb_ref, clg_ref, clb_ref, scw_ref, onorm_ref, pst_ref, cst_ref, sst_ref,
                   y_ref, a_o, g_o, z_o, v_o,
                   h_ref, bufa, bufb):
    n = pl.program_id(0)
    gw = GROUP_WIDTH
    nb = DEC_BATCH
    ns = DEC_SEQ

    @pl.when(n == 0)
    def _():
        hn = _rms(x_ref[...]) * nmix_ref[...]
        h_ref[...] = _slab_add(_slab_mul(hn, 1.0 + sc1_ref[...]), sh1_ref[...]).astype(_bf16)

    def proj():
        return jnp.dot(h_ref[...], w_ref[...], preferred_element_type=_f32)

    def emit(g, y):
        yn = _rms(y) * onorm_ref[:, g * gw:(g + 1) * gw]
        y_ref[:, g * gw:(g + 1) * gw] = yn.astype(_bf16)

    def slab(x, s):
        return x[s * nb:(s + 1) * nb, :]

    @pl.when(n == 0)
    def _():
        a = proj()
        a_o[...] = a.reshape(ns, nb, gw)
        npast = POOL_MAX - 1
        for s in range(ns):
            for gi, w in enumerate(POOL_WINDOWS):
                c0 = gi * POOL_CH
                acc = None
                for j in range(w):
                    e = npast + s - j
                    term = pst_ref[e, :, c0:c0 + POOL_CH] if e < npast else slab(a, e - npast)[:, c0:c0 + POOL_CH]
                    acc = term if acc is None else acc + term
                cnt = min(float(w), PAST_LEN + s + 1.0)
                pooled = acc / cnt - slab(a, s)[:, c0:c0 + POOL_CH]
                bufb[s * nb:(s + 1) * nb, c0:c0 + POOL_CH] = pooled
        pooled = bufb[...].astype(_bf16)
        for gi in range(len(POOL_WINDOWS)):
            c0 = gi * POOL_CH
            bufa[:, c0:c0 + POOL_CH] = jnp.dot(pooled[:, c0:c0 + POOL_CH], poolw_ref[gi].astype(_bf16),
                                               preferred_element_type=_f32)
        emit(0, bufa[...] * pscale_ref[...])

    @pl.when(n == 1)
    def _():
        bufa[...] = _gelu(proj())

    @pl.when(n == 2)
    def _():
        v = _ln(_gelu(proj())) * sgun_ref[...]
        v_o[...] = v.reshape(ns, nb, gw)
        for t in range(ns):
            m = jnp.broadcast_to(sgbc_ref[t:t + 1, :], (nb, gw))
            for s in range(t + 1):
                m = m + sgwc_ref[t * ns + s:t * ns + s + 1, :] * slab(v, s)
            bufb[t * nb:(t + 1) * nb, :] = m
        emit(1, bufa[...] * bufb[...])

    @pl.when(n == 3)
    def _():
        bufa[...] = proj()

    @pl.when(n == 4)
    def _():
        g = bufa[...] * jax.nn.sigmoid(proj())
        g_o[...] = g.reshape(ns, nb, gw)
        npast = CONF_KERNEL - 1
        for s in range(ns):
            acc = jnp.broadcast_to(cb_ref[...], (nb, gw))
            for k in range(CONF_KERNEL):
                e = s + k
                term = cst_ref[e] if e < npast else slab(g, e - npast)
                acc = acc + cw_ref[k:k + 1, :] * term
            bufb[s * nb:(s + 1) * nb, :] = acc
        yn = _ln(bufb[...]) * clg_ref[...] + clb_ref[...]
        emit(2, yn * jax.nn.sigmoid(yn))

    @pl.when(n == 5)
    def _():
        bufa[...] = proj()

    @pl.when(n == 6)
    def _():
        bufb[...] = proj()

    @pl.when(n == 7)
    def _():
        z = bufb[...] * proj()
        z_o[...] = z.reshape(ns, nb, gw)
        npast = SC_KERNEL - 1
        for s in range(ns):
            acc = None
            for k in range(SC_KERNEL):
                e = s + k
                term = scw_ref[k:k + 1, :] * (sst_ref[e] if e < npast else slab(z, e - npast))
                acc = term if acc is None else acc + term
            bufb[s * nb:(s + 1) * nb, :] = acc
        emit(3, bufa[...] * bufb[...])


def _a_sample(l, x_all, x_blk, mod_s, w_in_b, p, pool_t, conf_t, sc_t):
    gw = GROUP_WIDTH
    ns, nb = DEC_SEQ, DEC_BATCH
    marrs, mspecs = _mixer_params(l, p, 1)
    in_specs = [
        pl.BlockSpec((T_SAMPLE, D_MODEL), lambda n: (x_blk, 0)),
        pl.BlockSpec((None, nb, D_MODEL), lambda n: (l, 0, 0)),
        pl.BlockSpec((None, nb, D_MODEL), lambda n: (l, 0, 1)),
        _layer_spec((1, D_MODEL), l, 1),
        pl.BlockSpec((None, D_MODEL, A_TN), lambda n: (l, 0, n)),
        _layer_spec((len(POOL_WINDOWS), POOL_CH, POOL_CH), l, 1),
        _layer_spec((1, gw), l, 1),
        _layer_spec((1, gw), l, 1),
        _layer_spec((ns * ns, gw), l, 1),
        _layer_spec((ns, gw), l, 1),
    ] + mspecs + [
        _layer_spec((POOL_MAX - 1, nb, gw), l, 1),
        _layer_spec((CONF_KERNEL - 1, nb, gw), l, 1),
        _layer_spec((SC_KERNEL - 1, nb, gw), l, 1),
    ]
    out_specs = [pl.BlockSpec((T_SAMPLE, D_MODEL), lambda n: (0, 0))] + [
        pl.BlockSpec((ns, nb, gw), lambda n: (0, 0, 0)) for _ in range(4)]
    out_shape = [jax.ShapeDtypeStruct((T_SAMPLE, D_MODEL), _bf16)] + [
        jax.ShapeDtypeStruct((ns, nb, gw), _f32) for _ in range(4)]
    scratch = [
        pltpu.VMEM((T_SAMPLE, D_MODEL), _bf16),
        pltpu.VMEM((T_SAMPLE, gw), _f32),
        pltpu.VMEM((T_SAMPLE, gw), _f32),
    ]
    return pl.pallas_call(
        _a_sample_body,
        grid=(A_STEPS,),
        in_specs=in_specs,
        out_specs=out_specs,
        out_shape=out_shape,
        scratch_shapes=scratch,
        compiler_params=pltpu.CompilerParams(
            dimension_semantics=("arbitrary",), vmem_limit_bytes=VMEM_LIMIT),
        name="mixers_sample",
    )(x_all, mod_s, mod_s, p["nmix"], w_in_b, p["poolw"], p["pscale"], p["sgun"], p["sgwc"], p["sgbc"],
      *marrs, pool_t, conf_t, sc_t)


def _pick_mod(is_prompt, mp_ref, ms_ref):
    return jnp.where(is_prompt, jnp.broadcast_to(mp_ref[...], (DEC_BATCH, D_MODEL)), ms_ref[...])


def _b_body(xp_ref, xs_ref, yp_ref, ys_ref, w_ref, mpg1, mpsh2, mpsc2, msg1, mssh2, mssc2, nffn_ref, rw_ref, rb_ref,
            x1_ref, h2_ref, pos_ref, post_ref, meta_ref):
    i = pl.program_id(0)
    isp = i < B_PROMPT_BLOCKS
    ycat = jnp.where(isp, yp_ref[...], ys_ref[...])
    o = jnp.dot(ycat, w_ref[...], preferred_element_type=_f32)
    x1 = jnp.where(isp, xp_ref[...], xs_ref[...]) + _slab_mul(o, _pick_mod(isp, mpg1, msg1))
    x1_ref[...] = x1
    hn = _rms(x1) * nffn_ref[...]
    h2 = _slab_add(_slab_mul(hn, 1.0 + _pick_mod(isp, mpsc2, mssc2)), _pick_mod(isp, mpsh2, mssh2))
    hi = h2.astype(_bf16)
    h2_ref[:, 0:D_MODEL] = hi
    lo = (h2 - hi.astype(_f32)).astype(_bf16)
    l1 = jnp.dot(hi, rw_ref[...], preferred_element_type=_f32)
    l2 = jnp.dot(lo, rw_ref[:, 0:LANES], preferred_element_type=_f32)
    logits = l1[:, 0:LANES] + l1[:, LANES:] + l2 + rb_ref[...]
    lane = lax.broadcasted_iota(jnp.int32, (B_TM, LANES), 1).astype(_f32)
    neg = jnp.float32(-jnp.inf)
    cur = jnp.where(lane < N_EXPERTS, logits, neg)
    vals, ids = [], []
    for _ in range(TOP_K):
        m = jnp.max(cur, axis=-1, keepdims=True)
        ix = jnp.min(jnp.where(cur == m, lane, float(LANES)), axis=-1, keepdims=True)
        vals.append(m)
        ids.append(ix)
        cur = jnp.where(lane == ix, neg, cur)
    es = [jnp.exp(v - vals[0]) for v in vals]
    tot = es[0] + es[1] + es[2] + es[3]
    side = jnp.zeros((B_TM, LANES), _f32)
    for k in range(TOP_K):
        g = es[k] / tot
        g0 = g.astype(_bf16).astype(_f32)
        g1 = (g - g0).astype(_bf16).astype(_f32)
        for j, term in enumerate((g0, g1, g - g0 - g1)):
            side = jnp.where(lane == float(GATE_TERMS * k + j), term, side)
        side = jnp.where(lane == float(SIDE_ID0 + k), ids[k], side)
    h2_ref[:, D_MODEL:] = side.astype(_bf16)

    hit = [lane == ix for ix in ids]
    chosen = jnp.zeros((B_TM, LANES), _f32)
    for h in hit:
        chosen = jnp.where(h, 1.0, chosen)
    cnt = jnp.sum(chosen, axis=0, keepdims=True)
    seg_rows = jnp.floor((cnt + (SEG - 1.0)) * (1.0 / SEG)) * SEG
    er = lax.broadcasted_iota(jnp.int32, (LANES, LANES), 0)
    ec = lax.broadcasted_iota(jnp.int32, (LANES, LANES), 1)
    before = jnp.where(er < ec, 1.0, 0.0).astype(_bf16)
    seg_start = jnp.dot(jnp.broadcast_to(seg_rows, (8, LANES)).astype(_bf16), before,
                        preferred_element_type=_f32)[0:1, :]
    tr = lax.broadcasted_iota(jnp.int32, (B_TM, B_TM), 0)
    tc = lax.broadcasted_iota(jnp.int32, (B_TM, B_TM), 1)
    earlier = jnp.where(tr > tc, 1.0, 0.0).astype(_bf16)
    rank = jnp.dot(earlier, chosen.astype(_bf16), preferred_element_type=_f32)
    base = seg_start + rank
    pos = jnp.zeros((B_TM, LANES), _f32)
    for k in range(TOP_K):
        pk = jnp.sum(jnp.where(hit[k], base, 0.0), axis=-1, keepdims=True)
        pos = jnp.where(lane == float(k), pk, pos)
    pos_ref[...] = pos
    post_ref[...] = jnp.transpose(pos)[0:8, :]
    sub = lax.broadcasted_iota(jnp.int32, (8, LANES), 0)
    meta = jnp.where(sub == 0, jnp.broadcast_to(seg_start, (8, LANES)),
                     jnp.where(sub == 1, jnp.broadcast_to(seg_rows, (8, LANES)), 0.0))
    meta_ref[...] = meta.astype(jnp.int32)


def _stage_b(l, x_p, x_s, xs_blk, ycat_p, ycat_s, w_out_b, mod_p4, mod_s, p):
    def pb(i):
        return jnp.minimum(i // (SEQ // B_TM), BATCH - 1)

    def mp_spec(j):
        return pl.BlockSpec((None, None, 1, D_MODEL), lambda i: (l, pb(i), 0, j))

    def ms_spec(j):
        return pl.BlockSpec((None, DEC_BATCH, D_MODEL), lambda i: (l, 0, j))

    row_spec = pl.BlockSpec((B_TM, D_MODEL), lambda i: (i, 0))
    in_specs = [
        pl.BlockSpec((B_TM, D_MODEL), lambda i: (jnp.minimum(i, B_PROMPT_BLOCKS - 1), 0)),
        pl.BlockSpec((T_SAMPLE, D_MODEL), lambda i: (xs_blk, 0), pipeline_mode=pl.Buffered(1)),
        pl.BlockSpec((B_TM, D_MODEL), lambda i: (jnp.minimum(i, B_PROMPT_BLOCKS - 1), 0)),
        pl.BlockSpec((T_SAMPLE, D_MODEL), lambda i: (0, 0), pipeline_mode=pl.Buffered(1)),
        pl.BlockSpec((None, D_MODEL, D_MODEL), lambda i: (l, 0, 0), pipeline_mode=pl.Buffered(1)),
        mp_spec(2), mp_spec(3), mp_spec(4), ms_spec(2), ms_spec(3), ms_spec(4),
        _layer_spec((1, D_MODEL), l, 1),
        _layer_spec((D_MODEL, 2 * LANES), l, 1),
        _layer_spec((1, LANES), l, 1),
    ]
    lane_spec = pl.BlockSpec((B_TM, LANES), lambda i: (i, 0))
    return pl.pallas_call(
        _b_body,
        grid=(N_BLOCKS,),
        in_specs=in_specs,
        out_specs=[row_spec, pl.BlockSpec((B_TM, H_EXT), lambda i: (i, 0)), lane_spec,
                   pl.BlockSpec((None, 8, B_TM), lambda i: (i, 0, 0)),
                   pl.BlockSpec((None, 8, LANES), lambda i: (i, 0, 0))],
        out_shape=[
            jax.ShapeDtypeStruct((T_ALL, D_MODEL), _f32),
            jax.ShapeDtypeStruct((T_ALL, H_EXT), _bf16),
            jax.ShapeDtypeStruct((T_ALL, LANES), _f32),
            jax.ShapeDtypeStruct((N_BLOCKS, 8, B_TM), _f32),
            jax.ShapeDtypeStruct((N_BLOCKS, 8, LANES), jnp.int32),
        ],
        compiler_params=pltpu.CompilerParams(
            dimension_semantics=("arbitrary",), vmem_limit_bytes=VMEM_LIMIT),
        name="outproj_router",
    )(x_p, x_s, ycat_p, ycat_s, w_out_b, mod_p4, mod_p4, mod_p4, mod_s, mod_s, mod_s, p["nffn"], p["rw"], p["rb"])


def _sort_body(post_ref, h_ref, o_ref, so_ref):
    for c in range(SORT_CHUNKS):
        j = (c * SORT_TM + lax.broadcasted_iota(jnp.int32, (SORT_TM, B_TM), 0)).astype(_f32)
        sel = jnp.zeros((SORT_TM, B_TM), _f32)
        for k in range(TOP_K):
            sel = jnp.where(j == post_ref[k:k + 1, :], 1.0, sel)
        res = jnp.dot(sel.astype(_bf16), h_ref[...], preferred_element_type=_f32)
        o_ref[c * SORT_TM:(c + 1) * SORT_TM, :] = res[:, 0:D_MODEL].astype(_bf16)
        so_ref[c * SORT_TM:(c + 1) * SORT_TM, :] = res[:, D_MODEL:]


def _local_sort(post, h_ext):
    return pl.pallas_call(
        _sort_body,
        grid=(N_BLOCKS,),
        in_specs=[
            pl.BlockSpec((None, 8, B_TM), lambda b: (b, 0, 0)),
            pl.BlockSpec((B_TM, H_EXT), lambda b: (b, 0)),
        ],
        out_specs=[pl.BlockSpec((BLOCK_ROWS, D_MODEL), lambda b: (b, 0)),
                   pl.BlockSpec((BLOCK_ROWS, LANES), lambda b: (b, 0))],
        out_shape=[jax.ShapeDtypeStruct((N_BLOCKS * BLOCK_ROWS, D_MODEL), _bf16),
                   jax.ShapeDtypeStruct((N_BLOCKS * BLOCK_ROWS, LANES), _f32)],
        compiler_params=pltpu.CompilerParams(
            dimension_semantics=("arbitrary",), vmem_limit_bytes=VMEM_LIMIT),
        name="local_sort",
    )(post, h_ext)


def _moe_body(ie_ref, ilo_ref, int_ref, start_ref, rows_ref, used_ref,
              wg_ref, wl_ref, wd_ref, bg_ref, bl_ref, bd_ref, x_hbm, g_hbm, y_hbm,
              xbuf, hbuf, gbuf, wc, seg_src, nseg, pending, sem_x, sem_y):
    i = pl.program_id(0)
    n = pl.program_id(1)
    e = ie_ref[i]
    lo = ilo_ref[i]
    nt = int_ref[i]
    hi = lo + nt * MOE_TM
    tm, tn = MOE_TM, MOE_TN

    def x_copy(src, dst):
        return pltpu.make_async_copy(x_hbm.at[pl.ds(src, SEG)], xbuf.at[pl.ds(dst, SEG)], sem_x.at[dst // tm])

    def g_copy(src, dst):
        return pltpu.make_async_copy(g_hbm.at[pl.ds(src, SEG)], gbuf.at[pl.ds(dst, SEG)], sem_x.at[dst // tm])

    def y_copy(src, dst):
        return pltpu.make_async_copy(xbuf.at[pl.ds(src, SEG)], y_hbm.at[pl.ds(dst, SEG)], sem_y.at[0])

    def drain():
        def w(_, carry):
            y_copy(0, 0).wait()
            return carry
        lax.fori_loop(0, pending[0], w, 0)
        pending[0] = 0

    @pl.when((i == 0) & (n == 0))
    def _():
        pending[0] = 0

    @pl.when((n == 0) & (nt > 0))
    def _():
        drain()

        def per_block(b, p):
            rows = rows_ref[b * N_EXPERTS + e]
            start = start_ref[b * N_EXPERTS + e]

            def per_piece(g, carry):
                q = p + g * SEG

                @pl.when((q >= lo) & (q < hi))
                def _():
                    src = pl.multiple_of(b * BLOCK_ROWS + start + g * SEG, SEG)
                    dst = pl.multiple_of(q - lo, SEG)
                    seg_src[dst // SEG] = src
                    x_copy(src, dst).start()
                    g_copy(src, dst).start()
                return carry
            lax.fori_loop(0, rows // SEG, per_piece, 0)
            return p + rows
        end = lax.fori_loop(0, N_BLOCKS, per_block, 0)
        ns = (jnp.minimum(end, hi) - lo) // SEG
        nseg[0] = ns

        def zero_piece(g, carry):
            r = pl.multiple_of(g * SEG, SEG)
            xbuf[pl.ds(r, SEG), :] = jnp.zeros((SEG, D_MODEL), _bf16)
            gbuf[pl.ds(r, SEG), :] = jnp.zeros((SEG, LANES), _f32)
            return carry
        lax.fori_loop(ns, nt * TILE_SEGS, zero_piece, 0)

    def tile_pieces(t):
        return jnp.clip(nseg[0] - t * TILE_SEGS, 0, TILE_SEGS)

    for c in range(MOE_UP_STEPS):
        @pl.when((n == c) & (nt > 0))
        def _(c=c):
            wc[:, 0:tn] = wg_ref[...].astype(_bf16)
            wc[:, tn:] = wl_ref[...].astype(_bf16)

            def body(t, carry):
                if c == 0:
                    def w(_, cc):
                        x_copy(0, pl.multiple_of(t * tm, tm)).wait()
                        g_copy(0, pl.multiple_of(t * tm, tm)).wait()
                        return cc
                    lax.fori_loop(0, tile_pieces(t), w, 0)
                r0 = pl.multiple_of(t * tm, tm)
                up = jnp.dot(xbuf[pl.ds(r0, tm), :], wc[...], preferred_element_type=_f32)
                glu = jnp.minimum(up[:, 0:tn] + bg_ref[...], SWIGLU_LIMIT)
                lin = jnp.clip(up[:, tn:] + bl_ref[...], -SWIGLU_LIMIT, SWIGLU_LIMIT)
                act = glu * jax.nn.sigmoid(SWIGLU_ALPHA * glu) * (lin + 1.0)
                hbuf[pl.ds(r0, tm), c * tn:(c + 1) * tn] = act.astype(_bf16)
                return carry
            lax.fori_loop(0, nt, body, 0)

    for c in range(MOE_DOWN_STEPS):
        @pl.when((n == MOE_UP_STEPS + c) & (nt > 0))
        def _(c=c):
            wc[:, 0:tn] = wd_ref[...].astype(_bf16)

            def body(t, carry):
                r0 = pl.multiple_of(t * tm, tm)
                y = jnp.dot(hbuf[pl.ds(r0, tm), :], wc[:, 0:tn], preferred_element_type=_f32) + bd_ref[...]
                g = gbuf[pl.ds(r0, tm), :]
                gate = g[:, 0:1] + g[:, 1:2] + g[:, 2:3]
                xbuf[pl.ds(r0, tm), c * tn:(c + 1) * tn] = (y * gate).astype(_bf16)
                if c == MOE_DOWN_STEPS - 1:
                    np_ = tile_pieces(t)
                    for q in range(TILE_SEGS):
                        @pl.when(q < np_)
                        def _(q=q):
                            y_copy(pl.multiple_of(r0 + q * SEG, SEG),
                                   pl.multiple_of(seg_src[t * TILE_SEGS + q], SEG)).start()
                return carry
            lax.fori_loop(0, nt, body, 0)
            if c == MOE_DOWN_STEPS - 1:
                pending[0] = nseg[0]

    @pl.when((i == MOE_ITEMS - 1) & (n == MOE_STEPS - 1))
    def _():
        drain()
        hbuf[0:SEG, :] = jnp.zeros((SEG, D_EXPERT), _bf16)

        def zero_copy(row):
            return pltpu.make_async_copy(hbuf.at[pl.ds(0, SEG)], y_hbm.at[pl.ds(row, SEG)], sem_y.at[0])

        def per_block(b, carry):
            def start(g, cc):
                zero_copy(pl.multiple_of(b * BLOCK_ROWS + g * SEG, SEG)).start()
                return cc

            def wait(g, cc):
                zero_copy(0).wait()
                return cc
            first = used_ref[b] // SEG
            lax.fori_loop(first, BLOCK_ROWS // SEG, start, 0)
            lax.fori_loop(first, BLOCK_ROWS // SEG, wait, 0)
            return carry
        lax.fori_loop(0, N_BLOCKS, per_block, 0)


def _moe(l, item_e, item_lo, item_nt, seg_start, seg_rows, used, w_up, b_up4, w_down, b_down4, x_sorted,
         g_sorted):
    def col_up(n, valid):
        return jnp.where(valid, jnp.minimum(n, MOE_UP_STEPS - 1), MOE_UP_STEPS - 1)

    def col_down(n, valid):
        return jnp.where(valid, jnp.maximum(n - MOE_UP_STEPS, 0), MOE_DOWN_STEPS - 1)

    def glu_map(i, n, ie, ilo, int_, *_):
        return (l, ie[i], 0, col_up(n, int_[i] > 0))

    def lin_map(i, n, ie, ilo, int_, *_):
        return (l, ie[i], 0, MOE_UP_STEPS + col_up(n, int_[i] > 0))

    def down_map(i, n, ie, ilo, int_, *_):
        return (l, ie[i], 0, col_down(n, int_[i] > 0))

    grid_spec = pltpu.PrefetchScalarGridSpec(
        num_scalar_prefetch=6,
        grid=(MOE_ITEMS, MOE_STEPS),
        in_specs=[
            pl.BlockSpec((None, None, D_MODEL, MOE_TN), glu_map),
            pl.BlockSpec((None, None, D_MODEL, MOE_TN), lin_map),
            pl.BlockSpec((None, None, D_EXPERT, MOE_TN), down_map),
            pl.BlockSpec((None, None, 1, MOE_TN), glu_map),
            pl.BlockSpec((None, None, 1, MOE_TN), lin_map),
            pl.BlockSpec((None, None, 1, MOE_TN), down_map),
            pl.BlockSpec(memory_space=pl.ANY),
            pl.BlockSpec(memory_space=pl.ANY),
        ],
        out_specs=pl.BlockSpec(memory_space=pl.ANY),
        scratch_shapes=[
            pltpu.VMEM((MOE_MT * MOE_TM, D_MODEL), _bf16),
            pltpu.VMEM((MOE_MT * MOE_TM, D_EXPERT), _bf16),
            pltpu.VMEM((MOE_MT * MOE_TM, LANES), _f32),
            pltpu.VMEM((D_MODEL, 2 * MOE_TN), _bf16),
            pltpu.SMEM((MOE_MT * TILE_SEGS,), jnp.int32),
            pltpu.SMEM((1,), jnp.int32),
            pltpu.SMEM((1,), jnp.int32),
            pltpu.SemaphoreType.DMA((MOE_MT,)),
            pltpu.SemaphoreType.DMA((1,)),
        ],
    )
    return pl.pallas_call(
        _moe_body,
        grid_spec=grid_spec,
        out_shape=jax.ShapeDtypeStruct((N_BLOCKS * BLOCK_ROWS, D_MODEL), _bf16),
        compiler_params=pltpu.CompilerParams(
            dimension_semantics=("arbitrary", "arbitrary"), vmem_limit_bytes=VMEM_LIMIT),
        name="moe_experts",
    )(item_e, item_lo, item_nt, seg_start, seg_rows, used, w_up, w_up, w_down, b_up4, b_up4, b_down4, x_sorted,
      g_sorted)


BIG_TM = 2 * MOE_TM
SMALL_TM = MOE_TM // 2
MOE_DN = 1024
STREAM_UP_STEPS = D_EXPERT // MOE_TN
STREAM_DOWN_STEPS = D_MODEL // MOE_DN
STREAM_STEPS = STREAM_UP_STEPS + STREAM_DOWN_STEPS
MOE_WSLOTS = 2
CAST_ROWS = D_MODEL // MOE_MT


def _moe_stream_body(ie_ref, ilo_ref, int_ref, start_ref, rows_ref, used_ref, nitems_ref,
                     wup_hbm, wdn_hbm, bup_hbm, bdn_hbm, x_hbm, g_hbm, y_hbm,
                     wbuf, wc, xbuf, hbuf, gbuf, bup, bdn, seg_src, nseg, pending,
                     sem_w, sem_b, sem_x, sem_y, *, l):
    tm, tn = MOE_TM, MOE_TN
    n_items = nitems_ref[0]
    n_steps = n_items * STREAM_STEPS

    def w_half(src_ref, e, col0, slot, half):
        return pltpu.make_async_copy(src_ref.at[l, e, :, pl.ds(col0, tn)],
                                     wbuf.at[slot, :, pl.ds(half * tn, tn)], sem_w.at[slot])

    def issue_w(s):
        e = ie_ref[s // STREAM_STEPS]
        n = s % STREAM_STEPS
        slot = s % MOE_WSLOTS
        for c in range(STREAM_UP_STEPS):
            @pl.when(n == c)
            def _(c=c):
                w_half(wup_hbm, e, c * tn, slot, 0).start()
                w_half(wup_hbm, e, D_EXPERT + c * tn, slot, 1).start()
        for c in range(STREAM_DOWN_STEPS):
            @pl.when(n == STREAM_UP_STEPS + c)
            def _(c=c):
                w_half(wdn_hbm, e, c * MOE_DN, slot, 0).start()
                w_half(wdn_hbm, e, c * MOE_DN + tn, slot, 1).start()

    def wait_w(s):
        slot = s % MOE_WSLOTS
        for half in range(2):
            w_half(wdn_hbm, 0, 0, slot, half).wait()

    def b_copies(item):
        e = ie_ref[item]
        par = item % 2
        return (pltpu.make_async_copy(bup_hbm.at[l, e], bup.at[par], sem_b.at[par]),
                pltpu.make_async_copy(bdn_hbm.at[l, e], bdn.at[par], sem_b.at[par]))

    def x_copy(p, src, dst):
        return pltpu.make_async_copy(x_hbm.at[pl.ds(src, SEG)], xbuf.at[p, pl.ds(dst, SEG)],
                                     sem_x.at[p * MOE_MT + dst // tm])

    def g_copy(p, src, dst):
        return pltpu.make_async_copy(g_hbm.at[pl.ds(src, SEG)], gbuf.at[p, pl.ds(dst, SEG)],
                                     sem_x.at[p * MOE_MT + dst // tm])

    def y_copy(p, src, dst):
        return pltpu.make_async_copy(xbuf.at[p, pl.ds(src, SEG)], y_hbm.at[pl.ds(dst, SEG)], sem_y.at[p])

    def drain(p):
        def w(_, carry):
            y_copy(p, 0, 0).wait()
            return carry
        lax.fori_loop(0, pending[p], w, 0)
        pending[p] = 0

    def tile_pieces(p, t):
        return jnp.clip(nseg[p] - t * TILE_SEGS, 0, TILE_SEGS)

    def plan(rows):
        n_big = rows // BIG_TM
        rem = rows - n_big * BIG_TM
        round_up = rem > BIG_TM - SMALL_TM
        n_big = n_big + jnp.where(round_up, 1, 0)
        rem = jnp.where(round_up, 0, rem)
        has_mid = rem > SMALL_TM
        has_small = (rem > 0) & ((rem <= SMALL_TM) | (rem > tm))
        off_mid = n_big * BIG_TM
        off_small = off_mid + jnp.where(has_mid, tm, 0)
        return n_big, has_mid, has_small, off_mid, off_small

    def load_item(item):
        p = item % 2
        e = ie_ref[item]
        lo = ilo_ref[item]
        nt = int_ref[item]
        hi = lo + nt * tm

        def per_block(b, q0):
            rows = rows_ref[b * N_EXPERTS + e]
            start = start_ref[b * N_EXPERTS + e]

            def per_piece(g, cc):
                q = q0 + g * SEG

                @pl.when((q >= lo) & (q < hi))
                def _():
                    src = pl.multiple_of(b * BLOCK_ROWS + start + g * SEG, SEG)
                    dst = pl.multiple_of(q - lo, SEG)
                    seg_src[p * (MOE_MT * TILE_SEGS) + dst // SEG] = src
                    x_copy(p, src, dst).start()
                    g_copy(p, src, dst).start()
                return cc
            lax.fori_loop(0, rows // SEG, per_piece, 0)
            return q0 + rows
        end = lax.fori_loop(0, N_BLOCKS, per_block, 0)
        ns = (jnp.minimum(end, hi) - lo) // SEG
        nseg[p] = ns

        def zero_piece(g, cc):
            r = pl.multiple_of(g * SEG, SEG)
            xbuf[p, pl.ds(r, SEG), :] = jnp.zeros((SEG, D_MODEL), _bf16)
            gbuf[p, pl.ds(r, SEG), :] = jnp.zeros((SEG, LANES), _f32)
            return cc
        n_big, has_mid, has_small, _, off_small = plan(ns * SEG)
        computed = off_small + jnp.where(has_small, SMALL_TM, 0)
        lax.fori_loop(ns, computed // SEG, zero_piece, 0)

    pending[0] = 0
    pending[1] = 0
    issue_w(0)
    issue_w(1)
    for cp in b_copies(0):
        cp.start()
    load_item(0)
    wait_w(0)
    wc[0] = wbuf[0].astype(_bf16)

    def step(s, carry):
        item = s // STREAM_STEPS
        n = s % STREAM_STEPS
        e = ie_ref[item]
        lo = ilo_ref[item]
        nt = int_ref[item]
        hi = lo + nt * tm
        par = item % 2
        cur = s % 2
        nxt = (s + 1) % 2
        nslot = (s + 1) % MOE_WSLOTS

        @pl.when(s + 2 < n_steps)
        def _():
            issue_w(s + 2)

        @pl.when(s + 1 < n_steps)
        def _():
            wait_w(s + 1)

        def cast_slice(t):
            r = pl.multiple_of(t * CAST_ROWS, CAST_ROWS)
            wc[nxt, pl.ds(r, CAST_ROWS), :] = wbuf[nslot, pl.ds(r, CAST_ROWS), :].astype(_bf16)

        @pl.when(n == 0)
        def _():
            for cp in b_copies(item):
                cp.wait()

        @pl.when((n == 1) & (item + 1 < n_items))
        def _():
            for cp in b_copies(item + 1):
                cp.start()
            drain(1 - par)
            load_item(item + 1)

        n_big, has_mid, has_small, off_mid, off_small = plan(nseg[par] * SEG)
        mid_i = jnp.where(has_mid, 1, 0)
        mid_slice = 2 * n_big
        small_slice = mid_slice + mid_i

        def up_rows(c, r0, nrows, slices):
            if c == 0:
                for u in range(max(nrows // tm, 1)):
                    sub = r0 // tm + u

                    def w(_, c3, sub=sub):
                        x_copy(par, 0, pl.multiple_of(sub * tm, tm)).wait()
                        g_copy(par, 0, pl.multiple_of(sub * tm, tm)).wait()
                        return c3
                    lax.fori_loop(0, tile_pieces(par, sub), w, 0)
            for sl in slices:
                cast_slice(sl)
            up = jnp.dot(xbuf[par, pl.ds(r0, nrows), :], wc[cur], preferred_element_type=_f32)
            bg = bup[par, :, c * tn:(c + 1) * tn]
            bl = bup[par, :, D_EXPERT + c * tn:D_EXPERT + (c + 1) * tn]
            glu = jnp.minimum(up[:, 0:tn] + bg, SWIGLU_LIMIT)
            lin = jnp.clip(up[:, tn:] + bl, -SWIGLU_LIMIT, SWIGLU_LIMIT)
            act = glu * jax.nn.sigmoid(SWIGLU_ALPHA * glu) * (lin + 1.0)
            hbuf[pl.ds(r0, nrows), c * tn:(c + 1) * tn] = act.astype(_bf16)

        def down_rows(c, r0, nrows, slices):
            for sl in slices:
                cast_slice(sl)
            y = (jnp.dot(hbuf[pl.ds(r0, nrows), :], wc[cur], preferred_element_type=_f32)
                 + bdn[par, :, c * MOE_DN:(c + 1) * MOE_DN])
            g = gbuf[par, pl.ds(r0, nrows), :]
            ef = e.astype(_f32)
            gate = jnp.zeros((nrows, 1), _f32)
            for k in range(TOP_K):
                gk = g[:, GATE_TERMS * k:GATE_TERMS * k + 1]
                for j in range(1, GATE_TERMS):
                    gk = gk + g[:, GATE_TERMS * k + j:GATE_TERMS * k + j + 1]
                gate = jnp.where(g[:, SIDE_ID0 + k:SIDE_ID0 + k + 1] == ef, gk, gate)
            xbuf[par, pl.ds(r0, nrows), c * MOE_DN:(c + 1) * MOE_DN] = (y * gate).astype(_bf16)
            if c == STREAM_DOWN_STEPS - 1:
                for q in range(nrows // SEG):
                    piece = r0 // SEG + q

                    @pl.when(piece < nseg[par])
                    def _(q=q, piece=piece):
                        dst = seg_src[par * (MOE_MT * TILE_SEGS) + piece]
                        y_copy(par, pl.multiple_of(r0 + q * SEG, SEG), pl.multiple_of(dst, SEG)).start()

        def all_rows(fn, c):
            def big(t, cc):
                fn(c, pl.multiple_of(t * BIG_TM, BIG_TM), BIG_TM, (2 * t, 2 * t + 1))
                return cc
            lax.fori_loop(0, n_big, big, 0)

            @pl.when(has_mid)
            def _():
                fn(c, pl.multiple_of(off_mid, tm), tm, (mid_slice,))

            @pl.when(has_small)
            def _():
                fn(c, pl.multiple_of(off_small, SMALL_TM), SMALL_TM, (small_slice,))

        for c in range(STREAM_UP_STEPS):
            @pl.when(n == c)
            def _(c=c):
                all_rows(up_rows, c)

        for c in range(STREAM_DOWN_STEPS):
            @pl.when(n == STREAM_UP_STEPS + c)
            def _(c=c):
                all_rows(down_rows, c)
                if c == STREAM_DOWN_STEPS - 1:
                    pending[par] = nseg[par]

        def rest(t, cc):
            cast_slice(t)
            return cc
        lax.fori_loop(small_slice + jnp.where(has_small, 1, 0), MOE_MT, rest, 0)
        return carry
    lax.fori_loop(0, n_steps, step, 0)

    drain(0)
    drain(1)
    hbuf[0:SEG, :] = jnp.zeros((SEG, D_EXPERT), _bf16)

    def zero_copy(row):
        return pltpu.make_async_copy(hbuf.at[pl.ds(0, SEG)], y_hbm.at[pl.ds(row, SEG)], sem_y.at[0])

    def zero_block(b, carry):
        def zstart(g, cc):
            zero_copy(pl.multiple_of(b * BLOCK_ROWS + g * SEG, SEG)).start()
            return cc

        def zwait(g, cc):
            zero_copy(0).wait()
            return cc
        first = used_ref[b] // SEG
        lax.fori_loop(first, BLOCK_ROWS // SEG, zstart, 0)
        lax.fori_loop(first, BLOCK_ROWS // SEG, zwait, 0)
        return carry
    lax.fori_loop(0, N_BLOCKS, zero_block, 0)


def _moe_stream(l, item_e, item_lo, item_nt, seg_start, seg_rows, used, n_items, w_up, b_up4, w_down, b_down4,
                x_sorted, g_sorted):
    any_spec = pl.BlockSpec(memory_space=pl.ANY)
    grid_spec = pltpu.PrefetchScalarGridSpec(
        num_scalar_prefetch=7,
        grid=(1,),
        in_specs=[any_spec] * 6,
        out_specs=any_spec,
        scratch_shapes=[
            pltpu.VMEM((MOE_WSLOTS, D_MODEL, 2 * MOE_TN), _f32),
            pltpu.VMEM((2, D_MODEL, 2 * MOE_TN), _bf16),
            pltpu.VMEM((2, MOE_MT * MOE_TM, D_MODEL), _bf16),
            pltpu.VMEM((MOE_MT * MOE_TM, D_EXPERT), _bf16),
            pltpu.VMEM((2, MOE_MT * MOE_TM, LANES), _f32),
            pltpu.VMEM((2, 1, 2 * D_EXPERT), _f32),
            pltpu.VMEM((2, 1, D_MODEL), _f32),
            pltpu.SMEM((2 * MOE_MT * TILE_SEGS,), jnp.int32),
            pltpu.SMEM((2,), jnp.int32),
            pltpu.SMEM((2,), jnp.int32),
            pltpu.SemaphoreType.DMA((MOE_WSLOTS,)),
            pltpu.SemaphoreType.DMA((2,)),
            pltpu.SemaphoreType.DMA((2 * MOE_MT,)),
            pltpu.SemaphoreType.DMA((2,)),
        ],
    )
    return pl.pallas_call(
        functools.partial(_moe_stream_body, l=l),
        grid_spec=grid_spec,
        out_shape=jax.ShapeDtypeStruct((N_BLOCKS * BLOCK_ROWS, D_MODEL), _bf16),
        compiler_params=pltpu.CompilerParams(
            dimension_semantics=("arbitrary",), vmem_limit_bytes=60 * 1024 * 1024),
        name="moe_experts",
    )(item_e, item_lo, item_nt, seg_start, seg_rows, used, n_items, w_up, w_down, b_up4, b_down4, x_sorted,
      g_sorted)


def _work_items(meta):
    seg_start = meta[:, 0, :N_EXPERTS]
    seg_rows = meta[:, 1, :N_EXPERTS]
    rows_e = jnp.sum(seg_rows, axis=0)
    nt = (rows_e + MOE_TM - 1) // MOE_TM
    ipe = (nt + MOE_MT - 1) // MOE_MT
    cum = jnp.cumsum(ipe)
    first = cum - ipe
    total = cum[-1]
    ii = jnp.arange(MOE_ITEMS, dtype=jnp.int32)
    ii_c = jnp.minimum(ii, total - 1)
    e_i = jnp.minimum(jnp.sum((cum[None, :] <= ii_c[:, None]).astype(jnp.int32), axis=1), N_EXPERTS - 1)
    valid = ii < total
    j = ii - first[e_i]
    lo_i = jnp.where(valid, j * (MOE_MT * MOE_TM), 0)
    nt_i = jnp.where(valid, jnp.minimum(MOE_MT, nt[e_i] - j * MOE_MT), 0)
    used = jnp.sum(seg_rows, axis=1)
    i32 = jnp.int32
    return (e_i.astype(i32), lo_i.astype(i32), nt_i.astype(i32), seg_start.reshape(-1).astype(i32),
            seg_rows.reshape(-1).astype(i32), used.astype(i32), total.astype(i32).reshape(1))


def _c_body(x1_ref, y_ref, pos_ref, mpg2, msg2, nfin_ref, *rest, final):
    if final:
        outp_ref, outs_ref, sel_ref = rest
    else:
        out_ref, sel_ref = rest
    i = pl.program_id(0)
    isp = i < B_PROMPT_BLOCKS
    for c in range(SORT_CHUNKS):
        j = (c * SORT_TM + lax.broadcasted_iota(jnp.int32, (B_TM, SORT_TM), 1)).astype(_f32)
        sel = jnp.zeros((B_TM, SORT_TM), _f32)
        for k in range(TOP_K):
            sel = jnp.where(j == pos_ref[:, k:k + 1], 1.0, sel)
        sel_ref[:, c * SORT_TM:(c + 1) * SORT_TM] = sel.astype(_bf16)
    acc = jnp.dot(sel_ref[...], y_ref[...], preferred_element_type=_f32)
    x2 = x1_ref[...] + _slab_mul(acc, _pick_mod(isp, mpg2, msg2))
    if final:
        y = _rms(x2) * nfin_ref[...]

        @pl.when(isp)
        def _():
            outp_ref[...] = y

        @pl.when(jnp.logical_not(isp))
        def _():
            outs_ref[...] = y
    else:
        out_ref[...] = x2


def _combine(l, x1, y_sorted, pos, mod_p4, mod_s, nfin, final):
    def pb(i):
        return jnp.minimum(i // (SEQ // B_TM), BATCH - 1)

    row_spec = pl.BlockSpec((B_TM, D_MODEL), lambda i: (i, 0))
    lane_spec = pl.BlockSpec((B_TM, LANES), lambda i: (i, 0))
    if final:
        out_specs = [pl.BlockSpec((B_TM, D_MODEL), lambda i: (jnp.minimum(i, B_PROMPT_BLOCKS - 1), 0)),
                     pl.BlockSpec((T_SAMPLE, D_MODEL), lambda i: (0, 0))]
        out_shape = [jax.ShapeDtypeStruct((T_PROMPT, D_MODEL), _f32),
                     jax.ShapeDtypeStruct((T_SAMPLE, D_MODEL), _f32)]
    else:
        out_specs = row_spec
        out_shape = jax.ShapeDtypeStruct((T_ALL, D_MODEL), _f32)
    return pl.pallas_call(
        functools.partial(_c_body, final=final),
        grid=(N_BLOCKS,),
        in_specs=[
            row_spec,
            pl.BlockSpec((BLOCK_ROWS, D_MODEL), lambda i: (i, 0)),
            lane_spec,
            pl.BlockSpec((None, None, 1, D_MODEL), lambda i: (l, pb(i), 0, 5)),
            pl.BlockSpec((None, DEC_BATCH, D_MODEL), lambda i: (l, 0, 5)),
            pl.BlockSpec((1, D_MODEL), lambda i: (0, 0)),
        ],
        out_specs=out_specs,
        out_shape=out_shape,
        scratch_shapes=[pltpu.VMEM((B_TM, BLOCK_ROWS), _bf16)],
        compiler_params=pltpu.CompilerParams(
            dimension_semantics=("arbitrary",), vmem_limit_bytes=VMEM_LIMIT),
        name="moe_combine",
    )(x1, y_sorted, pos, mod_p4, mod_s, nfin)


def kernel(x_prompt, x_sample, c_prompt, c_sample, state_pool, state_conformer, state_shortconv, w_ada, b_ada, norm_mix, norm_ffn, norm_final, w_in, pool_w, pool_scale, sgu_norm, sgu_w, sgu_b, conf_w, conf_b, conf_ln_g, conf_ln_b, sc_w, out_norm, w_out, router_w, router_b, w_up, b_up, w_down, b_down):
    gw = GROUP_WIDTH
    ns, nb = DEC_SEQ, DEC_BATCH

    c_all = jnp.concatenate([c_sample, c_prompt, jnp.zeros((ADA_ROWS - nb - BATCH, D_MODEL), _f32)], axis=0)
    mod_s, mod_p8 = _ada(c_all, w_ada, b_ada)
    mod_p4 = mod_p8[:, :BATCH].reshape(DEPTH, BATCH, 1, 6 * D_MODEL)

    rw_hi = router_w.astype(_bf16)
    rw_lo = (router_w - rw_hi.astype(_f32)).astype(_bf16)
    lane_pad = ((0, 0), (0, 0), (0, LANES - N_EXPERTS))
    params = {
        "nmix": norm_mix.reshape(DEPTH, 1, D_MODEL),
        "nffn": norm_ffn.reshape(DEPTH, 1, D_MODEL),
        "poolw": pool_w,
        "pscale": pool_scale.reshape(DEPTH, 1, gw),
        "sgun": sgu_norm.reshape(DEPTH, 1, gw),
        "sguw": sgu_w,
        "sgub": jnp.repeat(jnp.swapaxes(sgu_b, 1, 2), SGU_CHUNK, axis=2),
        "sgwc": jnp.repeat(jnp.transpose(sgu_w[:, :, :ns, :ns], (0, 2, 3, 1)).reshape(DEPTH, ns * ns, SGU_HEADS),
                           SGU_CHUNK, axis=2),
        "sgbc": jnp.repeat(jnp.swapaxes(sgu_b[:, :, :ns], 1, 2), SGU_CHUNK, axis=2),
        "cw": conf_w,
        "cb": conf_b.reshape(DEPTH, 1, gw),
        "clg": conf_ln_g.reshape(DEPTH, 1, gw),
        "clb": conf_ln_b.reshape(DEPTH, 1, gw),
        "scw": sc_w,
        "onorm": out_norm.reshape(DEPTH, 1, D_MODEL),
        "rw": jnp.concatenate([jnp.pad(rw_hi, lane_pad), jnp.pad(rw_lo, lane_pad)], axis=2),
        "rb": jnp.pad(router_b, ((0, 0), (0, LANES - N_EXPERTS))).reshape(DEPTH, 1, LANES),
    }
    w_in_b = w_in.astype(_bf16)
    w_out_b = w_out.astype(_bf16)
    b_up4 = b_up.reshape(DEPTH, N_EXPERTS, 1, 2 * D_EXPERT)
    b_down4 = b_down.reshape(DEPTH, N_EXPERTS, 1, D_MODEL)
    nfin = norm_final.reshape(1, D_MODEL)

    pool_t = jnp.transpose(state_pool, (0, 2, 1, 3))
    conf_t = jnp.transpose(state_conformer, (0, 2, 1, 3))
    sc_t = jnp.transpose(state_shortconv, (0, 2, 1, 3))

    x_p = x_prompt.reshape(T_PROMPT, D_MODEL)
    x_s = jnp.transpose(x_sample, (1, 0, 2)).reshape(T_SAMPLE, D_MODEL)
    xs_blk = 0

    pool_p, conf_p, sc_p, pool_s, conf_s, sc_s, v_s = [], [], [], [], [], [], []
    for l in range(DEPTH):
        ycat_p, po, co, so = _a_prompt(l, x_p, mod_p4, w_in_b, params)
        ycat_s, a_new, g_new, z_new, v_new = _a_sample(l, x_s, xs_blk, mod_s, w_in_b, params, pool_t, conf_t, sc_t)
        pool_p.append(po[:, POOL_HALO - (POOL_MAX - 1):])
        conf_p.append(co[:, CONF_HALO - (CONF_KERNEL - 1):])
        sc_p.append(so[:, SC_HALO - (SC_KERNEL - 1):])
        pool_s.append(jnp.concatenate([state_pool[l][:, ns:], jnp.transpose(a_new, (1, 0, 2))], axis=1))
        conf_s.append(jnp.concatenate([state_conformer[l][:, ns:], jnp.transpose(g_new, (1, 0, 2))], axis=1))
        sc_s.append(jnp.transpose(z_new, (1, 0, 2))[:, ns - (SC_KERNEL - 1):])
        v_s.append(jnp.transpose(v_new, (1, 0, 2)))

        x1, h_ext, pos, post, meta = _stage_b(l, x_p, x_s, xs_blk, ycat_p, ycat_s, w_out_b, mod_p4, mod_s, params)
        x_sorted, g_sorted = _local_sort(post, h_ext)
        item_e, item_lo, item_nt, seg_start, seg_rows, used, n_items = _work_items(meta)
        y_sorted = _moe_stream(l, item_e, item_lo, item_nt, seg_start, seg_rows, used, n_items, w_up, b_up4,
                               w_down, b_down4, x_sorted, g_sorted)
        if l < DEPTH - 1:
            x_p = x_s = _combine(l, x1, y_sorted, pos, mod_p4, mod_s, nfin, final=False)
            xs_blk = T_PROMPT // T_SAMPLE
        else:
            y_p, y_s = _combine(l, x1, y_sorted, pos, mod_p4, mod_s, nfin, final=True)

    y_prompt = y_p.reshape(BATCH, SEQ, D_MODEL)
    y_sample = jnp.transpose(y_s.reshape(ns, nb, D_MODEL), (1, 0, 2))
    return (y_prompt, y_sample, jnp.stack(pool_p), jnp.stack(pool_s), jnp.stack(conf_p), jnp.stack(conf_s),
            jnp.stack(sc_p), jnp.stack(sc_s), jnp.stack(v_s))
```

```python
import functools

import jax
import jax.numpy as jnp
from jax import lax
from jax.experimental import pallas as pl
from jax.experimental.pallas import tpu as pltpu

D_MODEL = 2048
BATCH = 4
SEQ = 2048
DEPTH = 2
DEC_BATCH = 128
DEC_SEQ = 4
PAST_LEN = 16384
GROUP_WIDTH = 512
POOL_WINDOWS = (2, 4, 8, 16)
POOL_MAX = 16
POOL_CH = 128
SGU_HEADS = 4
SGU_CHUNK = 128
CONF_KERNEL = 31
SC_KERNEL = 3
IN_COLS = 8 * GROUP_WIDTH
N_EXPERTS = 32
TOP_K = 4
D_EXPERT = 2048
SWIGLU_ALPHA = 1.702
SWIGLU_LIMIT = 7.0
EPS = 1e-5

T_PROMPT = BATCH * SEQ
T_SAMPLE = DEC_BATCH * DEC_SEQ
T_ALL = T_PROMPT + T_SAMPLE

LANES = 128
BF16_ROWS = 16
VMEM_LIMIT = 56 * 1024 * 1024

A_TM = 1024
A_TN = 512
A_STEPS = IN_COLS // A_TN
POOL_HALO = 16
CONF_HALO = 32
SC_HALO = 8
CONV_CHUNK = 256
CONV_SUB = 64

B_TM = 512
N_BLOCKS = T_ALL // B_TM
B_PROMPT_BLOCKS = T_PROMPT // B_TM
SEG = BF16_ROWS
SORT_TM = 512
BLOCK_ROWS = -(-(B_TM * TOP_K + N_EXPERTS * (SEG - 1)) // SORT_TM) * SORT_TM
SORT_CHUNKS = BLOCK_ROWS // SORT_TM
GATE_TERMS = 3
SIDE_ID0 = GATE_TERMS * TOP_K
H_EXT = D_MODEL + LANES

MOE_TM = 256
MOE_MT = 8
MOE_TN = 512
TILE_SEGS = MOE_TM // SEG
MOE_TILES = (T_ALL * TOP_K + N_BLOCKS * N_EXPERTS * (SEG - 1) + N_EXPERTS * (MOE_TM - 1)) // MOE_TM
MOE_ITEMS = (MOE_TILES + N_EXPERTS * (MOE_MT - 1)) // MOE_MT

_f32 = jnp.float32
_bf16 = jnp.bfloat16


def _rms(x):
    return x * lax.rsqrt(jnp.mean(x * x, axis=-1, keepdims=True) + EPS)


def _ln(x):
    xc = x - jnp.mean(x, axis=-1, keepdims=True)
    return xc * lax.rsqrt(jnp.mean(xc * xc, axis=-1, keepdims=True) + EPS)


def _gelu(x):
    return 0.5 * x * (1.0 + lax.erf(x * (0.5 ** 0.5)))


def _slab_mul(x, m):
    r, c = x.shape
    return (x.reshape(r // 128, 128, c) * m[None]).reshape(r, c)


def _slab_add(x, m):
    r, c = x.shape
    return (x.reshape(r // 128, 128, c) + m[None]).reshape(r, c)


ADA_TN = 1024
ADA_ROWS = DEC_BATCH + 8


def _ada_body(c_ref, w_ref, b_ref, os_ref, op_ref):
    c = c_ref[...]
    a = (c * jax.nn.sigmoid(c)).astype(_bf16)
    r = jnp.dot(a, w_ref[...].astype(_bf16), preferred_element_type=_f32) + b_ref[...]
    os_ref[...] = r[:DEC_BATCH]
    op_ref[...] = r[DEC_BATCH:]


def _ada(c_all, w_ada, b_ada):
    nj = 6 * D_MODEL // ADA_TN
    return pl.pallas_call(
        _ada_body,
        grid=(DEPTH, nj),
        in_specs=[
            pl.BlockSpec((ADA_ROWS, D_MODEL), lambda l, j: (0, 0)),
            pl.BlockSpec((None, D_MODEL, ADA_TN), lambda l, j: (l, 0, j)),
            pl.BlockSpec((None, 1, ADA_TN), lambda l, j: (l, 0, j)),
        ],
        out_specs=[
            pl.BlockSpec((None, DEC_BATCH, ADA_TN), lambda l, j: (l, 0, j)),
            pl.BlockSpec((None, 8, ADA_TN), lambda l, j: (l, 0, j)),
        ],
        out_shape=[
            jax.ShapeDtypeStruct((DEPTH, DEC_BATCH, 6 * D_MODEL), _f32),
            jax.ShapeDtypeStruct((DEPTH, 8, 6 * D_MODEL), _f32),
        ],
        compiler_params=pltpu.CompilerParams(
            dimension_semantics=("arbitrary", "arbitrary"), vmem_limit_bytes=VMEM_LIMIT),
        name="ada_mod",
    )(c_all, w_ada, b_ada.reshape(DEPTH, 1, 6 * D_MODEL))


def _a_prompt_body(x_ref, sh1_ref, sc1_ref, nmix_ref, w_ref, poolw_ref, pscale_ref, sgun_ref, sguw_ref,
                   sgub_ref, cw_ref, cb_ref, clg_ref, clb_ref, scw_ref, onorm_ref,
                   y_ref, pool_o, conf_o, sc_o,
                   h_ref, bufa, bufb, exta, extc, extz, shifted):
    i = pl.program_id(1)
    n = pl.program_id(2)
    tm = A_TM
    gw = GROUP_WIDTH

    @pl.when(n == 0)
    def _():
        hn = _rms(x_ref[...]) * nmix_ref[...]
        h_ref[...] = (hn * (1.0 + sc1_ref[...]) + sh1_ref[...]).astype(_bf16)

    def proj():
        return jnp.dot(h_ref[...], w_ref[...], preferred_element_type=_f32)

    def emit(g, y):
        yn = _rms(y) * onorm_ref[:, g * gw:(g + 1) * gw]
        y_ref[:, g * gw:(g + 1) * gw] = yn.astype(_bf16)

    def carry_halo(ext, halo):
        @pl.when(i == 0)
        def _():
            ext[0:halo, :] = jnp.zeros((halo, gw), _f32)

        @pl.when(i > 0)
        def _():
            ext[0:halo, :] = ext[tm:tm + halo, :]

    @pl.when(n == 0)
    def _():
        a = proj()
        carry_halo(exta, POOL_HALO)
        exta[POOL_HALO:, :] = a
        pool_o[...] = a[tm - POOL_HALO:, :]
        pos = (i * tm + lax.broadcasted_iota(jnp.int32, (tm, 1), 0)).astype(_f32)
        for gi, w in enumerate(POOL_WINDOWS):
            c0 = gi * POOL_CH
            s = exta[POOL_HALO:POOL_HALO + tm, c0:c0 + POOL_CH]
            for j in range(1, w):
                s = s + exta[POOL_HALO - j:POOL_HALO - j + tm, c0:c0 + POOL_CH]
            cnt = jnp.minimum(float(w), pos + 1.0)
            pooled = s / cnt - a[:, c0:c0 + POOL_CH]
            yg = jnp.dot(pooled.astype(_bf16), poolw_ref[gi].astype(_bf16), preferred_element_type=_f32)
            bufb[:, c0:c0 + POOL_CH] = yg
        emit(0, bufb[...] * pscale_ref[...])

    @pl.when(n == 1)
    def _():
        bufa[...] = _gelu(proj())

    @pl.when(n == 2)
    def _():
        v = _ln(_gelu(proj())) * sgun_ref[...]
        vb = v.astype(_bf16)
        row = lax.broadcasted_iota(jnp.int32, (SGU_CHUNK, SGU_CHUNK), 0)
        col = lax.broadcasted_iota(jnp.int32, (SGU_CHUNK, SGU_CHUNK), 1)
        for g in range(SGU_HEADS):
            c0 = g * SGU_CHUNK
            wg = jnp.where(row >= col, sguw_ref[g], 0.0).astype(_bf16)
            for c in range(tm // SGU_CHUNK):
                r0 = c * SGU_CHUNK
                m = jnp.dot(wg, vb[r0:r0 + SGU_CHUNK, c0:c0 + SGU_CHUNK], preferred_element_type=_f32)
                bufb[r0:r0 + SGU_CHUNK, c0:c0 + SGU_CHUNK] = m + sgub_ref[:, c0:c0 + SGU_CHUNK]
        emit(1, bufa[...] * bufb[...])

    @pl.when(n == 3)
    def _():
        bufa[...] = proj()

    @pl.when(n == 4)
    def _():
        g = bufa[...] * jax.nn.sigmoid(proj())
        carry_halo(extc, CONF_HALO)
        extc[CONF_HALO:, :] = g
        conf_o[...] = g[tm - CONF_HALO:, :]
        off = CONF_HALO - (CONF_KERNEL - 1)
        for ch in range(tm // CONV_CHUNK):
            base = ch * CONV_CHUNK
            for r in range(8):
                nr = CONV_CHUNK + 8 * ((CONF_HALO - r) // 8)
                shifted[r, 0:nr, :] = extc[base + r:base + r + nr, :]

            def sub(s, carry, base=base):
                i0 = s * CONV_SUB
                acc = jnp.broadcast_to(cb_ref[...], (CONV_SUB, gw))
                for k in range(CONF_KERNEL):
                    q, r = divmod(off + k, 8)
                    acc = acc + cw_ref[k:k + 1, :] * shifted[r, pl.ds(pl.multiple_of(i0 + 8 * q, 8), CONV_SUB), :]
                bufb[pl.ds(pl.multiple_of(base + i0, CONV_SUB), CONV_SUB), :] = acc
                return carry
            lax.fori_loop(0, CONV_CHUNK // CONV_SUB, sub, 0)
        yn = _ln(bufb[...]) * clg_ref[...] + clb_ref[...]
        emit(2, yn * jax.nn.sigmoid(yn))

    @pl.when(n == 5)
    def _():
        bufa[...] = proj()

    @pl.when(n == 6)
    def _():
        bufb[...] = proj()

    @pl.when(n == 7)
    def _():
        z = bufb[...] * proj()
        carry_halo(extz, SC_HALO)
        extz[SC_HALO:, :] = z
        sc_o[...] = z[tm - SC_HALO:, :]
        off = SC_HALO - (SC_KERNEL - 1)
        acc = scw_ref[0:1, :] * extz[off:off + tm, :]
        for k in range(1, SC_KERNEL):
            acc = acc + scw_ref[k:k + 1, :] * extz[off + k:off + k + tm, :]
        emit(3, bufa[...] * acc)


def _layer_spec(shape, l, nd):
    zeros = (0,) * len(shape)
    if nd == 1:
        return pl.BlockSpec((None,) + shape, lambda a: (l,) + zeros)
    if nd == 2:
        return pl.BlockSpec((None,) + shape, lambda a, b: (l,) + zeros)
    return pl.BlockSpec((None,) + shape, lambda a, b, c: (l,) + zeros)


def _mixer_params(l, p, nd):
    gw = GROUP_WIDTH
    arrs = [p["cw"], p["cb"], p["clg"], p["clb"], p["scw"], p["onorm"]]
    shapes = [(CONF_KERNEL, gw), (1, gw), (1, gw), (1, gw), (SC_KERNEL, gw), (1, D_MODEL)]
    return arrs, [_layer_spec(s, l, nd) for s in shapes]


def _a_prompt(l, x_all, mod_p4, w_in_b, p):
    gw = GROUP_WIDTH
    ni = SEQ // A_TM
    marrs, mspecs = _mixer_params(l, p, 3)
    in_specs = [
        pl.BlockSpec((A_TM, D_MODEL), lambda b, i, n: (b * ni + i, 0)),
        pl.BlockSpec((None, None, 1, D_MODEL), lambda b, i, n: (l, b, 0, 0)),
        pl.BlockSpec((None, None, 1, D_MODEL), lambda b, i, n: (l, b, 0, 1)),
        _layer_spec((1, D_MODEL), l, 3),
        pl.BlockSpec((None, D_MODEL, A_TN), lambda b, i, n: (l, 0, n)),
        _layer_spec((len(POOL_WINDOWS), POOL_CH, POOL_CH), l, 3),
        _layer_spec((1, gw), l, 3),
        _layer_spec((1, gw), l, 3),
        _layer_spec((SGU_HEADS, SGU_CHUNK, SGU_CHUNK), l, 3),
        _layer_spec((SGU_CHUNK, gw), l, 3),
    ] + mspecs
    out_specs = [
        pl.BlockSpec((A_TM, D_MODEL), lambda b, i, n: (b * ni + i, 0)),
        pl.BlockSpec((None, POOL_HALO, gw), lambda b, i, n: (b, 0, 0)),
        pl.BlockSpec((None, CONF_HALO, gw), lambda b, i, n: (b, 0, 0)),
        pl.BlockSpec((None, SC_HALO, gw), lambda b, i, n: (b, 0, 0)),
    ]
    out_shape = [
        jax.ShapeDtypeStruct((T_PROMPT, D_MODEL), _bf16),
        jax.ShapeDtypeStruct((BATCH, POOL_HALO, gw), _f32),
        jax.ShapeDtypeStruct((BATCH, CONF_HALO, gw), _f32),
        jax.ShapeDtypeStruct((BATCH, SC_HALO, gw), _f32),
    ]
    scratch = [
        pltpu.VMEM((A_TM, D_MODEL), _bf16),
        pltpu.VMEM((A_TM, gw), _f32),
        pltpu.VMEM((A_TM, gw), _f32),
        pltpu.VMEM((POOL_HALO + A_TM, gw), _f32),
        pltpu.VMEM((CONF_HALO + A_TM, gw), _f32),
        pltpu.VMEM((SC_HALO + A_TM, gw), _f32),
        pltpu.VMEM((8, CONV_CHUNK + CONF_HALO, gw), _f32),
    ]
    return pl.pallas_call(
        _a_prompt_body,
        grid=(BATCH, ni, A_STEPS),
        in_specs=in_specs,
        out_specs=out_specs,
        out_shape=out_shape,
        scratch_shapes=scratch,
        compiler_params=pltpu.CompilerParams(
            dimension_semantics=("arbitrary", "arbitrary", "arbitrary"), vmem_limit_bytes=VMEM_LIMIT),
        name="mixers_prompt",
    )(x_all, mod_p4, mod_p4, p["nmix"], w_in_b, p["poolw"], p["pscale"], p["sgun"], p["sguw"], p["sgub"], *marrs)


def _a_sample_body(x_ref, sh1_ref, sc1_ref, nmix_ref, w_ref, poolw_ref, pscale_ref, sgun_ref, sgwc_ref,
                   sgbc_ref, cw_ref, cb_ref, clg_ref, clb_ref, scw_ref, onorm_ref, pst_ref, cst_ref, sst_ref,
                   y_ref, a_o, g_o, z_o, v_o,
                   h_ref, bufa, bufb):
    n = pl.program_id(0)
    gw = GROUP_WIDTH
    nb = DEC_BATCH
    ns = DEC_SEQ

    @pl.when(n == 0)
    def _():
        hn = _rms(x_ref[...]) * nmix_ref[...]
        h_ref[...] = _slab_add(_slab_mul(hn, 1.0 + sc1_ref[...]), sh1_ref[...]).astype(_bf16)

    def proj():
        return jnp.dot(h_ref[...], w_ref[...], preferred_element_type=_f32)

    def emit(g, y):
        yn = _rms(y) * onorm_ref[:, g * gw:(g + 1) * gw]
        y_ref[:, g * gw:(g + 1) * gw] = yn.astype(_bf16)

    def slab(x, s):
        return x[s * nb:(s + 1) * nb, :]

    @pl.when(n == 0)
    def _():
        a = proj()
        a_o[...] = a.reshape(ns, nb, gw)
        npast = POOL_MAX - 1
        for s in range(ns):
            for gi, w in enumerate(POOL_WINDOWS):
                c0 = gi * POOL_CH
                acc = None
                for j in range(w):
                    e = npast + s - j
                    term = pst_ref[e, :, c0:c0 + POOL_CH] if e < npast else slab(a, e - npast)[:, c0:c0 + POOL_CH]
                    acc = term if acc is None else acc + term
                cnt = min(float(w), PAST_LEN + s + 1.0)
                pooled = acc / cnt - slab(a, s)[:, c0:c0 + POOL_CH]
                bufb[s * nb:(s + 1) * nb, c0:c0 + POOL_CH] = pooled
        pooled = bufb[...].astype(_bf16)
        for gi in range(len(POOL_WINDOWS)):
            c0 = gi * POOL_CH
            bufa[:, c0:c0 + POOL_CH] = jnp.dot(pooled[:, c0:c0 + POOL_CH], poolw_ref[gi].astype(_bf16),
                                               preferred_element_type=_f32)
        emit(0, bufa[...] * pscale_ref[...])

    @pl.when(n == 1)
    def _():
        bufa[...] = _gelu(proj())

    @pl.when(n == 2)
    def _():
        v = _ln(_gelu(proj())) * sgun_ref[...]
        v_o[...] = v.reshape(ns, nb, gw)
        for t in range(ns):
            m = jnp.broadcast_to(sgbc_ref[t:t + 1, :], (nb, gw))
            for s in range(t + 1):
                m = m + sgwc_ref[t * ns + s:t * ns + s + 1, :] * slab(v, s)
            bufb[t * nb:(t + 1) * nb, :] = m
        emit(1, bufa[...] * bufb[...])

    @pl.when(n == 3)
    def _():
        bufa[...] = proj()

    @pl.when(n == 4)
    def _():
        g = bufa[...] * jax.nn.sigmoid(proj())
        g_o[...] = g.reshape(ns, nb, gw)
        npast = CONF_KERNEL - 1
        for s in range(ns):
            acc = jnp.broadcast_to(cb_ref[...], (nb, gw))
            for k in range(CONF_KERNEL):
                e = s + k
                term = cst_ref[e] if e < npast else slab(g, e - npast)
                acc = acc + cw_ref[k:k + 1, :] * term
            bufb[s * nb:(s + 1) * nb, :] = acc
        yn = _ln(bufb[...]) * clg_ref[...] + clb_ref[...]
        emit(2, yn * jax.nn.sigmoid(yn))

    @pl.when(n == 5)
    def _():
        bufa[...] = proj()

    @pl.when(n == 6)
    def _():
        bufb[...] = proj()

    @pl.when(n == 7)
    def _():
        z = bufb[...] * proj()
        z_o[...] = z.reshape(ns, nb, gw)
        npast = SC_KERNEL - 1
        for s in range(ns):
            acc = None
            for k in range(SC_KERNEL):
                e = s + k
                term = scw_ref[k:k + 1, :] * (sst_ref[e] if e < npast else slab(z, e - npast))
                acc = term if acc is None else acc + term
            bufb[s * nb:(s + 1) * nb, :] = acc
        emit(3, bufa[...] * bufb[...])


def _a_sample(l, x_all, x_blk, mod_s, w_in_b, p, pool_t, conf_t, sc_t):
    gw = GROUP_WIDTH
    ns, nb = DEC_SEQ, DEC_BATCH
    marrs, mspecs = _mixer_params(l, p, 1)
    in_specs = [
        pl.BlockSpec((T_SAMPLE, D_MODEL), lambda n: (x_blk, 0)),
        pl.BlockSpec((None, nb, D_MODEL), lambda n: (l, 0, 0)),
        pl.BlockSpec((None, nb, D_MODEL), lambda n: (l, 0, 1)),
        _layer_spec((1, D_MODEL), l, 1),
        pl.BlockSpec((None, D_MODEL, A_TN), lambda n: (l, 0, n)),
        _layer_spec((len(POOL_WINDOWS), POOL_CH, POOL_CH), l, 1),
        _layer_spec((1, gw), l, 1),
        _layer_spec((1, gw), l, 1),
        _layer_spec((ns * ns, gw), l, 1),
        _layer_spec((ns, gw), l, 1),
    ] + mspecs + [
        _layer_spec((POOL_MAX - 1, nb, gw), l, 1),
        _layer_spec((CONF_KERNEL - 1, nb, gw), l, 1),
        _layer_spec((SC_KERNEL - 1, nb, gw), l, 1),
    ]
    out_specs = [pl.BlockSpec((T_SAMPLE, D_MODEL), lambda n: (0, 0))] + [
        pl.BlockSpec((ns, nb, gw), lambda n: (0, 0, 0)) for _ in range(4)]
    out_shape = [jax.ShapeDtypeStruct((T_SAMPLE, D_MODEL), _bf16)] + [
        jax.ShapeDtypeStruct((ns, nb, gw), _f32) for _ in range(4)]
    scratch = [
        pltpu.VMEM((T_SAMPLE, D_MODEL), _bf16),
        pltpu.VMEM((T_SAMPLE, gw), _f32),
        pltpu.VMEM((T_SAMPLE, gw), _f32),
    ]
    return pl.pallas_call(
        _a_sample_body,
        grid=(A_STEPS,),
        in_specs=in_specs,
        out_specs=out_specs,
        out_shape=out_shape,
        scratch_shapes=scratch,
        compiler_params=pltpu.CompilerParams(
            dimension_semantics=("arbitrary",), vmem_limit_bytes=VMEM_LIMIT),
        name="mixers_sample",
    )(x_all, mod_s, mod_s, p["nmix"], w_in_b, p["poolw"], p["pscale"], p["sgun"], p["sgwc"], p["sgbc"],
      *marrs, pool_t, conf_t, sc_t)


def _pick_mod(is_prompt, mp_ref, ms_ref):
    return jnp.where(is_prompt, jnp.broadcast_to(mp_ref[...], (DEC_BATCH, D_MODEL)), ms_ref[...])


def _b_body(xp_ref, xs_ref, yp_ref, ys_ref, w_ref, mpg1, mpsh2, mpsc2, msg1, mssh2, mssc2, nffn_ref, rw_ref, rb_ref,
            x1_ref, h2_ref, pos_ref, post_ref, meta_ref):
    i = pl.program_id(0)
    isp = i < B_PROMPT_BLOCKS
    ycat = jnp.where(isp, yp_ref[...], ys_ref[...])
    o = jnp.dot(ycat, w_ref[...], preferred_element_type=_f32)
    x1 = jnp.where(isp, xp_ref[...], xs_ref[...]) + _slab_mul(o, _pick_mod(isp, mpg1, msg1))
    x1_ref[...] = x1
    hn = _rms(x1) * nffn_ref[...]
    h2 = _slab_add(_slab_mul(hn, 1.0 + _pick_mod(isp, mpsc2, mssc2)), _pick_mod(isp, mpsh2, mssh2))
    hi = h2.astype(_bf16)
    h2_ref[:, 0:D_MODEL] = hi
    lo = (h2 - hi.astype(_f32)).astype(_bf16)
    l1 = jnp.dot(hi, rw_ref[...], preferred_element_type=_f32)
    l2 = jnp.dot(lo, rw_ref[:, 0:LANES], preferred_element_type=_f32)
    logits = l1[:, 0:LANES] + l1[:, LANES:] + l2 + rb_ref[...]
    lane = lax.broadcasted_iota(jnp.int32, (B_TM, LANES), 1).astype(_f32)
    neg = jnp.float32(-jnp.inf)
    cur = jnp.where(lane < N_EXPERTS, logits, neg)
    vals, ids = [], []
    for _ in range(TOP_K):
        m = jnp.max(cur, axis=-1, keepdims=True)
        ix = jnp.min(jnp.where(cur == m, lane, float(LANES)), axis=-1, keepdims=True)
        vals.append(m)
        ids.append(ix)
        cur = jnp.where(lane == ix, neg, cur)
    es = [jnp.exp(v - vals[0]) for v in vals]
    tot = es[0] + es[1] + es[2] + es[3]
    side = jnp.zeros((B_TM, LANES), _f32)
    for k in range(TOP_K):
        g = es[k] / tot
        g0 = g.astype(_bf16).astype(_f32)
        g1 = (g - g0).astype(_bf16).astype(_f32)
        for j, term in enumerate((g0, g1, g - g0 - g1)):
            side = jnp.where(lane == float(GATE_TERMS * k + j), term, side)
        side = jnp.where(lane == float(SIDE_ID0 + k), ids[k], side)
    h2_ref[:, D_MODEL:] = side.astype(_bf16)

    hit = [lane == ix for ix in ids]
    chosen = jnp.zeros((B_TM, LANES), _f32)
    for h in hit:
        chosen = jnp.where(h, 1.0, chosen)
    cnt = jnp.sum(chosen, axis=0, keepdims=True)
    seg_rows = jnp.floor((cnt + (SEG - 1.0)) * (1.0 / SEG)) * SEG
    er = lax.broadcasted_iota(jnp.int32, (LANES, LANES), 0)
    ec = lax.broadcasted_iota(jnp.int32, (LANES, LANES), 1)
    before = jnp.where(er < ec, 1.0, 0.0).astype(_bf16)
    seg_start = jnp.dot(jnp.broadcast_to(seg_rows, (8, LANES)).astype(_bf16), before,
                        preferred_element_type=_f32)[0:1, :]
    tr = lax.broadcasted_iota(jnp.int32, (B_TM, B_TM), 0)
    tc = lax.broadcasted_iota(jnp.int32, (B_TM, B_TM), 1)
    earlier = jnp.where(tr > tc, 1.0, 0.0).astype(_bf16)
    rank = jnp.dot(earlier, chosen.astype(_bf16), preferred_element_type=_f32)
    base = seg_start + rank
    pos = jnp.zeros((B_TM, LANES), _f32)
    for k in range(TOP_K):
        pk = jnp.sum(jnp.where(hit[k], base, 0.0), axis=-1, keepdims=True)
        pos = jnp.where(lane == float(k), pk, pos)
    pos_ref[...] = pos
    post_ref[...] = jnp.transpose(pos)[0:8, :]
    sub = lax.broadcasted_iota(jnp.int32, (8, LANES), 0)
    meta = jnp.where(sub == 0, jnp.broadcast_to(seg_start, (8, LANES)),
                     jnp.where(sub == 1, jnp.broadcast_to(seg_rows, (8, LANES)), 0.0))
    meta_ref[...] = meta.astype(jnp.int32)


def _stage_b(l, x_p, x_s, xs_blk, ycat_p, ycat_s, w_out_b, mod_p4, mod_s, p):
    def pb(i):
        return jnp.minimum(i // (SEQ // B_TM), BATCH - 1)

    def mp_spec(j):
        return pl.BlockSpec((None, None, 1, D_MODEL), lambda i: (l, pb(i), 0, j))

    def ms_spec(j):
        return pl.BlockSpec((None, DEC_BATCH, D_MODEL), lambda i: (l, 0, j))

    row_spec = pl.BlockSpec((B_TM, D_MODEL), lambda i: (i, 0))
    in_specs = [
        pl.BlockSpec((B_TM, D_MODEL), lambda i: (jnp.minimum(i, B_PROMPT_BLOCKS - 1), 0)),
        pl.BlockSpec((T_SAMPLE, D_MODEL), lambda i: (xs_blk, 0), pipeline_mode=pl.Buffered(1)),
        pl.BlockSpec((B_TM, D_MODEL), lambda i: (jnp.minimum(i, B_PROMPT_BLOCKS - 1), 0)),
        pl.BlockSpec((T_SAMPLE, D_MODEL), lambda i: (0, 0), pipeline_mode=pl.Buffered(1)),
        pl.BlockSpec((None, D_MODEL, D_MODEL), lambda i: (l, 0, 0), pipeline_mode=pl.Buffered(1)),
        mp_spec(2), mp_spec(3), mp_spec(4), ms_spec(2), ms_spec(3), ms_spec(4),
        _layer_spec((1, D_MODEL), l, 1),
        _layer_spec((D_MODEL, 2 * LANES), l, 1),
        _layer_spec((1, LANES), l, 1),
    ]
    lane_spec = pl.BlockSpec((B_TM, LANES), lambda i: (i, 0))
    return pl.pallas_call(
        _b_body,
        grid=(N_BLOCKS,),
        in_specs=in_specs,
        out_specs=[row_spec, pl.BlockSpec((B_TM, H_EXT), lambda i: (i, 0)), lane_spec,
                   pl.BlockSpec((None, 8, B_TM), lambda i: (i, 0, 0)),
                   pl.BlockSpec((None, 8, LANES), lambda i: (i, 0, 0))],
        out_shape=[
            jax.ShapeDtypeStruct((T_ALL, D_MODEL), _f32),
            jax.ShapeDtypeStruct((T_ALL, H_EXT), _bf16),
            jax.ShapeDtypeStruct((T_ALL, LANES), _f32),
            jax.ShapeDtypeStruct((N_BLOCKS, 8, B_TM), _f32),
            jax.ShapeDtypeStruct((N_BLOCKS, 8, LANES), jnp.int32),
        ],
        compiler_params=pltpu.CompilerParams(
            dimension_semantics=("arbitrary",), vmem_limit_bytes=VMEM_LIMIT),
        name="outproj_router",
    )(x_p, x_s, ycat_p, ycat_s, w_out_b, mod_p4, mod_p4, mod_p4, mod_s, mod_s, mod_s, p["nffn"], p["rw"], p["rb"])


def _sort_body(post_ref, h_ref, o_ref, so_ref):
    for c in range(SORT_CHUNKS):
        j = (c * SORT_TM + lax.broadcasted_iota(jnp.int32, (SORT_TM, B_TM), 0)).astype(_f32)
        sel = jnp.zeros((SORT_TM, B_TM), _f32)
        for k in range(TOP_K):
            sel = jnp.where(j == post_ref[k:k + 1, :], 1.0, sel)
        res = jnp.dot(sel.astype(_bf16), h_ref[...], preferred_element_type=_f32)
        o_ref[c * SORT_TM:(c + 1) * SORT_TM, :] = res[:, 0:D_MODEL].astype(_bf16)
        so_ref[c * SORT_TM:(c + 1) * SORT_TM, :] = res[:, D_MODEL:]


def _local_sort(post, h_ext):
    return pl.pallas_call(
        _sort_body,
        grid=(N_BLOCKS,),
        in_specs=[
            pl.BlockSpec((None, 8, B_TM), lambda b: (b, 0, 0)),
            pl.BlockSpec((B_TM, H_EXT), lambda b: (b, 0)),
        ],
        out_specs=[pl.BlockSpec((BLOCK_ROWS, D_MODEL), lambda b: (b, 0)),
                   pl.BlockSpec((BLOCK_ROWS, LANES), lambda b: (b, 0))],
        out_shape=[jax.ShapeDtypeStruct((N_BLOCKS * BLOCK_ROWS, D_MODEL), _bf16),
                   jax.ShapeDtypeStruct((N_BLOCKS * BLOCK_ROWS, LANES), _f32)],
        compiler_params=pltpu.CompilerParams(
            dimension_semantics=("arbitrary",), vmem_limit_bytes=VMEM_LIMIT),
        name="local_sort",
    )(post, h_ext)


BIG_TM = 2 * MOE_TM
SMALL_TM = MOE_TM // 2
MOE_DN = 1024
STREAM_UP_STEPS = D_EXPERT // MOE_TN
STREAM_DOWN_STEPS = D_MODEL // MOE_DN
STREAM_STEPS = STREAM_UP_STEPS + STREAM_DOWN_STEPS
MOE_WSLOTS = 2
CAST_ROWS = D_MODEL // MOE_MT


def _moe_stream_body(ie_ref, ilo_ref, int_ref, start_ref, rows_ref, used_ref, nitems_ref,
                     wup_hbm, wdn_hbm, bup_hbm, bdn_hbm, x_hbm, g_hbm, y_hbm,
                     wbuf, wc, xbuf, hbuf, gbuf, bup, bdn, seg_src, nseg, pending,
                     sem_w, sem_b, sem_x, sem_y, *, l):
    tm, tn = MOE_TM, MOE_TN
    n_items = nitems_ref[0]
    n_steps = n_items * STREAM_STEPS

    def w_half(src_ref, e, col0, slot, half):
        return pltpu.make_async_copy(src_ref.at[l, e, :, pl.ds(col0, tn)],
                                     wbuf.at[slot, :, pl.ds(half * tn, tn)], sem_w.at[slot])

    def issue_w(s):
        e = ie_ref[s // STREAM_STEPS]
        n = s % STREAM_STEPS
        slot = s % MOE_WSLOTS
        for c in range(STREAM_UP_STEPS):
            @pl.when(n == c)
            def _(c=c):
                w_half(wup_hbm, e, c * tn, slot, 0).start()
                w_half(wup_hbm, e, D_EXPERT + c * tn, slot, 1).start()
        for c in range(STREAM_DOWN_STEPS):
            @pl.when(n == STREAM_UP_STEPS + c)
            def _(c=c):
                w_half(wdn_hbm, e, c * MOE_DN, slot, 0).start()
                w_half(wdn_hbm, e, c * MOE_DN + tn, slot, 1).start()

    def wait_w(s):
        slot = s % MOE_WSLOTS
        for half in range(2):
            w_half(wdn_hbm, 0, 0, slot, half).wait()

    def b_copies(item):
        e = ie_ref[item]
        par = item % 2
        return (pltpu.make_async_copy(bup_hbm.at[l, e], bup.at[par], sem_b.at[par]),
                pltpu.make_async_copy(bdn_hbm.at[l, e], bdn.at[par], sem_b.at[par]))

    def x_copy(p, src, dst):
        return pltpu.make_async_copy(x_hbm.at[pl.ds(src, SEG)], xbuf.at[p, pl.ds(dst, SEG)],
                                     sem_x.at[p * MOE_MT + dst // tm])

    def g_copy(p, src, dst):
        return pltpu.make_async_copy(g_hbm.at[pl.ds(src, SEG)], gbuf.at[p, pl.ds(dst, SEG)],
                                     sem_x.at[p * MOE_MT + dst // tm])

    def y_copy(p, src, dst):
        return pltpu.make_async_copy(xbuf.at[p, pl.ds(src, SEG)], y_hbm.at[pl.ds(dst, SEG)], sem_y.at[p])

    def drain(p):
        def w(_, carry):
            y_copy(p, 0, 0).wait()
            return carry
        lax.fori_loop(0, pending[p], w, 0)
        pending[p] = 0

    def tile_pieces(p, t):
        return jnp.clip(nseg[p] - t * TILE_SEGS, 0, TILE_SEGS)

    def plan(rows):
        n_big = rows // BIG_TM
        rem = rows - n_big * BIG_TM
        round_up = rem > BIG_TM - SMALL_TM
        n_big = n_big + jnp.where(round_up, 1, 0)
        rem = jnp.where(round_up, 0, rem)
        has_mid = rem > SMALL_TM
        has_small = (rem > 0) & ((rem <= SMALL_TM) | (rem > tm))
        off_mid = n_big * BIG_TM
        off_small = off_mid + jnp.where(has_mid, tm, 0)
        return n_big, has_mid, has_small, off_mid, off_small

    def load_item(item):
        p = item % 2
        e = ie_ref[item]
        lo = ilo_ref[item]
        nt = int_ref[item]
        hi = lo + nt * tm

        def per_block(b, q0):
            rows = rows_ref[b * N_EXPERTS + e]
            start = start_ref[b * N_EXPERTS + e]

            def per_piece(g, cc):
                q = q0 + g * SEG

                @pl.when((q >= lo) & (q < hi))
                def _():
                    src = pl.multiple_of(b * BLOCK_ROWS + start + g * SEG, SEG)
                    dst = pl.multiple_of(q - lo, SEG)
                    seg_src[p * (MOE_MT * TILE_SEGS) + dst // SEG] = src
                    x_copy(p, src, dst).start()
                    g_copy(p, src, dst).start()
                return cc
            lax.fori_loop(0, rows // SEG, per_piece, 0)
            return q0 + rows
        end = lax.fori_loop(0, N_BLOCKS, per_block, 0)
        ns = (jnp.minimum(end, hi) - lo) // SEG
        nseg[p] = ns

        def zero_piece(g, cc):
            r = pl.multiple_of(g * SEG, SEG)
            xbuf[p, pl.ds(r, SEG), :] = jnp.zeros((SEG, D_MODEL), _bf16)
            gbuf[p, pl.ds(r, SEG), :] = jnp.zeros((SEG, LANES), _f32)
            return cc
        n_big, has_mid, has_small, _, off_small = plan(ns * SEG)
        computed = off_small + jnp.where(has_small, SMALL_TM, 0)
        lax.fori_loop(ns, computed // SEG, zero_piece, 0)

    pending[0] = 0
    pending[1] = 0
    issue_w(0)
    issue_w(1)
    for cp in b_copies(0):
        cp.start()
    load_item(0)
    wait_w(0)
    wc[0] = wbuf[0].astype(_bf16)

    def step(s, carry):
        item = s // STREAM_STEPS
        n = s % STREAM_STEPS
        e = ie_ref[item]
        par = item % 2
        cur = s % 2
        nxt = (s + 1) % 2
        nslot = (s + 1) % MOE_WSLOTS

        @pl.when(s + 2 < n_steps)
        def _():
            issue_w(s + 2)

        @pl.when(s + 1 < n_steps)
        def _():
            wait_w(s + 1)

        def cast_slice(t):
            r = pl.multiple_of(t * CAST_ROWS, CAST_ROWS)
            wc[nxt, pl.ds(r, CAST_ROWS), :] = wbuf[nslot, pl.ds(r, CAST_ROWS), :].astype(_bf16)

        @pl.when(n == 0)
        def _():
            for cp in b_copies(item):
                cp.wait()

        @pl.when((n == 1) & (item + 1 < n_items))
        def _():
            for cp in b_copies(item + 1):
                cp.start()
            drain(1 - par)
            load_item(item + 1)

        n_big, has_mid, has_small, off_mid, off_small = plan(nseg[par] * SEG)
        mid_i = jnp.where(has_mid, 1, 0)
        mid_slice = 2 * n_big
        small_slice = mid_slice + mid_i

        def up_rows(c, r0, nrows, slices):
            if c == 0:
                for u in range(max(nrows // tm, 1)):
                    sub = r0 // tm + u

                    def w(_, c3, sub=sub):
                        x_copy(par, 0, pl.multiple_of(sub * tm, tm)).wait()
                        g_copy(par, 0, pl.multiple_of(sub * tm, tm)).wait()
                        return c3
                    lax.fori_loop(0, tile_pieces(par, sub), w, 0)
            for sl in slices:
                cast_slice(sl)
            up = jnp.dot(xbuf[par, pl.ds(r0, nrows), :], wc[cur], preferred_element_type=_f32)
            bg = bup[par, :, c * tn:(c + 1) * tn]
            bl = bup[par, :, D_EXPERT + c * tn:D_EXPERT + (c + 1) * tn]
            glu = jnp.minimum(up[:, 0:tn] + bg, SWIGLU_LIMIT)
            lin = jnp.clip(up[:, tn:] + bl, -SWIGLU_LIMIT, SWIGLU_LIMIT)
            act = glu * jax.nn.sigmoid(SWIGLU_ALPHA * glu) * (lin + 1.0)
            hbuf[pl.ds(r0, nrows), c * tn:(c + 1) * tn] = act.astype(_bf16)

        def down_rows(c, r0, nrows, slices):
            for sl in slices:
                cast_slice(sl)
            y = (jnp.dot(hbuf[pl.ds(r0, nrows), :], wc[cur], preferred_element_type=_f32)
                 + bdn[par, :, c * MOE_DN:(c + 1) * MOE_DN])
            g = gbuf[par, pl.ds(r0, nrows), :]
            ef = e.astype(_f32)
            gate = jnp.zeros((nrows, 1), _f32)
            for k in range(TOP_K):
                gk = g[:, GATE_TERMS * k:GATE_TERMS * k + 1]
                for j in range(1, GATE_TERMS):
                    gk = gk + g[:, GATE_TERMS * k + j:GATE_TERMS * k + j + 1]
                gate = jnp.where(g[:, SIDE_ID0 + k:SIDE_ID0 + k + 1] == ef, gk, gate)
            xbuf[par, pl.ds(r0, nrows), c * MOE_DN:(c + 1) * MOE_DN] = (y * gate).astype(_bf16)
            if c == STREAM_DOWN_STEPS - 1:
                for q in range(nrows // SEG):
                    piece = r0 // SEG + q

                    @pl.when(piece < nseg[par])
                    def _(q=q, piece=piece):
                        dst = seg_src[par * (MOE_MT * TILE_SEGS) + piece]
                        y_copy(par, pl.multiple_of(r0 + q * SEG, SEG), pl.multiple_of(dst, SEG)).start()

        def all_rows(fn, c):
            def big(t, cc):
                fn(c, pl.multiple_of(t * BIG_TM, BIG_TM), BIG_TM, (2 * t, 2 * t + 1))
                return cc
            lax.fori_loop(0, n_big, big, 0)

            @pl.when(has_mid)
            def _():
                fn(c, pl.multiple_of(off_mid, tm), tm, (mid_slice,))

            @pl.when(has_small)
            def _():
                fn(c, pl.multiple_of(off_small, SMALL_TM), SMALL_TM, (small_slice,))

        for c in range(STREAM_UP_STEPS):
            @pl.when(n == c)
            def _(c=c):
                all_rows(up_rows, c)

        for c in range(STREAM_DOWN_STEPS):
            @pl.when(n == STREAM_UP_STEPS + c)
            def _(c=c):
                all_rows(down_rows, c)
                if c == STREAM_DOWN_STEPS - 1:
                    pending[par] = nseg[par]

        def rest(t, cc):
            cast_slice(t)
            return cc
        lax.fori_loop(small_slice + jnp.where(has_small, 1, 0), MOE_MT, rest, 0)
        return carry
    lax.fori_loop(0, n_steps, step, 0)

    drain(0)
    drain(1)
    hbuf[0:SEG, :] = jnp.zeros((SEG, D_EXPERT), _bf16)

    def zero_copy(row):
        return pltpu.make_async_copy(hbuf.at[pl.ds(0, SEG)], y_hbm.at[pl.ds(row, SEG)], sem_y.at[0])

    def zero_block(b, carry):
        def zstart(g, cc):
            zero_copy(pl.multiple_of(b * BLOCK_ROWS + g * SEG, SEG)).start()
            return cc

        def zwait(g, cc):
            zero_copy(0).wait()
            return cc
        first = used_ref[b] // SEG
        lax.fori_loop(first, BLOCK_ROWS // SEG, zstart, 0)
        lax.fori_loop(first, BLOCK_ROWS // SEG, zwait, 0)
        return carry
    lax.fori_loop(0, N_BLOCKS, zero_block, 0)


def _moe_stream(l, item_e, item_lo, item_nt, seg_start, seg_rows, used, n_items, w_up, b_up4, w_down, b_down4,
                x_sorted, g_sorted):
    any_spec = pl.BlockSpec(memory_space=pl.ANY)
    grid_spec = pltpu.PrefetchScalarGridSpec(
        num_scalar_prefetch=7,
        grid=(1,),
        in_specs=[any_spec] * 6,
        out_specs=any_spec,
        scratch_shapes=[
            pltpu.VMEM((MOE_WSLOTS, D_MODEL, 2 * MOE_TN), _f32),
            pltpu.VMEM((2, D_MODEL, 2 * MOE_TN), _bf16),
            pltpu.VMEM((2, MOE_MT * MOE_TM, D_MODEL), _bf16),
            pltpu.VMEM((MOE_MT * MOE_TM, D_EXPERT), _bf16),
            pltpu.VMEM((2, MOE_MT * MOE_TM, LANES), _f32),
            pltpu.VMEM((2, 1, 2 * D_EXPERT), _f32),
            pltpu.VMEM((2, 1, D_MODEL), _f32),
            pltpu.SMEM((2 * MOE_MT * TILE_SEGS,), jnp.int32),
            pltpu.SMEM((2,), jnp.int32),
            pltpu.SMEM((2,), jnp.int32),
            pltpu.SemaphoreType.DMA((MOE_WSLOTS,)),
            pltpu.SemaphoreType.DMA((2,)),
            pltpu.SemaphoreType.DMA((2 * MOE_MT,)),
            pltpu.SemaphoreType.DMA((2,)),
        ],
    )
    return pl.pallas_call(
        functools.partial(_moe_stream_body, l=l),
        grid_spec=grid_spec,
        out_shape=jax.ShapeDtypeStruct((N_BLOCKS * BLOCK_ROWS, D_MODEL), _bf16),
        compiler_params=pltpu.CompilerParams(
            dimension_semantics=("arbitrary",), vmem_limit_bytes=60 * 1024 * 1024),
        name="moe_experts",
    )(item_e, item_lo, item_nt, seg_start, seg_rows, used, n_items, w_up, w_down, b_up4, b_down4, x_sorted,
      g_sorted)


def _work_items(meta):
    seg_start = meta[:, 0, :N_EXPERTS]
    seg_rows = meta[:, 1, :N_EXPERTS]
    rows_e = jnp.sum(seg_rows, axis=0)
    nt = (rows_e + MOE_TM - 1) // MOE_TM
    ipe = (nt + MOE_MT - 1) // MOE_MT
    cum = jnp.cumsum(ipe)
    first = cum - ipe
    total = cum[-1]
    ii = jnp.arange(MOE_ITEMS, dtype=jnp.int32)
    ii_c = jnp.minimum(ii, total - 1)
    e_i = jnp.minimum(jnp.sum((cum[None, :] <= ii_c[:, None]).astype(jnp.int32), axis=1), N_EXPERTS - 1)
    valid = ii < total
    j = ii - first[e_i]
    lo_i = jnp.where(valid, j * (MOE_MT * MOE_TM), 0)
    nt_i = jnp.where(valid, jnp.minimum(MOE_MT, nt[e_i] - j * MOE_MT), 0)
    used = jnp.sum(seg_rows, axis=1)
    i32 = jnp.int32
    return (e_i.astype(i32), lo_i.astype(i32), nt_i.astype(i32), seg_start.reshape(-1).astype(i32),
            seg_rows.reshape(-1).astype(i32), used.astype(i32), total.astype(i32).reshape(1))


def _c_body(x1_ref, y_ref, pos_ref, mpg2, msg2, nfin_ref, *rest, final):
    if final:
        outp_ref, outs_ref, sel_ref = rest
    else:
        out_ref, sel_ref = rest
    i = pl.program_id(0)
    isp = i < B_PROMPT_BLOCKS
    for c in range(SORT_CHUNKS):
        j = (c * SORT_TM + lax.broadcasted_iota(jnp.int32, (B_TM, SORT_TM), 1)).astype(_f32)
        sel = jnp.zeros((B_TM, SORT_TM), _f32)
        for k in range(TOP_K):
            sel = jnp.where(j == pos_ref[:, k:k + 1], 1.0, sel)
        sel_ref[:, c * SORT_TM:(c + 1) * SORT_TM] = sel.astype(_bf16)
    acc = jnp.dot(sel_ref[...], y_ref[...], preferred_element_type=_f32)
    x2 = x1_ref[...] + _slab_mul(acc, _pick_mod(isp, mpg2, msg2))
    if final:
        y = _rms(x2) * nfin_ref[...]

        @pl.when(isp)
        def _():
            outp_ref[...] = y

        @pl.when(jnp.logical_not(isp))
        def _():
            outs_ref[...] = y
    else:
        out_ref[...] = x2


def _combine(l, x1, y_sorted, pos, mod_p4, mod_s, nfin, final):
    def pb(i):
        return jnp.minimum(i // (SEQ // B_TM), BATCH - 1)

    row_spec = pl.BlockSpec((B_TM, D_MODEL), lambda i: (i, 0))
    lane_spec = pl.BlockSpec((B_TM, LANES), lambda i: (i, 0))
    if final:
        out_specs = [pl.BlockSpec((B_TM, D_MODEL), lambda i: (jnp.minimum(i, B_PROMPT_BLOCKS - 1), 0)),
                     pl.BlockSpec((T_SAMPLE, D_MODEL), lambda i: (0, 0))]
        out_shape = [jax.ShapeDtypeStruct((T_PROMPT, D_MODEL), _f32),
                     jax.ShapeDtypeStruct((T_SAMPLE, D_MODEL), _f32)]
    else:
        out_specs = row_spec
        out_shape = jax.ShapeDtypeStruct((T_ALL, D_MODEL), _f32)
    return pl.pallas_call(
        functools.partial(_c_body, final=final),
        grid=(N_BLOCKS,),
        in_specs=[
            row_spec,
            pl.BlockSpec((BLOCK_ROWS, D_MODEL), lambda i: (i, 0)),
            lane_spec,
            pl.BlockSpec((None, None, 1, D_MODEL), lambda i: (l, pb(i), 0, 5)),
            pl.BlockSpec((None, DEC_BATCH, D_MODEL), lambda i: (l, 0, 5)),
            pl.BlockSpec((1, D_MODEL), lambda i: (0, 0)),
        ],
        out_specs=out_specs,
        out_shape=out_shape,
        scratch_shapes=[pltpu.VMEM((B_TM, BLOCK_ROWS), _bf16)],
        compiler_params=pltpu.CompilerParams(
            dimension_semantics=("arbitrary",), vmem_limit_bytes=VMEM_LIMIT),
        name="moe_combine",
    )(x1, y_sorted, pos, mod_p4, mod_s, nfin)


def kernel(x_prompt, x_sample, c_prompt, c_sample, state_pool, state_conformer, state_shortconv, w_ada, b_ada, norm_mix, norm_ffn, norm_final, w_in, pool_w, pool_scale, sgu_norm, sgu_w, sgu_b, conf_w, conf_b, conf_ln_g, conf_ln_b, sc_w, out_norm, w_out, router_w, router_b, w_up, b_up, w_down, b_down):
    gw = GROUP_WIDTH
    ns, nb = DEC_SEQ, DEC_BATCH

    c_all = jnp.concatenate([c_sample, c_prompt, jnp.zeros((ADA_ROWS - nb - BATCH, D_MODEL), _f32)], axis=0)
    mod_s, mod_p8 = _ada(c_all, w_ada, b_ada)
    mod_p4 = mod_p8[:, :BATCH].reshape(DEPTH, BATCH, 1, 6 * D_MODEL)

    rw_hi = router_w.astype(_bf16)
    rw_lo = (router_w - rw_hi.astype(_f32)).astype(_bf16)
    lane_pad = ((0, 0), (0, 0), (0, LANES - N_EXPERTS))
    params = {
        "nmix": norm_mix.reshape(DEPTH, 1, D_MODEL),
        "nffn": norm_ffn.reshape(DEPTH, 1, D_MODEL),
        "poolw": pool_w,
        "pscale": pool_scale.reshape(DEPTH, 1, gw),
        "sgun": sgu_norm.reshape(DEPTH, 1, gw),
        "sguw": sgu_w,
        "sgub": jnp.repeat(jnp.swapaxes(sgu_b, 1, 2), SGU_CHUNK, axis=2),
        "sgwc": jnp.repeat(jnp.transpose(sgu_w[:, :, :ns, :ns], (0, 2, 3, 1)).reshape(DEPTH, ns * ns, SGU_HEADS),
                           SGU_CHUNK, axis=2),
        "sgbc": jnp.repeat(jnp.swapaxes(sgu_b[:, :, :ns], 1, 2), SGU_CHUNK, axis=2),
        "cw": conf_w,
        "cb": conf_b.reshape(DEPTH, 1, gw),
        "clg": conf_ln_g.reshape(DEPTH, 1, gw),
        "clb": conf_ln_b.reshape(DEPTH, 1, gw),
        "scw": sc_w,
        "onorm": out_norm.reshape(DEPTH, 1, D_MODEL),
        "rw": jnp.concatenate([jnp.pad(rw_hi, lane_pad), jnp.pad(rw_lo, lane_pad)], axis=2),
        "rb": jnp.pad(router_b, ((0, 0), (0, LANES - N_EXPERTS))).reshape(DEPTH, 1, LANES),
    }
    w_in_b = w_in.astype(_bf16)
    w_out_b = w_out.astype(_bf16)
    b_up4 = b_up.reshape(DEPTH, N_EXPERTS, 1, 2 * D_EXPERT)
    b_down4 = b_down.reshape(DEPTH, N_EXPERTS, 1, D_MODEL)
    nfin = norm_final.reshape(1, D_MODEL)

    pool_t = jnp.transpose(state_pool, (0, 2, 1, 3))
    conf_t = jnp.transpose(state_conformer, (0, 2, 1, 3))
    sc_t = jnp.transpose(state_shortconv, (0, 2, 1, 3))

    x_p = x_prompt.reshape(T_PROMPT, D_MODEL)
    x_s = jnp.transpose(x_sample, (1, 0, 2)).reshape(T_SAMPLE, D_MODEL)
    xs_blk = 0

    pool_p, conf_p, sc_p, pool_s, conf_s, sc_s, v_s = [], [], [], [], [], [], []
    for l in range(DEPTH):
        ycat_p, po, co, so = _a_prompt(l, x_p, mod_p4, w_in_b, params)
        ycat_s, a_new, g_new, z_new, v_new = _a_sample(l, x_s, xs_blk, mod_s, w_in_b, params, pool_t, conf_t, sc_t)
        pool_p.append(po[:, POOL_HALO - (POOL_MAX - 1):])
        conf_p.append(co[:, CONF_HALO - (CONF_KERNEL - 1):])
        sc_p.append(so[:, SC_HALO - (SC_KERNEL - 1):])
        pool_s.append(jnp.concatenate([state_pool[l][:, ns:], jnp.transpose(a_new, (1, 0, 2))], axis=1))
        conf_s.append(jnp.concatenate([state_conformer[l][:, ns:], jnp.transpose(g_new, (1, 0, 2))], axis=1))
        sc_s.append(jnp.transpose(z_new, (1, 0, 2))[:, ns - (SC_KERNEL - 1):])
        v_s.append(jnp.transpose(v_new, (1, 0, 2)))

        x1, h_ext, pos, post, meta = _stage_b(l, x_p, x_s, xs_blk, ycat_p, ycat_s, w_out_b, mod_p4, mod_s, params)
        x_sorted, g_sorted = _local_sort(post, h_ext)
        item_e, item_lo, item_nt, seg_start, seg_rows, used, n_items = _work_items(meta)
        y_sorted = _moe_stream(l, item_e, item_lo, item_nt, seg_start, seg_rows, used, n_items, w_up, b_up4,
                               w_down, b_down4, x_sorted, g_sorted)
        if l < DEPTH - 1:
            x_p = x_s = _combine(l, x1, y_sorted, pos, mod_p4, mod_s, nfin, final=False)
            xs_blk = T_PROMPT // T_SAMPLE
        else:
            y_p, y_s = _combine(l, x1, y_sorted, pos, mod_p4, mod_s, nfin, final=True)

    y_prompt = y_p.reshape(BATCH, SEQ, D_MODEL)
    y_sample = jnp.transpose(y_s.reshape(ns, nb, D_MODEL), (1, 0, 2))
    return (y_prompt, y_sample, jnp.stack(pool_p), jnp.stack(pool_s), jnp.stack(conf_p), jnp.stack(conf_s),
            jnp.stack(sc_p), jnp.stack(sc_s), jnp.stack(v_s))
```

```python
import functools

import jax
import jax.numpy as jnp
from jax import lax
from jax.experimental import pallas as pl
from jax.experimental.pallas import tpu as pltpu

D_MODEL = 2048
BATCH = 4
SEQ = 2048
DEPTH = 2
DEC_BATCH = 128
DEC_SEQ = 4
PAST_LEN = 16384
GROUP_WIDTH = 512
POOL_WINDOWS = (2, 4, 8, 16)
POOL_MAX = 16
POOL_CH = 128
SGU_HEADS = 4
SGU_CHUNK = 128
CONF_KERNEL = 31
SC_KERNEL = 3
IN_COLS = 8 * GROUP_WIDTH
N_EXPERTS = 32
TOP_K = 4
D_EXPERT = 2048
SWIGLU_ALPHA = 1.702
SWIGLU_LIMIT = 7.0
EPS = 1e-5

T_PROMPT = BATCH * SEQ
T_SAMPLE = DEC_BATCH * DEC_SEQ
T_ALL = T_PROMPT + T_SAMPLE

LANES = 128
BF16_ROWS = 16
VMEM_LIMIT = 56 * 1024 * 1024

A_TM = 1024
A_TN = 512
A_STEPS = IN_COLS // A_TN
POOL_HALO = 16
CONF_HALO = 32
SC_HALO = 8
CONV_CHUNK = 256
CONV_SUB = 64

B_TM = 512
N_BLOCKS = T_ALL // B_TM
B_PROMPT_BLOCKS = T_PROMPT // B_TM
SEG = BF16_ROWS
SORT_TM = 512
BLOCK_ROWS = -(-(B_TM * TOP_K + N_EXPERTS * (SEG - 1)) // SORT_TM) * SORT_TM
SORT_CHUNKS = BLOCK_ROWS // SORT_TM
GATE_TERMS = 3
SIDE_ID0 = GATE_TERMS * TOP_K
H_EXT = D_MODEL + LANES

MOE_TM = 256
MOE_MT = 8
MOE_TN = 512
TILE_SEGS = MOE_TM // SEG
MOE_TILES = (T_ALL * TOP_K + N_BLOCKS * N_EXPERTS * (SEG - 1) + N_EXPERTS * (MOE_TM - 1)) // MOE_TM
MOE_ITEMS = (MOE_TILES + N_EXPERTS * (MOE_MT - 1)) // MOE_MT

_f32 = jnp.float32
_bf16 = jnp.bfloat16


def _rms(x):
    return x * lax.rsqrt(jnp.mean(x * x, axis=-1, keepdims=True) + EPS)


def _ln(x):
    xc = x - jnp.mean(x, axis=-1, keepdims=True)
    return xc * lax.rsqrt(jnp.mean(xc * xc, axis=-1, keepdims=True) + EPS)


def _gelu(x):
    return 0.5 * x * (1.0 + lax.erf(x * (0.5 ** 0.5)))


def _slab_mul(x, m):
    r, c = x.shape
    return (x.reshape(r // 128, 128, c) * m[None]).reshape(r, c)


def _slab_add(x, m):
    r, c = x.shape
    return (x.reshape(r // 128, 128, c) + m[None]).reshape(r, c)


ADA_TN = 1024
ADA_ROWS = DEC_BATCH + 8


def _ada_body(c_ref, w_ref, b_ref, os_ref, op_ref):
    c = c_ref[...]
    a = (c * jax.nn.sigmoid(c)).astype(_bf16)
    r = jnp.dot(a, w_ref[...].astype(_bf16), preferred_element_type=_f32) + b_ref[...]
    os_ref[...] = r[:DEC_BATCH]
    op_ref[...] = r[DEC_BATCH:]


def _ada(c_all, w_ada, b_ada):
    nj = 6 * D_MODEL // ADA_TN
    return pl.pallas_call(
        _ada_body,
        grid=(DEPTH, nj),
        in_specs=[
            pl.BlockSpec((ADA_ROWS, D_MODEL), lambda l, j: (0, 0)),
            pl.BlockSpec((None, D_MODEL, ADA_TN), lambda l, j: (l, 0, j)),
            pl.BlockSpec((None, 1, ADA_TN), lambda l, j: (l, 0, j)),
        ],
        out_specs=[
            pl.BlockSpec((None, DEC_BATCH, ADA_TN), lambda l, j: (l, 0, j)),
            pl.BlockSpec((None, 8, ADA_TN), lambda l, j: (l, 0, j)),
        ],
        out_shape=[
            jax.ShapeDtypeStruct((DEPTH, DEC_BATCH, 6 * D_MODEL), _f32),
            jax.ShapeDtypeStruct((DEPTH, 8, 6 * D_MODEL), _f32),
        ],
        compiler_params=pltpu.CompilerParams(
            dimension_semantics=("arbitrary", "arbitrary"), vmem_limit_bytes=VMEM_LIMIT),
        name="ada_mod",
    )(c_all, w_ada, b_ada.reshape(DEPTH, 1, 6 * D_MODEL))


def _a_prompt_body(x_ref, sh1_ref, sc1_ref, nmix_ref, w_ref, poolw_ref, pscale_ref, sgun_ref, sguw_ref,
                   sgub_ref, cw_ref, cb_ref, clg_ref, clb_ref, scw_ref, onorm_ref,
                   y_ref, pool_o, conf_o, sc_o,
                   h_ref, bufa, bufb, exta, extc, extz, shifted):
    i = pl.program_id(1)
    n = pl.program_id(2)
    tm = A_TM
    gw = GROUP_WIDTH

    @pl.when(n == 0)
    def _():
        hn = _rms(x_ref[...]) * nmix_ref[...]
        h_ref[...] = (hn * (1.0 + sc1_ref[...]) + sh1_ref[...]).astype(_bf16)

    def proj():
        return jnp.dot(h_ref[...], w_ref[...], preferred_element_type=_f32)

    def emit(g, y):
        yn = _rms(y) * onorm_ref[:, g * gw:(g + 1) * gw]
        y_ref[:, g * gw:(g + 1) * gw] = yn.astype(_bf16)

    def carry_halo(ext, halo):
        @pl.when(i == 0)
        def _():
            ext[0:halo, :] = jnp.zeros((halo, gw), _f32)

        @pl.when(i > 0)
        def _():
            ext[0:halo, :] = ext[tm:tm + halo, :]

    @pl.when(n == 0)
    def _():
        a = proj()
        carry_halo(exta, POOL_HALO)
        exta[POOL_HALO:, :] = a
        pool_o[...] = a[tm - POOL_HALO:, :]
        pos = (i * tm + lax.broadcasted_iota(jnp.int32, (tm, 1), 0)).astype(_f32)
        for gi, w in enumerate(POOL_WINDOWS):
            c0 = gi * POOL_CH
            s = exta[POOL_HALO:POOL_HALO + tm, c0:c0 + POOL_CH]
            for j in range(1, w):
                s = s + exta[POOL_HALO - j:POOL_HALO - j + tm, c0:c0 + POOL_CH]
            cnt = jnp.minimum(float(w), pos + 1.0)
            pooled = s / cnt - a[:, c0:c0 + POOL_CH]
            yg = jnp.dot(pooled.astype(_bf16), poolw_ref[gi].astype(_bf16), preferred_element_type=_f32)
            bufb[:, c0:c0 + POOL_CH] = yg
        emit(0, bufb[...] * pscale_ref[...])

    @pl.when(n == 1)
    def _():
        bufa[...] = _gelu(proj())

    @pl.when(n == 2)
    def _():
        v = _ln(_gelu(proj())) * sgun_ref[...]
        vb = v.astype(_bf16)
        row = lax.broadcasted_iota(jnp.int32, (SGU_CHUNK, SGU_CHUNK), 0)
        col = lax.broadcasted_iota(jnp.int32, (SGU_CHUNK, SGU_CHUNK), 1)
        for g in range(SGU_HEADS):
            c0 = g * SGU_CHUNK
            wg = jnp.where(row >= col, sguw_ref[g], 0.0).astype(_bf16)
            for c in range(tm // SGU_CHUNK):
                r0 = c * SGU_CHUNK
                m = jnp.dot(wg, vb[r0:r0 + SGU_CHUNK, c0:c0 + SGU_CHUNK], preferred_element_type=_f32)
                bufb[r0:r0 + SGU_CHUNK, c0:c0 + SGU_CHUNK] = m + sgub_ref[:, c0:c0 + SGU_CHUNK]
        emit(1, bufa[...] * bufb[...])

    @pl.when(n == 3)
    def _():
        bufa[...] = proj()

    @pl.when(n == 4)
    def _():
        g = bufa[...] * jax.nn.sigmoid(proj())
        carry_halo(extc, CONF_HALO)
        extc[CONF_HALO:, :] = g
        conf_o[...] = g[tm - CONF_HALO:, :]
        off = CONF_HALO - (CONF_KERNEL - 1)
        for ch in range(tm // CONV_CHUNK):
            base = ch * CONV_CHUNK
            for r in range(8):
                nr = CONV_CHUNK + 8 * ((CONF_HALO - r) // 8)
                shifted[r, 0:nr, :] = extc[base + r:base + r + nr, :]

            def sub(s, carry, base=base):
                i0 = s * CONV_SUB
                acc = jnp.broadcast_to(cb_ref[...], (CONV_SUB, gw))
                for k in range(CONF_KERNEL):
                    q, r = divmod(off + k, 8)
                    acc = acc + cw_ref[k:k + 1, :] * shifted[r, pl.ds(pl.multiple_of(i0 + 8 * q, 8), CONV_SUB), :]
                bufb[pl.ds(pl.multiple_of(base + i0, CONV_SUB), CONV_SUB), :] = acc
                return carry
            lax.fori_loop(0, CONV_CHUNK // CONV_SUB, sub, 0)
        yn = _ln(bufb[...]) * clg_ref[...] + clb_ref[...]
        emit(2, yn * jax.nn.sigmoid(yn))

    @pl.when(n == 5)
    def _():
        bufa[...] = proj()

    @pl.when(n == 6)
    def _():
        bufb[...] = proj()

    @pl.when(n == 7)
    def _():
        z = bufb[...] * proj()
        carry_halo(extz, SC_HALO)
        extz[SC_HALO:, :] = z
        sc_o[...] = z[tm - SC_HALO:, :]
        off = SC_HALO - (SC_KERNEL - 1)
        acc = scw_ref[0:1, :] * extz[off:off + tm, :]
        for k in range(1, SC_KERNEL):
            acc = acc + scw_ref[k:k + 1, :] * extz[off + k:off + k + tm, :]
        emit(3, bufa[...] * acc)


def _layer_spec(shape, l, nd):
    zeros = (0,) * len(shape)
    if nd == 1:
        return pl.BlockSpec((None,) + shape, lambda a: (l,) + zeros)
    if nd == 2:
        return pl.BlockSpec((None,) + shape, lambda a, b: (l,) + zeros)
    return pl.BlockSpec((None,) + shape, lambda a, b, c: (l,) + zeros)


def _mixer_params(l, p, nd):
    gw = GROUP_WIDTH
    arrs = [p["cw"], p["cb"], p["clg"], p["clb"], p["scw"], p["onorm"]]
    shapes = [(CONF_KERNEL, gw), (1, gw), (1, gw), (1, gw), (SC_KERNEL, gw), (1, D_MODEL)]
    return arrs, [_layer_spec(s, l, nd) for s in shapes]


def _a_prompt(l, x_all, mod_p4, w_in_b, p):
    gw = GROUP_WIDTH
    ni = SEQ // A_TM
    marrs, mspecs = _mixer_params(l, p, 3)
    in_specs = [
        pl.BlockSpec((A_TM, D_MODEL), lambda b, i, n: (b * ni + i, 0)),
        pl.BlockSpec((None, None, 1, D_MODEL), lambda b, i, n: (l, b, 0, 0)),
        pl.BlockSpec((None, None, 1, D_MODEL), lambda b, i, n: (l, b, 0, 1)),
        _layer_spec((1, D_MODEL), l, 3),
        pl.BlockSpec((None, D_MODEL, A_TN), lambda b, i, n: (l, 0, n)),
        _layer_spec((len(POOL_WINDOWS), POOL_CH, POOL_CH), l, 3),
        _layer_spec((1, gw), l, 3),
        _layer_spec((1, gw), l, 3),
        _layer_spec((SGU_HEADS, SGU_CHUNK, SGU_CHUNK), l, 3),
        _layer_spec((SGU_CHUNK, gw), l, 3),
    ] + mspecs
    out_specs = [
        pl.BlockSpec((A_TM, D_MODEL), lambda b, i, n: (b * ni + i, 0)),
        pl.BlockSpec((None, POOL_HALO, gw), lambda b, i, n: (b, 0, 0)),
        pl.BlockSpec((None, CONF_HALO, gw), lambda b, i, n: (b, 0, 0)),
        pl.BlockSpec((None, SC_HALO, gw), lambda b, i, n: (b, 0, 0)),
    ]
    out_shape = [
        jax.ShapeDtypeStruct((T_PROMPT, D_MODEL), _bf16),
        jax.ShapeDtypeStruct((BATCH, POOL_HALO, gw), _f32),
        jax.ShapeDtypeStruct((BATCH, CONF_HALO, gw), _f32),
        jax.ShapeDtypeStruct((BATCH, SC_HALO, gw), _f32),
    ]
    scratch = [
        pltpu.VMEM((A_TM, D_MODEL), _bf16),
        pltpu.VMEM((A_TM, gw), _f32),
        pltpu.VMEM((A_TM, gw), _f32),
        pltpu.VMEM((POOL_HALO + A_TM, gw), _f32),
        pltpu.VMEM((CONF_HALO + A_TM, gw), _f32),
        pltpu.VMEM((SC_HALO + A_TM, gw), _f32),
        pltpu.VMEM((8, CONV_CHUNK + CONF_HALO, gw), _f32),
    ]
    return pl.pallas_call(
        _a_prompt_body,
        grid=(BATCH, ni, A_STEPS),
        in_specs=in_specs,
        out_specs=out_specs,
        out_shape=out_shape,
        scratch_shapes=scratch,
        compiler_params=pltpu.CompilerParams(
            dimension_semantics=("arbitrary", "arbitrary", "arbitrary"), vmem_limit_bytes=VMEM_LIMIT),
        name="mixers_prompt",
    )(x_all, mod_p4, mod_p4, p["nmix"], w_in_b, p["poolw"], p["pscale"], p["sgun"], p["sguw"], p["sgub"], *marrs)


def _a_sample_body(x_ref, sh1_ref, sc1_ref, nmix_ref, w_ref, poolw_ref, pscale_ref, sgun_ref, sgwc_ref,
                   sgbc_ref, cw_ref, cb_ref, clg_ref, clb_ref, scw_ref, onorm_ref, pst_ref, cst_ref, sst_ref,
                   y_ref, a_o, g_o, z_o, v_o,
                   h_ref, bufa, bufb):
    n = pl.program_id(0)
    gw = GROUP_WIDTH
    nb = DEC_BATCH
    ns = DEC_SEQ

    @pl.when(n == 0)
    def _():
        hn = _rms(x_ref[...]) * nmix_ref[...]
        h_ref[...] = _slab_add(_slab_mul(hn, 1.0 + sc1_ref[...]), sh1_ref[...]).astype(_bf16)

    def proj():
        return jnp.dot(h_ref[...], w_ref[...], preferred_element_type=_f32)

    def emit(g, y):
        yn = _rms(y) * onorm_ref[:, g * gw:(g + 1) * gw]
        y_ref[:, g * gw:(g + 1) * gw] = yn.astype(_bf16)

    def slab(x, s):
        return x[s * nb:(s + 1) * nb, :]

    @pl.when(n == 0)
    def _():
        a = proj()
        a_o[...] = a.reshape(ns, nb, gw)
        npast = POOL_MAX - 1
        for s in range(ns):
            for gi, w in enumerate(POOL_WINDOWS):
                c0 = gi * POOL_CH
                acc = None
                for j in range(w):
                    e = npast + s - j
                    term = pst_ref[e, :, c0:c0 + POOL_CH] if e < npast else slab(a, e - npast)[:, c0:c0 + POOL_CH]
                    acc = term if acc is None else acc + term
                cnt = min(float(w), PAST_LEN + s + 1.0)
                pooled = acc / cnt - slab(a, s)[:, c0:c0 + POOL_CH]
                bufb[s * nb:(s + 1) * nb, c0:c0 + POOL_CH] = pooled
        pooled = bufb[...].astype(_bf16)
        for gi in range(len(POOL_WINDOWS)):
            c0 = gi * POOL_CH
            bufa[:, c0:c0 + POOL_CH] = jnp.dot(pooled[:, c0:c0 + POOL_CH], poolw_ref[gi].astype(_bf16),
                                               preferred_element_type=_f32)
        emit(0, bufa[...] * pscale_ref[...])

    @pl.when(n == 1)
    def _():
        bufa[...] = _gelu(proj())

    @pl.when(n == 2)
    def _():
        v = _ln(_gelu(proj())) * sgun_ref[...]
        v_o[...] = v.reshape(ns, nb, gw)
        for t in range(ns):
            m = jnp.broadcast_to(sgbc_ref[t:t + 1, :], (nb, gw))
            for s in range(t + 1):
                m = m + sgwc_ref[t * ns + s:t * ns + s + 1, :] * slab(v, s)
            bufb[t * nb:(t + 1) * nb, :] = m
        emit(1, bufa[...] * bufb[...])

    @pl.when(n == 3)
    def _():
        bufa[...] = proj()

    @pl.when(n == 4)
    def _():
        g = bufa[...] * jax.nn.sigmoid(proj())
        g_o[...] = g.reshape(ns, nb, gw)
        npast = CONF_KERNEL - 1
        for s in range(ns):
            acc = jnp.broadcast_to(cb_ref[...], (nb, gw))
            for k in range(CONF_KERNEL):
                e = s + k
                term = cst_ref[e] if e < npast else slab(g, e - npast)
                acc = acc + cw_ref[k:k + 1, :] * term
            bufb[s * nb:(s + 1) * nb, :] = acc
        yn = _ln(bufb[...]) * clg_ref[...] + clb_ref[...]
        emit(2, yn * jax.nn.sigmoid(yn))

    @pl.when(n == 5)
    def _():
        bufa[...] = proj()

    @pl.when(n == 6)
    def _():
        bufb[...] = proj()

    @pl.when(n == 7)
    def _():
        z = bufb[...] * proj()
        z_o[...] = z.reshape(ns, nb, gw)
        npast = SC_KERNEL - 1
        for s in range(ns):
            acc = None
            for k in range(SC_KERNEL):
                e = s + k
                term = scw_ref[k:k + 1, :] * (sst_ref[e] if e < npast else slab(z, e - npast))
                acc = term if acc is None else acc + term
            bufb[s * nb:(s + 1) * nb, :] = acc
        emit(3, bufa[...] * bufb[...])


def _a_sample(l, x_all, x_blk, mod_s, w_in_b, p, pool_t, conf_t, sc_t):
    gw = GROUP_WIDTH
    ns, nb = DEC_SEQ, DEC_BATCH
    marrs, mspecs = _mixer_params(l, p, 1)
    in_specs = [
        pl.BlockSpec((T_SAMPLE, D_MODEL), lambda n: (x_blk, 0)),
        pl.BlockSpec((None, nb, D_MODEL), lambda n: (l, 0, 0)),
        pl.BlockSpec((None, nb, D_MODEL), lambda n: (l, 0, 1)),
        _layer_spec((1, D_MODEL), l, 1),
        pl.BlockSpec((None, D_MODEL, A_TN), lambda n: (l, 0, n)),
        _layer_spec((len(POOL_WINDOWS), POOL_CH, POOL_CH), l, 1),
        _layer_spec((1, gw), l, 1),
        _layer_spec((1, gw), l, 1),
        _layer_spec((ns * ns, gw), l, 1),
        _layer_spec((ns, gw), l, 1),
    ] + mspecs + [
        _layer_spec((POOL_MAX - 1, nb, gw), l, 1),
        _layer_spec((CONF_KERNEL - 1, nb, gw), l, 1),
        _layer_spec((SC_KERNEL - 1, nb, gw), l, 1),
    ]
    out_specs = [pl.BlockSpec((T_SAMPLE, D_MODEL), lambda n: (0, 0))] + [
        pl.BlockSpec((ns, nb, gw), lambda n: (0, 0, 0)) for _ in range(4)]
    out_shape = [jax.ShapeDtypeStruct((T_SAMPLE, D_MODEL), _bf16)] + [
        jax.ShapeDtypeStruct((ns, nb, gw), _f32) for _ in range(4)]
    scratch = [
        pltpu.VMEM((T_SAMPLE, D_MODEL), _bf16),
        pltpu.VMEM((T_SAMPLE, gw), _f32),
        pltpu.VMEM((T_SAMPLE, gw), _f32),
    ]
    return pl.pallas_call(
        _a_sample_body,
        grid=(A_STEPS,),
        in_specs=in_specs,
        out_specs=out_specs,
        out_shape=out_shape,
        scratch_shapes=scratch,
        compiler_params=pltpu.CompilerParams(
            dimension_semantics=("arbitrary",), vmem_limit_bytes=VMEM_LIMIT),
        name="mixers_sample",
    )(x_all, mod_s, mod_s, p["nmix"], w_in_b, p["poolw"], p["pscale"], p["sgun"], p["sgwc"], p["sgbc"],
      *marrs, pool_t, conf_t, sc_t)


def _pick_mod(is_prompt, mp_ref, ms_ref):
    return jnp.where(is_prompt, jnp.broadcast_to(mp_ref[...], (DEC_BATCH, D_MODEL)), ms_ref[...])


def _b_body(xp_ref, xs_ref, yp_ref, ys_ref, w_ref, mpg1, mpsh2, mpsc2, msg1, mssh2, mssc2, nffn_ref, rw_ref, rb_ref,
            x1_ref, h2_ref, pos_ref, post_ref, meta_ref):
    i = pl.program_id(0)
    isp = i < B_PROMPT_BLOCKS
    ycat = jnp.where(isp, yp_ref[...], ys_ref[...])
    o = jnp.dot(ycat, w_ref[...], preferred_element_type=_f32)
    x1 = jnp.where(isp, xp_ref[...], xs_ref[...]) + _slab_mul(o, _pick_mod(isp, mpg1, msg1))
    x1_ref[...] = x1
    hn = _rms(x1) * nffn_ref[...]
    h2 = _slab_add(_slab_mul(hn, 1.0 + _pick_mod(isp, mpsc2, mssc2)), _pick_mod(isp, mpsh2, mssh2))
    hi = h2.astype(_bf16)
    h2_ref[:, 0:D_MODEL] = hi
    lo = (h2 - hi.astype(_f32)).astype(_bf16)
    l1 = jnp.dot(hi, rw_ref[...], preferred_element_type=_f32)
    l2 = jnp.dot(lo, rw_ref[:, 0:LANES], preferred_element_type=_f32)
    logits = l1[:, 0:LANES] + l1[:, LANES:] + l2 + rb_ref[...]
    lane = lax.broadcasted_iota(jnp.int32, (B_TM, LANES), 1).astype(_f32)
    neg = jnp.float32(-jnp.inf)
    cur = jnp.where(lane < N_EXPERTS, logits, neg)
    vals, ids = [], []
    for _ in range(TOP_K):
        m = jnp.max(cur, axis=-1, keepdims=True)
        ix = jnp.min(jnp.where(cur == m, lane, float(LANES)), axis=-1, keepdims=True)
        vals.append(m)
        ids.append(ix)
        cur = jnp.where(lane == ix, neg, cur)
    es = [jnp.exp(v - vals[0]) for v in vals]
    tot = es[0] + es[1] + es[2] + es[3]
    side = jnp.zeros((B_TM, LANES), _f32)
    for k in range(TOP_K):
        g = es[k] / tot
        g0 = g.astype(_bf16).astype(_f32)
        g1 = (g - g0).astype(_bf16).astype(_f32)
        for j, term in enumerate((g0, g1, g - g0 - g1)):
            side = jnp.where(lane == float(GATE_TERMS * k + j), term, side)
        side = jnp.where(lane == float(SIDE_ID0 + k), ids[k], side)
    h2_ref[:, D_MODEL:] = side.astype(_bf16)

    hit = [lane == ix for ix in ids]
    chosen = jnp.zeros((B_TM, LANES), _f32)
    for h in hit:
        chosen = jnp.where(h, 1.0, chosen)
    cnt = jnp.sum(chosen, axis=0, keepdims=True)
    seg_rows = jnp.floor((cnt + (SEG - 1.0)) * (1.0 / SEG)) * SEG
    er = lax.broadcasted_iota(jnp.int32, (LANES, LANES), 0)
    ec = lax.broadcasted_iota(jnp.int32, (LANES, LANES), 1)
    before = jnp.where(er < ec, 1.0, 0.0).astype(_bf16)
    seg_start = jnp.dot(jnp.broadcast_to(seg_rows, (8, LANES)).astype(_bf16), before,
                        preferred_element_type=_f32)[0:1, :]
    tr = lax.broadcasted_iota(jnp.int32, (B_TM, B_TM), 0)
    tc = lax.broadcasted_iota(jnp.int32, (B_TM, B_TM), 1)
    earlier = jnp.where(tr > tc, 1.0, 0.0).astype(_bf16)
    rank = jnp.dot(earlier, chosen.astype(_bf16), preferred_element_type=_f32)
    base = seg_start + rank
    pos = jnp.zeros((B_TM, LANES), _f32)
    for k in range(TOP_K):
        pk = jnp.sum(jnp.where(hit[k], base, 0.0), axis=-1, keepdims=True)
        pos = jnp.where(lane == float(k), pk, pos)
    pos_ref[...] = pos
    post_ref[...] = jnp.transpose(pos)[0:8, :]
    sub = lax.broadcasted_iota(jnp.int32, (8, LANES), 0)
    meta = jnp.where(sub == 0, jnp.broadcast_to(seg_start, (8, LANES)),
                     jnp.where(sub == 1, jnp.broadcast_to(seg_rows, (8, LANES)), 0.0))
    meta_ref[...] = meta.astype(jnp.int32)


def _stage_b(l, x_p, x_s, xs_blk, ycat_p, ycat_s, w_out_b, mod_p4, mod_s, p):
    def pb(i):
        return jnp.minimum(i // (SEQ // B_TM), BATCH - 1)

    def mp_spec(j):
        return pl.BlockSpec((None, None, 1, D_MODEL), lambda i: (l, pb(i), 0, j))

    def ms_spec(j):
        return pl.BlockSpec((None, DEC_BATCH, D_MODEL), lambda i: (l, 0, j))

    row_spec = pl.BlockSpec((B_TM, D_MODEL), lambda i: (i, 0))
    in_specs = [
        pl.BlockSpec((B_TM, D_MODEL), lambda i: (jnp.minimum(i, B_PROMPT_BLOCKS - 1), 0)),
        pl.BlockSpec((T_SAMPLE, D_MODEL), lambda i: (xs_blk, 0), pipeline_mode=pl.Buffered(1)),
        pl.BlockSpec((B_TM, D_MODEL), lambda i: (jnp.minimum(i, B_PROMPT_BLOCKS - 1), 0)),
        pl.BlockSpec((T_SAMPLE, D_MODEL), lambda i: (0, 0), pipeline_mode=pl.Buffered(1)),
        pl.BlockSpec((None, D_MODEL, D_MODEL), lambda i: (l, 0, 0), pipeline_mode=pl.Buffered(1)),
        mp_spec(2), mp_spec(3), mp_spec(4), ms_spec(2), ms_spec(3), ms_spec(4),
        _layer_spec((1, D_MODEL), l, 1),
        _layer_spec((D_MODEL, 2 * LANES), l, 1),
        _layer_spec((1, LANES), l, 1),
    ]
    lane_spec = pl.BlockSpec((B_TM, LANES), lambda i: (i, 0))
    return pl.pallas_call(
        _b_body,
        grid=(N_BLOCKS,),
        in_specs=in_specs,
        out_specs=[row_spec, pl.BlockSpec((B_TM, H_EXT), lambda i: (i, 0)), lane_spec,
                   pl.BlockSpec((None, 8, B_TM), lambda i: (i, 0, 0)),
                   pl.BlockSpec((None, 8, LANES), lambda i: (i, 0, 0))],
        out_shape=[
            jax.ShapeDtypeStruct((T_ALL, D_MODEL), _f32),
            jax.ShapeDtypeStruct((T_ALL, H_EXT), _bf16),
            jax.ShapeDtypeStruct((T_ALL, LANES), _f32),
            jax.ShapeDtypeStruct((N_BLOCKS, 8, B_TM), _f32),
            jax.ShapeDtypeStruct((N_BLOCKS, 8, LANES), jnp.int32),
        ],
        compiler_params=pltpu.CompilerParams(
            dimension_semantics=("arbitrary",), vmem_limit_bytes=VMEM_LIMIT),
        name="outproj_router",
    )(x_p, x_s, ycat_p, ycat_s, w_out_b, mod_p4, mod_p4, mod_p4, mod_s, mod_s, mod_s, p["nffn"], p["rw"], p["rb"])


def _sort_body(post_ref, h_ref, o_ref, so_ref):
    for c in range(SORT_CHUNKS):
        j = (c * SORT_TM + lax.broadcasted_iota(jnp.int32, (SORT_TM, B_TM), 0)).astype(_f32)
        sel = jnp.zeros((SORT_TM, B_TM), _f32)
        for k in range(TOP_K):
            sel = jnp.where(j == post_ref[k:k + 1, :], 1.0, sel)
        res = jnp.dot(sel.astype(_bf16), h_ref[...], preferred_element_type=_f32)
        o_ref[c * SORT_TM:(c + 1) * SORT_TM, :] = res[:, 0:D_MODEL].astype(_bf16)
        so_ref[c * SORT_TM:(c + 1) * SORT_TM, :] = res[:, D_MODEL:]


def _local_sort(post, h_ext):
    return pl.pallas_call(
        _sort_body,
        grid=(N_BLOCKS,),
        in_specs=[
            pl.BlockSpec((None, 8, B_TM), lambda b: (b, 0, 0)),
            pl.BlockSpec((B_TM, H_EXT), lambda b: (b, 0)),
        ],
        out_specs=[pl.BlockSpec((BLOCK_ROWS, D_MODEL), lambda b: (b, 0)),
                   pl.BlockSpec((BLOCK_ROWS, LANES), lambda b: (b, 0))],
        out_shape=[jax.ShapeDtypeStruct((N_BLOCKS * BLOCK_ROWS, D_MODEL), _bf16),
                   jax.ShapeDtypeStruct((N_BLOCKS * BLOCK_ROWS, LANES), _f32)],
        compiler_params=pltpu.CompilerParams(
            dimension_semantics=("arbitrary",), vmem_limit_bytes=VMEM_LIMIT),
        name="local_sort",
    )(post, h_ext)


BIG_TM = 2 * MOE_TM
SMALL_TM = MOE_TM // 2
MOE_DN = 1024
STREAM_UP_STEPS = D_EXPERT // MOE_TN
STREAM_DOWN_STEPS = D_MODEL // MOE_DN
STREAM_STEPS = STREAM_UP_STEPS + STREAM_DOWN_STEPS
MOE_WSLOTS = 3


def _moe_stream_body(ie_ref, ilo_ref, int_ref, start_ref, rows_ref, used_ref, nitems_ref,
                     wup_hbm, wdn_hbm, bup_hbm, bdn_hbm, x_hbm, g_hbm, y_hbm,
                     wbuf, xbuf, hbuf, gbuf, bup, bdn, seg_src, nseg, pending,
                     sem_w, sem_b, sem_x, sem_y, *, l):
    tm, tn = MOE_TM, MOE_TN
    n_items = nitems_ref[0]
    n_steps = n_items * STREAM_STEPS

    def w_half(src_ref, e, col0, slot, half):
        return pltpu.make_async_copy(src_ref.at[l, e, :, pl.ds(col0, tn)],
                                     wbuf.at[slot, :, pl.ds(half * tn, tn)], sem_w.at[slot])

    def issue_w(s):
        e = ie_ref[s // STREAM_STEPS]
        n = s % STREAM_STEPS
        slot = s % MOE_WSLOTS
        for c in range(STREAM_UP_STEPS):
            @pl.when(n == c)
            def _(c=c):
                w_half(wup_hbm, e, c * tn, slot, 0).start()
                w_half(wup_hbm, e, D_EXPERT + c * tn, slot, 1).start()
        for c in range(STREAM_DOWN_STEPS):
            @pl.when(n == STREAM_UP_STEPS + c)
            def _(c=c):
                w_half(wdn_hbm, e, c * MOE_DN, slot, 0).start()
                w_half(wdn_hbm, e, c * MOE_DN + tn, slot, 1).start()

    def wait_w(s):
        slot = s % MOE_WSLOTS
        for half in range(2):
            w_half(wdn_hbm, 0, 0, slot, half).wait()

    def b_copies(item):
        e = ie_ref[item]
        par = item % 2
        return (pltpu.make_async_copy(bup_hbm.at[l, e], bup.at[par], sem_b.at[par]),
                pltpu.make_async_copy(bdn_hbm.at[l, e], bdn.at[par], sem_b.at[par]))

    def x_copy(p, src, dst):
        return pltpu.make_async_copy(x_hbm.at[pl.ds(src, SEG)], xbuf.at[p, pl.ds(dst, SEG)],
                                     sem_x.at[p * MOE_MT + dst // tm])

    def g_copy(p, src, dst):
        return pltpu.make_async_copy(g_hbm.at[pl.ds(src, SEG)], gbuf.at[p, pl.ds(dst, SEG)],
                                     sem_x.at[p * MOE_MT + dst // tm])

    def y_copy(p, src, dst):
        return pltpu.make_async_copy(xbuf.at[p, pl.ds(src, SEG)], y_hbm.at[pl.ds(dst, SEG)], sem_y.at[p])

    def drain(p):
        def w(_, carry):
            y_copy(p, 0, 0).wait()
            return carry
        lax.fori_loop(0, pending[p], w, 0)
        pending[p] = 0

    def tile_pieces(p, t):
        return jnp.clip(nseg[p] - t * TILE_SEGS, 0, TILE_SEGS)

    def plan(rows):
        n_big = rows // BIG_TM
        rem = rows - n_big * BIG_TM
        round_up = rem > BIG_TM - SMALL_TM
        n_big = n_big + jnp.where(round_up, 1, 0)
        rem = jnp.where(round_up, 0, rem)
        has_mid = rem > SMALL_TM
        has_small = (rem > 0) & ((rem <= SMALL_TM) | (rem > tm))
        off_mid = n_big * BIG_TM
        off_small = off_mid + jnp.where(has_mid, tm, 0)
        return n_big, has_mid, has_small, off_mid, off_small

    def load_item(item):
        p = item % 2
        e = ie_ref[item]
        lo = ilo_ref[item]
        nt = int_ref[item]
        hi = lo + nt * tm

        def per_block(b, q0):
            rows = rows_ref[b * N_EXPERTS + e]
            start = start_ref[b * N_EXPERTS + e]

            def per_piece(g, cc):
                q = q0 + g * SEG

                @pl.when((q >= lo) & (q < hi))
                def _():
                    src = pl.multiple_of(b * BLOCK_ROWS + start + g * SEG, SEG)
                    dst = pl.multiple_of(q - lo, SEG)
                    seg_src[p * (MOE_MT * TILE_SEGS) + dst // SEG] = src
                    x_copy(p, src, dst).start()
                    g_copy(p, src, dst).start()
                return cc
            lax.fori_loop(0, rows // SEG, per_piece, 0)
            return q0 + rows
        end = lax.fori_loop(0, N_BLOCKS, per_block, 0)
        ns = (jnp.minimum(end, hi) - lo) // SEG
        nseg[p] = ns

        def zero_piece(g, cc):
            r = pl.multiple_of(g * SEG, SEG)
            xbuf[p, pl.ds(r, SEG), :] = jnp.zeros((SEG, D_MODEL), _bf16)
            gbuf[p, pl.ds(r, SEG), :] = jnp.zeros((SEG, LANES), _f32)
            return cc
        n_big, has_mid, has_small, _, off_small = plan(ns * SEG)
        computed = off_small + jnp.where(has_small, SMALL_TM, 0)
        lax.fori_loop(ns, computed // SEG, zero_piece, 0)

    pending[0] = 0
    pending[1] = 0
    issue_w(0)
    issue_w(1)
    for cp in b_copies(0):
        cp.start()
    load_item(0)

    def step(s, carry):
        item = s // STREAM_STEPS
        n = s % STREAM_STEPS
        e = ie_ref[item]
        par = item % 2
        slot = s % MOE_WSLOTS

        @pl.when(s + 2 < n_steps)
        def _():
            issue_w(s + 2)
        wait_w(s)

        @pl.when(n == 0)
        def _():
            for cp in b_copies(item):
                cp.wait()

        @pl.when((n == 1) & (item + 1 < n_items))
        def _():
            for cp in b_copies(item + 1):
                cp.start()
            drain(1 - par)
            load_item(item + 1)

        n_big, has_mid, has_small, off_mid, off_small = plan(nseg[par] * SEG)

        def up_rows(c, r0, nrows):
            if c == 0:
                for u in range(max(nrows // tm, 1)):
                    sub = r0 // tm + u

                    def w(_, c3, sub=sub):
                        x_copy(par, 0, pl.multiple_of(sub * tm, tm)).wait()
                        g_copy(par, 0, pl.multiple_of(sub * tm, tm)).wait()
                        return c3
                    lax.fori_loop(0, tile_pieces(par, sub), w, 0)
            up = lax.dot_general(xbuf[par, pl.ds(r0, nrows), :], wbuf[slot], (((1,), (0,)), ((), ())),
                                 preferred_element_type=_f32)
            bg = bup[par, :, c * tn:(c + 1) * tn]
            bl = bup[par, :, D_EXPERT + c * tn:D_EXPERT + (c + 1) * tn]
            glu = jnp.minimum(up[:, 0:tn] + bg, SWIGLU_LIMIT)
            lin = jnp.clip(up[:, tn:] + bl, -SWIGLU_LIMIT, SWIGLU_LIMIT)
            act = glu * jax.nn.sigmoid(SWIGLU_ALPHA * glu) * (lin + 1.0)
            hbuf[pl.ds(r0, nrows), c * tn:(c + 1) * tn] = act.astype(_bf16)

        def down_rows(c, r0, nrows):
            y = (lax.dot_general(hbuf[pl.ds(r0, nrows), :], wbuf[slot], (((1,), (0,)), ((), ())),
                                 preferred_element_type=_f32)
                 + bdn[par, :, c * MOE_DN:(c + 1) * MOE_DN])
            g = gbuf[par, pl.ds(r0, nrows), :]
            ef = e.astype(_f32)
            gate = jnp.zeros((nrows, 1), _f32)
            for k in range(TOP_K):
                gk = g[:, GATE_TERMS * k:GATE_TERMS * k + 1]
                for j in range(1, GATE_TERMS):
                    gk = gk + g[:, GATE_TERMS * k + j:GATE_TERMS * k + j + 1]
                gate = jnp.where(g[:, SIDE_ID0 + k:SIDE_ID0 + k + 1] == ef, gk, gate)
            xbuf[par, pl.ds(r0, nrows), c * MOE_DN:(c + 1) * MOE_DN] = (y * gate).astype(_bf16)
            if c == STREAM_DOWN_STEPS - 1:
                for q in range(nrows // SEG):
                    piece = r0 // SEG + q

                    @pl.when(piece < nseg[par])
                    def _(q=q, piece=piece):
                        dst = seg_src[par * (MOE_MT * TILE_SEGS) + piece]
                        y_copy(par, pl.multiple_of(r0 + q * SEG, SEG), pl.multiple_of(dst, SEG)).start()

        def all_rows(fn, c):
            def big(t, cc):
                fn(c, pl.multiple_of(t * BIG_TM, BIG_TM), BIG_TM)
                return cc
            lax.fori_loop(0, n_big, big, 0)

            @pl.when(has_mid)
            def _():
                fn(c, pl.multiple_of(off_mid, tm), tm)

            @pl.when(has_small)
            def _():
                fn(c, pl.multiple_of(off_small, SMALL_TM), SMALL_TM)

        for c in range(STREAM_UP_STEPS):
            @pl.when(n == c)
            def _(c=c):
                all_rows(up_rows, c)

        for c in range(STREAM_DOWN_STEPS):
            @pl.when(n == STREAM_UP_STEPS + c)
            def _(c=c):
                all_rows(down_rows, c)
                if c == STREAM_DOWN_STEPS - 1:
                    pending[par] = nseg[par]
        return carry
    lax.fori_loop(0, n_steps, step, 0)

    drain(0)
    drain(1)
    hbuf[0:SEG, :] = jnp.zeros((SEG, D_EXPERT), _bf16)

    def zero_copy(row):
        return pltpu.make_async_copy(hbuf.at[pl.ds(0, SEG)], y_hbm.at[pl.ds(row, SEG)], sem_y.at[0])

    def zero_block(b, carry):
        def zstart(g, cc):
            zero_copy(pl.multiple_of(b * BLOCK_ROWS + g * SEG, SEG)).start()
            return cc

        def zwait(g, cc):
            zero_copy(0).wait()
            return cc
        first = used_ref[b] // SEG
        lax.fori_loop(first, BLOCK_ROWS // SEG, zstart, 0)
        lax.fori_loop(first, BLOCK_ROWS // SEG, zwait, 0)
        return carry
    lax.fori_loop(0, N_BLOCKS, zero_block, 0)


def _moe_stream(l, item_e, item_lo, item_nt, seg_start, seg_rows, used, n_items, w_up, b_up4, w_down, b_down4,
                x_sorted, g_sorted):
    any_spec = pl.BlockSpec(memory_space=pl.ANY)
    grid_spec = pltpu.PrefetchScalarGridSpec(
        num_scalar_prefetch=7,
        grid=(1,),
        in_specs=[any_spec] * 6,
        out_specs=any_spec,
        scratch_shapes=[
            pltpu.VMEM((MOE_WSLOTS, D_MODEL, 2 * MOE_TN), _f32),
            pltpu.VMEM((2, MOE_MT * MOE_TM, D_MODEL), _bf16),
            pltpu.VMEM((MOE_MT * MOE_TM, D_EXPERT), _bf16),
            pltpu.VMEM((2, MOE_MT * MOE_TM, LANES), _f32),
            pltpu.VMEM((2, 1, 2 * D_EXPERT), _f32),
            pltpu.VMEM((2, 1, D_MODEL), _f32),
            pltpu.SMEM((2 * MOE_MT * TILE_SEGS,), jnp.int32),
            pltpu.SMEM((2,), jnp.int32),
            pltpu.SMEM((2,), jnp.int32),
            pltpu.SemaphoreType.DMA((MOE_WSLOTS,)),
            pltpu.SemaphoreType.DMA((2,)),
            pltpu.SemaphoreType.DMA((2 * MOE_MT,)),
            pltpu.SemaphoreType.DMA((2,)),
        ],
    )
    return pl.pallas_call(
        functools.partial(_moe_stream_body, l=l),
        grid_spec=grid_spec,
        out_shape=jax.ShapeDtypeStruct((N_BLOCKS * BLOCK_ROWS, D_MODEL), _bf16),
        compiler_params=pltpu.CompilerParams(
            dimension_semantics=("arbitrary",), vmem_limit_bytes=60 * 1024 * 1024),
        name="moe_experts",
    )(item_e, item_lo, item_nt, seg_start, seg_rows, used, n_items, w_up, w_down, b_up4, b_down4, x_sorted,
      g_sorted)


def _work_items(meta):
    seg_start = meta[:, 0, :N_EXPERTS]
    seg_rows = meta[:, 1, :N_EXPERTS]
    rows_e = jnp.sum(seg_rows, axis=0)
    nt = (rows_e + MOE_TM - 1) // MOE_TM
    ipe = (nt + MOE_MT - 1) // MOE_MT
    cum = jnp.cumsum(ipe)
    first = cum - ipe
    total = cum[-1]
    ii = jnp.arange(MOE_ITEMS, dtype=jnp.int32)
    ii_c = jnp.minimum(ii, total - 1)
    e_i = jnp.minimum(jnp.sum((cum[None, :] <= ii_c[:, None]).astype(jnp.int32), axis=1), N_EXPERTS - 1)
    valid = ii < total
    j = ii - first[e_i]
    lo_i = jnp.where(valid, j * (MOE_MT * MOE_TM), 0)
    nt_i = jnp.where(valid, jnp.minimum(MOE_MT, nt[e_i] - j * MOE_MT), 0)
    used = jnp.sum(seg_rows, axis=1)
    i32 = jnp.int32
    return (e_i.astype(i32), lo_i.astype(i32), nt_i.astype(i32), seg_start.reshape(-1).astype(i32),
            seg_rows.reshape(-1).astype(i32), used.astype(i32), total.astype(i32).reshape(1))


def _c_body(x1_ref, y_ref, pos_ref, mpg2, msg2, nfin_ref, *rest, final):
    if final:
        outp_ref, outs_ref, sel_ref = rest
    else:
        out_ref, sel_ref = rest
    i = pl.program_id(0)
    isp = i < B_PROMPT_BLOCKS
    for c in range(SORT_CHUNKS):
        j = (c * SORT_TM + lax.broadcasted_iota(jnp.int32, (B_TM, SORT_TM), 1)).astype(_f32)
        sel = jnp.zeros((B_TM, SORT_TM), _f32)
        for k in range(TOP_K):
            sel = jnp.where(j == pos_ref[:, k:k + 1], 1.0, sel)
        sel_ref[:, c * SORT_TM:(c + 1) * SORT_TM] = sel.astype(_bf16)
    acc = jnp.dot(sel_ref[...], y_ref[...], preferred_element_type=_f32)
    x2 = x1_ref[...] + _slab_mul(acc, _pick_mod(isp, mpg2, msg2))
    if final:
        y = _rms(x2) * nfin_ref[...]

        @pl.when(isp)
        def _():
            outp_ref[...] = y

        @pl.when(jnp.logical_not(isp))
        def _():
            outs_ref[...] = y
    else:
        out_ref[...] = x2


def _combine(l, x1, y_sorted, pos, mod_p4, mod_s, nfin, final):
    def pb(i):
        return jnp.minimum(i // (SEQ // B_TM), BATCH - 1)

    row_spec = pl.BlockSpec((B_TM, D_MODEL), lambda i: (i, 0))
    lane_spec = pl.BlockSpec((B_TM, LANES), lambda i: (i, 0))
    if final:
        out_specs = [pl.BlockSpec((B_TM, D_MODEL), lambda i: (jnp.minimum(i, B_PROMPT_BLOCKS - 1), 0)),
                     pl.BlockSpec((T_SAMPLE, D_MODEL), lambda i: (0, 0))]
        out_shape = [jax.ShapeDtypeStruct((T_PROMPT, D_MODEL), _f32),
                     jax.ShapeDtypeStruct((T_SAMPLE, D_MODEL), _f32)]
    else:
        out_specs = row_spec
        out_shape = jax.ShapeDtypeStruct((T_ALL, D_MODEL), _f32)
    return pl.pallas_call(
        functools.partial(_c_body, final=final),
        grid=(N_BLOCKS,),
        in_specs=[
            row_spec,
            pl.BlockSpec((BLOCK_ROWS, D_MODEL), lambda i: (i, 0)),
            lane_spec,
            pl.BlockSpec((None, None, 1, D_MODEL), lambda i: (l, pb(i), 0, 5)),
            pl.BlockSpec((None, DEC_BATCH, D_MODEL), lambda i: (l, 0, 5)),
            pl.BlockSpec((1, D_MODEL), lambda i: (0, 0)),
        ],
        out_specs=out_specs,
        out_shape=out_shape,
        scratch_shapes=[pltpu.VMEM((B_TM, BLOCK_ROWS), _bf16)],
        compiler_params=pltpu.CompilerParams(
            dimension_semantics=("arbitrary",), vmem_limit_bytes=VMEM_LIMIT),
        name="moe_combine",
    )(x1, y_sorted, pos, mod_p4, mod_s, nfin)


def kernel(x_prompt, x_sample, c_prompt, c_sample, state_pool, state_conformer, state_shortconv, w_ada, b_ada, norm_mix, norm_ffn, norm_final, w_in, pool_w, pool_scale, sgu_norm, sgu_w, sgu_b, conf_w, conf_b, conf_ln_g, conf_ln_b, sc_w, out_norm, w_out, router_w, router_b, w_up, b_up, w_down, b_down):
    gw = GROUP_WIDTH
    ns, nb = DEC_SEQ, DEC_BATCH

    c_all = jnp.concatenate([c_sample, c_prompt, jnp.zeros((ADA_ROWS - nb - BATCH, D_MODEL), _f32)], axis=0)
    mod_s, mod_p8 = _ada(c_all, w_ada, b_ada)
    mod_p4 = mod_p8[:, :BATCH].reshape(DEPTH, BATCH, 1, 6 * D_MODEL)

    rw_hi = router_w.astype(_bf16)
    rw_lo = (router_w - rw_hi.astype(_f32)).astype(_bf16)
    lane_pad = ((0, 0), (0, 0), (0, LANES - N_EXPERTS))
    params = {
        "nmix": norm_mix.reshape(DEPTH, 1, D_MODEL),
        "nffn": norm_ffn.reshape(DEPTH, 1, D_MODEL),
        "poolw": pool_w,
        "pscale": pool_scale.reshape(DEPTH, 1, gw),
        "sgun": sgu_norm.reshape(DEPTH, 1, gw),
        "sguw": sgu_w,
        "sgub": jnp.repeat(jnp.swapaxes(sgu_b, 1, 2), SGU_CHUNK, axis=2),
        "sgwc": jnp.repeat(jnp.transpose(sgu_w[:, :, :ns, :ns], (0, 2, 3, 1)).reshape(DEPTH, ns * ns, SGU_HEADS),
                           SGU_CHUNK, axis=2),
        "sgbc": jnp.repeat(jnp.swapaxes(sgu_b[:, :, :ns], 1, 2), SGU_CHUNK, axis=2),
        "cw": conf_w,
        "cb": conf_b.reshape(DEPTH, 1, gw),
        "clg": conf_ln_g.reshape(DEPTH, 1, gw),
        "clb": conf_ln_b.reshape(DEPTH, 1, gw),
        "scw": sc_w,
        "onorm": out_norm.reshape(DEPTH, 1, D_MODEL),
        "rw": jnp.concatenate([jnp.pad(rw_hi, lane_pad), jnp.pad(rw_lo, lane_pad)], axis=2),
        "rb": jnp.pad(router_b, ((0, 0), (0, LANES - N_EXPERTS))).reshape(DEPTH, 1, LANES),
    }
    w_in_b = w_in.astype(_bf16)
    w_out_b = w_out.astype(_bf16)
    b_up4 = b_up.reshape(DEPTH, N_EXPERTS, 1, 2 * D_EXPERT)
    b_down4 = b_down.reshape(DEPTH, N_EXPERTS, 1, D_MODEL)
    nfin = norm_final.reshape(1, D_MODEL)

    pool_t = jnp.transpose(state_pool, (0, 2, 1, 3))
    conf_t = jnp.transpose(state_conformer, (0, 2, 1, 3))
    sc_t = jnp.transpose(state_shortconv, (0, 2, 1, 3))

    x_p = x_prompt.reshape(T_PROMPT, D_MODEL)
    x_s = jnp.transpose(x_sample, (1, 0, 2)).reshape(T_SAMPLE, D_MODEL)
    xs_blk = 0

    pool_p, conf_p, sc_p, pool_s, conf_s, sc_s, v_s = [], [], [], [], [], [], []
    for l in range(DEPTH):
        ycat_p, po, co, so = _a_prompt(l, x_p, mod_p4, w_in_b, params)
        ycat_s, a_new, g_new, z_new, v_new = _a_sample(l, x_s, xs_blk, mod_s, w_in_b, params, pool_t, conf_t, sc_t)
        pool_p.append(po[:, POOL_HALO - (POOL_MAX - 1):])
        conf_p.append(co[:, CONF_HALO - (CONF_KERNEL - 1):])
        sc_p.append(so[:, SC_HALO - (SC_KERNEL - 1):])
        pool_s.append(jnp.concatenate([state_pool[l][:, ns:], jnp.transpose(a_new, (1, 0, 2))], axis=1))
        conf_s.append(jnp.concatenate([state_conformer[l][:, ns:], jnp.transpose(g_new, (1, 0, 2))], axis=1))
        sc_s.append(jnp.transpose(z_new, (1, 0, 2))[:, ns - (SC_KERNEL - 1):])
        v_s.append(jnp.transpose(v_new, (1, 0, 2)))

        x1, h_ext, pos, post, meta = _stage_b(l, x_p, x_s, xs_blk, ycat_p, ycat_s, w_out_b, mod_p4, mod_s, params)
        x_sorted, g_sorted = _local_sort(post, h_ext)
        item_e, item_lo, item_nt, seg_start, seg_rows, used, n_items = _work_items(meta)
        y_sorted = _moe_stream(l, item_e, item_lo, item_nt, seg_start, seg_rows, used, n_items, w_up, b_up4,
                               w_down, b_down4, x_sorted, g_sorted)
        if l < DEPTH - 1:
            x_p = x_s = _combine(l, x1, y_sorted, pos, mod_p4, mod_s, nfin, final=False)
            xs_blk = T_PROMPT // T_SAMPLE
        else:
            y_p, y_s = _combine(l, x1, y_sorted, pos, mod_p4, mod_s, nfin, final=True)

    y_prompt = y_p.reshape(BATCH, SEQ, D_MODEL)
    y_sample = jnp.transpose(y_s.reshape(ns, nb, D_MODEL), (1, 0, 2))
    return (y_prompt, y_sample, jnp.stack(pool_p), jnp.stack(pool_s), jnp.stack(conf_p), jnp.stack(conf_s),
            jnp.stack(sc_p), jnp.stack(sc_s), jnp.stack(v_s))
```

```python
import functools

import jax
import jax.numpy as jnp
from jax import lax
from jax.experimental import pallas as pl
from jax.experimental.pallas import tpu as pltpu

D_MODEL = 2048
BATCH = 4
SEQ = 2048
DEPTH = 2
DEC_BATCH = 128
DEC_SEQ = 4
PAST_LEN = 16384
GROUP_WIDTH = 512
POOL_WINDOWS = (2, 4, 8, 16)
POOL_MAX = 16
POOL_CH = 128
SGU_HEADS = 4
SGU_CHUNK = 128
CONF_KERNEL = 31
SC_KERNEL = 3
IN_COLS = 8 * GROUP_WIDTH
N_EXPERTS = 32
TOP_K = 4
D_EXPERT = 2048
SWIGLU_ALPHA = 1.702
SWIGLU_LIMIT = 7.0
EPS = 1e-5

T_PROMPT = BATCH * SEQ
T_SAMPLE = DEC_BATCH * DEC_SEQ
T_ALL = T_PROMPT + T_SAMPLE

LANES = 128
BF16_ROWS = 16
VMEM_LIMIT = 56 * 1024 * 1024

A_TM = 1024
A_TN = 512
A_STEPS = IN_COLS // A_TN
POOL_HALO = 16
CONF_HALO = 32
SC_HALO = 8
CONV_CHUNK = 256
CONV_SUB = 64

B_TM = 512
N_BLOCKS = T_ALL // B_TM
B_PROMPT_BLOCKS = T_PROMPT // B_TM
SEG = BF16_ROWS
SORT_TM = 512
BLOCK_ROWS = -(-(B_TM * TOP_K + N_EXPERTS * (SEG - 1)) // SORT_TM) * SORT_TM
SORT_CHUNKS = BLOCK_ROWS // SORT_TM
TAIL_TM = 256
GATE_TERMS = 3
SIDE_ID0 = GATE_TERMS * TOP_K
H_EXT = D_MODEL + LANES

MOE_TM = 256
MOE_MT = 8
MOE_TN = 512
TILE_SEGS = MOE_TM // SEG
MOE_TILES = (T_ALL * TOP_K + N_BLOCKS * N_EXPERTS * (SEG - 1) + N_EXPERTS * (MOE_TM - 1)) // MOE_TM
MOE_ITEMS = (MOE_TILES + N_EXPERTS * (MOE_MT - 1)) // MOE_MT

_f32 = jnp.float32
_bf16 = jnp.bfloat16


def _rms(x):
    return x * lax.rsqrt(jnp.mean(x * x, axis=-1, keepdims=True) + EPS)


def _ln(x):
    xc = x - jnp.mean(x, axis=-1, keepdims=True)
    return xc * lax.rsqrt(jnp.mean(xc * xc, axis=-1, keepdims=True) + EPS)


def _gelu(x):
    return 0.5 * x * (1.0 + lax.erf(x * (0.5 ** 0.5)))


def _slab_mul(x, m):
    r, c = x.shape
    return (x.reshape(r // 128, 128, c) * m[None]).reshape(r, c)


def _slab_add(x, m):
    r, c = x.shape
    return (x.reshape(r // 128, 128, c) + m[None]).reshape(r, c)


ADA_TN = 1024
ADA_ROWS = DEC_BATCH + 8


def _ada_body(c_ref, w_ref, b_ref, os_ref, op_ref):
    c = c_ref[...]
    a = (c * jax.nn.sigmoid(c)).astype(_bf16)
    r = jnp.dot(a, w_ref[...].astype(_bf16), preferred_element_type=_f32) + b_ref[...]
    os_ref[...] = r[:DEC_BATCH]
    op_ref[...] = r[DEC_BATCH:]


def _ada(c_all, w_ada, b_ada):
    nj = 6 * D_MODEL // ADA_TN
    return pl.pallas_call(
        _ada_body,
        grid=(DEPTH, nj),
        in_specs=[
            pl.BlockSpec((ADA_ROWS, D_MODEL), lambda l, j: (0, 0)),
            pl.BlockSpec((None, D_MODEL, ADA_TN), lambda l, j: (l, 0, j)),
            pl.BlockSpec((None, 1, ADA_TN), lambda l, j: (l, 0, j)),
        ],
        out_specs=[
            pl.BlockSpec((None, DEC_BATCH, ADA_TN), lambda l, j: (l, 0, j)),
            pl.BlockSpec((None, 8, ADA_TN), lambda l, j: (l, 0, j)),
        ],
        out_shape=[
            jax.ShapeDtypeStruct((DEPTH, DEC_BATCH, 6 * D_MODEL), _f32),
            jax.ShapeDtypeStruct((DEPTH, 8, 6 * D_MODEL), _f32),
        ],
        compiler_params=pltpu.CompilerParams(
            dimension_semantics=("arbitrary", "arbitrary"), vmem_limit_bytes=VMEM_LIMIT),
        name="ada_mod",
    )(c_all, w_ada, b_ada.reshape(DEPTH, 1, 6 * D_MODEL))


def _a_prompt_body(x_ref, sh1_ref, sc1_ref, nmix_ref, w_ref, poolw_ref, pscale_ref, sgun_ref, sguw_ref,
                   sgub_ref, cw_ref, cb_ref, clg_ref, clb_ref, scw_ref, onorm_ref,
                   y_ref, pool_o, conf_o, sc_o,
                   h_ref, bufa, bufb, exta, extc, extz, shifted):
    i = pl.program_id(1)
    n = pl.program_id(2)
    tm = A_TM
    gw = GROUP_WIDTH

    @pl.when(n == 0)
    def _():
        hn = _rms(x_ref[...]) * nmix_ref[...]
        h_ref[...] = (hn * (1.0 + sc1_ref[...]) + sh1_ref[...]).astype(_bf16)

    def proj():
        return jnp.dot(h_ref[...], w_ref[...], preferred_element_type=_f32)

    def emit(g, y):
        yn = _rms(y) * onorm_ref[:, g * gw:(g + 1) * gw]
        y_ref[:, g * gw:(g + 1) * gw] = yn.astype(_bf16)

    def carry_halo(ext, halo):
        @pl.when(i == 0)
        def _():
            ext[0:halo, :] = jnp.zeros((halo, gw), _f32)

        @pl.when(i > 0)
        def _():
            ext[0:halo, :] = ext[tm:tm + halo, :]

    @pl.when(n == 0)
    def _():
        a = proj()
        carry_halo(exta, POOL_HALO)
        exta[POOL_HALO:, :] = a
        pool_o[...] = a[tm - POOL_HALO:, :]
        pos = (i * tm + lax.broadcasted_iota(jnp.int32, (tm, 1), 0)).astype(_f32)
        for gi, w in enumerate(POOL_WINDOWS):
            c0 = gi * POOL_CH
            s = exta[POOL_HALO:POOL_HALO + tm, c0:c0 + POOL_CH]
            for j in range(1, w):
                s = s + exta[POOL_HALO - j:POOL_HALO - j + tm, c0:c0 + POOL_CH]
            cnt = jnp.minimum(float(w), pos + 1.0)
            pooled = s / cnt - a[:, c0:c0 + POOL_CH]
            yg = jnp.dot(pooled.astype(_bf16), poolw_ref[gi].astype(_bf16), preferred_element_type=_f32)
            bufb[:, c0:c0 + POOL_CH] = yg
        emit(0, bufb[...] * pscale_ref[...])

    @pl.when(n == 1)
    def _():
        bufa[...] = _gelu(proj())

    @pl.when(n == 2)
    def _():
        v = _ln(_gelu(proj())) * sgun_ref[...]
        vb = v.astype(_bf16)
        row = lax.broadcasted_iota(jnp.int32, (SGU_CHUNK, SGU_CHUNK), 0)
        col = lax.broadcasted_iota(jnp.int32, (SGU_CHUNK, SGU_CHUNK), 1)
        for g in range(SGU_HEADS):
            c0 = g * SGU_CHUNK
            wg = jnp.where(row >= col, sguw_ref[g], 0.0).astype(_bf16)
            for c in range(tm // SGU_CHUNK):
                r0 = c * SGU_CHUNK
                m = jnp.dot(wg, vb[r0:r0 + SGU_CHUNK, c0:c0 + SGU_CHUNK], preferred_element_type=_f32)
                bufb[r0:r0 + SGU_CHUNK, c0:c0 + SGU_CHUNK] = m + sgub_ref[:, c0:c0 + SGU_CHUNK]
        emit(1, bufa[...] * bufb[...])

    @pl.when(n == 3)
    def _():
        bufa[...] = proj()

    @pl.when(n == 4)
    def _():
        g = bufa[...] * jax.nn.sigmoid(proj())
        carry_halo(extc, CONF_HALO)
        extc[CONF_HALO:, :] = g
        conf_o[...] = g[tm - CONF_HALO:, :]
        off = CONF_HALO - (CONF_KERNEL - 1)
        for ch in range(tm // CONV_CHUNK):
            base = ch * CONV_CHUNK
            for r in range(8):
                nr = CONV_CHUNK + 8 * ((CONF_HALO - r) // 8)
                shifted[r, 0:nr, :] = extc[base + r:base + r + nr, :]

            def sub(s, carry, base=base):
                i0 = s * CONV_SUB
                acc = jnp.broadcast_to(cb_ref[...], (CONV_SUB, gw))
                for k in range(CONF_KERNEL):
                    q, r = divmod(off + k, 8)
                    acc = acc + cw_ref[k:k + 1, :] * shifted[r, pl.ds(pl.multiple_of(i0 + 8 * q, 8), CONV_SUB), :]
                bufb[pl.ds(pl.multiple_of(base + i0, CONV_SUB), CONV_SUB), :] = acc
                return carry
            lax.fori_loop(0, CONV_CHUNK // CONV_SUB, sub, 0)
        yn = _ln(bufb[...]) * clg_ref[...] + clb_ref[...]
        emit(2, yn * jax.nn.sigmoid(yn))

    @pl.when(n == 5)
    def _():
        bufa[...] = proj()

    @pl.when(n == 6)
    def _():
        bufb[...] = proj()

    @pl.when(n == 7)
    def _():
        z = bufb[...] * proj()
        carry_halo(extz, SC_HALO)
        extz[SC_HALO:, :] = z
        sc_o[...] = z[tm - SC_HALO:, :]
        off = SC_HALO - (SC_KERNEL - 1)
        acc = scw_ref[0:1, :] * extz[off:off + tm, :]
        for k in range(1, SC_KERNEL):
            acc = acc + scw_ref[k:k + 1, :] * extz[off + k:off + k + tm, :]
        emit(3, bufa[...] * acc)


def _layer_spec(shape, l, nd):
    zeros = (0,) * len(shape)
    if nd == 1:
        return pl.BlockSpec((None,) + shape, lambda a: (l,) + zeros)
    if nd == 2:
        return pl.BlockSpec((None,) + shape, lambda a, b: (l,) + zeros)
    return pl.BlockSpec((None,) + shape, lambda a, b, c: (l,) + zeros)


def _mixer_params(l, p, nd):
    gw = GROUP_WIDTH
    arrs = [p["cw"], p["cb"], p["clg"], p["clb"], p["scw"], p["onorm"]]
    shapes = [(CONF_KERNEL, gw), (1, gw), (1, gw), (1, gw), (SC_KERNEL, gw), (1, D_MODEL)]
    return arrs, [_layer_spec(s, l, nd) for s in shapes]


def _a_prompt(l, x_all, mod_p4, w_in_b, p):
    gw = GROUP_WIDTH
    ni = SEQ // A_TM
    marrs, mspecs = _mixer_params(l, p, 3)
    in_specs = [
        pl.BlockSpec((A_TM, D_MODEL), lambda b, i, n: (b * ni + i, 0)),
        pl.BlockSpec((None, None, 1, D_MODEL), lambda b, i, n: (l, b, 0, 0)),
        pl.BlockSpec((None, None, 1, D_MODEL), lambda b, i, n: (l, b, 0, 1)),
        _layer_spec((1, D_MODEL), l, 3),
        pl.BlockSpec((None, D_MODEL, A_TN), lambda b, i, n: (l, 0, n)),
        _layer_spec((len(POOL_WINDOWS), POOL_CH, POOL_CH), l, 3),
        _layer_spec((1, gw), l, 3),
        _layer_spec((1, gw), l, 3),
        _layer_spec((SGU_HEADS, SGU_CHUNK, SGU_CHUNK), l, 3),
        _layer_spec((SGU_CHUNK, gw), l, 3),
    ] + mspecs
    out_specs = [
        pl.BlockSpec((A_TM, D_MODEL), lambda b, i, n: (b * ni + i, 0)),
        pl.BlockSpec((None, POOL_HALO, gw), lambda b, i, n: (b, 0, 0)),
        pl.BlockSpec((None, CONF_HALO, gw), lambda b, i, n: (b, 0, 0)),
        pl.BlockSpec((None, SC_HALO, gw), lambda b, i, n: (b, 0, 0)),
    ]
    out_shape = [
        jax.ShapeDtypeStruct((T_PROMPT, D_MODEL), _bf16),
        jax.ShapeDtypeStruct((BATCH, POOL_HALO, gw), _f32),
        jax.ShapeDtypeStruct((BATCH, CONF_HALO, gw), _f32),
        jax.ShapeDtypeStruct((BATCH, SC_HALO, gw), _f32),
    ]
    scratch = [
        pltpu.VMEM((A_TM, D_MODEL), _bf16),
        pltpu.VMEM((A_TM, gw), _f32),
        pltpu.VMEM((A_TM, gw), _f32),
        pltpu.VMEM((POOL_HALO + A_TM, gw), _f32),
        pltpu.VMEM((CONF_HALO + A_TM, gw), _f32),
        pltpu.VMEM((SC_HALO + A_TM, gw), _f32),
        pltpu.VMEM((8, CONV_CHUNK + CONF_HALO, gw), _f32),
    ]
    return pl.pallas_call(
        _a_prompt_body,
        grid=(BATCH, ni, A_STEPS),
        in_specs=in_specs,
        out_specs=out_specs,
        out_shape=out_shape,
        scratch_shapes=scratch,
        compiler_params=pltpu.CompilerParams(
            dimension_semantics=("arbitrary", "arbitrary", "arbitrary"), vmem_limit_bytes=VMEM_LIMIT),
        name="mixers_prompt",
    )(x_all, mod_p4, mod_p4, p["nmix"], w_in_b, p["poolw"], p["pscale"], p["sgun"], p["sguw"], p["sgub"], *marrs)


def _a_sample_body(x_ref, sh1_ref, sc1_ref, nmix_ref, w_ref, poolw_ref, pscale_ref, sgun_ref, sgwc_ref,
                   sgbc_ref, cw_ref, cb_ref, clg_ref, clb_ref, scw_ref, onorm_ref, pst_ref, cst_ref, sst_ref,
                   y_ref, a_o, g_o, z_o, v_o,
                   h_ref, bufa, bufb):
    n = pl.program_id(0)
    gw = GROUP_WIDTH
    nb = DEC_BATCH
    ns = DEC_SEQ

    @pl.when(n == 0)
    def _():
        hn = _rms(x_ref[...]) * nmix_ref[...]
        h_ref[...] = _slab_add(_slab_mul(hn, 1.0 + sc1_ref[...]), sh1_ref[...]).astype(_bf16)

    def proj():
        return jnp.dot(h_ref[...], w_ref[...], preferred_element_type=_f32)

    def emit(g, y):
        yn = _rms(y) * onorm_ref[:, g * gw:(g + 1) * gw]
        y_ref[:, g * gw:(g + 1) * gw] = yn.astype(_bf16)

    def slab(x, s):
        return x[s * nb:(s + 1) * nb, :]

    @pl.when(n == 0)
    def _():
        a = proj()
        a_o[...] = a.reshape(ns, nb, gw)
        npast = POOL_MAX - 1
        for s in range(ns):
            for gi, w in enumerate(POOL_WINDOWS):
                c0 = gi * POOL_CH
                acc = None
                for j in range(w):
                    e = npast + s - j
                    term = pst_ref[e, :, c0:c0 + POOL_CH] if e < npast else slab(a, e - npast)[:, c0:c0 + POOL_CH]
                    acc = term if acc is None else acc + term
                cnt = min(float(w), PAST_LEN + s + 1.0)
                pooled = acc / cnt - slab(a, s)[:, c0:c0 + POOL_CH]
                bufb[s * nb:(s + 1) * nb, c0:c0 + POOL_CH] = pooled
        pooled = bufb[...].astype(_bf16)
        for gi in range(len(POOL_WINDOWS)):
            c0 = gi * POOL_CH
            bufa[:, c0:c0 + POOL_CH] = jnp.dot(pooled[:, c0:c0 + POOL_CH], poolw_ref[gi].astype(_bf16),
                                               preferred_element_type=_f32)
        emit(0, bufa[...] * pscale_ref[...])

    @pl.when(n == 1)
    def _():
        bufa[...] = _gelu(proj())

    @pl.when(n == 2)
    def _():
        v = _ln(_gelu(proj())) * sgun_ref[...]
        v_o[...] = v.reshape(ns, nb, gw)
        for t in range(ns):
            m = jnp.broadcast_to(sgbc_ref[t:t + 1, :], (nb, gw))
            for s in range(t + 1):
                m = m + sgwc_ref[t * ns + s:t * ns + s + 1, :] * slab(v, s)
            bufb[t * nb:(t + 1) * nb, :] = m
        emit(1, bufa[...] * bufb[...])

    @pl.when(n == 3)
    def _():
        bufa[...] = proj()

    @pl.when(n == 4)
    def _():
        g = bufa[...] * jax.nn.sigmoid(proj())
        g_o[...] = g.reshape(ns, nb, gw)
        npast = CONF_KERNEL - 1
        for s in range(ns):
            acc = jnp.broadcast_to(cb_ref[...], (nb, gw))
            for k in range(CONF_KERNEL):
                e = s + k
                term = cst_ref[e] if e < npast else slab(g, e - npast)
                acc = acc + cw_ref[k:k + 1, :] * term
            bufb[s * nb:(s + 1) * nb, :] = acc
        yn = _ln(bufb[...]) * clg_ref[...] + clb_ref[...]
        emit(2, yn * jax.nn.sigmoid(yn))

    @pl.when(n == 5)
    def _():
        bufa[...] = proj()

    @pl.when(n == 6)
    def _():
        bufb[...] = proj()

    @pl.when(n == 7)
    def _():
        z = bufb[...] * proj()
        z_o[...] = z.reshape(ns, nb, gw)
        npast = SC_KERNEL - 1
        for s in range(ns):
            acc = None
            for k in range(SC_KERNEL):
                e = s + k
                term = scw_ref[k:k + 1, :] * (sst_ref[e] if e < npast else slab(z, e - npast))
                acc = term if acc is None else acc + term
            bufb[s * nb:(s + 1) * nb, :] = acc
        emit(3, bufa[...] * bufb[...])


def _a_sample(l, x_all, x_blk, mod_s, w_in_b, p, pool_t, conf_t, sc_t):
    gw = GROUP_WIDTH
    ns, nb = DEC_SEQ, DEC_BATCH
    marrs, mspecs = _mixer_params(l, p, 1)
    in_specs = [
        pl.BlockSpec((T_SAMPLE, D_MODEL), lambda n: (x_blk, 0)),
        pl.BlockSpec((None, nb, D_MODEL), lambda n: (l, 0, 0)),
        pl.BlockSpec((None, nb, D_MODEL), lambda n: (l, 0, 1)),
        _layer_spec((1, D_MODEL), l, 1),
        pl.BlockSpec((None, D_MODEL, A_TN), lambda n: (l, 0, n)),
        _layer_spec((len(POOL_WINDOWS), POOL_CH, POOL_CH), l, 1),
        _layer_spec((1, gw), l, 1),
        _layer_spec((1, gw), l, 1),
        _layer_spec((ns * ns, gw), l, 1),
        _layer_spec((ns, gw), l, 1),
    ] + mspecs + [
        _layer_spec((POOL_MAX - 1, nb, gw), l, 1),
        _layer_spec((CONF_KERNEL - 1, nb, gw), l, 1),
        _layer_spec((SC_KERNEL - 1, nb, gw), l, 1),
    ]
    out_specs = [pl.BlockSpec((T_SAMPLE, D_MODEL), lambda n: (0, 0))] + [
        pl.BlockSpec((ns, nb, gw), lambda n: (0, 0, 0)) for _ in range(4)]
    out_shape = [jax.ShapeDtypeStruct((T_SAMPLE, D_MODEL), _bf16)] + [
        jax.ShapeDtypeStruct((ns, nb, gw), _f32) for _ in range(4)]
    scratch = [
        pltpu.VMEM((T_SAMPLE, D_MODEL), _bf16),
        pltpu.VMEM((T_SAMPLE, gw), _f32),
        pltpu.VMEM((T_SAMPLE, gw), _f32),
    ]
    return pl.pallas_call(
        _a_sample_body,
        grid=(A_STEPS,),
        in_specs=in_specs,
        out_specs=out_specs,
        out_shape=out_shape,
        scratch_shapes=scratch,
        compiler_params=pltpu.CompilerParams(
            dimension_semantics=("arbitrary",), vmem_limit_bytes=VMEM_LIMIT),
        name="mixers_sample",
    )(x_all, mod_s, mod_s, p["nmix"], w_in_b, p["poolw"], p["pscale"], p["sgun"], p["sgwc"], p["sgbc"],
      *marrs, pool_t, conf_t, sc_t)


def _pick_mod(is_prompt, mp_ref, ms_ref):
    return jnp.where(is_prompt, jnp.broadcast_to(mp_ref[...], (DEC_BATCH, D_MODEL)), ms_ref[...])


def _b_body(xp_ref, xs_ref, yp_ref, ys_ref, w_ref, mpg1, mpsh2, mpsc2, msg1, mssh2, mssc2, nffn_ref, rw_ref, rb_ref,
            x1_ref, h2_ref, pos_ref, post_ref, meta_ref):
    i = pl.program_id(0)
    isp = i < B_PROMPT_BLOCKS
    ycat = jnp.where(isp, yp_ref[...], ys_ref[...])
    o = jnp.dot(ycat, w_ref[...], preferred_element_type=_f32)
    x1 = jnp.where(isp, xp_ref[...], xs_ref[...]) + _slab_mul(o, _pick_mod(isp, mpg1, msg1))
    x1_ref[...] = x1
    hn = _rms(x1) * nffn_ref[...]
    h2 = _slab_add(_slab_mul(hn, 1.0 + _pick_mod(isp, mpsc2, mssc2)), _pick_mod(isp, mpsh2, mssh2))
    hi = h2.astype(_bf16)
    h2_ref[:, 0:D_MODEL] = hi
    lo = (h2 - hi.astype(_f32)).astype(_bf16)
    l1 = jnp.dot(hi, rw_ref[...], preferred_element_type=_f32)
    l2 = jnp.dot(lo, rw_ref[:, 0:LANES], preferred_element_type=_f32)
    logits = l1[:, 0:LANES] + l1[:, LANES:] + l2 + rb_ref[...]
    lane = lax.broadcasted_iota(jnp.int32, (B_TM, LANES), 1).astype(_f32)
    neg = jnp.float32(-jnp.inf)
    cur = jnp.where(lane < N_EXPERTS, logits, neg)
    vals, ids = [], []
    for _ in range(TOP_K):
        m = jnp.max(cur, axis=-1, keepdims=True)
        ix = jnp.min(jnp.where(cur == m, lane, float(LANES)), axis=-1, keepdims=True)
        vals.append(m)
        ids.append(ix)
        cur = jnp.where(lane == ix, neg, cur)
    es = [jnp.exp(v - vals[0]) for v in vals]
    tot = es[0] + es[1] + es[2] + es[3]
    side = jnp.zeros((B_TM, LANES), _f32)
    for k in range(TOP_K):
        g = es[k] / tot
        g0 = g.astype(_bf16).astype(_f32)
        g1 = (g - g0).astype(_bf16).astype(_f32)
        for j, term in enumerate((g0, g1, g - g0 - g1)):
            side = jnp.where(lane == float(GATE_TERMS * k + j), term, side)
        side = jnp.where(lane == float(SIDE_ID0 + k), ids[k], side)
    h2_ref[:, D_MODEL:] = side.astype(_bf16)

    hit = [lane == ix for ix in ids]
    chosen = jnp.zeros((B_TM, LANES), _f32)
    for h in hit:
        chosen = jnp.where(h, 1.0, chosen)
    cnt = jnp.sum(chosen, axis=0, keepdims=True)
    seg_rows = jnp.floor((cnt + (SEG - 1.0)) * (1.0 / SEG)) * SEG
    er = lax.broadcasted_iota(jnp.int32, (LANES, LANES), 0)
    ec = lax.broadcasted_iota(jnp.int32, (LANES, LANES), 1)
    before = jnp.where(er < ec, 1.0, 0.0).astype(_bf16)
    seg_start = jnp.dot(jnp.broadcast_to(seg_rows, (8, LANES)).astype(_bf16), before,
                        preferred_element_type=_f32)[0:1, :]
    tr = lax.broadcasted_iota(jnp.int32, (B_TM, B_TM), 0)
    tc = lax.broadcasted_iota(jnp.int32, (B_TM, B_TM), 1)
    earlier = jnp.where(tr > tc, 1.0, 0.0).astype(_bf16)
    rank = jnp.dot(earlier, chosen.astype(_bf16), preferred_element_type=_f32)
    base = seg_start + rank
    pos = jnp.zeros((B_TM, LANES), _f32)
    for k in range(TOP_K):
        pk = jnp.sum(jnp.where(hit[k], base, 0.0), axis=-1, keepdims=True)
        pos = jnp.where(lane == float(k), pk, pos)
    pos_ref[...] = pos
    post_ref[...] = jnp.transpose(pos)[0:8, :]
    sub = lax.broadcasted_iota(jnp.int32, (8, LANES), 0)
    meta = jnp.where(sub == 0, jnp.broadcast_to(seg_start, (8, LANES)),
                     jnp.where(sub == 1, jnp.broadcast_to(seg_rows, (8, LANES)), 0.0))
    meta_ref[...] = meta.astype(jnp.int32)


def _stage_b(l, x_p, x_s, xs_blk, ycat_p, ycat_s, w_out_b, mod_p4, mod_s, p):
    def pb(i):
        return jnp.minimum(i // (SEQ // B_TM), BATCH - 1)

    def mp_spec(j):
        return pl.BlockSpec((None, None, 1, D_MODEL), lambda i: (l, pb(i), 0, j))

    def ms_spec(j):
        return pl.BlockSpec((None, DEC_BATCH, D_MODEL), lambda i: (l, 0, j))

    row_spec = pl.BlockSpec((B_TM, D_MODEL), lambda i: (i, 0))
    in_specs = [
        pl.BlockSpec((B_TM, D_MODEL), lambda i: (jnp.minimum(i, B_PROMPT_BLOCKS - 1), 0)),
        pl.BlockSpec((T_SAMPLE, D_MODEL), lambda i: (xs_blk, 0), pipeline_mode=pl.Buffered(1)),
        pl.BlockSpec((B_TM, D_MODEL), lambda i: (jnp.minimum(i, B_PROMPT_BLOCKS - 1), 0)),
        pl.BlockSpec((T_SAMPLE, D_MODEL), lambda i: (0, 0), pipeline_mode=pl.Buffered(1)),
        pl.BlockSpec((None, D_MODEL, D_MODEL), lambda i: (l, 0, 0), pipeline_mode=pl.Buffered(1)),
        mp_spec(2), mp_spec(3), mp_spec(4), ms_spec(2), ms_spec(3), ms_spec(4),
        _layer_spec((1, D_MODEL), l, 1),
        _layer_spec((D_MODEL, 2 * LANES), l, 1),
        _layer_spec((1, LANES), l, 1),
    ]
    lane_spec = pl.BlockSpec((B_TM, LANES), lambda i: (i, 0))
    return pl.pallas_call(
        _b_body,
        grid=(N_BLOCKS,),
        in_specs=in_specs,
        out_specs=[row_spec, pl.BlockSpec((B_TM, H_EXT), lambda i: (i, 0)), lane_spec,
                   pl.BlockSpec((None, 8, B_TM), lambda i: (i, 0, 0)),
                   pl.BlockSpec((None, 8, LANES), lambda i: (i, 0, 0))],
        out_shape=[
            jax.ShapeDtypeStruct((T_ALL, D_MODEL), _f32),
            jax.ShapeDtypeStruct((T_ALL, H_EXT), _bf16),
            jax.ShapeDtypeStruct((T_ALL, LANES), _f32),
            jax.ShapeDtypeStruct((N_BLOCKS, 8, B_TM), _f32),
            jax.ShapeDtypeStruct((N_BLOCKS, 8, LANES), jnp.int32),
        ],
        compiler_params=pltpu.CompilerParams(
            dimension_semantics=("arbitrary",), vmem_limit_bytes=VMEM_LIMIT),
        name="outproj_router",
    )(x_p, x_s, ycat_p, ycat_s, w_out_b, mod_p4, mod_p4, mod_p4, mod_s, mod_s, mod_s, p["nffn"], p["rw"], p["rb"])


def _sort_body(used_ref, post_ref, h_ref, o_ref, so_ref):
    used = used_ref[pl.program_id(0)]

    def rows(r0, nr):
        j = (r0 + lax.broadcasted_iota(jnp.int32, (nr, B_TM), 0)).astype(_f32)
        sel = jnp.zeros((nr, B_TM), _f32)
        for k in range(TOP_K):
            sel = jnp.where(j == post_ref[k:k + 1, :], 1.0, sel)
        res = jnp.dot(sel.astype(_bf16), h_ref[...], preferred_element_type=_f32)
        o_ref[r0:r0 + nr, :] = res[:, 0:D_MODEL].astype(_bf16)
        so_ref[r0:r0 + nr, :] = res[:, D_MODEL:]

    for c in range(B_TM * TOP_K // SORT_TM):
        rows(c * SORT_TM, SORT_TM)
    for r0 in range(B_TM * TOP_K, BLOCK_ROWS, TAIL_TM):
        @pl.when(used > r0)
        def _(r0=r0):
            rows(r0, TAIL_TM)

        @pl.when(used <= r0)
        def _(r0=r0):
            o_ref[r0:r0 + TAIL_TM, :] = jnp.zeros((TAIL_TM, D_MODEL), _bf16)
            so_ref[r0:r0 + TAIL_TM, :] = jnp.zeros((TAIL_TM, LANES), _f32)


def _local_sort(used, post, h_ext):
    grid_spec = pltpu.PrefetchScalarGridSpec(
        num_scalar_prefetch=1,
        grid=(N_BLOCKS,),
        in_specs=[
            pl.BlockSpec((None, 8, B_TM), lambda b, u: (b, 0, 0)),
            pl.BlockSpec((B_TM, H_EXT), lambda b, u: (b, 0)),
        ],
        out_specs=[pl.BlockSpec((BLOCK_ROWS, D_MODEL), lambda b, u: (b, 0)),
                   pl.BlockSpec((BLOCK_ROWS, LANES), lambda b, u: (b, 0))],
    )
    return pl.pallas_call(
        _sort_body,
        grid_spec=grid_spec,
        out_shape=[jax.ShapeDtypeStruct((N_BLOCKS * BLOCK_ROWS, D_MODEL), _bf16),
                   jax.ShapeDtypeStruct((N_BLOCKS * BLOCK_ROWS, LANES), _f32)],
        compiler_params=pltpu.CompilerParams(
            dimension_semantics=("arbitrary",), vmem_limit_bytes=VMEM_LIMIT),
        name="local_sort",
    )(used, post, h_ext)


BIG_TM = 2 * MOE_TM
SMALL_TM = MOE_TM // 2
MOE_DN = 1024
STREAM_UP_STEPS = D_EXPERT // MOE_TN
STREAM_DOWN_STEPS = D_MODEL // MOE_DN
STREAM_STEPS = STREAM_UP_STEPS + STREAM_DOWN_STEPS
MOE_WSLOTS = 3


def _moe_stream_body(ie_ref, ilo_ref, int_ref, start_ref, rows_ref, used_ref, nitems_ref,
                     wup_hbm, wdn_hbm, bup_hbm, bdn_hbm, x_hbm, g_hbm, y_hbm,
                     wbuf, xbuf, hbuf, gbuf, bup, bdn, seg_src, nseg, pending,
                     sem_w, sem_b, sem_x, sem_y, *, l):
    tm, tn = MOE_TM, MOE_TN
    n_items = nitems_ref[0]
    n_steps = n_items * STREAM_STEPS

    def w_half(src_ref, e, col0, slot, half):
        return pltpu.make_async_copy(src_ref.at[l, e, :, pl.ds(col0, tn)],
                                     wbuf.at[slot, :, pl.ds(half * tn, tn)], sem_w.at[slot])

    def issue_w(s):
        e = ie_ref[s // STREAM_STEPS]
        n = s % STREAM_STEPS
        slot = s % MOE_WSLOTS
        for c in range(STREAM_UP_STEPS):
            @pl.when(n == c)
            def _(c=c):
                w_half(wup_hbm, e, c * tn, slot, 0).start()
                w_half(wup_hbm, e, D_EXPERT + c * tn, slot, 1).start()
        for c in range(STREAM_DOWN_STEPS):
            @pl.when(n == STREAM_UP_STEPS + c)
            def _(c=c):
                w_half(wdn_hbm, e, c * MOE_DN, slot, 0).start()
                w_half(wdn_hbm, e, c * MOE_DN + tn, slot, 1).start()

    def wait_w(s):
        slot = s % MOE_WSLOTS
        for half in range(2):
            w_half(wdn_hbm, 0, 0, slot, half).wait()

    def b_copies(item):
        e = ie_ref[item]
        par = item % 2
        return (pltpu.make_async_copy(bup_hbm.at[l, e], bup.at[par], sem_b.at[par]),
                pltpu.make_async_copy(bdn_hbm.at[l, e], bdn.at[par], sem_b.at[par]))

    def x_copy(p, src, dst):
        return pltpu.make_async_copy(x_hbm.at[pl.ds(src, SEG)], xbuf.at[p, pl.ds(dst, SEG)],
                                     sem_x.at[p * MOE_MT + dst // tm])

    def g_copy(p, src, dst):
        return pltpu.make_async_copy(g_hbm.at[pl.ds(src, SEG)], gbuf.at[p, pl.ds(dst, SEG)],
                                     sem_x.at[p * MOE_MT + dst // tm])

    def y_copy(p, src, dst):
        return pltpu.make_async_copy(xbuf.at[p, pl.ds(src, SEG)], y_hbm.at[pl.ds(dst, SEG)], sem_y.at[p])

    def drain(p):
        def w(_, carry):
            y_copy(p, 0, 0).wait()
            return carry
        lax.fori_loop(0, pending[p], w, 0)
        pending[p] = 0

    def tile_pieces(p, t):
        return jnp.clip(nseg[p] - t * TILE_SEGS, 0, TILE_SEGS)

    def plan(rows):
        n_big = rows // BIG_TM
        rem = rows - n_big * BIG_TM
        round_up = rem > BIG_TM - SMALL_TM
        n_big = n_big + jnp.where(round_up, 1, 0)
        rem = jnp.where(round_up, 0, rem)
        has_mid = rem > SMALL_TM
        has_small = (rem > 0) & ((rem <= SMALL_TM) | (rem > tm))
        off_mid = n_big * BIG_TM
        off_small = off_mid + jnp.where(has_mid, tm, 0)
        return n_big, has_mid, has_small, off_mid, off_small

    def load_item(item):
        p = item % 2
        e = ie_ref[item]
        lo = ilo_ref[item]
        nt = int_ref[item]
        hi = lo + nt * tm

        def per_block(b, q0):
            rows = rows_ref[b * N_EXPERTS + e]
            start = start_ref[b * N_EXPERTS + e]

            def per_piece(g, cc):
                src = pl.multiple_of(b * BLOCK_ROWS + start + g * SEG, SEG)
                dst = pl.multiple_of(q0 + g * SEG - lo, SEG)
                seg_src[p * (MOE_MT * TILE_SEGS) + dst // SEG] = src
                x_copy(p, src, dst).start()
                g_copy(p, src, dst).start()
                return cc
            pieces = rows // SEG
            g_lo = jnp.minimum((jnp.maximum(lo - q0, 0) + SEG - 1) // SEG, pieces)
            g_hi = jnp.minimum((jnp.maximum(hi - q0, 0) + SEG - 1) // SEG, pieces)
            lax.fori_loop(g_lo, g_hi, per_piece, 0)
            return q0 + rows
        end = lax.fori_loop(0, N_BLOCKS, per_block, 0)
        ns = (jnp.minimum(end, hi) - lo) // SEG
        nseg[p] = ns

        def zero_piece(g, cc):
            r = pl.multiple_of(g * SEG, SEG)
            xbuf[p, pl.ds(r, SEG), :] = jnp.zeros((SEG, D_MODEL), _bf16)
            gbuf[p, pl.ds(r, SEG), :] = jnp.zeros((SEG, LANES), _f32)
            return cc
        n_big, has_mid, has_small, _, off_small = plan(ns * SEG)
        computed = off_small + jnp.where(has_small, SMALL_TM, 0)
        lax.fori_loop(ns, computed // SEG, zero_piece, 0)

    pending[0] = 0
    pending[1] = 0
    issue_w(0)
    issue_w(1)
    for cp in b_copies(0):
        cp.start()
    load_item(0)

    def step(s, carry):
        item = s // STREAM_STEPS
        n = s % STREAM_STEPS
        e = ie_ref[item]
        par = item % 2
        slot = s % MOE_WSLOTS

        @pl.when(s + 2 < n_steps)
        def _():
            issue_w(s + 2)
        wait_w(s)

        @pl.when(n == 0)
        def _():
            for cp in b_copies(item):
                cp.wait()

        @pl.when((n == 1) & (item + 1 < n_items))
        def _():
            for cp in b_copies(item + 1):
                cp.start()
            drain(1 - par)
            load_item(item + 1)

        n_big, has_mid, has_small, off_mid, off_small = plan(nseg[par] * SEG)

        def up_rows(c, r0, nrows):
            if c == 0:
                for u in range(max(nrows // tm, 1)):
                    sub = r0 // tm + u

                    def w(_, c3, sub=sub):
                        x_copy(par, 0, pl.multiple_of(sub * tm, tm)).wait()
                        g_copy(par, 0, pl.multiple_of(sub * tm, tm)).wait()
                        return c3
                    lax.fori_loop(0, tile_pieces(par, sub), w, 0)
            up = lax.dot_general(xbuf[par, pl.ds(r0, nrows), :], wbuf[slot], (((1,), (0,)), ((), ())),
                                 preferred_element_type=_f32)
            bg = bup[par, :, c * tn:(c + 1) * tn]
            bl = bup[par, :, D_EXPERT + c * tn:D_EXPERT + (c + 1) * tn]
            glu = jnp.minimum(up[:, 0:tn] + bg, SWIGLU_LIMIT)
            lin = jnp.clip(up[:, tn:] + bl, -SWIGLU_LIMIT, SWIGLU_LIMIT)
            act = glu * jax.nn.sigmoid(SWIGLU_ALPHA * glu) * (lin + 1.0)
            hbuf[pl.ds(r0, nrows), c * tn:(c + 1) * tn] = act.astype(_bf16)

        def down_rows(c, r0, nrows):
            y = (lax.dot_general(hbuf[pl.ds(r0, nrows), :], wbuf[slot], (((1,), (0,)), ((), ())),
                                 preferred_element_type=_f32)
                 + bdn[par, :, c * MOE_DN:(c + 1) * MOE_DN])
            g = gbuf[par, pl.ds(r0, nrows), :]
            ef = e.astype(_f32)
            gate = jnp.zeros((nrows, 1), _f32)
            for k in range(TOP_K):
                gk = g[:, GATE_TERMS * k:GATE_TERMS * k + 1]
                for j in range(1, GATE_TERMS):
                    gk = gk + g[:, GATE_TERMS * k + j:GATE_TERMS * k + j + 1]
                gate = jnp.where(g[:, SIDE_ID0 + k:SIDE_ID0 + k + 1] == ef, gk, gate)
            xbuf[par, pl.ds(r0, nrows), c * MOE_DN:(c + 1) * MOE_DN] = (y * gate).astype(_bf16)
            if c == STREAM_DOWN_STEPS - 1:
                for q in range(nrows // SEG):
                    piece = r0 // SEG + q

                    @pl.when(piece < nseg[par])
                    def _(q=q, piece=piece):
                        dst = seg_src[par * (MOE_MT * TILE_SEGS) + piece]
                        y_copy(par, pl.multiple_of(r0 + q * SEG, SEG), pl.multiple_of(dst, SEG)).start()

        def all_rows(fn, c):
            def big(t, cc):
                fn(c, pl.multiple_of(t * BIG_TM, BIG_TM), BIG_TM)
                return cc
            lax.fori_loop(0, n_big, big, 0)

            @pl.when(has_mid)
            def _():
                fn(c, pl.multiple_of(off_mid, tm), tm)

            @pl.when(has_small)
            def _():
                fn(c, pl.multiple_of(off_small, SMALL_TM), SMALL_TM)

        for c in range(STREAM_UP_STEPS):
            @pl.when(n == c)
            def _(c=c):
                all_rows(up_rows, c)

        for c in range(STREAM_DOWN_STEPS):
            @pl.when(n == STREAM_UP_STEPS + c)
            def _(c=c):
                all_rows(down_rows, c)
                if c == STREAM_DOWN_STEPS - 1:
                    pending[par] = nseg[par]
        return carry
    lax.fori_loop(0, n_steps, step, 0)

    drain(0)
    drain(1)
    hbuf[0:SEG, :] = jnp.zeros((SEG, D_EXPERT), _bf16)

    def zero_copy(row):
        return pltpu.make_async_copy(hbuf.at[pl.ds(0, SEG)], y_hbm.at[pl.ds(row, SEG)], sem_y.at[0])

    def zero_block(b, carry):
        def zstart(g, cc):
            zero_copy(pl.multiple_of(b * BLOCK_ROWS + g * SEG, SEG)).start()
            return cc

        def zwait(g, cc):
            zero_copy(0).wait()
            return cc
        first = used_ref[b] // SEG
        lax.fori_loop(first, BLOCK_ROWS // SEG, zstart, 0)
        lax.fori_loop(first, BLOCK_ROWS // SEG, zwait, 0)
        return carry
    lax.fori_loop(0, N_BLOCKS, zero_block, 0)


def _moe_stream(l, item_e, item_lo, item_nt, seg_start, seg_rows, used, n_items, w_up, b_up4, w_down, b_down4,
                x_sorted, g_sorted):
    any_spec = pl.BlockSpec(memory_space=pl.ANY)
    grid_spec = pltpu.PrefetchScalarGridSpec(
        num_scalar_prefetch=7,
        grid=(1,),
        in_specs=[any_spec] * 6,
        out_specs=any_spec,
        scratch_shapes=[
            pltpu.VMEM((MOE_WSLOTS, D_MODEL, 2 * MOE_TN), _f32),
            pltpu.VMEM((2, MOE_MT * MOE_TM, D_MODEL), _bf16),
            pltpu.VMEM((MOE_MT * MOE_TM, D_EXPERT), _bf16),
            pltpu.VMEM((2, MOE_MT * MOE_TM, LANES), _f32),
            pltpu.VMEM((2, 1, 2 * D_EXPERT), _f32),
            pltpu.VMEM((2, 1, D_MODEL), _f32),
            pltpu.SMEM((2 * MOE_MT * TILE_SEGS,), jnp.int32),
            pltpu.SMEM((2,), jnp.int32),
            pltpu.SMEM((2,), jnp.int32),
            pltpu.SemaphoreType.DMA((MOE_WSLOTS,)),
            pltpu.SemaphoreType.DMA((2,)),
            pltpu.SemaphoreType.DMA((2 * MOE_MT,)),
            pltpu.SemaphoreType.DMA((2,)),
        ],
    )
    return pl.pallas_call(
        functools.partial(_moe_stream_body, l=l),
        grid_spec=grid_spec,
        out_shape=jax.ShapeDtypeStruct((N_BLOCKS * BLOCK_ROWS, D_MODEL), _bf16),
        compiler_params=pltpu.CompilerParams(
            dimension_semantics=("arbitrary",), vmem_limit_bytes=60 * 1024 * 1024),
        name="moe_experts",
    )(item_e, item_lo, item_nt, seg_start, seg_rows, used, n_items, w_up, w_down, b_up4, b_down4, x_sorted,
      g_sorted)


def _work_items(meta):
    seg_start = meta[:, 0, :N_EXPERTS]
    seg_rows = meta[:, 1, :N_EXPERTS]
    rows_e = jnp.sum(seg_rows, axis=0)
    nt = (rows_e + MOE_TM - 1) // MOE_TM
    ipe = (nt + MOE_MT - 1) // MOE_MT
    cum = jnp.cumsum(ipe)
    first = cum - ipe
    total = cum[-1]
    ii = jnp.arange(MOE_ITEMS, dtype=jnp.int32)
    ii_c = jnp.minimum(ii, total - 1)
    e_i = jnp.minimum(jnp.sum((cum[None, :] <= ii_c[:, None]).astype(jnp.int32), axis=1), N_EXPERTS - 1)
    valid = ii < total
    j = ii - first[e_i]
    lo_i = jnp.where(valid, j * (MOE_MT * MOE_TM), 0)
    nt_i = jnp.where(valid, jnp.minimum(MOE_MT, nt[e_i] - j * MOE_MT), 0)
    used = jnp.sum(seg_rows, axis=1)
    i32 = jnp.int32
    return (e_i.astype(i32), lo_i.astype(i32), nt_i.astype(i32), seg_start.reshape(-1).astype(i32),
            seg_rows.reshape(-1).astype(i32), used.astype(i32), total.astype(i32).reshape(1))


def _c_body(x1_ref, y_ref, pos_ref, mpg2, msg2, nfin_ref, *rest, final):
    if final:
        outp_ref, outs_ref, sel_ref = rest
    else:
        out_ref, sel_ref = rest
    i = pl.program_id(0)
    isp = i < B_PROMPT_BLOCKS
    for c in range(SORT_CHUNKS):
        j = (c * SORT_TM + lax.broadcasted_iota(jnp.int32, (B_TM, SORT_TM), 1)).astype(_f32)
        sel = jnp.zeros((B_TM, SORT_TM), _f32)
        for k in range(TOP_K):
            sel = jnp.where(j == pos_ref[:, k:k + 1], 1.0, sel)
        sel_ref[:, c * SORT_TM:(c + 1) * SORT_TM] = sel.astype(_bf16)
    acc = jnp.dot(sel_ref[...], y_ref[...], preferred_element_type=_f32)
    x2 = x1_ref[...] + _slab_mul(acc, _pick_mod(isp, mpg2, msg2))
    if final:
        y = _rms(x2) * nfin_ref[...]

        @pl.when(isp)
        def _():
            outp_ref[...] = y

        @pl.when(jnp.logical_not(isp))
        def _():
            outs_ref[...] = y
    else:
        out_ref[...] = x2


def _combine(l, x1, y_sorted, pos, mod_p4, mod_s, nfin, final):
    def pb(i):
        return jnp.minimum(i // (SEQ // B_TM), BATCH - 1)

    row_spec = pl.BlockSpec((B_TM, D_MODEL), lambda i: (i, 0))
    lane_spec = pl.BlockSpec((B_TM, LANES), lambda i: (i, 0))
    if final:
        out_specs = [pl.BlockSpec((B_TM, D_MODEL), lambda i: (jnp.minimum(i, B_PROMPT_BLOCKS - 1), 0)),
                     pl.BlockSpec((T_SAMPLE, D_MODEL), lambda i: (0, 0))]
        out_shape = [jax.ShapeDtypeStruct((T_PROMPT, D_MODEL), _f32),
                     jax.ShapeDtypeStruct((T_SAMPLE, D_MODEL), _f32)]
    else:
        out_specs = row_spec
        out_shape = jax.ShapeDtypeStruct((T_ALL, D_MODEL), _f32)
    return pl.pallas_call(
        functools.partial(_c_body, final=final),
        grid=(N_BLOCKS,),
        in_specs=[
            row_spec,
            pl.BlockSpec((BLOCK_ROWS, D_MODEL), lambda i: (i, 0)),
            lane_spec,
            pl.BlockSpec((None, None, 1, D_MODEL), lambda i: (l, pb(i), 0, 5)),
            pl.BlockSpec((None, DEC_BATCH, D_MODEL), lambda i: (l, 0, 5)),
            pl.BlockSpec((1, D_MODEL), lambda i: (0, 0)),
        ],
        out_specs=out_specs,
        out_shape=out_shape,
        scratch_shapes=[pltpu.VMEM((B_TM, BLOCK_ROWS), _bf16)],
        compiler_params=pltpu.CompilerParams(
            dimension_semantics=("arbitrary",), vmem_limit_bytes=VMEM_LIMIT),
        name="moe_combine",
    )(x1, y_sorted, pos, mod_p4, mod_s, nfin)


def kernel(x_prompt, x_sample, c_prompt, c_sample, state_pool, state_conformer, state_shortconv, w_ada, b_ada, norm_mix, norm_ffn, norm_final, w_in, pool_w, pool_scale, sgu_norm, sgu_w, sgu_b, conf_w, conf_b, conf_ln_g, conf_ln_b, sc_w, out_norm, w_out, router_w, router_b, w_up, b_up, w_down, b_down):
    gw = GROUP_WIDTH
    ns, nb = DEC_SEQ, DEC_BATCH

    c_all = jnp.concatenate([c_sample, c_prompt, jnp.zeros((ADA_ROWS - nb - BATCH, D_MODEL), _f32)], axis=0)
    mod_s, mod_p8 = _ada(c_all, w_ada, b_ada)
    mod_p4 = mod_p8[:, :BATCH].reshape(DEPTH, BATCH, 1, 6 * D_MODEL)

    rw_hi = router_w.astype(_bf16)
    rw_lo = (router_w - rw_hi.astype(_f32)).astype(_bf16)
    lane_pad = ((0, 0), (0, 0), (0, LANES - N_EXPERTS))
    params = {
        "nmix": norm_mix.reshape(DEPTH, 1, D_MODEL),
        "nffn": norm_ffn.reshape(DEPTH, 1, D_MODEL),
        "poolw": pool_w,
        "pscale": pool_scale.reshape(DEPTH, 1, gw),
        "sgun": sgu_norm.reshape(DEPTH, 1, gw),
        "sguw": sgu_w,
        "sgub": jnp.repeat(jnp.swapaxes(sgu_b, 1, 2), SGU_CHUNK, axis=2),
        "sgwc": jnp.repeat(jnp.transpose(sgu_w[:, :, :ns, :ns], (0, 2, 3, 1)).reshape(DEPTH, ns * ns, SGU_HEADS),
                           SGU_CHUNK, axis=2),
        "sgbc": jnp.repeat(jnp.swapaxes(sgu_b[:, :, :ns], 1, 2), SGU_CHUNK, axis=2),
        "cw": conf_w,
        "cb": conf_b.reshape(DEPTH, 1, gw),
        "clg": conf_ln_g.reshape(DEPTH, 1, gw),
        "clb": conf_ln_b.reshape(DEPTH, 1, gw),
        "scw": sc_w,
        "onorm": out_norm.reshape(DEPTH, 1, D_MODEL),
        "rw": jnp.concatenate([jnp.pad(rw_hi, lane_pad), jnp.pad(rw_lo, lane_pad)], axis=2),
        "rb": jnp.pad(router_b, ((0, 0), (0, LANES - N_EXPERTS))).reshape(DEPTH, 1, LANES),
    }
    w_in_b = w_in.astype(_bf16)
    w_out_b = w_out.astype(_bf16)
    b_up4 = b_up.reshape(DEPTH, N_EXPERTS, 1, 2 * D_EXPERT)
    b_down4 = b_down.reshape(DEPTH, N_EXPERTS, 1, D_MODEL)
    nfin = norm_final.reshape(1, D_MODEL)

    pool_t = jnp.transpose(state_pool, (0, 2, 1, 3))
    conf_t = jnp.transpose(state_conformer, (0, 2, 1, 3))
    sc_t = jnp.transpose(state_shortconv, (0, 2, 1, 3))

    x_p = x_prompt.reshape(T_PROMPT, D_MODEL)
    x_s = jnp.transpose(x_sample, (1, 0, 2)).reshape(T_SAMPLE, D_MODEL)
    xs_blk = 0

    pool_p, conf_p, sc_p, pool_s, conf_s, sc_s, v_s = [], [], [], [], [], [], []
    for l in range(DEPTH):
        ycat_p, po, co, so = _a_prompt(l, x_p, mod_p4, w_in_b, params)
        ycat_s, a_new, g_new, z_new, v_new = _a_sample(l, x_s, xs_blk, mod_s, w_in_b, params, pool_t, conf_t, sc_t)
        pool_p.append(po[:, POOL_HALO - (POOL_MAX - 1):])
        conf_p.append(co[:, CONF_HALO - (CONF_KERNEL - 1):])
        sc_p.append(so[:, SC_HALO - (SC_KERNEL - 1):])
        pool_s.append(jnp.concatenate([state_pool[l][:, ns:], jnp.transpose(a_new, (1, 0, 2))], axis=1))
        conf_s.append(jnp.concatenate([state_conformer[l][:, ns:], jnp.transpose(g_new, (1, 0, 2))], axis=1))
        sc_s.append(jnp.transpose(z_new, (1, 0, 2))[:, ns - (SC_KERNEL - 1):])
        v_s.append(jnp.transpose(v_new, (1, 0, 2)))

        x1, h_ext, pos, post, meta = _stage_b(l, x_p, x_s, xs_blk, ycat_p, ycat_s, w_out_b, mod_p4, mod_s, params)
        item_e, item_lo, item_nt, seg_start, seg_rows, used, n_items = _work_items(meta)
        x_sorted, g_sorted = _local_sort(used, post, h_ext)
        y_sorted = _moe_stream(l, item_e, item_lo, item_nt, seg_start, seg_rows, used, n_items, w_up, b_up4,
                               w_down, b_down4, x_sorted, g_sorted)
        if l < DEPTH - 1:
            x_p = x_s = _combine(l, x1, y_sorted, pos, mod_p4, mod_s, nfin, final=False)
            xs_blk = T_PROMPT // T_SAMPLE
        else:
            y_p, y_s = _combine(l, x1, y_sorted, pos, mod_p4, mod_s, nfin, final=True)

    y_prompt = y_p.reshape(BATCH, SEQ, D_MODEL)
    y_sample = jnp.transpose(y_s.reshape(ns, nb, D_MODEL), (1, 0, 2))
    return (y_prompt, y_sample, jnp.stack(pool_p), jnp.stack(pool_s), jnp.stack(conf_p), jnp.stack(conf_s),
            jnp.stack(sc_p), jnp.stack(sc_s), jnp.stack(v_s))
```

```python
import functools

import jax
import jax.numpy as jnp
from jax import lax
from jax.experimental import pallas as pl
from jax.experimental.pallas import tpu as pltpu

D_MODEL = 2048
BATCH = 4
SEQ = 2048
DEPTH = 2
DEC_BATCH = 128
DEC_SEQ = 4
PAST_LEN = 16384
GROUP_WIDTH = 512
POOL_WINDOWS = (2, 4, 8, 16)
POOL_MAX = 16
POOL_CH = 128
SGU_HEADS = 4
SGU_CHUNK = 128
CONF_KERNEL = 31
SC_KERNEL = 3
IN_COLS = 8 * GROUP_WIDTH
N_EXPERTS = 32
TOP_K = 4
D_EXPERT = 2048
SWIGLU_ALPHA = 1.702
SWIGLU_LIMIT = 7.0
EPS = 1e-5

T_PROMPT = BATCH * SEQ
T_SAMPLE = DEC_BATCH * DEC_SEQ
T_ALL = T_PROMPT + T_SAMPLE

LANES = 128
BF16_ROWS = 16
VMEM_LIMIT = 56 * 1024 * 1024

A_TM = 1024
A_TN = 512
A_STEPS = IN_COLS // A_TN
POOL_HALO = 16
CONF_HALO = 32
SC_HALO = 8
CONV_CHUNK = 256
CONV_SUB = 64

B_TM = 512
N_BLOCKS = T_ALL // B_TM
B_PROMPT_BLOCKS = T_PROMPT // B_TM
SEG = BF16_ROWS
SORT_TM = 512
BLOCK_ROWS = -(-(B_TM * TOP_K + N_EXPERTS * (SEG - 1)) // SORT_TM) * SORT_TM
SORT_CHUNKS = BLOCK_ROWS // SORT_TM
TAIL_TM = 256
GATE_TERMS = 3
SIDE_ID0 = GATE_TERMS * TOP_K
H_EXT = D_MODEL + LANES

MOE_TM = 256
MOE_MT = 8
MOE_TN = 512
TILE_SEGS = MOE_TM // SEG
MOE_TILES = (T_ALL * TOP_K + N_BLOCKS * N_EXPERTS * (SEG - 1) + N_EXPERTS * (MOE_TM - 1)) // MOE_TM
MOE_ITEMS = (MOE_TILES + N_EXPERTS * (MOE_MT - 1)) // MOE_MT

_f32 = jnp.float32
_bf16 = jnp.bfloat16


def _rms(x):
    return x * lax.rsqrt(jnp.mean(x * x, axis=-1, keepdims=True) + EPS)


def _ln(x):
    xc = x - jnp.mean(x, axis=-1, keepdims=True)
    return xc * lax.rsqrt(jnp.mean(xc * xc, axis=-1, keepdims=True) + EPS)


def _gelu(x):
    return 0.5 * x * (1.0 + lax.erf(x * (0.5 ** 0.5)))


def _slab_mul(x, m):
    r, c = x.shape
    return (x.reshape(r // 128, 128, c) * m[None]).reshape(r, c)


def _slab_add(x, m):
    r, c = x.shape
    return (x.reshape(r // 128, 128, c) + m[None]).reshape(r, c)


ADA_TN = 1024
ADA_ROWS = DEC_BATCH + 8


def _ada_body(c_ref, w_ref, b_ref, os_ref, op_ref):
    c = c_ref[...]
    a = (c * jax.nn.sigmoid(c)).astype(_bf16)
    r = jnp.dot(a, w_ref[...].astype(_bf16), preferred_element_type=_f32) + b_ref[...]
    os_ref[...] = r[:DEC_BATCH]
    op_ref[...] = r[DEC_BATCH:]


def _ada(c_all, w_ada, b_ada):
    nj = 6 * D_MODEL // ADA_TN
    return pl.pallas_call(
        _ada_body,
        grid=(DEPTH, nj),
        in_specs=[
            pl.BlockSpec((ADA_ROWS, D_MODEL), lambda l, j: (0, 0)),
            pl.BlockSpec((None, D_MODEL, ADA_TN), lambda l, j: (l, 0, j)),
            pl.BlockSpec((None, 1, ADA_TN), lambda l, j: (l, 0, j)),
        ],
        out_specs=[
            pl.BlockSpec((None, DEC_BATCH, ADA_TN), lambda l, j: (l, 0, j)),
            pl.BlockSpec((None, 8, ADA_TN), lambda l, j: (l, 0, j)),
        ],
        out_shape=[
            jax.ShapeDtypeStruct((DEPTH, DEC_BATCH, 6 * D_MODEL), _f32),
            jax.ShapeDtypeStruct((DEPTH, 8, 6 * D_MODEL), _f32),
        ],
        compiler_params=pltpu.CompilerParams(
            dimension_semantics=("arbitrary", "arbitrary"), vmem_limit_bytes=VMEM_LIMIT),
        name="ada_mod",
    )(c_all, w_ada, b_ada.reshape(DEPTH, 1, 6 * D_MODEL))


def _a_prompt_body(x_ref, sh1_ref, sc1_ref, nmix_ref, w_ref, poolw_ref, pscale_ref, sgun_ref, sguw_ref,
                   sgub_ref, cw_ref, cb_ref, clg_ref, clb_ref, scw_ref, onorm_ref,
                   y_ref, pool_o, conf_o, sc_o,
                   h_ref, bufa, bufb, exta, extc, extz, shifted):
    i = pl.program_id(1)
    n = pl.program_id(2)
    tm = A_TM
    gw = GROUP_WIDTH

    @pl.when(n == 0)
    def _():
        hn = _rms(x_ref[...]) * nmix_ref[...]
        h_ref[...] = (hn * (1.0 + sc1_ref[...]) + sh1_ref[...]).astype(_bf16)

    def proj():
        return lax.dot_general(h_ref[...], w_ref[...], (((1,), (0,)), ((), ())), preferred_element_type=_f32)

    def emit(g, y):
        yn = _rms(y) * onorm_ref[:, g * gw:(g + 1) * gw]
        y_ref[:, g * gw:(g + 1) * gw] = yn.astype(_bf16)

    def carry_halo(ext, halo):
        @pl.when(i == 0)
        def _():
            ext[0:halo, :] = jnp.zeros((halo, gw), _f32)

        @pl.when(i > 0)
        def _():
            ext[0:halo, :] = ext[tm:tm + halo, :]

    @pl.when(n == 0)
    def _():
        a = proj()
        carry_halo(exta, POOL_HALO)
        exta[POOL_HALO:, :] = a
        pool_o[...] = a[tm - POOL_HALO:, :]
        pos = (i * tm + lax.broadcasted_iota(jnp.int32, (tm, 1), 0)).astype(_f32)
        for gi, w in enumerate(POOL_WINDOWS):
            c0 = gi * POOL_CH
            s = exta[POOL_HALO:POOL_HALO + tm, c0:c0 + POOL_CH]
            for j in range(1, w):
                s = s + exta[POOL_HALO - j:POOL_HALO - j + tm, c0:c0 + POOL_CH]
            cnt = jnp.minimum(float(w), pos + 1.0)
            pooled = s / cnt - a[:, c0:c0 + POOL_CH]
            yg = jnp.dot(pooled.astype(_bf16), poolw_ref[gi].astype(_bf16), preferred_element_type=_f32)
            bufb[:, c0:c0 + POOL_CH] = yg
        emit(0, bufb[...] * pscale_ref[...])

    @pl.when(n == 1)
    def _():
        bufa[...] = _gelu(proj())

    @pl.when(n == 2)
    def _():
        v = _ln(_gelu(proj())) * sgun_ref[...]
        vb = v.astype(_bf16)
        row = lax.broadcasted_iota(jnp.int32, (SGU_CHUNK, SGU_CHUNK), 0)
        col = lax.broadcasted_iota(jnp.int32, (SGU_CHUNK, SGU_CHUNK), 1)
        for g in range(SGU_HEADS):
            c0 = g * SGU_CHUNK
            wg = jnp.where(row >= col, sguw_ref[g], 0.0).astype(_bf16)
            for c in range(tm // SGU_CHUNK):
                r0 = c * SGU_CHUNK
                m = jnp.dot(wg, vb[r0:r0 + SGU_CHUNK, c0:c0 + SGU_CHUNK], preferred_element_type=_f32)
                bufb[r0:r0 + SGU_CHUNK, c0:c0 + SGU_CHUNK] = m + sgub_ref[:, c0:c0 + SGU_CHUNK]
        emit(1, bufa[...] * bufb[...])

    @pl.when(n == 3)
    def _():
        bufa[...] = proj()

    @pl.when(n == 4)
    def _():
        g = bufa[...] * jax.nn.sigmoid(proj())
        carry_halo(extc, CONF_HALO)
        extc[CONF_HALO:, :] = g
        conf_o[...] = g[tm - CONF_HALO:, :]
        off = CONF_HALO - (CONF_KERNEL - 1)
        for ch in range(tm // CONV_CHUNK):
            base = ch * CONV_CHUNK
            for r in range(8):
                nr = CONV_CHUNK + 8 * ((CONF_HALO - r) // 8)
                shifted[r, 0:nr, :] = extc[base + r:base + r + nr, :]

            def sub(s, carry, base=base):
                i0 = s * CONV_SUB
                acc = jnp.broadcast_to(cb_ref[...], (CONV_SUB, gw))
                for k in range(CONF_KERNEL):
                    q, r = divmod(off + k, 8)
                    acc = acc + cw_ref[k:k + 1, :] * shifted[r, pl.ds(pl.multiple_of(i0 + 8 * q, 8), CONV_SUB), :]
                bufb[pl.ds(pl.multiple_of(base + i0, CONV_SUB), CONV_SUB), :] = acc
                return carry
            lax.fori_loop(0, CONV_CHUNK // CONV_SUB, sub, 0)
        yn = _ln(bufb[...]) * clg_ref[...] + clb_ref[...]
        emit(2, yn * jax.nn.sigmoid(yn))

    @pl.when(n == 5)
    def _():
        bufa[...] = proj()

    @pl.when(n == 6)
    def _():
        bufb[...] = proj()

    @pl.when(n == 7)
    def _():
        z = bufb[...] * proj()
        carry_halo(extz, SC_HALO)
        extz[SC_HALO:, :] = z
        sc_o[...] = z[tm - SC_HALO:, :]
        off = SC_HALO - (SC_KERNEL - 1)
        acc = scw_ref[0:1, :] * extz[off:off + tm, :]
        for k in range(1, SC_KERNEL):
            acc = acc + scw_ref[k:k + 1, :] * extz[off + k:off + k + tm, :]
        emit(3, bufa[...] * acc)


def _layer_spec(shape, l, nd):
    zeros = (0,) * len(shape)
    if nd == 1:
        return pl.BlockSpec((None,) + shape, lambda a: (l,) + zeros)
    if nd == 2:
        return pl.BlockSpec((None,) + shape, lambda a, b: (l,) + zeros)
    return pl.BlockSpec((None,) + shape, lambda a, b, c: (l,) + zeros)


def _mixer_params(l, p, nd):
    gw = GROUP_WIDTH
    arrs = [p["cw"], p["cb"], p["clg"], p["clb"], p["scw"], p["onorm"]]
    shapes = [(CONF_KERNEL, gw), (1, gw), (1, gw), (1, gw), (SC_KERNEL, gw), (1, D_MODEL)]
    return arrs, [_layer_spec(s, l, nd) for s in shapes]


def _a_prompt(l, x_all, mod_p4, w_in, p):
    gw = GROUP_WIDTH
    ni = SEQ // A_TM
    marrs, mspecs = _mixer_params(l, p, 3)
    in_specs = [
        pl.BlockSpec((A_TM, D_MODEL), lambda b, i, n: (b * ni + i, 0)),
        pl.BlockSpec((None, None, 1, D_MODEL), lambda b, i, n: (l, b, 0, 0)),
        pl.BlockSpec((None, None, 1, D_MODEL), lambda b, i, n: (l, b, 0, 1)),
        _layer_spec((1, D_MODEL), l, 3),
        pl.BlockSpec((None, D_MODEL, A_TN), lambda b, i, n: (l, 0, n)),
        _layer_spec((len(POOL_WINDOWS), POOL_CH, POOL_CH), l, 3),
        _layer_spec((1, gw), l, 3),
        _layer_spec((1, gw), l, 3),
        _layer_spec((SGU_HEADS, SGU_CHUNK, SGU_CHUNK), l, 3),
        _layer_spec((SGU_CHUNK, gw), l, 3),
    ] + mspecs
    out_specs = [
        pl.BlockSpec((A_TM, D_MODEL), lambda b, i, n: (b * ni + i, 0)),
        pl.BlockSpec((None, POOL_HALO, gw), lambda b, i, n: (b, 0, 0)),
        pl.BlockSpec((None, CONF_HALO, gw), lambda b, i, n: (b, 0, 0)),
        pl.BlockSpec((None, SC_HALO, gw), lambda b, i, n: (b, 0, 0)),
    ]
    out_shape = [
        jax.ShapeDtypeStruct((T_PROMPT, D_MODEL), _bf16),
        jax.ShapeDtypeStruct((BATCH, POOL_HALO, gw), _f32),
        jax.ShapeDtypeStruct((BATCH, CONF_HALO, gw), _f32),
        jax.ShapeDtypeStruct((BATCH, SC_HALO, gw), _f32),
    ]
    scratch = [
        pltpu.VMEM((A_TM, D_MODEL), _bf16),
        pltpu.VMEM((A_TM, gw), _f32),
        pltpu.VMEM((A_TM, gw), _f32),
        pltpu.VMEM((POOL_HALO + A_TM, gw), _f32),
        pltpu.VMEM((CONF_HALO + A_TM, gw), _f32),
        pltpu.VMEM((SC_HALO + A_TM, gw), _f32),
        pltpu.VMEM((8, CONV_CHUNK + CONF_HALO, gw), _f32),
    ]
    return pl.pallas_call(
        _a_prompt_body,
        grid=(BATCH, ni, A_STEPS),
        in_specs=in_specs,
        out_specs=out_specs,
        out_shape=out_shape,
        scratch_shapes=scratch,
        compiler_params=pltpu.CompilerParams(
            dimension_semantics=("arbitrary", "arbitrary", "arbitrary"), vmem_limit_bytes=VMEM_LIMIT),
        name="mixers_prompt",
    )(x_all, mod_p4, mod_p4, p["nmix"], w_in, p["poolw"], p["pscale"], p["sgun"], p["sguw"], p["sgub"], *marrs)


def _a_sample_body(x_ref, sh1_ref, sc1_ref, nmix_ref, w_ref, poolw_ref, pscale_ref, sgun_ref, sgwc_ref,
                   sgbc_ref, cw_ref, cb_ref, clg_ref, clb_ref, scw_ref, onorm_ref, pst_ref, cst_ref, sst_ref,
                   y_ref, a_o, g_o, z_o, v_o,
                   h_ref, bufa, bufb):
    n = pl.program_id(0)
    gw = GROUP_WIDTH
    nb = DEC_BATCH
    ns = DEC_SEQ

    @pl.when(n == 0)
    def _():
        hn = _rms(x_ref[...]) * nmix_ref[...]
        h_ref[...] = _slab_add(_slab_mul(hn, 1.0 + sc1_ref[...]), sh1_ref[...]).astype(_bf16)

    def proj():
        return lax.dot_general(h_ref[...], w_ref[...], (((1,), (0,)), ((), ())), preferred_element_type=_f32)

    def emit(g, y):
        yn = _rms(y) * onorm_ref[:, g * gw:(g + 1) * gw]
        y_ref[:, g * gw:(g + 1) * gw] = yn.astype(_bf16)

    def slab(x, s):
        return x[s * nb:(s + 1) * nb, :]

    @pl.when(n == 0)
    def _():
        a = proj()
        a_o[...] = a.reshape(ns, nb, gw)
        npast = POOL_MAX - 1
        for s in range(ns):
            for gi, w in enumerate(POOL_WINDOWS):
                c0 = gi * POOL_CH
                acc = None
                for j in range(w):
                    e = npast + s - j
                    term = pst_ref[e, :, c0:c0 + POOL_CH] if e < npast else slab(a, e - npast)[:, c0:c0 + POOL_CH]
                    acc = term if acc is None else acc + term
                cnt = min(float(w), PAST_LEN + s + 1.0)
                pooled = acc / cnt - slab(a, s)[:, c0:c0 + POOL_CH]
                bufb[s * nb:(s + 1) * nb, c0:c0 + POOL_CH] = pooled
        pooled = bufb[...].astype(_bf16)
        for gi in range(len(POOL_WINDOWS)):
            c0 = gi * POOL_CH
            bufa[:, c0:c0 + POOL_CH] = jnp.dot(pooled[:, c0:c0 + POOL_CH], poolw_ref[gi].astype(_bf16),
                                               preferred_element_type=_f32)
        emit(0, bufa[...] * pscale_ref[...])

    @pl.when(n == 1)
    def _():
        bufa[...] = _gelu(proj())

    @pl.when(n == 2)
    def _():
        v = _ln(_gelu(proj())) * sgun_ref[...]
        v_o[...] = v.reshape(ns, nb, gw)
        for t in range(ns):
            m = jnp.broadcast_to(sgbc_ref[t:t + 1, :], (nb, gw))
            for s in range(t + 1):
                m = m + sgwc_ref[t * ns + s:t * ns + s + 1, :] * slab(v, s)
            bufb[t * nb:(t + 1) * nb, :] = m
        emit(1, bufa[...] * bufb[...])

    @pl.when(n == 3)
    def _():
        bufa[...] = proj()

    @pl.when(n == 4)
    def _():
        g = bufa[...] * jax.nn.sigmoid(proj())
        g_o[...] = g.reshape(ns, nb, gw)
        npast = CONF_KERNEL - 1
        for s in range(ns):
            acc = jnp.broadcast_to(cb_ref[...], (nb, gw))
            for k in range(CONF_KERNEL):
                e = s + k
                term = cst_ref[e] if e < npast else slab(g, e - npast)
                acc = acc + cw_ref[k:k + 1, :] * term
            bufb[s * nb:(s + 1) * nb, :] = acc
        yn = _ln(bufb[...]) * clg_ref[...] + clb_ref[...]
        emit(2, yn * jax.nn.sigmoid(yn))

    @pl.when(n == 5)
    def _():
        bufa[...] = proj()

    @pl.when(n == 6)
    def _():
        bufb[...] = proj()

    @pl.when(n == 7)
    def _():
        z = bufb[...] * proj()
        z_o[...] = z.reshape(ns, nb, gw)
        npast = SC_KERNEL - 1
        for s in range(ns):
            acc = None
            for k in range(SC_KERNEL):
                e = s + k
                term = scw_ref[k:k + 1, :] * (sst_ref[e] if e < npast else slab(z, e - npast))
                acc = term if acc is None else acc + term
            bufb[s * nb:(s + 1) * nb, :] = acc
        emit(3, bufa[...] * bufb[...])


def _a_sample(l, x_all, x_blk, mod_s, w_in, p, pool_t, conf_t, sc_t):
    gw = GROUP_WIDTH
    ns, nb = DEC_SEQ, DEC_BATCH
    marrs, mspecs = _mixer_params(l, p, 1)
    in_specs = [
        pl.BlockSpec((T_SAMPLE, D_MODEL), lambda n: (x_blk, 0)),
        pl.BlockSpec((None, nb, D_MODEL), lambda n: (l, 0, 0)),
        pl.BlockSpec((None, nb, D_MODEL), lambda n: (l, 0, 1)),
        _layer_spec((1, D_MODEL), l, 1),
        pl.BlockSpec((None, D_MODEL, A_TN), lambda n: (l, 0, n)),
        _layer_spec((len(POOL_WINDOWS), POOL_CH, POOL_CH), l, 1),
        _layer_spec((1, gw), l, 1),
        _layer_spec((1, gw), l, 1),
        _layer_spec((ns * ns, gw), l, 1),
        _layer_spec((ns, gw), l, 1),
    ] + mspecs + [
        _layer_spec((POOL_MAX - 1, nb, gw), l, 1),
        _layer_spec((CONF_KERNEL - 1, nb, gw), l, 1),
        _layer_spec((SC_KERNEL - 1, nb, gw), l, 1),
    ]
    out_specs = [pl.BlockSpec((T_SAMPLE, D_MODEL), lambda n: (0, 0))] + [
        pl.BlockSpec((ns, nb, gw), lambda n: (0, 0, 0)) for _ in range(4)]
    out_shape = [jax.ShapeDtypeStruct((T_SAMPLE, D_MODEL), _bf16)] + [
        jax.ShapeDtypeStruct((ns, nb, gw), _f32) for _ in range(4)]
    scratch = [
        pltpu.VMEM((T_SAMPLE, D_MODEL), _bf16),
        pltpu.VMEM((T_SAMPLE, gw), _f32),
        pltpu.VMEM((T_SAMPLE, gw), _f32),
    ]
    return pl.pallas_call(
        _a_sample_body,
        grid=(A_STEPS,),
        in_specs=in_specs,
        out_specs=out_specs,
        out_shape=out_shape,
        scratch_shapes=scratch,
        compiler_params=pltpu.CompilerParams(
            dimension_semantics=("arbitrary",), vmem_limit_bytes=VMEM_LIMIT),
        name="mixers_sample",
    )(x_all, mod_s, mod_s, p["nmix"], w_in, p["poolw"], p["pscale"], p["sgun"], p["sgwc"], p["sgbc"],
      *marrs, pool_t, conf_t, sc_t)


def _pick_mod(is_prompt, mp_ref, ms_ref):
    return jnp.where(is_prompt, jnp.broadcast_to(mp_ref[...], (DEC_BATCH, D_MODEL)), ms_ref[...])


def _b_body(xp_ref, xs_ref, yp_ref, ys_ref, w_ref, mpg1, mpsh2, mpsc2, msg1, mssh2, mssc2, nffn_ref, rw_ref, rb_ref,
            x1_ref, h2_ref, pos_ref, post_ref, meta_ref):
    i = pl.program_id(0)
    isp = i < B_PROMPT_BLOCKS
    ycat = jnp.where(isp, yp_ref[...], ys_ref[...])
    o = jnp.dot(ycat, w_ref[...], preferred_element_type=_f32)
    x1 = jnp.where(isp, xp_ref[...], xs_ref[...]) + _slab_mul(o, _pick_mod(isp, mpg1, msg1))
    x1_ref[...] = x1
    hn = _rms(x1) * nffn_ref[...]
    h2 = _slab_add(_slab_mul(hn, 1.0 + _pick_mod(isp, mpsc2, mssc2)), _pick_mod(isp, mpsh2, mssh2))
    hi = h2.astype(_bf16)
    h2_ref[:, 0:D_MODEL] = hi
    lo = (h2 - hi.astype(_f32)).astype(_bf16)
    l1 = jnp.dot(hi, rw_ref[...], preferred_element_type=_f32)
    l2 = jnp.dot(lo, rw_ref[:, 0:LANES], preferred_element_type=_f32)
    logits = l1[:, 0:LANES] + l1[:, LANES:] + l2 + rb_ref[...]
    lane = lax.broadcasted_iota(jnp.int32, (B_TM, LANES), 1).astype(_f32)
    neg = jnp.float32(-jnp.inf)
    cur = jnp.where(lane < N_EXPERTS, logits, neg)
    vals, ids = [], []
    for _ in range(TOP_K):
        m = jnp.max(cur, axis=-1, keepdims=True)
        ix = jnp.min(jnp.where(cur == m, lane, float(LANES)), axis=-1, keepdims=True)
        vals.append(m)
        ids.append(ix)
        cur = jnp.where(lane == ix, neg, cur)
    es = [jnp.exp(v - vals[0]) for v in vals]
    tot = es[0] + es[1] + es[2] + es[3]
    side = jnp.zeros((B_TM, LANES), _f32)
    for k in range(TOP_K):
        g = es[k] / tot
        g0 = g.astype(_bf16).astype(_f32)
        g1 = (g - g0).astype(_bf16).astype(_f32)
        for j, term in enumerate((g0, g1, g - g0 - g1)):
            side = jnp.where(lane == float(GATE_TERMS * k + j), term, side)
        side = jnp.where(lane == float(SIDE_ID0 + k), ids[k], side)
    h2_ref[:, D_MODEL:] = side.astype(_bf16)

    hit = [lane == ix for ix in ids]
    chosen = jnp.zeros((B_TM, LANES), _f32)
    for h in hit:
        chosen = jnp.where(h, 1.0, chosen)
    cnt = jnp.sum(chosen, axis=0, keepdims=True)
    seg_rows = jnp.floor((cnt + (SEG - 1.0)) * (1.0 / SEG)) * SEG
    er = lax.broadcasted_iota(jnp.int32, (LANES, LANES), 0)
    ec = lax.broadcasted_iota(jnp.int32, (LANES, LANES), 1)
    before = jnp.where(er < ec, 1.0, 0.0).astype(_bf16)
    seg_start = jnp.dot(jnp.broadcast_to(seg_rows, (8, LANES)).astype(_bf16), before,
                        preferred_element_type=_f32)[0:1, :]
    tr = lax.broadcasted_iota(jnp.int32, (B_TM, B_TM), 0)
    tc = lax.broadcasted_iota(jnp.int32, (B_TM, B_TM), 1)
    earlier = jnp.where(tr > tc, 1.0, 0.0).astype(_bf16)
    rank = jnp.dot(earlier, chosen.astype(_bf16), preferred_element_type=_f32)
    base = seg_start + rank
    pos = jnp.zeros((B_TM, LANES), _f32)
    for k in range(TOP_K):
        pk = jnp.sum(jnp.where(hit[k], base, 0.0), axis=-1, keepdims=True)
        pos = jnp.where(lane == float(k), pk, pos)
    pos_ref[...] = pos
    post_ref[...] = jnp.transpose(pos)[0:8, :]
    sub = lax.broadcasted_iota(jnp.int32, (8, LANES), 0)
    meta = jnp.where(sub == 0, jnp.broadcast_to(seg_start, (8, LANES)),
                     jnp.where(sub == 1, jnp.broadcast_to(seg_rows, (8, LANES)), 0.0))
    meta_ref[...] = meta.astype(jnp.int32)


def _stage_b(l, x_p, x_s, xs_blk, ycat_p, ycat_s, w_out_b, mod_p4, mod_s, p):
    def pb(i):
        return jnp.minimum(i // (SEQ // B_TM), BATCH - 1)

    def mp_spec(j):
        return pl.BlockSpec((None, None, 1, D_MODEL), lambda i: (l, pb(i), 0, j))

    def ms_spec(j):
        return pl.BlockSpec((None, DEC_BATCH, D_MODEL), lambda i: (l, 0, j))

    row_spec = pl.BlockSpec((B_TM, D_MODEL), lambda i: (i, 0))
    in_specs = [
        pl.BlockSpec((B_TM, D_MODEL), lambda i: (jnp.minimum(i, B_PROMPT_BLOCKS - 1), 0)),
        pl.BlockSpec((T_SAMPLE, D_MODEL), lambda i: (xs_blk, 0), pipeline_mode=pl.Buffered(1)),
        pl.BlockSpec((B_TM, D_MODEL), lambda i: (jnp.minimum(i, B_PROMPT_BLOCKS - 1), 0)),
        pl.BlockSpec((T_SAMPLE, D_MODEL), lambda i: (0, 0), pipeline_mode=pl.Buffered(1)),
        pl.BlockSpec((None, D_MODEL, D_MODEL), lambda i: (l, 0, 0), pipeline_mode=pl.Buffered(1)),
        mp_spec(2), mp_spec(3), mp_spec(4), ms_spec(2), ms_spec(3), ms_spec(4),
        _layer_spec((1, D_MODEL), l, 1),
        _layer_spec((D_MODEL, 2 * LANES), l, 1),
        _layer_spec((1, LANES), l, 1),
    ]
    lane_spec = pl.BlockSpec((B_TM, LANES), lambda i: (i, 0))
    return pl.pallas_call(
        _b_body,
        grid=(N_BLOCKS,),
        in_specs=in_specs,
        out_specs=[row_spec, pl.BlockSpec((B_TM, H_EXT), lambda i: (i, 0)), lane_spec,
                   pl.BlockSpec((None, 8, B_TM), lambda i: (i, 0, 0)),
                   pl.BlockSpec((None, 8, LANES), lambda i: (i, 0, 0))],
        out_shape=[
            jax.ShapeDtypeStruct((T_ALL, D_MODEL), _f32),
            jax.ShapeDtypeStruct((T_ALL, H_EXT), _bf16),
            jax.ShapeDtypeStruct((T_ALL, LANES), _f32),
            jax.ShapeDtypeStruct((N_BLOCKS, 8, B_TM), _f32),
            jax.ShapeDtypeStruct((N_BLOCKS, 8, LANES), jnp.int32),
        ],
        compiler_params=pltpu.CompilerParams(
            dimension_semantics=("arbitrary",), vmem_limit_bytes=VMEM_LIMIT),
        name="outproj_router",
    )(x_p, x_s, ycat_p, ycat_s, w_out_b, mod_p4, mod_p4, mod_p4, mod_s, mod_s, mod_s, p["nffn"], p["rw"], p["rb"])


def _sort_body(used_ref, post_ref, h_ref, o_ref, so_ref):
    used = used_ref[pl.program_id(0)]

    def rows(r0, nr):
        j = (r0 + lax.broadcasted_iota(jnp.int32, (nr, B_TM), 0)).astype(_f32)
        sel = jnp.zeros((nr, B_TM), _f32)
        for k in range(TOP_K):
            sel = jnp.where(j == post_ref[k:k + 1, :], 1.0, sel)
        res = jnp.dot(sel.astype(_bf16), h_ref[...], preferred_element_type=_f32)
        o_ref[r0:r0 + nr, :] = res[:, 0:D_MODEL].astype(_bf16)
        so_ref[r0:r0 + nr, :] = res[:, D_MODEL:]

    for c in range(B_TM * TOP_K // SORT_TM):
        rows(c * SORT_TM, SORT_TM)
    for r0 in range(B_TM * TOP_K, BLOCK_ROWS, TAIL_TM):
        @pl.when(used > r0)
        def _(r0=r0):
            rows(r0, TAIL_TM)

        @pl.when(used <= r0)
        def _(r0=r0):
            o_ref[r0:r0 + TAIL_TM, :] = jnp.zeros((TAIL_TM, D_MODEL), _bf16)
            so_ref[r0:r0 + TAIL_TM, :] = jnp.zeros((TAIL_TM, LANES), _f32)


def _local_sort(used, post, h_ext):
    grid_spec = pltpu.PrefetchScalarGridSpec(
        num_scalar_prefetch=1,
        grid=(N_BLOCKS,),
        in_specs=[
            pl.BlockSpec((None, 8, B_TM), lambda b, u: (b, 0, 0)),
            pl.BlockSpec((B_TM, H_EXT), lambda b, u: (b, 0)),
        ],
        out_specs=[pl.BlockSpec((BLOCK_ROWS, D_MODEL), lambda b, u: (b, 0)),
                   pl.BlockSpec((BLOCK_ROWS, LANES), lambda b, u: (b, 0))],
    )
    return pl.pallas_call(
        _sort_body,
        grid_spec=grid_spec,
        out_shape=[jax.ShapeDtypeStruct((N_BLOCKS * BLOCK_ROWS, D_MODEL), _bf16),
                   jax.ShapeDtypeStruct((N_BLOCKS * BLOCK_ROWS, LANES), _f32)],
        compiler_params=pltpu.CompilerParams(
            dimension_semantics=("arbitrary",), vmem_limit_bytes=VMEM_LIMIT),
        name="local_sort",
    )(used, post, h_ext)


BIG_TM = 2 * MOE_TM
SMALL_TM = MOE_TM // 2
MOE_DN = 1024
STREAM_UP_STEPS = D_EXPERT // MOE_TN
STREAM_DOWN_STEPS = D_MODEL // MOE_DN
STREAM_STEPS = STREAM_UP_STEPS + STREAM_DOWN_STEPS
MOE_WSLOTS = 3


def _moe_stream_body(ie_ref, ilo_ref, int_ref, start_ref, rows_ref, used_ref, nitems_ref,
                     wup_hbm, wdn_hbm, bup_hbm, bdn_hbm, x_hbm, g_hbm, y_hbm,
                     wbuf, xbuf, hbuf, gbuf, bup, bdn, seg_src, nseg, pending,
                     sem_w, sem_b, sem_x, sem_y, *, l):
    tm, tn = MOE_TM, MOE_TN
    n_items = nitems_ref[0]
    n_steps = n_items * STREAM_STEPS

    def w_half(src_ref, e, col0, slot, half):
        return pltpu.make_async_copy(src_ref.at[l, e, :, pl.ds(col0, tn)],
                                     wbuf.at[slot, :, pl.ds(half * tn, tn)], sem_w.at[slot])

    def issue_w(s):
        e = ie_ref[s // STREAM_STEPS]
        n = s % STREAM_STEPS
        slot = s % MOE_WSLOTS
        for c in range(STREAM_UP_STEPS):
            @pl.when(n == c)
            def _(c=c):
                w_half(wup_hbm, e, c * tn, slot, 0).start()
                w_half(wup_hbm, e, D_EXPERT + c * tn, slot, 1).start()
        for c in range(STREAM_DOWN_STEPS):
            @pl.when(n == STREAM_UP_STEPS + c)
            def _(c=c):
                w_half(wdn_hbm, e, c * MOE_DN, slot, 0).start()
                w_half(wdn_hbm, e, c * MOE_DN + tn, slot, 1).start()

    def wait_w(s):
        slot = s % MOE_WSLOTS
        for half in range(2):
            w_half(wdn_hbm, 0, 0, slot, half).wait()

    def b_copies(item):
        e = ie_ref[item]
        par = item % 2
        return (pltpu.make_async_copy(bup_hbm.at[l, e], bup.at[par], sem_b.at[par]),
                pltpu.make_async_copy(bdn_hbm.at[l, e], bdn.at[par], sem_b.at[par]))

    def x_copy(p, src, dst):
        return pltpu.make_async_copy(x_hbm.at[pl.ds(src, SEG)], xbuf.at[p, pl.ds(dst, SEG)],
                                     sem_x.at[p * MOE_MT + dst // tm])

    def g_copy(p, src, dst):
        return pltpu.make_async_copy(g_hbm.at[pl.ds(src, SEG)], gbuf.at[p, pl.ds(dst, SEG)],
                                     sem_x.at[p * MOE_MT + dst // tm])

    def y_copy(p, src, dst):
        return pltpu.make_async_copy(xbuf.at[p, pl.ds(src, SEG)], y_hbm.at[pl.ds(dst, SEG)], sem_y.at[p])

    def drain(p):
        def w(_, carry):
            y_copy(p, 0, 0).wait()
            return carry
        lax.fori_loop(0, pending[p], w, 0)
        pending[p] = 0

    def tile_pieces(p, t):
        return jnp.clip(nseg[p] - t * TILE_SEGS, 0, TILE_SEGS)

    def plan(rows):
        n_big = rows // BIG_TM
        rem = rows - n_big * BIG_TM
        round_up = rem > BIG_TM - SMALL_TM
        n_big = n_big + jnp.where(round_up, 1, 0)
        rem = jnp.where(round_up, 0, rem)
        has_mid = rem > SMALL_TM
        has_small = (rem > 0) & ((rem <= SMALL_TM) | (rem > tm))
        off_mid = n_big * BIG_TM
        off_small = off_mid + jnp.where(has_mid, tm, 0)
        return n_big, has_mid, has_small, off_mid, off_small

    def load_item(item):
        p = item % 2
        e = ie_ref[item]
        lo = ilo_ref[item]
        nt = int_ref[item]
        hi = lo + nt * tm

        def per_block(b, q0):
            rows = rows_ref[b * N_EXPERTS + e]
            start = start_ref[b * N_EXPERTS + e]

            def per_piece(g, cc):
                src = pl.multiple_of(b * BLOCK_ROWS + start + g * SEG, SEG)
                dst = pl.multiple_of(q0 + g * SEG - lo, SEG)
                seg_src[p * (MOE_MT * TILE_SEGS) + dst // SEG] = src
                x_copy(p, src, dst).start()
                g_copy(p, src, dst).start()
                return cc
            pieces = rows // SEG
            g_lo = jnp.minimum((jnp.maximum(lo - q0, 0) + SEG - 1) // SEG, pieces)
            g_hi = jnp.minimum((jnp.maximum(hi - q0, 0) + SEG - 1) // SEG, pieces)
            lax.fori_loop(g_lo, g_hi, per_piece, 0)
            return q0 + rows
        end = lax.fori_loop(0, N_BLOCKS, per_block, 0)
        ns = (jnp.minimum(end, hi) - lo) // SEG
        nseg[p] = ns

        def zero_piece(g, cc):
            r = pl.multiple_of(g * SEG, SEG)
            xbuf[p, pl.ds(r, SEG), :] = jnp.zeros((SEG, D_MODEL), _bf16)
            gbuf[p, pl.ds(r, SEG), :] = jnp.zeros((SEG, LANES), _f32)
            return cc
        n_big, has_mid, has_small, _, off_small = plan(ns * SEG)
        computed = off_small + jnp.where(has_small, SMALL_TM, 0)
        lax.fori_loop(ns, computed // SEG, zero_piece, 0)

    pending[0] = 0
    pending[1] = 0
    issue_w(0)
    issue_w(1)
    for cp in b_copies(0):
        cp.start()
    load_item(0)

    def step(s, carry):
        item = s // STREAM_STEPS
        n = s % STREAM_STEPS
        e = ie_ref[item]
        par = item % 2
        slot = s % MOE_WSLOTS

        @pl.when(s + 2 < n_steps)
        def _():
            issue_w(s + 2)
        wait_w(s)

        @pl.when(n == 0)
        def _():
            for cp in b_copies(item):
                cp.wait()

        @pl.when((n == 1) & (item + 1 < n_items))
        def _():
            for cp in b_copies(item + 1):
                cp.start()
            drain(1 - par)
            load_item(item + 1)

        n_big, has_mid, has_small, off_mid, off_small = plan(nseg[par] * SEG)

        def up_rows(c, r0, nrows):
            if c == 0:
                for u in range(max(nrows // tm, 1)):
                    sub = r0 // tm + u

                    def w(_, c3, sub=sub):
                        x_copy(par, 0, pl.multiple_of(sub * tm, tm)).wait()
                        g_copy(par, 0, pl.multiple_of(sub * tm, tm)).wait()
                        return c3
                    lax.fori_loop(0, tile_pieces(par, sub), w, 0)
            up = lax.dot_general(xbuf[par, pl.ds(r0, nrows), :], wbuf[slot], (((1,), (0,)), ((), ())),
                                 preferred_element_type=_f32)
            bg = bup[par, :, c * tn:(c + 1) * tn]
            bl = bup[par, :, D_EXPERT + c * tn:D_EXPERT + (c + 1) * tn]
            glu = jnp.minimum(up[:, 0:tn] + bg, SWIGLU_LIMIT)
            lin = jnp.clip(up[:, tn:] + bl, -SWIGLU_LIMIT, SWIGLU_LIMIT)
            act = glu * jax.nn.sigmoid(SWIGLU_ALPHA * glu) * (lin + 1.0)
            hbuf[pl.ds(r0, nrows), c * tn:(c + 1) * tn] = act.astype(_bf16)

        def down_rows(c, r0, nrows):
            y = (lax.dot_general(hbuf[pl.ds(r0, nrows), :], wbuf[slot], (((1,), (0,)), ((), ())),
                                 preferred_element_type=_f32)
                 + bdn[par, :, c * MOE_DN:(c + 1) * MOE_DN])
            g = gbuf[par, pl.ds(r0, nrows), :]
            ef = e.astype(_f32)
            gate = jnp.zeros((nrows, 1), _f32)
            for k in range(TOP_K):
                gk = g[:, GATE_TERMS * k:GATE_TERMS * k + 1]
                for j in range(1, GATE_TERMS):
                    gk = gk + g[:, GATE_TERMS * k + j:GATE_TERMS * k + j + 1]
                gate = jnp.where(g[:, SIDE_ID0 + k:SIDE_ID0 + k + 1] == ef, gk, gate)
            xbuf[par, pl.ds(r0, nrows), c * MOE_DN:(c + 1) * MOE_DN] = (y * gate).astype(_bf16)
            if c == STREAM_DOWN_STEPS - 1:
                for q in range(nrows // SEG):
                    piece = r0 // SEG + q

                    @pl.when(piece < nseg[par])
                    def _(q=q, piece=piece):
                        dst = seg_src[par * (MOE_MT * TILE_SEGS) + piece]
                        y_copy(par, pl.multiple_of(r0 + q * SEG, SEG), pl.multiple_of(dst, SEG)).start()

        def all_rows(fn, c):
            def big(t, cc):
                fn(c, pl.multiple_of(t * BIG_TM, BIG_TM), BIG_TM)
                return cc
            lax.fori_loop(0, n_big, big, 0)

            @pl.when(has_mid)
            def _():
                fn(c, pl.multiple_of(off_mid, tm), tm)

            @pl.when(has_small)
            def _():
                fn(c, pl.multiple_of(off_small, SMALL_TM), SMALL_TM)

        for c in range(STREAM_UP_STEPS):
            @pl.when(n == c)
            def _(c=c):
                all_rows(up_rows, c)

        for c in range(STREAM_DOWN_STEPS):
            @pl.when(n == STREAM_UP_STEPS + c)
            def _(c=c):
                all_rows(down_rows, c)
                if c == STREAM_DOWN_STEPS - 1:
                    pending[par] = nseg[par]
        return carry
    lax.fori_loop(0, n_steps, step, 0)

    drain(0)
    drain(1)
    hbuf[0:SEG, :] = jnp.zeros((SEG, D_EXPERT), _bf16)

    def zero_copy(row):
        return pltpu.make_async_copy(hbuf.at[pl.ds(0, SEG)], y_hbm.at[pl.ds(row, SEG)], sem_y.at[0])

    def zero_block(b, carry):
        def zstart(g, cc):
            zero_copy(pl.multiple_of(b * BLOCK_ROWS + g * SEG, SEG)).start()
            return cc

        def zwait(g, cc):
            zero_copy(0).wait()
            return cc
        first = used_ref[b] // SEG
        lax.fori_loop(first, BLOCK_ROWS // SEG, zstart, 0)
        lax.fori_loop(first, BLOCK_ROWS // SEG, zwait, 0)
        return carry
    lax.fori_loop(0, N_BLOCKS, zero_block, 0)


def _moe_stream(l, item_e, item_lo, item_nt, seg_start, seg_rows, used, n_items, w_up, b_up4, w_down, b_down4,
                x_sorted, g_sorted):
    any_spec = pl.BlockSpec(memory_space=pl.ANY)
    grid_spec = pltpu.PrefetchScalarGridSpec(
        num_scalar_prefetch=7,
        grid=(1,),
        in_specs=[any_spec] * 6,
        out_specs=any_spec,
        scratch_shapes=[
            pltpu.VMEM((MOE_WSLOTS, D_MODEL, 2 * MOE_TN), _f32),
            pltpu.VMEM((2, MOE_MT * MOE_TM, D_MODEL), _bf16),
            pltpu.VMEM((MOE_MT * MOE_TM, D_EXPERT), _bf16),
            pltpu.VMEM((2, MOE_MT * MOE_TM, LANES), _f32),
            pltpu.VMEM((2, 1, 2 * D_EXPERT), _f32),
            pltpu.VMEM((2, 1, D_MODEL), _f32),
            pltpu.SMEM((2 * MOE_MT * TILE_SEGS,), jnp.int32),
            pltpu.SMEM((2,), jnp.int32),
            pltpu.SMEM((2,), jnp.int32),
            pltpu.SemaphoreType.DMA((MOE_WSLOTS,)),
            pltpu.SemaphoreType.DMA((2,)),
            pltpu.SemaphoreType.DMA((2 * MOE_MT,)),
            pltpu.SemaphoreType.DMA((2,)),
        ],
    )
    return pl.pallas_call(
        functools.partial(_moe_stream_body, l=l),
        grid_spec=grid_spec,
        out_shape=jax.ShapeDtypeStruct((N_BLOCKS * BLOCK_ROWS, D_MODEL), _bf16),
        compiler_params=pltpu.CompilerParams(
            dimension_semantics=("arbitrary",), vmem_limit_bytes=60 * 1024 * 1024),
        name="moe_experts",
    )(item_e, item_lo, item_nt, seg_start, seg_rows, used, n_items, w_up, w_down, b_up4, b_down4, x_sorted,
      g_sorted)


def _work_items(meta):
    seg_start = meta[:, 0, :N_EXPERTS]
    seg_rows = meta[:, 1, :N_EXPERTS]
    rows_e = jnp.sum(seg_rows, axis=0)
    nt = (rows_e + MOE_TM - 1) // MOE_TM
    ipe = (nt + MOE_MT - 1) // MOE_MT
    cum = jnp.cumsum(ipe)
    first = cum - ipe
    total = cum[-1]
    ii = jnp.arange(MOE_ITEMS, dtype=jnp.int32)
    ii_c = jnp.minimum(ii, total - 1)
    e_i = jnp.minimum(jnp.sum((cum[None, :] <= ii_c[:, None]).astype(jnp.int32), axis=1), N_EXPERTS - 1)
    valid = ii < total
    j = ii - first[e_i]
    lo_i = jnp.where(valid, j * (MOE_MT * MOE_TM), 0)
    nt_i = jnp.where(valid, jnp.minimum(MOE_MT, nt[e_i] - j * MOE_MT), 0)
    used = jnp.sum(seg_rows, axis=1)
    i32 = jnp.int32
    return (e_i.astype(i32), lo_i.astype(i32), nt_i.astype(i32), seg_start.reshape(-1).astype(i32),
            seg_rows.reshape(-1).astype(i32), used.astype(i32), total.astype(i32).reshape(1))


def _c_body(x1_ref, y_ref, pos_ref, mpg2, msg2, nfin_ref, *rest, final):
    if final:
        outp_ref, outs_ref, sel_ref = rest
    else:
        out_ref, sel_ref = rest
    i = pl.program_id(0)
    isp = i < B_PROMPT_BLOCKS
    for c in range(SORT_CHUNKS):
        j = (c * SORT_TM + lax.broadcasted_iota(jnp.int32, (B_TM, SORT_TM), 1)).astype(_f32)
        sel = jnp.zeros((B_TM, SORT_TM), _f32)
        for k in range(TOP_K):
            sel = jnp.where(j == pos_ref[:, k:k + 1], 1.0, sel)
        sel_ref[:, c * SORT_TM:(c + 1) * SORT_TM] = sel.astype(_bf16)
    acc = jnp.dot(sel_ref[...], y_ref[...], preferred_element_type=_f32)
    x2 = x1_ref[...] + _slab_mul(acc, _pick_mod(isp, mpg2, msg2))
    if final:
        y = _rms(x2) * nfin_ref[...]

        @pl.when(isp)
        def _():
            outp_ref[...] = y

        @pl.when(jnp.logical_not(isp))
        def _():
            outs_ref[...] = y
    else:
        out_ref[...] = x2


def _combine(l, x1, y_sorted, pos, mod_p4, mod_s, nfin, final):
    def pb(i):
        return jnp.minimum(i // (SEQ // B_TM), BATCH - 1)

    row_spec = pl.BlockSpec((B_TM, D_MODEL), lambda i: (i, 0))
    lane_spec = pl.BlockSpec((B_TM, LANES), lambda i: (i, 0))
    if final:
        out_specs = [pl.BlockSpec((B_TM, D_MODEL), lambda i: (jnp.minimum(i, B_PROMPT_BLOCKS - 1), 0)),
                     pl.BlockSpec((T_SAMPLE, D_MODEL), lambda i: (0, 0))]
        out_shape = [jax.ShapeDtypeStruct((T_PROMPT, D_MODEL), _f32),
                     jax.ShapeDtypeStruct((T_SAMPLE, D_MODEL), _f32)]
    else:
        out_specs = row_spec
        out_shape = jax.ShapeDtypeStruct((T_ALL, D_MODEL), _f32)
    return pl.pallas_call(
        functools.partial(_c_body, final=final),
        grid=(N_BLOCKS,),
        in_specs=[
            row_spec,
            pl.BlockSpec((BLOCK_ROWS, D_MODEL), lambda i: (i, 0)),
            lane_spec,
            pl.BlockSpec((None, None, 1, D_MODEL), lambda i: (l, pb(i), 0, 5)),
            pl.BlockSpec((None, DEC_BATCH, D_MODEL), lambda i: (l, 0, 5)),
            pl.BlockSpec((1, D_MODEL), lambda i: (0, 0)),
        ],
        out_specs=out_specs,
        out_shape=out_shape,
        scratch_shapes=[pltpu.VMEM((B_TM, BLOCK_ROWS), _bf16)],
        compiler_params=pltpu.CompilerParams(
            dimension_semantics=("arbitrary",), vmem_limit_bytes=VMEM_LIMIT),
        name="moe_combine",
    )(x1, y_sorted, pos, mod_p4, mod_s, nfin)


def kernel(x_prompt, x_sample, c_prompt, c_sample, state_pool, state_conformer, state_shortconv, w_ada, b_ada, norm_mix, norm_ffn, norm_final, w_in, pool_w, pool_scale, sgu_norm, sgu_w, sgu_b, conf_w, conf_b, conf_ln_g, conf_ln_b, sc_w, out_norm, w_out, router_w, router_b, w_up, b_up, w_down, b_down):
    gw = GROUP_WIDTH
    ns, nb = DEC_SEQ, DEC_BATCH

    c_all = jnp.concatenate([c_sample, c_prompt, jnp.zeros((ADA_ROWS - nb - BATCH, D_MODEL), _f32)], axis=0)
    mod_s, mod_p8 = _ada(c_all, w_ada, b_ada)
    mod_p4 = mod_p8[:, :BATCH].reshape(DEPTH, BATCH, 1, 6 * D_MODEL)

    rw_hi = router_w.astype(_bf16)
    rw_lo = (router_w - rw_hi.astype(_f32)).astype(_bf16)
    lane_pad = ((0, 0), (0, 0), (0, LANES - N_EXPERTS))
    params = {
        "nmix": norm_mix.reshape(DEPTH, 1, D_MODEL),
        "nffn": norm_ffn.reshape(DEPTH, 1, D_MODEL),
        "poolw": pool_w,
        "pscale": pool_scale.reshape(DEPTH, 1, gw),
        "sgun": sgu_norm.reshape(DEPTH, 1, gw),
        "sguw": sgu_w,
        "sgub": jnp.repeat(jnp.swapaxes(sgu_b, 1, 2), SGU_CHUNK, axis=2),
        "sgwc": jnp.repeat(jnp.transpose(sgu_w[:, :, :ns, :ns], (0, 2, 3, 1)).reshape(DEPTH, ns * ns, SGU_HEADS),
                           SGU_CHUNK, axis=2),
        "sgbc": jnp.repeat(jnp.swapaxes(sgu_b[:, :, :ns], 1, 2), SGU_CHUNK, axis=2),
        "cw": conf_w,
        "cb": conf_b.reshape(DEPTH, 1, gw),
        "clg": conf_ln_g.reshape(DEPTH, 1, gw),
        "clb": conf_ln_b.reshape(DEPTH, 1, gw),
        "scw": sc_w,
        "onorm": out_norm.reshape(DEPTH, 1, D_MODEL),
        "rw": jnp.concatenate([jnp.pad(rw_hi, lane_pad), jnp.pad(rw_lo, lane_pad)], axis=2),
        "rb": jnp.pad(router_b, ((0, 0), (0, LANES - N_EXPERTS))).reshape(DEPTH, 1, LANES),
    }
    w_out_b = w_out.astype(_bf16)
    b_up4 = b_up.reshape(DEPTH, N_EXPERTS, 1, 2 * D_EXPERT)
    b_down4 = b_down.reshape(DEPTH, N_EXPERTS, 1, D_MODEL)
    nfin = norm_final.reshape(1, D_MODEL)

    pool_t = jnp.transpose(state_pool, (0, 2, 1, 3))
    conf_t = jnp.transpose(state_conformer, (0, 2, 1, 3))
    sc_t = jnp.transpose(state_shortconv, (0, 2, 1, 3))

    x_p = x_prompt.reshape(T_PROMPT, D_MODEL)
    x_s = jnp.transpose(x_sample, (1, 0, 2)).reshape(T_SAMPLE, D_MODEL)
    xs_blk = 0

    pool_p, conf_p, sc_p, pool_s, conf_s, sc_s, v_s = [], [], [], [], [], [], []
    for l in range(DEPTH):
        ycat_p, po, co, so = _a_prompt(l, x_p, mod_p4, w_in, params)
        ycat_s, a_new, g_new, z_new, v_new = _a_sample(l, x_s, xs_blk, mod_s, w_in, params, pool_t, conf_t, sc_t)
        pool_p.append(po[:, POOL_HALO - (POOL_MAX - 1):])
        conf_p.append(co[:, CONF_HALO - (CONF_KERNEL - 1):])
        sc_p.append(so[:, SC_HALO - (SC_KERNEL - 1):])
        pool_s.append(jnp.concatenate([state_pool[l][:, ns:], jnp.transpose(a_new, (1, 0, 2))], axis=1))
        conf_s.append(jnp.concatenate([state_conformer[l][:, ns:], jnp.transpose(g_new, (1, 0, 2))], axis=1))
        sc_s.append(jnp.transpose(z_new, (1, 0, 2))[:, ns - (SC_KERNEL - 1):])
        v_s.append(jnp.transpose(v_new, (1, 0, 2)))

        x1, h_ext, pos, post, meta = _stage_b(l, x_p, x_s, xs_blk, ycat_p, ycat_s, w_out_b, mod_p4, mod_s, params)
        item_e, item_lo, item_nt, seg_start, seg_rows, used, n_items = _work_items(meta)
        x_sorted, g_sorted = _local_sort(used, post, h_ext)
        y_sorted = _moe_stream(l, item_e, item_lo, item_nt, seg_start, seg_rows, used, n_items, w_up, b_up4,
                               w_down, b_down4, x_sorted, g_sorted)
        if l < DEPTH - 1:
            x_p = x_s = _combine(l, x1, y_sorted, pos, mod_p4, mod_s, nfin, final=False)
            xs_blk = T_PROMPT // T_SAMPLE
        else:
            y_p, y_s = _combine(l, x1, y_sorted, pos, mod_p4, mod_s, nfin, final=True)

    y_prompt = y_p.reshape(BATCH, SEQ, D_MODEL)
    y_sample = jnp.transpose(y_s.reshape(ns, nb, D_MODEL), (1, 0, 2))
    return (y_prompt, y_sample, jnp.stack(pool_p), jnp.stack(pool_s), jnp.stack(conf_p), jnp.stack(conf_s),
            jnp.stack(sc_p), jnp.stack(sc_s), jnp.stack(v_s))
```

```python
import functools

import jax
import jax.numpy as jnp
from jax import lax
from jax.experimental import pallas as pl
from jax.experimental.pallas import tpu as pltpu

D_MODEL = 2048
BATCH = 4
SEQ = 2048
DEPTH = 2
DEC_BATCH = 128
DEC_SEQ = 4
PAST_LEN = 16384
GROUP_WIDTH = 512
POOL_WINDOWS = (2, 4, 8, 16)
POOL_MAX = 16
POOL_CH = 128
SGU_HEADS = 4
SGU_CHUNK = 128
CONF_KERNEL = 31
SC_KERNEL = 3
IN_COLS = 8 * GROUP_WIDTH
N_EXPERTS = 32
TOP_K = 4
D_EXPERT = 2048
SWIGLU_ALPHA = 1.702
SWIGLU_LIMIT = 7.0
EPS = 1e-5

T_PROMPT = BATCH * SEQ
T_SAMPLE = DEC_BATCH * DEC_SEQ
T_ALL = T_PROMPT + T_SAMPLE

LANES = 128
BF16_ROWS = 16
VMEM_LIMIT = 56 * 1024 * 1024

A_TM = 1024
A_TN = 512
A_STEPS = IN_COLS // A_TN
POOL_HALO = 16
CONF_HALO = 32
SC_HALO = 8
CONV_CHUNK = 256
CONV_SUB = 64

B_TM = 512
N_BLOCKS = T_ALL // B_TM
B_PROMPT_BLOCKS = T_PROMPT // B_TM
SEG = BF16_ROWS
SORT_TM = 512
BLOCK_ROWS = -(-(B_TM * TOP_K + N_EXPERTS * (SEG - 1)) // SORT_TM) * SORT_TM
SORT_CHUNKS = BLOCK_ROWS // SORT_TM
TAIL_TM = 256
GATE_TERMS = 3
SIDE_ID0 = GATE_TERMS * TOP_K
H_EXT = D_MODEL + LANES

MOE_TM = 256
MOE_MT = 8
MOE_TN = 512
TILE_SEGS = MOE_TM // SEG
MOE_TILES = (T_ALL * TOP_K + N_BLOCKS * N_EXPERTS * (SEG - 1) + N_EXPERTS * (MOE_TM - 1)) // MOE_TM
MOE_ITEMS = (MOE_TILES + N_EXPERTS * (MOE_MT - 1)) // MOE_MT

_f32 = jnp.float32
_bf16 = jnp.bfloat16


def _rms(x):
    return x * lax.rsqrt(jnp.mean(x * x, axis=-1, keepdims=True) + EPS)


def _ln(x):
    xc = x - jnp.mean(x, axis=-1, keepdims=True)
    return xc * lax.rsqrt(jnp.mean(xc * xc, axis=-1, keepdims=True) + EPS)


def _gelu(x):
    return 0.5 * x * (1.0 + lax.erf(x * (0.5 ** 0.5)))


def _slab_mul(x, m):
    r, c = x.shape
    return (x.reshape(r // 128, 128, c) * m[None]).reshape(r, c)


def _slab_add(x, m):
    r, c = x.shape
    return (x.reshape(r // 128, 128, c) + m[None]).reshape(r, c)


ADA_TN = 1024
ADA_ROWS = DEC_BATCH + 8


def _ada_body(c_ref, w_ref, b_ref, os_ref, op_ref):
    c = c_ref[...]
    a = (c * jax.nn.sigmoid(c)).astype(_bf16)
    r = jnp.dot(a, w_ref[...].astype(_bf16), preferred_element_type=_f32) + b_ref[...]
    os_ref[...] = r[:DEC_BATCH]
    op_ref[...] = r[DEC_BATCH:]


def _ada(c_all, w_ada, b_ada):
    nj = 6 * D_MODEL // ADA_TN
    return pl.pallas_call(
        _ada_body,
        grid=(DEPTH, nj),
        in_specs=[
            pl.BlockSpec((ADA_ROWS, D_MODEL), lambda l, j: (0, 0)),
            pl.BlockSpec((None, D_MODEL, ADA_TN), lambda l, j: (l, 0, j)),
            pl.BlockSpec((None, 1, ADA_TN), lambda l, j: (l, 0, j)),
        ],
        out_specs=[
            pl.BlockSpec((None, DEC_BATCH, ADA_TN), lambda l, j: (l, 0, j)),
            pl.BlockSpec((None, 8, ADA_TN), lambda l, j: (l, 0, j)),
        ],
        out_shape=[
            jax.ShapeDtypeStruct((DEPTH, DEC_BATCH, 6 * D_MODEL), _f32),
            jax.ShapeDtypeStruct((DEPTH, 8, 6 * D_MODEL), _f32),
        ],
        compiler_params=pltpu.CompilerParams(
            dimension_semantics=("arbitrary", "arbitrary"), vmem_limit_bytes=VMEM_LIMIT),
        name="ada_mod",
    )(c_all, w_ada, b_ada.reshape(DEPTH, 1, 6 * D_MODEL))


def _a_prompt_body(x_ref, sh1_ref, sc1_ref, nmix_ref, w_ref, poolw_ref, pscale_ref, sgun_ref, sguw_ref,
                   sgub_ref, cw_ref, cb_ref, clg_ref, clb_ref, scw_ref, onorm_ref,
                   y_ref, pool_o, conf_o, sc_o,
                   h_ref, bufa, bufb, exta, extc, extz, shifted):
    i = pl.program_id(1)
    n = pl.program_id(2)
    tm = A_TM
    gw = GROUP_WIDTH

    @pl.when(n == 0)
    def _():
        hn = _rms(x_ref[...]) * nmix_ref[...]
        h_ref[...] = (hn * (1.0 + sc1_ref[...]) + sh1_ref[...]).astype(_bf16)

    def proj():
        return jnp.dot(h_ref[...], w_ref[...], preferred_element_type=_f32)

    def emit(g, y):
        yn = _rms(y) * onorm_ref[:, g * gw:(g + 1) * gw]
        y_ref[:, g * gw:(g + 1) * gw] = yn.astype(_bf16)

    def carry_halo(ext, halo):
        @pl.when(i == 0)
        def _():
            ext[0:halo, :] = jnp.zeros((halo, gw), _f32)

        @pl.when(i > 0)
        def _():
            ext[0:halo, :] = ext[tm:tm + halo, :]

    @pl.when(n == 0)
    def _():
        a = proj()
        carry_halo(exta, POOL_HALO)
        exta[POOL_HALO:, :] = a
        pool_o[...] = a[tm - POOL_HALO:, :]
        pos = (i * tm + lax.broadcasted_iota(jnp.int32, (tm, 1), 0)).astype(_f32)
        for gi, w in enumerate(POOL_WINDOWS):
            c0 = gi * POOL_CH
            s = exta[POOL_HALO:POOL_HALO + tm, c0:c0 + POOL_CH]
            for j in range(1, w):
                s = s + exta[POOL_HALO - j:POOL_HALO - j + tm, c0:c0 + POOL_CH]
            cnt = jnp.minimum(float(w), pos + 1.0)
            pooled = s / cnt - a[:, c0:c0 + POOL_CH]
            yg = jnp.dot(pooled.astype(_bf16), poolw_ref[gi].astype(_bf16), preferred_element_type=_f32)
            bufb[:, c0:c0 + POOL_CH] = yg
        emit(0, bufb[...] * pscale_ref[...])

    @pl.when(n == 1)
    def _():
        bufa[...] = _gelu(proj())

    @pl.when(n == 2)
    def _():
        v = _ln(_gelu(proj())) * sgun_ref[...]
        vb = v.astype(_bf16)
        row = lax.broadcasted_iota(jnp.int32, (SGU_CHUNK, SGU_CHUNK), 0)
        col = lax.broadcasted_iota(jnp.int32, (SGU_CHUNK, SGU_CHUNK), 1)
        for g in range(SGU_HEADS):
            c0 = g * SGU_CHUNK
            wg = jnp.where(row >= col, sguw_ref[g], 0.0).astype(_bf16)
            for c in range(tm // SGU_CHUNK):
                r0 = c * SGU_CHUNK
                m = jnp.dot(wg, vb[r0:r0 + SGU_CHUNK, c0:c0 + SGU_CHUNK], preferred_element_type=_f32)
                bufb[r0:r0 + SGU_CHUNK, c0:c0 + SGU_CHUNK] = m + sgub_ref[:, c0:c0 + SGU_CHUNK]
        emit(1, bufa[...] * bufb[...])

    @pl.when(n == 3)
    def _():
        bufa[...] = proj()

    @pl.when(n == 4)
    def _():
        g = bufa[...] * jax.nn.sigmoid(proj())
        carry_halo(extc, CONF_HALO)
        extc[CONF_HALO:, :] = g
        conf_o[...] = g[tm - CONF_HALO:, :]
        off = CONF_HALO - (CONF_KERNEL - 1)
        for ch in range(tm // CONV_CHUNK):
            base = ch * CONV_CHUNK
            for r in range(8):
                nr = CONV_CHUNK + 8 * ((CONF_HALO - r) // 8)
                shifted[r, 0:nr, :] = extc[base + r:base + r + nr, :]

            def sub(s, carry, base=base):
                i0 = s * CONV_SUB
                acc = jnp.broadcast_to(cb_ref[...], (CONV_SUB, gw))
                for k in range(CONF_KERNEL):
                    q, r = divmod(off + k, 8)
                    acc = acc + cw_ref[k:k + 1, :] * shifted[r, pl.ds(pl.multiple_of(i0 + 8 * q, 8), CONV_SUB), :]
                bufb[pl.ds(pl.multiple_of(base + i0, CONV_SUB), CONV_SUB), :] = acc
                return carry
            lax.fori_loop(0, CONV_CHUNK // CONV_SUB, sub, 0)
        yn = _ln(bufb[...]) * clg_ref[...] + clb_ref[...]
        emit(2, yn * jax.nn.sigmoid(yn))

    @pl.when(n == 5)
    def _():
        bufa[...] = proj()

    @pl.when(n == 6)
    def _():
        bufb[...] = proj()

    @pl.when(n == 7)
    def _():
        z = bufb[...] * proj()
        carry_halo(extz, SC_HALO)
        extz[SC_HALO:, :] = z
        sc_o[...] = z[tm - SC_HALO:, :]
        off = SC_HALO - (SC_KERNEL - 1)
        acc = scw_ref[0:1, :] * extz[off:off + tm, :]
        for k in range(1, SC_KERNEL):
            acc = acc + scw_ref[k:k + 1, :] * extz[off + k:off + k + tm, :]
        emit(3, bufa[...] * acc)


def _layer_spec(shape, l, nd):
    zeros = (0,) * len(shape)
    if nd == 1:
        return pl.BlockSpec((None,) + shape, lambda a: (l,) + zeros)
    if nd == 2:
        return pl.BlockSpec((None,) + shape, lambda a, b: (l,) + zeros)
    return pl.BlockSpec((None,) + shape, lambda a, b, c: (l,) + zeros)


def _mixer_params(l, p, nd):
    gw = GROUP_WIDTH
    arrs = [p["cw"], p["cb"], p["clg"], p["clb"], p["scw"], p["onorm"]]
    shapes = [(CONF_KERNEL, gw), (1, gw), (1, gw), (1, gw), (SC_KERNEL, gw), (1, D_MODEL)]
    return arrs, [_layer_spec(s, l, nd) for s in shapes]


def _a_prompt(l, x_all, mod_p4, w_in_b, p):
    gw = GROUP_WIDTH
    ni = SEQ // A_TM
    marrs, mspecs = _mixer_params(l, p, 3)
    in_specs = [
        pl.BlockSpec((A_TM, D_MODEL), lambda b, i, n: (b * ni + i, 0)),
        pl.BlockSpec((None, None, 1, D_MODEL), lambda b, i, n: (l, b, 0, 0)),
        pl.BlockSpec((None, None, 1, D_MODEL), lambda b, i, n: (l, b, 0, 1)),
        _layer_spec((1, D_MODEL), l, 3),
        pl.BlockSpec((None, D_MODEL, A_TN), lambda b, i, n: (l, 0, n)),
        _layer_spec((len(POOL_WINDOWS), POOL_CH, POOL_CH), l, 3),
        _layer_spec((1, gw), l, 3),
        _layer_spec((1, gw), l, 3),
        _layer_spec((SGU_HEADS, SGU_CHUNK, SGU_CHUNK), l, 3),
        _layer_spec((SGU_CHUNK, gw), l, 3),
    ] + mspecs
    out_specs = [
        pl.BlockSpec((A_TM, D_MODEL), lambda b, i, n: (b * ni + i, 0)),
        pl.BlockSpec((None, POOL_HALO, gw), lambda b, i, n: (b, 0, 0)),
        pl.BlockSpec((None, CONF_HALO, gw), lambda b, i, n: (b, 0, 0)),
        pl.BlockSpec((None, SC_HALO, gw), lambda b, i, n: (b, 0, 0)),
    ]
    out_shape = [
        jax.ShapeDtypeStruct((T_PROMPT, D_MODEL), _bf16),
        jax.ShapeDtypeStruct((BATCH, POOL_HALO, gw), _f32),
        jax.ShapeDtypeStruct((BATCH, CONF_HALO, gw), _f32),
        jax.ShapeDtypeStruct((BATCH, SC_HALO, gw), _f32),
    ]
    scratch = [
        pltpu.VMEM((A_TM, D_MODEL), _bf16),
        pltpu.VMEM((A_TM, gw), _f32),
        pltpu.VMEM((A_TM, gw), _f32),
        pltpu.VMEM((POOL_HALO + A_TM, gw), _f32),
        pltpu.VMEM((CONF_HALO + A_TM, gw), _f32),
        pltpu.VMEM((SC_HALO + A_TM, gw), _f32),
        pltpu.VMEM((8, CONV_CHUNK + CONF_HALO, gw), _f32),
    ]
    return pl.pallas_call(
        _a_prompt_body,
        grid=(BATCH, ni, A_STEPS),
        in_specs=in_specs,
        out_specs=out_specs,
        out_shape=out_shape,
        scratch_shapes=scratch,
        compiler_params=pltpu.CompilerParams(
            dimension_semantics=("arbitrary", "arbitrary", "arbitrary"), vmem_limit_bytes=VMEM_LIMIT),
        name="mixers_prompt",
    )(x_all, mod_p4, mod_p4, p["nmix"], w_in_b, p["poolw"], p["pscale"], p["sgun"], p["sguw"], p["sgub"], *marrs)


def _a_sample_body(x_ref, sh1_ref, sc1_ref, nmix_ref, w_ref, poolw_ref, pscale_ref, sgun_ref, sgwc_ref,
                   sgbc_ref, cw_ref, cb_ref, clg_ref, clb_ref, scw_ref, onorm_ref, pst_ref, cst_ref, sst_ref,
                   y_ref, a_o, g_o, z_o, v_o,
                   h_ref, bufa, bufb):
    n = pl.program_id(0)
    gw = GROUP_WIDTH
    nb = DEC_BATCH
    ns = DEC_SEQ

    @pl.when(n == 0)
    def _():
        hn = _rms(x_ref[...]) * nmix_ref[...]
        h_ref[...] = _slab_add(_slab_mul(hn, 1.0 + sc1_ref[...]), sh1_ref[...]).astype(_bf16)

    def proj():
        return jnp.dot(h_ref[...], w_ref[...], preferred_element_type=_f32)

    def emit(g, y):
        yn = _rms(y) * onorm_ref[:, g * gw:(g + 1) * gw]
        y_ref[:, g * gw:(g + 1) * gw] = yn.astype(_bf16)

    def slab(x, s):
        return x[s * nb:(s + 1) * nb, :]

    @pl.when(n == 0)
    def _():
        a = proj()
        a_o[...] = a.reshape(ns, nb, gw)
        npast = POOL_MAX - 1
        for s in range(ns):
            for gi, w in enumerate(POOL_WINDOWS):
                c0 = gi * POOL_CH
                acc = None
                for j in range(w):
                    e = npast + s - j
                    term = pst_ref[e, :, c0:c0 + POOL_CH] if e < npast else slab(a, e - npast)[:, c0:c0 + POOL_CH]
                    acc = term if acc is None else acc + term
                cnt = min(float(w), PAST_LEN + s + 1.0)
                pooled = acc / cnt - slab(a, s)[:, c0:c0 + POOL_CH]
                bufb[s * nb:(s + 1) * nb, c0:c0 + POOL_CH] = pooled
        pooled = bufb[...].astype(_bf16)
        for gi in range(len(POOL_WINDOWS)):
            c0 = gi * POOL_CH
            bufa[:, c0:c0 + POOL_CH] = jnp.dot(pooled[:, c0:c0 + POOL_CH], poolw_ref[gi].astype(_bf16),
                                               preferred_element_type=_f32)
        emit(0, bufa[...] * pscale_ref[...])

    @pl.when(n == 1)
    def _():
        bufa[...] = _gelu(proj())

    @pl.when(n == 2)
    def _():
        v = _ln(_gelu(proj())) * sgun_ref[...]
        v_o[...] = v.reshape(ns, nb, gw)
        for t in range(ns):
            m = jnp.broadcast_to(sgbc_ref[t:t + 1, :], (nb, gw))
            for s in range(t + 1):
                m = m + sgwc_ref[t * ns + s:t * ns + s + 1, :] * slab(v, s)
            bufb[t * nb:(t + 1) * nb, :] = m
        emit(1, bufa[...] * bufb[...])

    @pl.when(n == 3)
    def _():
        bufa[...] = proj()

    @pl.when(n == 4)
    def _():
        g = bufa[...] * jax.nn.sigmoid(proj())
        g_o[...] = g.reshape(ns, nb, gw)
        npast = CONF_KERNEL - 1
        for s in range(ns):
            acc = jnp.broadcast_to(cb_ref[...], (nb, gw))
            for k in range(CONF_KERNEL):
                e = s + k
                term = cst_ref[e] if e < npast else slab(g, e - npast)
                acc = acc + cw_ref[k:k + 1, :] * term
            bufb[s * nb:(s + 1) * nb, :] = acc
        yn = _ln(bufb[...]) * clg_ref[...] + clb_ref[...]
        emit(2, yn * jax.nn.sigmoid(yn))

    @pl.when(n == 5)
    def _():
        bufa[...] = proj()

    @pl.when(n == 6)
    def _():
        bufb[...] = proj()

    @pl.when(n == 7)
    def _():
        z = bufb[...] * proj()
        z_o[...] = z.reshape(ns, nb, gw)
        npast = SC_KERNEL - 1
        for s in range(ns):
            acc = None
            for k in range(SC_KERNEL):
                e = s + k
                term = scw_ref[k:k + 1, :] * (sst_ref[e] if e < npast else slab(z, e - npast))
                acc = term if acc is None else acc + term
            bufb[s * nb:(s + 1) * nb, :] = acc
        emit(3, bufa[...] * bufb[...])


def _a_sample(l, x_all, x_blk, mod_s, w_in_b, p, pool_t, conf_t, sc_t):
    gw = GROUP_WIDTH
    ns, nb = DEC_SEQ, DEC_BATCH
    marrs, mspecs = _mixer_params(l, p, 1)
    in_specs = [
        pl.BlockSpec((T_SAMPLE, D_MODEL), lambda n: (x_blk, 0)),
        pl.BlockSpec((None, nb, D_MODEL), lambda n: (l, 0, 0)),
        pl.BlockSpec((None, nb, D_MODEL), lambda n: (l, 0, 1)),
        _layer_spec((1, D_MODEL), l, 1),
        pl.BlockSpec((None, D_MODEL, A_TN), lambda n: (l, 0, n)),
        _layer_spec((len(POOL_WINDOWS), POOL_CH, POOL_CH), l, 1),
        _layer_spec((1, gw), l, 1),
        _layer_spec((1, gw), l, 1),
        _layer_spec((ns * ns, gw), l, 1),
        _layer_spec((ns, gw), l, 1),
    ] + mspecs + [
        _layer_spec((POOL_MAX - 1, nb, gw), l, 1),
        _layer_spec((CONF_KERNEL - 1, nb, gw), l, 1),
        _layer_spec((SC_KERNEL - 1, nb, gw), l, 1),
    ]
    out_specs = [pl.BlockSpec((T_SAMPLE, D_MODEL), lambda n: (0, 0))] + [
        pl.BlockSpec((ns, nb, gw), lambda n: (0, 0, 0)) for _ in range(4)]
    out_shape = [jax.ShapeDtypeStruct((T_SAMPLE, D_MODEL), _bf16)] + [
        jax.ShapeDtypeStruct((ns, nb, gw), _f32) for _ in range(4)]
    scratch = [
        pltpu.VMEM((T_SAMPLE, D_MODEL), _bf16),
        pltpu.VMEM((T_SAMPLE, gw), _f32),
        pltpu.VMEM((T_SAMPLE, gw), _f32),
    ]
    return pl.pallas_call(
        _a_sample_body,
        grid=(A_STEPS,),
        in_specs=in_specs,
        out_specs=out_specs,
        out_shape=out_shape,
        scratch_shapes=scratch,
        compiler_params=pltpu.CompilerParams(
            dimension_semantics=("arbitrary",), vmem_limit_bytes=VMEM_LIMIT),
        name="mixers_sample",
    )(x_all, mod_s, mod_s, p["nmix"], w_in_b, p["poolw"], p["pscale"], p["sgun"], p["sgwc"], p["sgbc"],
      *marrs, pool_t, conf_t, sc_t)


def _pick_mod(is_prompt, mp_ref, ms_ref):
    return jnp.where(is_prompt, jnp.broadcast_to(mp_ref[...], (DEC_BATCH, D_MODEL)), ms_ref[...])


def _b_body(xp_ref, xs_ref, yp_ref, ys_ref, w_ref, mpg1, mpsh2, mpsc2, msg1, mssh2, mssc2, nffn_ref, rw_ref, rb_ref,
            x1_ref, h2_ref, pos_ref, post_ref, meta_ref):
    i = pl.program_id(0)
    isp = i < B_PROMPT_BLOCKS
    ycat = jnp.where(isp, yp_ref[...], ys_ref[...])
    o = jnp.dot(ycat, w_ref[...], preferred_element_type=_f32)
    x1 = jnp.where(isp, xp_ref[...], xs_ref[...]) + _slab_mul(o, _pick_mod(isp, mpg1, msg1))
    x1_ref[...] = x1
    hn = _rms(x1) * nffn_ref[...]
    h2 = _slab_add(_slab_mul(hn, 1.0 + _pick_mod(isp, mpsc2, mssc2)), _pick_mod(isp, mpsh2, mssh2))
    hi = h2.astype(_bf16)
    h2_ref[:, 0:D_MODEL] = hi
    lo = (h2 - hi.astype(_f32)).astype(_bf16)
    l1 = jnp.dot(hi, rw_ref[...], preferred_element_type=_f32)
    l2 = jnp.dot(lo, rw_ref[:, 0:LANES], preferred_element_type=_f32)
    logits = l1[:, 0:LANES] + l1[:, LANES:] + l2 + rb_ref[...]
    lane = lax.broadcasted_iota(jnp.int32, (B_TM, LANES), 1).astype(_f32)
    neg = jnp.float32(-jnp.inf)
    cur = jnp.where(lane < N_EXPERTS, logits, neg)
    vals, ids = [], []
    for _ in range(TOP_K):
        m = jnp.max(cur, axis=-1, keepdims=True)
        ix = jnp.min(jnp.where(cur == m, lane, float(LANES)), axis=-1, keepdims=True)
        vals.append(m)
        ids.append(ix)
        cur = jnp.where(lane == ix, neg, cur)
    es = [jnp.exp(v - vals[0]) for v in vals]
    tot = es[0] + es[1] + es[2] + es[3]
    side = jnp.zeros((B_TM, LANES), _f32)
    for k in range(TOP_K):
        g = es[k] / tot
        g0 = g.astype(_bf16).astype(_f32)
        g1 = (g - g0).astype(_bf16).astype(_f32)
        for j, term in enumerate((g0, g1, g - g0 - g1)):
            side = jnp.where(lane == float(GATE_TERMS * k + j), term, side)
        side = jnp.where(lane == float(SIDE_ID0 + k), ids[k], side)
    h2_ref[:, D_MODEL:] = side.astype(_bf16)

    hit = [lane == ix for ix in ids]
    chosen = jnp.zeros((B_TM, LANES), _f32)
    for h in hit:
        chosen = jnp.where(h, 1.0, chosen)
    cnt = jnp.sum(chosen, axis=0, keepdims=True)
    seg_rows = jnp.floor((cnt + (SEG - 1.0)) * (1.0 / SEG)) * SEG
    er = lax.broadcasted_iota(jnp.int32, (LANES, LANES), 0)
    ec = lax.broadcasted_iota(jnp.int32, (LANES, LANES), 1)
    before = jnp.where(er < ec, 1.0, 0.0).astype(_bf16)
    seg_start = jnp.dot(jnp.broadcast_to(seg_rows, (8, LANES)).astype(_bf16), before,
                        preferred_element_type=_f32)[0:1, :]
    tr = lax.broadcasted_iota(jnp.int32, (B_TM, B_TM), 0)
    tc = lax.broadcasted_iota(jnp.int32, (B_TM, B_TM), 1)
    earlier = jnp.where(tr > tc, 1.0, 0.0).astype(_bf16)
    rank = jnp.dot(earlier, chosen.astype(_bf16), preferred_element_type=_f32)
    base = seg_start + rank
    pos = jnp.zeros((B_TM, LANES), _f32)
    for k in range(TOP_K):
        pk = jnp.sum(jnp.where(hit[k], base, 0.0), axis=-1, keepdims=True)
        pos = jnp.where(lane == float(k), pk, pos)
    pos_ref[...] = pos
    post_ref[...] = jnp.transpose(pos)[0:8, :]
    sub = lax.broadcasted_iota(jnp.int32, (8, LANES), 0)
    meta = jnp.where(sub == 0, jnp.broadcast_to(seg_start, (8, LANES)),
                     jnp.where(sub == 1, jnp.broadcast_to(seg_rows, (8, LANES)), 0.0))
    meta_ref[...] = meta.astype(jnp.int32)


def _stage_b(l, x_p, x_s, xs_blk, ycat_p, ycat_s, w_out_b, mod_p4, mod_s, p):
    def pb(i):
        return jnp.minimum(i // (SEQ // B_TM), BATCH - 1)

    def mp_spec(j):
        return pl.BlockSpec((None, None, 1, D_MODEL), lambda i: (l, pb(i), 0, j))

    def ms_spec(j):
        return pl.BlockSpec((None, DEC_BATCH, D_MODEL), lambda i: (l, 0, j))

    row_spec = pl.BlockSpec((B_TM, D_MODEL), lambda i: (i, 0))
    in_specs = [
        pl.BlockSpec((B_TM, D_MODEL), lambda i: (jnp.minimum(i, B_PROMPT_BLOCKS - 1), 0)),
        pl.BlockSpec((T_SAMPLE, D_MODEL), lambda i: (xs_blk, 0), pipeline_mode=pl.Buffered(1)),
        pl.BlockSpec((B_TM, D_MODEL), lambda i: (jnp.minimum(i, B_PROMPT_BLOCKS - 1), 0)),
        pl.BlockSpec((T_SAMPLE, D_MODEL), lambda i: (0, 0), pipeline_mode=pl.Buffered(1)),
        pl.BlockSpec((None, D_MODEL, D_MODEL), lambda i: (l, 0, 0), pipeline_mode=pl.Buffered(1)),
        mp_spec(2), mp_spec(3), mp_spec(4), ms_spec(2), ms_spec(3), ms_spec(4),
        _layer_spec((1, D_MODEL), l, 1),
        _layer_spec((D_MODEL, 2 * LANES), l, 1),
        _layer_spec((1, LANES), l, 1),
    ]
    lane_spec = pl.BlockSpec((B_TM, LANES), lambda i: (i, 0))
    return pl.pallas_call(
        _b_body,
        grid=(N_BLOCKS,),
        in_specs=in_specs,
        out_specs=[row_spec, pl.BlockSpec((B_TM, H_EXT), lambda i: (i, 0)), lane_spec,
                   pl.BlockSpec((None, 8, B_TM), lambda i: (i, 0, 0)),
                   pl.BlockSpec((None, 8, LANES), lambda i: (i, 0, 0))],
        out_shape=[
            jax.ShapeDtypeStruct((T_ALL, D_MODEL), _f32),
            jax.ShapeDtypeStruct((T_ALL, H_EXT), _bf16),
            jax.ShapeDtypeStruct((T_ALL, LANES), _f32),
            jax.ShapeDtypeStruct((N_BLOCKS, 8, B_TM), _f32),
            jax.ShapeDtypeStruct((N_BLOCKS, 8, LANES), jnp.int32),
        ],
        compiler_params=pltpu.CompilerParams(
            dimension_semantics=("arbitrary",), vmem_limit_bytes=VMEM_LIMIT),
        name="outproj_router",
    )(x_p, x_s, ycat_p, ycat_s, w_out_b, mod_p4, mod_p4, mod_p4, mod_s, mod_s, mod_s, p["nffn"], p["rw"], p["rb"])


def _sort_body(used_ref, post_ref, h_ref, o_ref, so_ref):
    used = used_ref[pl.program_id(0)]

    def rows(r0, nr):
        j = (r0 + lax.broadcasted_iota(jnp.int32, (nr, B_TM), 0)).astype(_f32)
        sel = jnp.zeros((nr, B_TM), _f32)
        for k in range(TOP_K):
            sel = jnp.where(j == post_ref[k:k + 1, :], 1.0, sel)
        res = jnp.dot(sel.astype(_bf16), h_ref[...], preferred_element_type=_f32)
        o_ref[r0:r0 + nr, :] = res[:, 0:D_MODEL].astype(_bf16)
        so_ref[r0:r0 + nr, :] = res[:, D_MODEL:]

    for c in range(B_TM * TOP_K // SORT_TM):
        rows(c * SORT_TM, SORT_TM)
    for r0 in range(B_TM * TOP_K, BLOCK_ROWS, TAIL_TM):
        @pl.when(used > r0)
        def _(r0=r0):
            rows(r0, TAIL_TM)

        @pl.when(used <= r0)
        def _(r0=r0):
            o_ref[r0:r0 + TAIL_TM, :] = jnp.zeros((TAIL_TM, D_MODEL), _bf16)
            so_ref[r0:r0 + TAIL_TM, :] = jnp.zeros((TAIL_TM, LANES), _f32)


def _local_sort(used, post, h_ext):
    grid_spec = pltpu.PrefetchScalarGridSpec(
        num_scalar_prefetch=1,
        grid=(N_BLOCKS,),
        in_specs=[
            pl.BlockSpec((None, 8, B_TM), lambda b, u: (b, 0, 0)),
            pl.BlockSpec((B_TM, H_EXT), lambda b, u: (b, 0)),
        ],
        out_specs=[pl.BlockSpec((BLOCK_ROWS, D_MODEL), lambda b, u: (b, 0)),
                   pl.BlockSpec((BLOCK_ROWS, LANES), lambda b, u: (b, 0))],
    )
    return pl.pallas_call(
        _sort_body,
        grid_spec=grid_spec,
        out_shape=[jax.ShapeDtypeStruct((N_BLOCKS * BLOCK_ROWS, D_MODEL), _bf16),
                   jax.ShapeDtypeStruct((N_BLOCKS * BLOCK_ROWS, LANES), _f32)],
        compiler_params=pltpu.CompilerParams(
            dimension_semantics=("arbitrary",), vmem_limit_bytes=VMEM_LIMIT),
        name="local_sort",
    )(used, post, h_ext)


BIG_TM = 2 * MOE_TM
SMALL_TM = MOE_TM // 2
MOE_DN = 1024
STREAM_UP_STEPS = D_EXPERT // MOE_TN
STREAM_DOWN_STEPS = D_MODEL // MOE_DN
STREAM_STEPS = STREAM_UP_STEPS + STREAM_DOWN_STEPS
MOE_WSLOTS = 3
WEIGHT_DMA_PRIORITY = 1


def _moe_stream_body(ie_ref, ilo_ref, int_ref, start_ref, rows_ref, used_ref, nitems_ref,
                     wup_hbm, wdn_hbm, bup_hbm, bdn_hbm, x_hbm, g_hbm, y_hbm,
                     wbuf, xbuf, hbuf, gbuf, bup, bdn, seg_src, nseg, pending,
                     sem_w, sem_b, sem_x, sem_y, *, l):
    tm, tn = MOE_TM, MOE_TN
    n_items = nitems_ref[0]
    n_steps = n_items * STREAM_STEPS

    def w_half(src_ref, e, col0, slot, half):
        return pltpu.make_async_copy(src_ref.at[l, e, :, pl.ds(col0, tn)],
                                     wbuf.at[slot, :, pl.ds(half * tn, tn)], sem_w.at[slot])

    def issue_w(s):
        e = ie_ref[s // STREAM_STEPS]
        n = s % STREAM_STEPS
        slot = s % MOE_WSLOTS
        for c in range(STREAM_UP_STEPS):
            @pl.when(n == c)
            def _(c=c):
                w_half(wup_hbm, e, c * tn, slot, 0).start(priority=WEIGHT_DMA_PRIORITY)
                w_half(wup_hbm, e, D_EXPERT + c * tn, slot, 1).start(priority=WEIGHT_DMA_PRIORITY)
        for c in range(STREAM_DOWN_STEPS):
            @pl.when(n == STREAM_UP_STEPS + c)
            def _(c=c):
                w_half(wdn_hbm, e, c * MOE_DN, slot, 0).start(priority=WEIGHT_DMA_PRIORITY)
                w_half(wdn_hbm, e, c * MOE_DN + tn, slot, 1).start(priority=WEIGHT_DMA_PRIORITY)

    def wait_w(s):
        slot = s % MOE_WSLOTS
        for half in range(2):
            w_half(wdn_hbm, 0, 0, slot, half).wait()

    def b_copies(item):
        e = ie_ref[item]
        par = item % 2
        return (pltpu.make_async_copy(bup_hbm.at[l, e], bup.at[par], sem_b.at[par]),
                pltpu.make_async_copy(bdn_hbm.at[l, e], bdn.at[par], sem_b.at[par]))

    def x_copy(p, src, dst):
        return pltpu.make_async_copy(x_hbm.at[pl.ds(src, SEG)], xbuf.at[p, pl.ds(dst, SEG)],
                                     sem_x.at[p * MOE_MT + dst // tm])

    def g_copy(p, src, dst):
        return pltpu.make_async_copy(g_hbm.at[pl.ds(src, SEG)], gbuf.at[p, pl.ds(dst, SEG)],
                                     sem_x.at[p * MOE_MT + dst // tm])

    def y_copy(p, src, dst):
        return pltpu.make_async_copy(xbuf.at[p, pl.ds(src, SEG)], y_hbm.at[pl.ds(dst, SEG)], sem_y.at[p])

    def drain(p):
        def w(_, carry):
            y_copy(p, 0, 0).wait()
            return carry
        lax.fori_loop(0, pending[p], w, 0)
        pending[p] = 0

    def tile_pieces(p, t):
        return jnp.clip(nseg[p] - t * TILE_SEGS, 0, TILE_SEGS)

    def plan(rows):
        n_big = rows // BIG_TM
        rem = rows - n_big * BIG_TM
        round_up = rem > BIG_TM - SMALL_TM
        n_big = n_big + jnp.where(round_up, 1, 0)
        rem = jnp.where(round_up, 0, rem)
        has_mid = rem > SMALL_TM
        has_small = (rem > 0) & ((rem <= SMALL_TM) | (rem > tm))
        off_mid = n_big * BIG_TM
        off_small = off_mid + jnp.where(has_mid, tm, 0)
        return n_big, has_mid, has_small, off_mid, off_small

    def load_item(item):
        p = item % 2
        e = ie_ref[item]
        lo = ilo_ref[item]
        nt = int_ref[item]
        hi = lo + nt * tm

        def per_block(b, q0):
            rows = rows_ref[b * N_EXPERTS + e]
            start = start_ref[b * N_EXPERTS + e]

            def per_piece(g, cc):
                src = pl.multiple_of(b * BLOCK_ROWS + start + g * SEG, SEG)
                dst = pl.multiple_of(q0 + g * SEG - lo, SEG)
                seg_src[p * (MOE_MT * TILE_SEGS) + dst // SEG] = src
                x_copy(p, src, dst).start()
                g_copy(p, src, dst).start()
                return cc
            pieces = rows // SEG
            g_lo = jnp.minimum((jnp.maximum(lo - q0, 0) + SEG - 1) // SEG, pieces)
            g_hi = jnp.minimum((jnp.maximum(hi - q0, 0) + SEG - 1) // SEG, pieces)
            lax.fori_loop(g_lo, g_hi, per_piece, 0)
            return q0 + rows
        end = lax.fori_loop(0, N_BLOCKS, per_block, 0)
        ns = (jnp.minimum(end, hi) - lo) // SEG
        nseg[p] = ns

        def zero_piece(g, cc):
            r = pl.multiple_of(g * SEG, SEG)
            xbuf[p, pl.ds(r, SEG), :] = jnp.zeros((SEG, D_MODEL), _bf16)
            gbuf[p, pl.ds(r, SEG), :] = jnp.zeros((SEG, LANES), _f32)
            return cc
        n_big, has_mid, has_small, _, off_small = plan(ns * SEG)
        computed = off_small + jnp.where(has_small, SMALL_TM, 0)
        lax.fori_loop(ns, computed // SEG, zero_piece, 0)

    pending[0] = 0
    pending[1] = 0
    issue_w(0)
    issue_w(1)
    for cp in b_copies(0):
        cp.start()
    load_item(0)

    def step(s, carry):
        item = s // STREAM_STEPS
        n = s % STREAM_STEPS
        e = ie_ref[item]
        par = item % 2
        slot = s % MOE_WSLOTS

        @pl.when(s + 2 < n_steps)
        def _():
            issue_w(s + 2)
        wait_w(s)

        @pl.when(n == 0)
        def _():
            for cp in b_copies(item):
                cp.wait()

        @pl.when((n == 1) & (item + 1 < n_items))
        def _():
            for cp in b_copies(item + 1):
                cp.start()
            drain(1 - par)
            load_item(item + 1)

        n_big, has_mid, has_small, off_mid, off_small = plan(nseg[par] * SEG)

        def up_rows(c, r0, nrows):
            if c == 0:
                for u in range(max(nrows // tm, 1)):
                    sub = r0 // tm + u

                    def w(_, c3, sub=sub):
                        x_copy(par, 0, pl.multiple_of(sub * tm, tm)).wait()
                        g_copy(par, 0, pl.multiple_of(sub * tm, tm)).wait()
                        return c3
                    lax.fori_loop(0, tile_pieces(par, sub), w, 0)
            up = lax.dot_general(xbuf[par, pl.ds(r0, nrows), :], wbuf[slot], (((1,), (0,)), ((), ())),
                                 preferred_element_type=_f32)
            bg = bup[par, :, c * tn:(c + 1) * tn]
            bl = bup[par, :, D_EXPERT + c * tn:D_EXPERT + (c + 1) * tn]
            glu = jnp.minimum(up[:, 0:tn] + bg, SWIGLU_LIMIT)
            lin = jnp.clip(up[:, tn:] + bl, -SWIGLU_LIMIT, SWIGLU_LIMIT)
            act = glu * jax.nn.sigmoid(SWIGLU_ALPHA * glu) * (lin + 1.0)
            hbuf[pl.ds(r0, nrows), c * tn:(c + 1) * tn] = act.astype(_bf16)

        def down_rows(c, r0, nrows):
            y = (lax.dot_general(hbuf[pl.ds(r0, nrows), :], wbuf[slot], (((1,), (0,)), ((), ())),
                                 preferred_element_type=_f32)
                 + bdn[par, :, c * MOE_DN:(c + 1) * MOE_DN])
            g = gbuf[par, pl.ds(r0, nrows), :]
            ef = e.astype(_f32)
            gate = jnp.zeros((nrows, 1), _f32)
            for k in range(TOP_K):
                gk = g[:, GATE_TERMS * k:GATE_TERMS * k + 1]
                for j in range(1, GATE_TERMS):
                    gk = gk + g[:, GATE_TERMS * k + j:GATE_TERMS * k + j + 1]
                gate = jnp.where(g[:, SIDE_ID0 + k:SIDE_ID0 + k + 1] == ef, gk, gate)
            xbuf[par, pl.ds(r0, nrows), c * MOE_DN:(c + 1) * MOE_DN] = (y * gate).astype(_bf16)
            if c == STREAM_DOWN_STEPS - 1:
                for q in range(nrows // SEG):
                    piece = r0 // SEG + q

                    @pl.when(piece < nseg[par])
                    def _(q=q, piece=piece):
                        dst = seg_src[par * (MOE_MT * TILE_SEGS) + piece]
                        y_copy(par, pl.multiple_of(r0 + q * SEG, SEG), pl.multiple_of(dst, SEG)).start()

        def all_rows(fn, c):
            def big(t, cc):
                fn(c, pl.multiple_of(t * BIG_TM, BIG_TM), BIG_TM)
                return cc
            lax.fori_loop(0, n_big, big, 0)

            @pl.when(has_mid)
            def _():
                fn(c, pl.multiple_of(off_mid, tm), tm)

            @pl.when(has_small)
            def _():
                fn(c, pl.multiple_of(off_small, SMALL_TM), SMALL_TM)

        for c in range(STREAM_UP_STEPS):
            @pl.when(n == c)
            def _(c=c):
                all_rows(up_rows, c)

        for c in range(STREAM_DOWN_STEPS):
            @pl.when(n == STREAM_UP_STEPS + c)
            def _(c=c):
                all_rows(down_rows, c)
                if c == STREAM_DOWN_STEPS - 1:
                    pending[par] = nseg[par]
        return carry
    lax.fori_loop(0, n_steps, step, 0)

    drain(0)
    drain(1)
    hbuf[0:SEG, :] = jnp.zeros((SEG, D_EXPERT), _bf16)

    def zero_copy(row):
        return pltpu.make_async_copy(hbuf.at[pl.ds(0, SEG)], y_hbm.at[pl.ds(row, SEG)], sem_y.at[0])

    def zero_block(b, carry):
        def zstart(g, cc):
            zero_copy(pl.multiple_of(b * BLOCK_ROWS + g * SEG, SEG)).start()
            return cc

        def zwait(g, cc):
            zero_copy(0).wait()
            return cc
        first = used_ref[b] // SEG
        lax.fori_loop(first, BLOCK_ROWS // SEG, zstart, 0)
        lax.fori_loop(first, BLOCK_ROWS // SEG, zwait, 0)
        return carry
    lax.fori_loop(0, N_BLOCKS, zero_block, 0)


def _moe_stream(l, item_e, item_lo, item_nt, seg_start, seg_rows, used, n_items, w_up, b_up4, w_down, b_down4,
                x_sorted, g_sorted):
    any_spec = pl.BlockSpec(memory_space=pl.ANY)
    grid_spec = pltpu.PrefetchScalarGridSpec(
        num_scalar_prefetch=7,
        grid=(1,),
        in_specs=[any_spec] * 6,
        out_specs=any_spec,
        scratch_shapes=[
            pltpu.VMEM((MOE_WSLOTS, D_MODEL, 2 * MOE_TN), _f32),
            pltpu.VMEM((2, MOE_MT * MOE_TM, D_MODEL), _bf16),
            pltpu.VMEM((MOE_MT * MOE_TM, D_EXPERT), _bf16),
            pltpu.VMEM((2, MOE_MT * MOE_TM, LANES), _f32),
            pltpu.VMEM((2, 1, 2 * D_EXPERT), _f32),
            pltpu.VMEM((2, 1, D_MODEL), _f32),
            pltpu.SMEM((2 * MOE_MT * TILE_SEGS,), jnp.int32),
            pltpu.SMEM((2,), jnp.int32),
            pltpu.SMEM((2,), jnp.int32),
            pltpu.SemaphoreType.DMA((MOE_WSLOTS,)),
            pltpu.SemaphoreType.DMA((2,)),
            pltpu.SemaphoreType.DMA((2 * MOE_MT,)),
            pltpu.SemaphoreType.DMA((2,)),
        ],
    )
    return pl.pallas_call(
        functools.partial(_moe_stream_body, l=l),
        grid_spec=grid_spec,
        out_shape=jax.ShapeDtypeStruct((N_BLOCKS * BLOCK_ROWS, D_MODEL), _bf16),
        compiler_params=pltpu.CompilerParams(
            dimension_semantics=("arbitrary",), vmem_limit_bytes=60 * 1024 * 1024),
        name="moe_experts",
    )(item_e, item_lo, item_nt, seg_start, seg_rows, used, n_items, w_up, w_down, b_up4, b_down4, x_sorted,
      g_sorted)


def _work_items(meta):
    seg_start = meta[:, 0, :N_EXPERTS]
    seg_rows = meta[:, 1, :N_EXPERTS]
    rows_e = jnp.sum(seg_rows, axis=0)
    nt = (rows_e + MOE_TM - 1) // MOE_TM
    ipe = (nt + MOE_MT - 1) // MOE_MT
    cum = jnp.cumsum(ipe)
    first = cum - ipe
    total = cum[-1]
    ii = jnp.arange(MOE_ITEMS, dtype=jnp.int32)
    ii_c = jnp.minimum(ii, total - 1)
    e_i = jnp.minimum(jnp.sum((cum[None, :] <= ii_c[:, None]).astype(jnp.int32), axis=1), N_EXPERTS - 1)
    valid = ii < total
    j = ii - first[e_i]
    lo_i = jnp.where(valid, j * (MOE_MT * MOE_TM), 0)
    nt_i = jnp.where(valid, jnp.minimum(MOE_MT, nt[e_i] - j * MOE_MT), 0)
    used = jnp.sum(seg_rows, axis=1)
    i32 = jnp.int32
    return (e_i.astype(i32), lo_i.astype(i32), nt_i.astype(i32), seg_start.reshape(-1).astype(i32),
            seg_rows.reshape(-1).astype(i32), used.astype(i32), total.astype(i32).reshape(1))


def _c_body(x1_ref, y_ref, pos_ref, mpg2, msg2, nfin_ref, *rest, final):
    if final:
        outp_ref, outs_ref, sel_ref = rest
    else:
        out_ref, sel_ref = rest
    i = pl.program_id(0)
    isp = i < B_PROMPT_BLOCKS
    for c in range(SORT_CHUNKS):
        j = (c * SORT_TM + lax.broadcasted_iota(jnp.int32, (B_TM, SORT_TM), 1)).astype(_f32)
        sel = jnp.zeros((B_TM, SORT_TM), _f32)
        for k in range(TOP_K):
            sel = jnp.where(j == pos_ref[:, k:k + 1], 1.0, sel)
        sel_ref[:, c * SORT_TM:(c + 1) * SORT_TM] = sel.astype(_bf16)
    acc = jnp.dot(sel_ref[...], y_ref[...], preferred_element_type=_f32)
    x2 = x1_ref[...] + _slab_mul(acc, _pick_mod(isp, mpg2, msg2))
    if final:
        y = _rms(x2) * nfin_ref[...]

        @pl.when(isp)
        def _():
            outp_ref[...] = y

        @pl.when(jnp.logical_not(isp))
        def _():
            outs_ref[...] = y
    else:
        out_ref[...] = x2


def _combine(l, x1, y_sorted, pos, mod_p4, mod_s, nfin, final):
    def pb(i):
        return jnp.minimum(i // (SEQ // B_TM), BATCH - 1)

    row_spec = pl.BlockSpec((B_TM, D_MODEL), lambda i: (i, 0))
    lane_spec = pl.BlockSpec((B_TM, LANES), lambda i: (i, 0))
    if final:
        out_specs = [pl.BlockSpec((B_TM, D_MODEL), lambda i: (jnp.minimum(i, B_PROMPT_BLOCKS - 1), 0)),
                     pl.BlockSpec((T_SAMPLE, D_MODEL), lambda i: (0, 0))]
        out_shape = [jax.ShapeDtypeStruct((T_PROMPT, D_MODEL), _f32),
                     jax.ShapeDtypeStruct((T_SAMPLE, D_MODEL), _f32)]
    else:
        out_specs = row_spec
        out_shape = jax.ShapeDtypeStruct((T_ALL, D_MODEL), _f32)
    return pl.pallas_call(
        functools.partial(_c_body, final=final),
        grid=(N_BLOCKS,),
        in_specs=[
            row_spec,
            pl.BlockSpec((BLOCK_ROWS, D_MODEL), lambda i: (i, 0)),
            lane_spec,
            pl.BlockSpec((None, None, 1, D_MODEL), lambda i: (l, pb(i), 0, 5)),
            pl.BlockSpec((None, DEC_BATCH, D_MODEL), lambda i: (l, 0, 5)),
            pl.BlockSpec((1, D_MODEL), lambda i: (0, 0)),
        ],
        out_specs=out_specs,
        out_shape=out_shape,
        scratch_shapes=[pltpu.VMEM((B_TM, BLOCK_ROWS), _bf16)],
        compiler_params=pltpu.CompilerParams(
            dimension_semantics=("arbitrary",), vmem_limit_bytes=VMEM_LIMIT),
        name="moe_combine",
    )(x1, y_sorted, pos, mod_p4, mod_s, nfin)


def kernel(x_prompt, x_sample, c_prompt, c_sample, state_pool, state_conformer, state_shortconv, w_ada, b_ada, norm_mix, norm_ffn, norm_final, w_in, pool_w, pool_scale, sgu_norm, sgu_w, sgu_b, conf_w, conf_b, conf_ln_g, conf_ln_b, sc_w, out_norm, w_out, router_w, router_b, w_up, b_up, w_down, b_down):
    gw = GROUP_WIDTH
    ns, nb = DEC_SEQ, DEC_BATCH

    c_all = jnp.concatenate([c_sample, c_prompt, jnp.zeros((ADA_ROWS - nb - BATCH, D_MODEL), _f32)], axis=0)
    mod_s, mod_p8 = _ada(c_all, w_ada, b_ada)
    mod_p4 = mod_p8[:, :BATCH].reshape(DEPTH, BATCH, 1, 6 * D_MODEL)

    rw_hi = router_w.astype(_bf16)
    rw_lo = (router_w - rw_hi.astype(_f32)).astype(_bf16)
    lane_pad = ((0, 0), (0, 0), (0, LANES - N_EXPERTS))
    params = {
        "nmix": norm_mix.reshape(DEPTH, 1, D_MODEL),
        "nffn": norm_ffn.reshape(DEPTH, 1, D_MODEL),
        "poolw": pool_w,
        "pscale": pool_scale.reshape(DEPTH, 1, gw),
        "sgun": sgu_norm.reshape(DEPTH, 1, gw),
        "sguw": sgu_w,
        "sgub": jnp.repeat(jnp.swapaxes(sgu_b, 1, 2), SGU_CHUNK, axis=2),
        "sgwc": jnp.repeat(jnp.transpose(sgu_w[:, :, :ns, :ns], (0, 2, 3, 1)).reshape(DEPTH, ns * ns, SGU_HEADS),
                           SGU_CHUNK, axis=2),
        "sgbc": jnp.repeat(jnp.swapaxes(sgu_b[:, :, :ns], 1, 2), SGU_CHUNK, axis=2),
        "cw": conf_w,
        "cb": conf_b.reshape(DEPTH, 1, gw),
        "clg": conf_ln_g.reshape(DEPTH, 1, gw),
        "clb": conf_ln_b.reshape(DEPTH, 1, gw),
        "scw": sc_w,
        "onorm": out_norm.reshape(DEPTH, 1, D_MODEL),
        "rw": jnp.concatenate([jnp.pad(rw_hi, lane_pad), jnp.pad(rw_lo, lane_pad)], axis=2),
        "rb": jnp.pad(router_b, ((0, 0), (0, LANES - N_EXPERTS))).reshape(DEPTH, 1, LANES),
    }
    w_in_b = w_in.astype(_bf16)
    w_out_b = w_out.astype(_bf16)
    b_up4 = b_up.reshape(DEPTH, N_EXPERTS, 1, 2 * D_EXPERT)
    b_down4 = b_down.reshape(DEPTH, N_EXPERTS, 1, D_MODEL)
    nfin = norm_final.reshape(1, D_MODEL)

    pool_t = jnp.transpose(state_pool, (0, 2, 1, 3))
    conf_t = jnp.transpose(state_conformer, (0, 2, 1, 3))
    sc_t = jnp.transpose(state_shortconv, (0, 2, 1, 3))

    x_p = x_prompt.reshape(T_PROMPT, D_MODEL)
    x_s = jnp.transpose(x_sample, (1, 0, 2)).reshape(T_SAMPLE, D_MODEL)
    xs_blk = 0

    pool_p, conf_p, sc_p, pool_s, conf_s, sc_s, v_s = [], [], [], [], [], [], []
    for l in range(DEPTH):
        ycat_p, po, co, so = _a_prompt(l, x_p, mod_p4, w_in_b, params)
        ycat_s, a_new, g_new, z_new, v_new = _a_sample(l, x_s, xs_blk, mod_s, w_in_b, params, pool_t, conf_t, sc_t)
        pool_p.append(po[:, POOL_HALO - (POOL_MAX - 1):])
        conf_p.append(co[:, CONF_HALO - (CONF_KERNEL - 1):])
        sc_p.append(so[:, SC_HALO - (SC_KERNEL - 1):])
        pool_s.append(jnp.concatenate([state_pool[l][:, ns:], jnp.transpose(a_new, (1, 0, 2))], axis=1))
        conf_s.append(jnp.concatenate([state_conformer[l][:, ns:], jnp.transpose(g_new, (1, 0, 2))], axis=1))
        sc_s.append(jnp.transpose(z_new, (1, 0, 2))[:, ns - (SC_KERNEL - 1):])
        v_s.append(jnp.transpose(v_new, (1, 0, 2)))

        x1, h_ext, pos, post, meta = _stage_b(l, x_p, x_s, xs_blk, ycat_p, ycat_s, w_out_b, mod_p4, mod_s, params)
        item_e, item_lo, item_nt, seg_start, seg_rows, used, n_items = _work_items(meta)
        x_sorted, g_sorted = _local_sort(used, post, h_ext)
        y_sorted = _moe_stream(l, item_e, item_lo, item_nt, seg_start, seg_rows, used, n_items, w_up, b_up4,
                               w_down, b_down4, x_sorted, g_sorted)
        if l < DEPTH - 1:
            x_p = x_s = _combine(l, x1, y_sorted, pos, mod_p4, mod_s, nfin, final=False)
            xs_blk = T_PROMPT // T_SAMPLE
        else:
            y_p, y_s = _combine(l, x1, y_sorted, pos, mod_p4, mod_s, nfin, final=True)

    y_prompt = y_p.reshape(BATCH, SEQ, D_MODEL)
    y_sample = jnp.transpose(y_s.reshape(ns, nb, D_MODEL), (1, 0, 2))
    return (y_prompt, y_sample, jnp.stack(pool_p), jnp.stack(pool_s), jnp.stack(conf_p), jnp.stack(conf_s),
            jnp.stack(sc_p), jnp.stack(sc_s), jnp.stack(v_s))
```
